```python
import math
import jax
import jax.numpy as jnp
from jax import lax
import numpy as np

D_MODEL = 4096
BATCH = 2
SEQ = 4096
DEPTH = 1
DEC_BATCH = 32
DEC_SEQ = 1
PAST_LEN = 8192
PAGE_SIZE = 128

GDN_HEADS = 16
GDN_DK = 128
GDN_DV = 128
GDN_CONV = 4
GDN_CHUNK = 64
NSA_HEADS = 16
NSA_KV_GROUPS = 4
NSA_HPG = NSA_HEADS // NSA_KV_GROUPS
NSA_DH = 128
CMP_BLOCK = 32
CMP_STRIDE = 16
SEL_BLOCK = 64
SEL_TOPK = 16
WINDOW = 512
Q_BLOCK = 128
FORCE_SCORE = 1.0e4
REL_BUCKETS = 32
REL_MAX_DIST = 1024
MOE_GROUPS = 4
MOE_PER_GROUP = 8
MOE_EXPERTS = MOE_GROUPS * MOE_PER_GROUP
MOE_TOPK = 2
EXPERT_HIDDEN = 512
PLE_DIM = 256
EPS = 1e-6

GDN_QK_W = GDN_HEADS * GDN_DK
GDN_V_W = GDN_HEADS * GDN_DV
CONV_CH = 2 * GDN_QK_W + GDN_V_W
NSA_Q_W = NSA_HEADS * NSA_DH
NSA_KV_W = NSA_KV_GROUPS * NSA_DH
PROJ_SPLITS = (GDN_QK_W, GDN_QK_W, GDN_V_W, GDN_V_W, GDN_HEADS, GDN_HEADS,
               NSA_Q_W, 6 * NSA_KV_W, 3 * NSA_HEADS, D_MODEL, D_MODEL)
PROJ_COLS = sum(PROJ_SPLITS)

kernel_name = "hybrid_gdn_nsa_hiermoe_step"


def rmsnorm(x, g):
    xf = x.astype(jnp.float32)
    y = xf * lax.rsqrt(jnp.mean(xf * xf, -1, keepdims=True) + EPS)
    return (y * g.astype(jnp.float32)).astype(x.dtype)


def l2norm(x):
    return x * lax.rsqrt(jnp.sum(x * x, -1, keepdims=True) + EPS)


def masked_softmax(s, mask):
    s = jnp.where(mask, s.astype(jnp.float32), -jnp.inf)
    m = jnp.max(s, -1, keepdims=True)
    m = jnp.where(jnp.isfinite(m), m, 0.0)
    e = jnp.exp(s - m)
    return e / jnp.maximum(jnp.sum(e, -1, keepdims=True), 1e-30)


def rel_bucket(dist):
    n = jnp.maximum(dist, 0)
    max_exact = REL_BUCKETS // 2
    nf = jnp.maximum(n, 1).astype(jnp.float32)
    large = max_exact + (jnp.log(nf / max_exact) / math.log(REL_MAX_DIST / max_exact)
                         * (REL_BUCKETS - max_exact)).astype(jnp.int32)
    large = jnp.minimum(large, REL_BUCKETS - 1)
    return jnp.where(n < max_exact, n, large)


def split_cols(z):
    offs = np.cumsum(PROJ_SPLITS)[:-1].tolist()
    return jnp.split(z, offs, axis=-1)


def causal_conv(x, buf, w):
    T = x.shape[1]
    xp = jnp.concatenate([buf.astype(x.dtype), x], axis=1)
    y = xp[:, 0:T] * w[0]
    for j in range(1, GDN_CONV):
        y = y + xp[:, j:j + T] * w[j]
    return jax.nn.silu(y), xp[:, T:]


def gated_delta_rule(q, k, v, g, beta, s0):
    Bn, T, H, dk = q.shape
    dv = v.shape[-1]
    cs = min(GDN_CHUNK, T)
    nc = -(-T // cs)
    pad = nc * cs - T

    def prep(a):
        a = jnp.pad(a, [(0, 0), (0, pad)] + [(0, 0)] * (a.ndim - 2))
        a = a.reshape((Bn, nc, cs) + a.shape[2:])
        return jnp.moveaxis(a, 3, 1)

    q, k, v, g, beta = prep(q), prep(k), prep(v), prep(g), prep(beta)
    gc = jnp.cumsum(g, axis=-1)
    ar = jnp.arange(cs)
    incl = ar[:, None] >= ar[None, :]
    strict = ar[:, None] > ar[None, :]
    decay = jnp.exp(jnp.where(incl, gc[..., :, None] - gc[..., None, :], -jnp.inf))
    kb = k * beta[..., None]
    lmat = jnp.einsum("bhnid,bhnjd->bhnij", kb, k) * jnp.where(strict, decay, 0.0)
    a_mat = jnp.eye(cs, dtype=jnp.float32) + lmat
    rhs = jnp.concatenate([kb * jnp.exp(gc)[..., None], v * beta[..., None]], -1)
    wu = lax.linalg.triangular_solve(a_mat, rhs, left_side=True, lower=True, unit_diagonal=True)
    w_c, u_c = wu[..., :dk], wu[..., dk:]
    aqk = jnp.einsum("bhnid,bhnjd->bhnij", q, k) * decay
    qg = q * jnp.exp(gc)[..., None]
    kg = k * jnp.exp(gc[..., -1:] - gc)[..., None]
    glast = jnp.exp(gc[..., -1])
    xs = tuple(jnp.moveaxis(t, 2, 0) for t in (qg, kg, w_c, u_c, aqk, glast))

    def step(s, inp):
        qg_i, kg_i, w_i, u_i, aqk_i, gl_i = inp
        v_new = u_i - jnp.einsum("bhid,bhde->bhie", w_i, s)
        o = jnp.einsum("bhid,bhde->bhie", qg_i, s) + jnp.einsum("bhij,bhje->bhie", aqk_i, v_new)
        s = gl_i[..., None, None] * s + jnp.einsum("bhid,bhie->bhde", kg_i, v_new)
        return s, o

    s_fin, o = lax.scan(step, s0, xs)
    o = jnp.moveaxis(o, 0, 2).reshape(Bn, H, nc * cs, dv)[:, :, :T]
    return jnp.moveaxis(o, 1, 2), s_fin


def gdn_mixer(qa, ka, va, za, b_logit, a_logit, conv_buf, s0, conv_w, dt_bias, a_log, head_gain):
    Bn, T, _ = qa.shape
    f32 = jnp.float32
    xc, new_buf = causal_conv(jnp.concatenate([qa, ka, va], -1), conv_buf, conv_w)
    q, k, v = jnp.split(xc.astype(f32), [GDN_QK_W, 2 * GDN_QK_W], -1)
    q = l2norm(q.reshape(Bn, T, GDN_HEADS, GDN_DK)) * (GDN_DK ** -0.5)
    k = l2norm(k.reshape(Bn, T, GDN_HEADS, GDN_DK))
    v = v.reshape(Bn, T, GDN_HEADS, GDN_DV)
    g = -jnp.exp(a_log.astype(f32)) * jax.nn.softplus(a_logit.astype(f32) + dt_bias.astype(f32))
    beta = jax.nn.sigmoid(b_logit.astype(f32))
    o, s_new = gated_delta_rule(q, k, v, g, beta, s0.astype(f32))
    o = rmsnorm(o, head_gain) * jax.nn.silu(za.astype(f32).reshape(Bn, T, GDN_HEADS, GDN_DV))
    return o.reshape(Bn, T, GDN_V_W).astype(qa.dtype), new_buf, s_new


def subblock_proj(rows, pe, w):
    r = CMP_BLOCK // CMP_STRIDE
    Bn, L = rows.shape[:2]
    ns = L // CMP_STRIDE
    sub = rows[:, :ns * CMP_STRIDE].reshape(Bn, ns, CMP_STRIDE, 2, NSA_KV_GROUPS, NSA_DH)
    w_r = w.reshape(2, r, CMP_STRIDE, NSA_DH, NSA_DH)
    pe_r = pe.reshape(2, r, CMP_STRIDE, NSA_DH)
    proj = jnp.einsum("bnsxgd,xmsde->mbnxge", sub, w_r)
    pe_term = jnp.einsum("xmsd,xmsde->xe", pe_r, w_r)
    return proj, pe_term


def compress(proj, pe_term):
    r, Bn, ns = proj.shape[:3]
    nc = ns - r + 1
    kvc = proj[0][:, 0:nc]
    for m in range(1, r):
        kvc = kvc + proj[m][:, m:m + nc]
    kvc = kvc + pe_term[None, None, :, None, :]
    c_end = jnp.arange(nc, dtype=jnp.int32) * CMP_STRIDE + (CMP_BLOCK - 1)
    return kvc, c_end


def cover_matrix(nc, ns):
    cs = np.arange(nc) * CMP_STRIDE
    ss = np.arange(ns) * SEL_BLOCK
    inter = np.minimum(cs[:, None] + CMP_BLOCK, ss[None, :] + SEL_BLOCK) - np.maximum(cs[:, None], ss[None, :])
    return jnp.asarray(np.clip(inter, 0, None) / CMP_BLOCK, dtype=jnp.float32)


def nsa_core(q, q_pos, kvc, c_end, cover, gather_sel, kvw, w_pos, gates, rel_bias):
    f32 = jnp.float32
    Bn, Tq = q.shape[:2]
    scale = NSA_DH ** -0.5
    tbl = rel_bias.astype(f32).reshape(REL_BUCKETS, NSA_KV_GROUPS, NSA_HPG)
    dist_c = q_pos[:, None] - c_end[None, :]
    s_c = jnp.einsum("btghd,bngd->bghtn", q, kvc[:, :, 0]).astype(f32) * scale
    s_c = s_c + jnp.transpose(tbl[rel_bucket(dist_c)], (2, 3, 0, 1))
    p_c = masked_softmax(s_c, dist_c >= 0)
    o_c = jnp.einsum("bghtn,bngd->btghd", p_c, kvc[:, :, 1].astype(f32))
    ns = cover.shape[1]
    imp = jnp.einsum("bghtn,nm->bgtm", p_c, cover)
    blk = jnp.arange(ns, dtype=jnp.int32)[None, :]
    cur = (q_pos // SEL_BLOCK)[:, None]
    forced = (blk == 0) | (blk == cur) | (blk == cur - 1)
    valid = blk * SEL_BLOCK <= q_pos[:, None]
    score = jnp.where(valid, jnp.where(forced, FORCE_SCORE, imp), -1.0)
    _, idx = lax.top_k(score, min(SEL_TOPK, ns))
    n_sel = idx.shape[-1]
    sel = gather_sel(idx)
    pos = idx[..., None] * SEL_BLOCK + jnp.arange(SEL_BLOCK, dtype=jnp.int32)
    dist_s = q_pos[None, None, :, None, None] - pos
    gi = jnp.arange(NSA_KV_GROUPS)[None, :, None, None, None]
    bias_s = jnp.transpose(tbl, (1, 0, 2))[gi, rel_bucket(dist_s)]
    s_s = jnp.einsum("btghd,bgtnkd->bghtnk", q, sel[..., 0, :]).astype(f32) * scale + jnp.moveaxis(bias_s, -1, 2)
    flat = (Bn, NSA_KV_GROUPS, NSA_HPG, Tq, n_sel * SEL_BLOCK)
    mask_s = (dist_s >= 0).reshape(Bn, NSA_KV_GROUPS, 1, Tq, n_sel * SEL_BLOCK)
    p_s = masked_softmax(s_s.reshape(flat), mask_s).reshape(s_s.shape)
    o_s = jnp.einsum("bghtnk,bgtnkd->btghd", p_s, sel[..., 1, :].astype(f32))
    dist_w = q_pos[:, None] - w_pos[None, :]
    s_w = jnp.einsum("btghd,bsgd->bghts", q, kvw[:, :, 0]).astype(f32) * scale
    s_w = s_w + jnp.transpose(tbl[rel_bucket(dist_w)], (2, 3, 0, 1))
    p_w = masked_softmax(s_w, (dist_w >= 0) & (dist_w < WINDOW) & (w_pos[None, :] >= 0))
    o_w = jnp.einsum("bghts,bsgd->btghd", p_w, kvw[:, :, 1].astype(f32))
    gt = jax.nn.sigmoid(gates.astype(f32)).reshape(Bn, Tq, NSA_KV_GROUPS, NSA_HPG, 3)
    o = gt[..., 0:1] * o_c + gt[..., 1:2] * o_s + gt[..., 2:3] * o_w
    return o.reshape(Bn, Tq, NSA_Q_W).astype(q.dtype)


def nsa_prompt(q, kv_c, kv_s, kv_w, gates, cmp_pe, cmp_w, rel_bias):
    Bn, T = q.shape[:2]
    proj, pe_term = subblock_proj(kv_c, cmp_pe, cmp_w)
    kvc, c_end = compress(proj, pe_term)
    ns = -(-T // SEL_BLOCK)
    kvs_blk = jnp.pad(kv_s, [(0, 0), (0, ns * SEL_BLOCK - T), (0, 0), (0, 0), (0, 0)])
    kvs_blk = kvs_blk.reshape(Bn, ns, SEL_BLOCK, 2, NSA_KV_GROUPS, NSA_DH)
    bi = jnp.arange(Bn)[:, None, None, None]
    gi = jnp.arange(NSA_KV_GROUPS)[None, :, None, None]

    def gather_sel(idx):
        return kvs_blk[bi, idx, :, :, gi, :]

    cover = cover_matrix(kvc.shape[1], ns)
    kvw_pad = jnp.pad(kv_w, [(0, 0), (WINDOW, 0), (0, 0), (0, 0), (0, 0)])

    def one_block(i):
        start = i * Q_BLOCK
        qb = lax.dynamic_slice_in_dim(q, start, Q_BLOCK, 1)
        gb = lax.dynamic_slice_in_dim(gates, start, Q_BLOCK, 1)
        kwb = lax.dynamic_slice_in_dim(kvw_pad, start, WINDOW + Q_BLOCK, 1)
        q_pos = start + jnp.arange(Q_BLOCK, dtype=jnp.int32)
        w_pos = start - WINDOW + jnp.arange(WINDOW + Q_BLOCK, dtype=jnp.int32)
        return nsa_core(qb, q_pos, kvc, c_end, cover, gather_sel, kwb, w_pos, gb, rel_bias)

    o = lax.map(one_block, jnp.arange(T // Q_BLOCK, dtype=jnp.int32))
    o = jnp.moveaxis(o, 0, 1).reshape(Bn, T, NSA_Q_W)
    return o, kv_w[:, T - min(WINDOW, T):]


def nsa_sample(q, kv_c, kv_s, kv_w, gates, cache_c, cache_s, page_table, win_buf, cmp_pe, cmp_w, rel_bias):
    Bn, T = q.shape[:2]
    past = page_table.shape[1] * PAGE_SIZE
    past_c = cache_c[page_table].reshape(Bn, past, 2, NSA_KV_GROUPS, NSA_DH)
    proj, pe_term = subblock_proj(past_c, cmp_pe, cmp_w)
    if T >= CMP_STRIDE:
        proj_new, _ = subblock_proj(kv_c, cmp_pe, cmp_w)
        proj = jnp.concatenate([proj, proj_new.astype(proj.dtype)], axis=2)
    kvc, c_end = compress(proj, pe_term)
    bpp = PAGE_SIZE // SEL_BLOCK
    n_past_blk = past // SEL_BLOCK
    n_new_blk = -(-T // SEL_BLOCK)
    pool = cache_s.reshape(cache_s.shape[0] * bpp, SEL_BLOCK, 2, NSA_KV_GROUPS, NSA_DH)
    new_blk = jnp.pad(kv_s, [(0, 0), (0, n_new_blk * SEL_BLOCK - T), (0, 0), (0, 0), (0, 0)])
    new_blk = new_blk.reshape(Bn, n_new_blk, SEL_BLOCK, 2, NSA_KV_GROUPS, NSA_DH)
    bi = jnp.arange(Bn)[:, None, None, None]
    gi = jnp.arange(NSA_KV_GROUPS)[None, :, None, None]

    def gather_sel(idx):
        in_past = idx < n_past_blk
        jp = jnp.minimum(idx, n_past_blk - 1)
        phys = page_table[bi, jp // bpp] * bpp + jp % bpp
        from_pool = pool[phys, :, :, gi, :]
        jn = jnp.clip(idx - n_past_blk, 0, n_new_blk - 1)
        from_new = new_blk[bi, jn, :, :, gi, :]
        return jnp.where(in_past[..., None, None, None], from_pool, from_new.astype(from_pool.dtype))

    cover = cover_matrix(kvc.shape[1], n_past_blk + n_new_blk)
    w_buf = win_buf.shape[1]
    kvw = jnp.concatenate([win_buf.astype(kv_w.dtype), kv_w], axis=1)
    w_pos = past - w_buf + jnp.arange(w_buf + T, dtype=jnp.int32)
    q_pos = past + jnp.arange(T, dtype=jnp.int32)
    o = nsa_core(q, q_pos, kvc, c_end, cover, gather_sel, kvw, w_pos, gates, rel_bias)
    return o, kvw[:, T:]


def hier_moe(m, w_rg, b_rg, w_re, b_re, w_gate, w_up, w_down):
    f32 = jnp.float32
    Bn, T, D = m.shape
    x = m.reshape(Bn * T, D)
    n = x.shape[0]
    pg = jax.nn.softmax((x @ w_rg).astype(f32) + b_rg.astype(f32), -1)
    pg_top, g_idx = lax.top_k(pg, 1)
    le = ((x @ w_re).astype(f32) + b_re.astype(f32)).reshape(n, MOE_GROUPS, MOE_PER_GROUP)
    le_g = le[jnp.arange(n), g_idx[:, 0]]
    e_val, e_idx = lax.top_k(jax.nn.softmax(le_g, -1), MOE_TOPK)
    wts = pg_top * e_val / jnp.sum(e_val, -1, keepdims=True)
    expert = g_idx * MOE_PER_GROUP + e_idx
    combine = jnp.sum(jax.nn.one_hot(expert, MOE_EXPERTS, dtype=f32) * wts[..., None], axis=1)
    hid = jax.nn.silu(jnp.einsum("nd,edf->nef", x, w_gate)) * jnp.einsum("nd,edf->nef", x, w_up)
    y = jnp.einsum("nef,efd->nd", hid * combine[..., None].astype(hid.dtype), w_down)
    return y.reshape(Bn, T, D).astype(m.dtype)


def layer_forward(h, ple, conv_buf, s0, past, lw, rel_bias):
    (g_mix, w_in, conv_w, dt_bias, a_log, gdn_norm, cmp_pe, cmp_w, w_proj_a, w_proj_b, w_out,
     g_ffn, w_rg, b_rg, w_re, b_re, w_gate, w_up, w_down, g_ple, w_ple_gate, w_ple_proj) = lw
    Bn, T, _ = h.shape
    a = rmsnorm(h, g_mix)
    qa, ka, va, za, b_logit, a_logit, qb, kvb, gate_b, gm_a, gm_b = split_cols(a @ w_in)
    o_a, conv_new, s_new = gdn_mixer(qa, ka, va, za, b_logit, a_logit, conv_buf, s0,
                                     conv_w, dt_bias, a_log, gdn_norm)
    q = qb.reshape(Bn, T, NSA_KV_GROUPS, NSA_HPG, NSA_DH)
    kv = kvb.reshape(Bn, T, 3, 2, NSA_KV_GROUPS, NSA_DH)
    kv_c, kv_s, kv_w = kv[:, :, 0], kv[:, :, 1], kv[:, :, 2]
    gates = gate_b.reshape(Bn, T, NSA_HEADS, 3)
    if past is None:
        o_b, win_new = nsa_prompt(q, kv_c, kv_s, kv_w, gates, cmp_pe, cmp_w, rel_bias)
    else:
        cache_c, cache_s, page_table, win_buf = past
        o_b, win_new = nsa_sample(q, kv_c, kv_s, kv_w, gates, cache_c, cache_s, page_table, win_buf,
                                  cmp_pe, cmp_w, rel_bias)
    merged = jax.nn.sigmoid(gm_a) * (o_a @ w_proj_a) + jax.nn.sigmoid(gm_b) * (o_b @ w_proj_b)
    h = h + merged @ w_out
    h = h + hier_moe(rmsnorm(h, g_ffn), w_rg, b_rg, w_re, b_re, w_gate, w_up, w_down)
    h = h + jax.nn.sigmoid(rmsnorm(h, g_ple) @ w_ple_gate) * (ple @ w_ple_proj)
    return h, (kv_c, kv_s, win_new, s_new, conv_new)


def setup_inputs(seed: int = 0) -> dict:
    key = jax.random.key(seed)
    keys = iter(jax.random.split(key, 48))
    f32 = jnp.float32
    n_pages = PAST_LEN // PAGE_SIZE
    n_phys = (DEC_BATCH * n_pages * 5) // 4
    w_buf = min(WINDOW, PAST_LEN)

    def nrm(shape, scale):
        return jax.random.normal(next(keys), shape, f32) * scale

    def gain(shape):
        return 1.0 + 0.05 * jax.random.normal(next(keys), shape, f32)

    page_table = jax.random.permutation(next(keys), n_phys)[:DEC_BATCH * n_pages]
    page_table = page_table.reshape(DEC_BATCH, n_pages).astype(jnp.int32)
    dt = jax.random.uniform(next(keys), (DEPTH, GDN_HEADS), f32, 0.001, 0.1)
    a_init = jax.random.uniform(next(keys), (DEPTH, GDN_HEADS), f32, 1.0, 16.0)
    kv_shape = (2, NSA_KV_GROUPS, NSA_DH)
    return {
        "x_prompt": nrm((BATCH, SEQ, D_MODEL), 1.0),
        "x_sample": nrm((DEC_BATCH, DEC_SEQ, D_MODEL), 1.0),
        "p_prompt": nrm((DEPTH, BATCH, SEQ, PLE_DIM), 1.0),
        "p_sample": nrm((DEPTH, DEC_BATCH, DEC_SEQ, PLE_DIM), 1.0),
        "cache_cmp_kv": nrm((DEPTH, n_phys, PAGE_SIZE) + kv_shape, 1.0),
        "cache_slc_kv": nrm((DEPTH, n_phys, PAGE_SIZE) + kv_shape, 1.0),
        "page_table": page_table,
        "state_win_kv": nrm((DEPTH, DEC_BATCH, w_buf) + kv_shape, 1.0),
        "state_gdn": nrm((DEPTH, DEC_BATCH, GDN_HEADS, GDN_DK, GDN_DV), 0.1),
        "state_conv": nrm((DEPTH, DEC_BATCH, GDN_CONV - 1, CONV_CH), 1.0),
        "g_mix": gain((DEPTH, D_MODEL)),
        "w_in": nrm((DEPTH, D_MODEL, PROJ_COLS), D_MODEL ** -0.5),
        "gdn_conv_w": nrm((DEPTH, GDN_CONV, CONV_CH), GDN_CONV ** -0.5),
        "gdn_dt_bias": dt + jnp.log(-jnp.expm1(-dt)),
        "gdn_a_log": jnp.log(a_init),
        "gdn_norm": gain((DEPTH, GDN_DV)),
        "cmp_pe": nrm((DEPTH, 2, CMP_BLOCK, NSA_DH), 0.02),
        "cmp_w": nrm((DEPTH, 2, CMP_BLOCK, NSA_DH, NSA_DH), (CMP_BLOCK * NSA_DH) ** -0.5),
        "rel_bias": nrm((REL_BUCKETS, NSA_HEADS), 0.1),
        "w_proj_a": nrm((DEPTH, GDN_V_W, D_MODEL), GDN_V_W ** -0.5),
        "w_proj_b": nrm((DEPTH, NSA_Q_W, D_MODEL), NSA_Q_W ** -0.5),
        "w_out": nrm((DEPTH, D_MODEL, D_MODEL), D_MODEL ** -0.5),
        "g_ffn": gain((DEPTH, D_MODEL)),
        "w_router_group": nrm((DEPTH, D_MODEL, MOE_GROUPS), D_MODEL ** -0.5),
        "b_router_group": nrm((DEPTH, MOE_GROUPS), 0.01),
        "w_router_expert": nrm((DEPTH, D_MODEL, MOE_EXPERTS), D_MODEL ** -0.5),
        "b_router_expert": nrm((DEPTH, MOE_EXPERTS), 0.01),
        "w_gate": nrm((DEPTH, MOE_EXPERTS, D_MODEL, EXPERT_HIDDEN), D_MODEL ** -0.5),
        "w_up": nrm((DEPTH, MOE_EXPERTS, D_MODEL, EXPERT_HIDDEN), D_MODEL ** -0.5),
        "w_down": nrm((DEPTH, MOE_EXPERTS, EXPERT_HIDDEN, D_MODEL), EXPERT_HIDDEN ** -0.5),
        "g_ple": gain((DEPTH, D_MODEL)),
        "w_ple_gate": nrm((DEPTH, D_MODEL, D_MODEL), D_MODEL ** -0.5),
        "w_ple_proj": nrm((DEPTH, PLE_DIM, D_MODEL), PLE_DIM ** -0.5),
        "g_final": gain((D_MODEL,)),
    }


def reference(x_prompt, x_sample, p_prompt, p_sample, cache_cmp_kv, cache_slc_kv, page_table,
              state_win_kv, state_gdn, state_conv, g_mix, w_in, gdn_conv_w, gdn_dt_bias, gdn_a_log,
              gdn_norm, cmp_pe, cmp_w, rel_bias, w_proj_a, w_proj_b, w_out, g_ffn, w_router_group,
              b_router_group, w_router_expert, b_router_expert, w_gate, w_up, w_down, g_ple,
              w_ple_gate, w_ple_proj, g_final):
    hp, hs = x_prompt, x_sample
    pc, ps, pw, pg, pv = [], [], [], [], []
    sc, ss, sw, sg, sv = [], [], [], [], []
    for i in range(DEPTH):
        lw = (g_mix[i], w_in[i], gdn_conv_w[i], gdn_dt_bias[i], gdn_a_log[i], gdn_norm[i], cmp_pe[i],
              cmp_w[i], w_proj_a[i], w_proj_b[i], w_out[i], g_ffn[i], w_router_group[i],
              b_router_group[i], w_router_expert[i], b_router_expert[i], w_gate[i], w_up[i],
              w_down[i], g_ple[i], w_ple_gate[i], w_ple_proj[i])
        bp = hp.shape[0]
        conv0 = jnp.zeros((bp, GDN_CONV - 1, CONV_CH), hp.dtype)
        s0 = jnp.zeros((bp, GDN_HEADS, GDN_DK, GDN_DV), jnp.float32)
        hp, st_p = layer_forward(hp, p_prompt[i], conv0, s0, None, lw, rel_bias)
        past = (cache_cmp_kv[i], cache_slc_kv[i], page_table, state_win_kv[i])
        hs, st_s = layer_forward(hs, p_sample[i], state_conv[i], state_gdn[i], past, lw, rel_bias)
        pc.append(st_p[0]); ps.append(st_p[1]); pw.append(st_p[2]); pg.append(st_p[3]); pv.append(st_p[4])
        sc.append(st_s[0]); ss.append(st_s[1]); sw.append(st_s[2]); sg.append(st_s[3]); sv.append(st_s[4])
    y_prompt = rmsnorm(hp, g_final)
    y_sample = rmsnorm(hs, g_final)
    return (y_prompt, y_sample, jnp.stack(pc), jnp.stack(ps), jnp.stack(pw), jnp.stack(pg), jnp.stack(pv),
            jnp.stack(sc), jnp.stack(ss), jnp.stack(sw), jnp.stack(sg), jnp.stack(sv))
```

```python
import functools
import math

import jax
import jax.numpy as jnp
import numpy as np
from jax import lax
from jax.experimental import pallas as pl
from jax.experimental.pallas import tpu as pltpu

D_MODEL = 4096
GDN_HEADS = 16
GDN_DK = 128
GDN_DV = 128
GDN_CONV = 4
GDN_CHUNK = 64
NSA_HEADS = 16
NSA_KV_GROUPS = 4
NSA_HPG = NSA_HEADS // NSA_KV_GROUPS
NSA_DH = 128
CMP_BLOCK = 32
CMP_STRIDE = 16
SEL_BLOCK = 64
SEL_TOPK = 16
WINDOW = 512
Q_BLOCK = 128
FORCE_SCORE = 1.0e4
REL_BUCKETS = 32
REL_MAX_DIST = 1024
PAGE_SIZE = 128
MOE_GROUPS = 4
MOE_PER_GROUP = 8
MOE_EXPERTS = MOE_GROUPS * MOE_PER_GROUP
MOE_TOPK = 2
EXPERT_HIDDEN = 512
EPS = 1e-6

GDN_QK_W = GDN_HEADS * GDN_DK
GDN_V_W = GDN_HEADS * GDN_DV
CONV_CH = 2 * GDN_QK_W + GDN_V_W
NSA_Q_W = NSA_HEADS * NSA_DH
NSA_KV_W = NSA_KV_GROUPS * NSA_DH

LANES = 128
SUBLANES = 8
VMEM_LIMIT = 56 * 1024 * 1024

ROW_ALIGN = 768
TM_DENSE = 768
TN_DENSE = 512
TM_ROWS = 256

Z_Q, Z_K, Z_V, Z_ZG = 0, 2048, 4096, 6144
Z_NQ = 8192
Z_KV = 10240
Z_GA = 13312
Z_GB = 17408
Z_COLS = 21504

bf16 = jnp.bfloat16
f32 = jnp.float32


def _cparams(sem):
    return pltpu.CompilerParams(dimension_semantics=sem, vmem_limit_bytes=VMEM_LIMIT)


def _rmsnorm_body(x_ref, g_ref, o_ref):
    x = x_ref[...]
    y = x * lax.rsqrt(jnp.mean(x * x, -1, keepdims=True) + EPS)
    o_ref[...] = (y * g_ref[...]).astype(o_ref.dtype)


def rmsnorm_rows(x, g, out_dtype):
    m, d = x.shape
    return pl.pallas_call(
        _rmsnorm_body,
        grid=(m // TM_ROWS,),
        in_specs=[pl.BlockSpec((TM_ROWS, d), lambda i: (i, 0)),
                  pl.BlockSpec((1, d), lambda i: (0, 0))],
        out_specs=pl.BlockSpec((TM_ROWS, d), lambda i: (i, 0)),
        out_shape=jax.ShapeDtypeStruct((m, d), out_dtype),
        compiler_params=_cparams(("parallel",)),
        name="rmsnorm_rows",
    )(x, g.reshape(1, d))


def _cast_body(x_ref, o_ref):
    o_ref[...] = x_ref[...].astype(o_ref.dtype)


def cast_bf16(x2d, tr):
    r, c = x2d.shape
    return pl.pallas_call(
        _cast_body,
        grid=(r // tr,),
        in_specs=[pl.BlockSpec((tr, c), lambda i: (i, 0))],
        out_specs=pl.BlockSpec((tr, c), lambda i: (i, 0)),
        out_shape=jax.ShapeDtypeStruct((r, c), bf16),
        compiler_params=_cparams(("parallel",)),
        name="cast_bf16",
    )(x2d)


def _proj_body(a_ref, w_ref, o_ref, wb_ref):
    @pl.when(pl.program_id(1) == 0)
    def _():
        wb_ref[...] = w_ref[...].astype(bf16)
    o_ref[...] = jnp.dot(a_ref[...], wb_ref[...], preferred_element_type=f32).astype(o_ref.dtype)


def proj_matmul(a, w, out_dtype, tn):
    m, k = a.shape
    n = w.shape[1]
    return pl.pallas_call(
        _proj_body,
        grid=(n // tn, m // TM_DENSE),
        in_specs=[pl.BlockSpec((TM_DENSE, k), lambda j, i: (i, 0)),
                  pl.BlockSpec((k, tn), lambda j, i: (0, j))],
        out_specs=pl.BlockSpec((TM_DENSE, tn), lambda j, i: (i, j)),
        out_shape=jax.ShapeDtypeStruct((m, n), out_dtype),
        scratch_shapes=[pltpu.VMEM((k, tn), bf16)],
        compiler_params=_cparams(("arbitrary", "arbitrary")),
        name="proj_matmul",
    )(a, w)


def _merge_body(oa_ref, ob_ref, ga_ref, gb_ref, wa_ref, wb_ref, o_ref, wa_s, wb_s):
    @pl.when(pl.program_id(1) == 0)
    def _():
        wa_s[...] = wa_ref[...].astype(bf16)
        wb_s[...] = wb_ref[...].astype(bf16)
    pa = jnp.dot(oa_ref[...], wa_s[...], preferred_element_type=f32)
    pb = jnp.dot(ob_ref[...], wb_s[...], preferred_element_type=f32)
    o_ref[...] = (jax.nn.sigmoid(ga_ref[...]) * pa + jax.nn.sigmoid(gb_ref[...]) * pb).astype(o_ref.dtype)


def merge_matmul(o_a, o_b, z, w_a, w_b):
    m, ka = o_a.shape
    kb = o_b.shape[1]
    n = w_a.shape[1]
    tn = TN_DENSE
    ja, jb = Z_GA // tn, Z_GB // tn
    return pl.pallas_call(
        _merge_body,
        grid=(n // tn, m // TM_DENSE),
        in_specs=[pl.BlockSpec((TM_DENSE, ka), lambda j, i: (i, 0)),
                  pl.BlockSpec((TM_DENSE, kb), lambda j, i: (i, 0)),
                  pl.BlockSpec((TM_DENSE, tn), lambda j, i: (i, ja + j)),
                  pl.BlockSpec((TM_DENSE, tn), lambda j, i: (i, jb + j)),
                  pl.BlockSpec((ka, tn), lambda j, i: (0, j)),
                  pl.BlockSpec((kb, tn), lambda j, i: (0, j))],
        out_specs=pl.BlockSpec((TM_DENSE, tn), lambda j, i: (i, j)),
        out_shape=jax.ShapeDtypeStruct((m, n), bf16),
        scratch_shapes=[pltpu.VMEM((ka, tn), bf16), pltpu.VMEM((kb, tn), bf16)],
        compiler_params=_cparams(("arbitrary", "arbitrary")),
        name="merge_matmul",
    )(o_a, o_b, z, z, w_a, w_b)


def _resid_body(a_ref, w_ref, h_ref, o_ref, wb_ref):
    @pl.when(pl.program_id(1) == 0)
    def _():
        wb_ref[...] = w_ref[...].astype(bf16)
    o_ref[...] = h_ref[...] + jnp.dot(a_ref[...], wb_ref[...], preferred_element_type=f32)


def resid_matmul(a, w, h):
    m, k = a.shape
    n = w.shape[1]
    tn = TN_DENSE
    return pl.pallas_call(
        _resid_body,
        grid=(n // tn, m // TM_DENSE),
        in_specs=[pl.BlockSpec((TM_DENSE, k), lambda j, i: (i, 0)),
                  pl.BlockSpec((k, tn), lambda j, i: (0, j)),
                  pl.BlockSpec((TM_DENSE, tn), lambda j, i: (i, j))],
        out_specs=pl.BlockSpec((TM_DENSE, tn), lambda j, i: (i, j)),
        out_shape=jax.ShapeDtypeStruct((m, n), f32),
        scratch_shapes=[pltpu.VMEM((k, tn), bf16)],
        compiler_params=_cparams(("arbitrary", "arbitrary")),
        name="resid_matmul",
    )(a, w, h)


def _ple_body(a_ref, w_ref, p_ref, wp_ref, h_ref, o_ref, wb_ref, wpb_ref):
    @pl.when(pl.program_id(1) == 0)
    def _():
        wb_ref[...] = w_ref[...].astype(bf16)
        wpb_ref[...] = wp_ref[...].astype(bf16)
    gate = jax.nn.sigmoid(jnp.dot(a_ref[...], wb_ref[...], preferred_element_type=f32))
    emb = jnp.dot(p_ref[...], wpb_ref[...], preferred_element_type=f32)
    o_ref[...] = h_ref[...] + gate * emb


def ple_matmul(a, w_gate, p, w_proj, h):
    m, k = a.shape
    kp = p.shape[1]
    n = w_gate.shape[1]
    tn = TN_DENSE
    return pl.pallas_call(
        _ple_body,
        grid=(n // tn, m // TM_DENSE),
        in_specs=[pl.BlockSpec((TM_DENSE, k), lambda j, i: (i, 0)),
                  pl.BlockSpec((k, tn), lambda j, i: (0, j)),
                  pl.BlockSpec((TM_DENSE, kp), lambda j, i: (i, 0)),
                  pl.BlockSpec((kp, tn), lambda j, i: (0, j)),
                  pl.BlockSpec((TM_DENSE, tn), lambda j, i: (i, j))],
        out_specs=pl.BlockSpec((TM_DENSE, tn), lambda j, i: (i, j)),
        out_shape=jax.ShapeDtypeStruct((m, n), f32),
        scratch_shapes=[pltpu.VMEM((k, tn), bf16), pltpu.VMEM((kp, tn), bf16)],
        compiler_params=_cparams(("arbitrary", "arbitrary")),
        name="ple_matmul",
    )(a, w_gate, p, w_proj, h)


def _router_body(h_ref, g_ref, wr_ref, br_ref, m_ref, r_ref):
    x = h_ref[...]
    y = x * lax.rsqrt(jnp.mean(x * x, -1, keepdims=True) + EPS) * g_ref[...]
    m_ref[...] = y.astype(bf16)
    logits = jnp.dot(y, wr_ref[...], preferred_element_type=f32,
                     precision=lax.Precision.HIGHEST) + br_ref[...]
    lane = lax.broadcasted_iota(jnp.int32, logits.shape, 1)
    neg = -jnp.inf
    lg = jnp.where(lane < MOE_GROUPS, logits, neg)
    eg = jnp.exp(lg - jnp.max(lg, -1, keepdims=True))
    pg = eg / jnp.sum(eg, -1, keepdims=True)
    pg_top = jnp.max(pg, -1, keepdims=True)
    g_idx = jnp.min(jnp.where(pg == pg_top, lane, LANES), -1, keepdims=True)
    lo = MOE_GROUPS + MOE_PER_GROUP * g_idx
    emask = (lane >= lo) & (lane < lo + MOE_PER_GROUP)
    le = jnp.where(emask, logits, neg)
    ee = jnp.exp(le - jnp.max(le, -1, keepdims=True))
    pe = jnp.where(emask, ee / jnp.sum(ee, -1, keepdims=True), -1.0)
    v1 = jnp.max(pe, -1, keepdims=True)
    i1 = jnp.min(jnp.where(pe == v1, lane, LANES), -1, keepdims=True)
    pe2 = jnp.where(lane == i1, -1.0, pe)
    v2 = jnp.max(pe2, -1, keepdims=True)
    i2 = jnp.min(jnp.where(pe2 == v2, lane, LANES), -1, keepdims=True)
    den = v1 + v2
    w1 = pg_top * v1 / den
    w2 = pg_top * v2 / den
    e1 = (i1 - MOE_GROUPS).astype(f32)
    e2 = (i2 - MOE_GROUPS).astype(f32)
    r_ref[...] = jnp.where(lane == 0, e1, jnp.where(lane == 1, e2,
                           jnp.where(lane == 2, w1, jnp.where(lane == 3, w2, 0.0))))


def moe_router(h, g_ffn, w_router, b_router):
    m, d = h.shape
    return pl.pallas_call(
        _router_body,
        grid=(m // TM_ROWS,),
        in_specs=[pl.BlockSpec((TM_ROWS, d), lambda i: (i, 0)),
                  pl.BlockSpec((1, d), lambda i: (0, 0)),
                  pl.BlockSpec((d, LANES), lambda i: (0, 0)),
                  pl.BlockSpec((1, LANES), lambda i: (0, 0))],
        out_specs=[pl.BlockSpec((TM_ROWS, d), lambda i: (i, 0)),
                   pl.BlockSpec((TM_ROWS, LANES), lambda i: (i, 0))],
        out_shape=[jax.ShapeDtypeStruct((m, d), bf16),
                   jax.ShapeDtypeStruct((m, LANES), f32)],
        compiler_params=_cparams(("parallel",)),
        name="moe_router",
    )(h, g_ffn.reshape(1, d), w_router, b_router)


def _expert_body(te_ref, tv_ref, x_ref, rw_ref, wg_ref, wu_ref, wd_ref, o_ref):
    t = pl.program_id(0)

    @pl.when(tv_ref[t] > 0)
    def _():
        x = x_ref[...]
        gate = jnp.dot(x, wg_ref[...], preferred_element_type=f32)
        up = jnp.dot(x, wu_ref[...], preferred_element_type=f32)
        hid = (jax.nn.silu(gate) * up * rw_ref[...]).astype(bf16)
        o_ref[...] = jnp.dot(hid, wd_ref[...], preferred_element_type=f32)

    @pl.when(tv_ref[t] == 0)
    def _():
        o_ref[...] = jnp.zeros_like(o_ref)


def expert_matmul(tile_expert, tile_valid, xs, row_w, wg, wu, wd):
    r, d = xs.shape
    f = wg.shape[2]
    n_tiles = r // TM_ROWS
    grid_spec = pltpu.PrefetchScalarGridSpec(
        num_scalar_prefetch=2,
        grid=(n_tiles,),
        in_specs=[pl.BlockSpec((TM_ROWS, d), lambda t, te, tv: (t, 0)),
                  pl.BlockSpec((TM_ROWS, 1), lambda t, te, tv: (t, 0)),
                  pl.BlockSpec((None, d, f), lambda t, te, tv: (te[t], 0, 0)),
                  pl.BlockSpec((None, d, f), lambda t, te, tv: (te[t], 0, 0)),
                  pl.BlockSpec((None, f, d), lambda t, te, tv: (te[t], 0, 0))],
        out_specs=pl.BlockSpec((TM_ROWS, d), lambda t, te, tv: (t, 0)),
    )
    return pl.pallas_call(
        _expert_body,
        grid_spec=grid_spec,
        out_shape=jax.ShapeDtypeStruct((r, d), f32),
        compiler_params=_cparams(("arbitrary",)),
        name="expert_matmul",
    )(tile_expert, tile_valid, xs, row_w, wg, wu, wd)


def hier_moe(h, n_real, g_ffn, w_rg, b_rg, w_re, b_re, w_gate, w_up, w_down):
    mp, d = h.shape
    n_route = MOE_GROUPS + MOE_EXPERTS
    w_router = jnp.zeros((d, LANES), f32).at[:, :MOE_GROUPS].set(w_rg).at[:, MOE_GROUPS:n_route].set(w_re)
    b_router = jnp.zeros((1, LANES), f32).at[0, :MOE_GROUPS].set(b_rg).at[0, MOE_GROUPS:n_route].set(b_re)
    m_bf, slab = moe_router(h, g_ffn, w_router, b_router)
    ids = slab[:n_real, 0:2].astype(jnp.int32)
    wts = slab[:n_real, 2:4]

    tm = TM_ROWS
    n_assign = n_real * MOE_TOPK
    n_slots = -(-(n_assign + MOE_EXPERTS * (tm - 1)) // tm) * tm
    e_flat = ids.reshape(-1)
    order = jnp.argsort(e_flat, stable=True)
    e_sorted = e_flat[order]
    counts = jnp.sum(jax.nn.one_hot(e_flat, MOE_EXPERTS, dtype=jnp.int32), axis=0)
    padded = -(-counts // tm) * tm
    start_p = jnp.cumsum(padded) - padded
    start = jnp.cumsum(counts) - counts
    slot_sorted = start_p[e_sorted] + (jnp.arange(n_assign, dtype=jnp.int32) - start[e_sorted])
    slot_of = jnp.zeros((n_assign,), jnp.int32).at[order].set(slot_sorted.astype(jnp.int32))
    src_tok = jnp.zeros((n_slots,), jnp.int32).at[slot_of].set(jnp.arange(n_assign, dtype=jnp.int32) // MOE_TOPK)
    row_w = jnp.zeros((n_slots,), f32).at[slot_of].set(wts.reshape(-1))
    tile_start = jnp.arange(n_slots // tm, dtype=jnp.int32) * tm
    ends = jnp.cumsum(padded)
    tile_expert = jnp.minimum(jnp.searchsorted(ends, tile_start, side="right"), MOE_EXPERTS - 1).astype(jnp.int32)
    tile_valid = (tile_start < ends[-1]).astype(jnp.int32)

    xs = jnp.take(m_bf, src_tok, axis=0)
    wg = cast_bf16(w_gate.reshape(MOE_EXPERTS * d, EXPERT_HIDDEN), 4096).reshape(MOE_EXPERTS, d, EXPERT_HIDDEN)
    wu = cast_bf16(w_up.reshape(MOE_EXPERTS * d, EXPERT_HIDDEN), 4096).reshape(MOE_EXPERTS, d, EXPERT_HIDDEN)
    wd = cast_bf16(w_down.reshape(MOE_EXPERTS * EXPERT_HIDDEN, d), 512).reshape(MOE_EXPERTS, EXPERT_HIDDEN, d)
    ys = expert_matmul(tile_expert, tile_valid, xs, row_w.reshape(n_slots, 1), wg, wu, wd)
    slot2 = slot_of.reshape(n_real, MOE_TOPK)
    y = jnp.take(ys, slot2[:, 0], axis=0) + jnp.take(ys, slot2[:, 1], axis=0)
    return h.at[:n_real].add(y)


def _rmsnorm_j(x, g):
    xf = x.astype(f32)
    y = xf * lax.rsqrt(jnp.mean(xf * xf, -1, keepdims=True) + EPS)
    return (y * g.astype(f32)).astype(x.dtype)


def _l2norm(x):
    return x * lax.rsqrt(jnp.sum(x * x, -1, keepdims=True) + EPS)


def _masked_softmax(s, mask):
    s = jnp.where(mask, s.astype(f32), -jnp.inf)
    m = jnp.max(s, -1, keepdims=True)
    m = jnp.where(jnp.isfinite(m), m, 0.0)
    e = jnp.exp(s - m)
    return e / jnp.maximum(jnp.sum(e, -1, keepdims=True), 1e-30)


def _rel_bucket(dist):
    n = jnp.maximum(dist, 0)
    max_exact = REL_BUCKETS // 2
    nf = jnp.maximum(n, 1).astype(f32)
    large = max_exact + (jnp.log(nf / max_exact) / math.log(REL_MAX_DIST / max_exact)
                         * (REL_BUCKETS - max_exact)).astype(jnp.int32)
    large = jnp.minimum(large, REL_BUCKETS - 1)
    return jnp.where(n < max_exact, n, large)


def _causal_conv(x, buf, w):
    T = x.shape[1]
    xp = jnp.concatenate([buf.astype(x.dtype), x], axis=1)
    y = xp[:, 0:T] * w[0]
    for j in range(1, GDN_CONV):
        y = y + xp[:, j:j + T] * w[j]
    return jax.nn.silu(y), xp[:, T:]


def _gated_delta_rule(q, k, v, g, beta, s0):
    Bn, T, H, dk = q.shape
    dv = v.shape[-1]
    cs = min(GDN_CHUNK, T)
    nc = -(-T // cs)
    pad = nc * cs - T

    def prep(a):
        a = jnp.pad(a, [(0, 0), (0, pad)] + [(0, 0)] * (a.ndim - 2))
        a = a.reshape((Bn, nc, cs) + a.shape[2:])
        return jnp.moveaxis(a, 3, 1)

    q, k, v, g, beta = prep(q), prep(k), prep(v), prep(g), prep(beta)
    gc = jnp.cumsum(g, axis=-1)
    ar = jnp.arange(cs)
    incl = ar[:, None] >= ar[None, :]
    strict = ar[:, None] > ar[None, :]
    decay = jnp.exp(jnp.where(incl, gc[..., :, None] - gc[..., None, :], -jnp.inf))
    kb = k * beta[..., None]
    lmat = jnp.einsum("bhnid,bhnjd->bhnij", kb, k) * jnp.where(strict, decay, 0.0)
    a_mat = jnp.eye(cs, dtype=f32) + lmat
    rhs = jnp.concatenate([kb * jnp.exp(gc)[..., None], v * beta[..., None]], -1)
    wu = lax.linalg.triangular_solve(a_mat, rhs, left_side=True, lower=True, unit_diagonal=True)
    w_c, u_c = wu[..., :dk], wu[..., dk:]
    aqk = jnp.einsum("bhnid,bhnjd->bhnij", q, k) * decay
    qg = q * jnp.exp(gc)[..., None]
    kg = k * jnp.exp(gc[..., -1:] - gc)[..., None]
    glast = jnp.exp(gc[..., -1])
    xs = tuple(jnp.moveaxis(t, 2, 0) for t in (qg, kg, w_c, u_c, aqk, glast))

    def step(s, inp):
        qg_i, kg_i, w_i, u_i, aqk_i, gl_i = inp
        v_new = u_i - jnp.einsum("bhid,bhde->bhie", w_i, s)
        o = jnp.einsum("bhid,bhde->bhie", qg_i, s) + jnp.einsum("bhij,bhje->bhie", aqk_i, v_new)
        s = gl_i[..., None, None] * s + jnp.einsum("bhid,bhie->bhde", kg_i, v_new)
        return s, o

    s_fin, o = lax.scan(step, s0, xs)
    o = jnp.moveaxis(o, 0, 2).reshape(Bn, H, nc * cs, dv)[:, :, :T]
    return jnp.moveaxis(o, 1, 2), s_fin


def _gdn_mixer(qa, ka, va, za, b_logit, a_logit, conv_buf, s0, conv_w, dt_bias, a_log, head_gain):
    Bn, T, _ = qa.shape
    xc, new_buf = _causal_conv(jnp.concatenate([qa, ka, va], -1), conv_buf, conv_w)
    q, k, v = jnp.split(xc.astype(f32), [GDN_QK_W, 2 * GDN_QK_W], -1)
    q = _l2norm(q.reshape(Bn, T, GDN_HEADS, GDN_DK)) * (GDN_DK ** -0.5)
    k = _l2norm(k.reshape(Bn, T, GDN_HEADS, GDN_DK))
    v = v.reshape(Bn, T, GDN_HEADS, GDN_DV)
    g = -jnp.exp(a_log.astype(f32)) * jax.nn.softplus(a_logit.astype(f32) + dt_bias.astype(f32))
    beta = jax.nn.sigmoid(b_logit.astype(f32))
    o, s_new = _gated_delta_rule(q, k, v, g, beta, s0.astype(f32))
    o = _rmsnorm_j(o, head_gain) * jax.nn.silu(za.astype(f32).reshape(Bn, T, GDN_HEADS, GDN_DV))
    return o.reshape(Bn, T, GDN_V_W).astype(qa.dtype), new_buf, s_new


def _subblock_proj(rows, pe, w):
    r = CMP_BLOCK // CMP_STRIDE
    Bn, L = rows.shape[:2]
    ns = L // CMP_STRIDE
    sub = rows[:, :ns * CMP_STRIDE].reshape(Bn, ns, CMP_STRIDE, 2, NSA_KV_GROUPS, NSA_DH)
    w_r = w.reshape(2, r, CMP_STRIDE, NSA_DH, NSA_DH)
    pe_r = pe.reshape(2, r, CMP_STRIDE, NSA_DH)
    proj = jnp.einsum("bnsxgd,xmsde->mbnxge", sub, w_r)
    pe_term = jnp.einsum("xmsd,xmsde->xe", pe_r, w_r)
    return proj, pe_term


def _compress(proj, pe_term):
    r, Bn, ns = proj.shape[:3]
    nc = ns - r + 1
    kvc = proj[0][:, 0:nc]
    for m in range(1, r):
        kvc = kvc + proj[m][:, m:m + nc]
    kvc = kvc + pe_term[None, None, :, None, :]
    c_end = jnp.arange(nc, dtype=jnp.int32) * CMP_STRIDE + (CMP_BLOCK - 1)
    return kvc, c_end


def _cover_matrix(nc, ns):
    cs = np.arange(nc) * CMP_STRIDE
    ss = np.arange(ns) * SEL_BLOCK
    inter = np.minimum(cs[:, None] + CMP_BLOCK, ss[None, :] + SEL_BLOCK) - np.maximum(cs[:, None], ss[None, :])
    return jnp.asarray(np.clip(inter, 0, None) / CMP_BLOCK, dtype=f32)


def _nsa_core(q, q_pos, kvc, c_end, cover, gather_sel, kvw, w_pos, gates, rel_bias):
    Bn, Tq = q.shape[:2]
    scale = NSA_DH ** -0.5
    tbl = rel_bias.astype(f32).reshape(REL_BUCKETS, NSA_KV_GROUPS, NSA_HPG)
    dist_c = q_pos[:, None] - c_end[None, :]
    s_c = jnp.einsum("btghd,bngd->bghtn", q, kvc[:, :, 0]).astype(f32) * scale
    s_c = s_c + jnp.transpose(tbl[_rel_bucket(dist_c)], (2, 3, 0, 1))
    p_c = _masked_softmax(s_c, dist_c >= 0)
    o_c = jnp.einsum("bghtn,bngd->btghd", p_c, kvc[:, :, 1].astype(f32))
    ns = cover.shape[1]
    imp = jnp.einsum("bghtn,nm->bgtm", p_c, cover)
    blk = jnp.arange(ns, dtype=jnp.int32)[None, :]
    cur = (q_pos // SEL_BLOCK)[:, None]
    forced = (blk == 0) | (blk == cur) | (blk == cur - 1)
    valid = blk * SEL_BLOCK <= q_pos[:, None]
    score = jnp.where(valid, jnp.where(forced, FORCE_SCORE, imp), -1.0)
    _, idx = lax.top_k(score, min(SEL_TOPK, ns))
    n_sel = idx.shape[-1]
    sel = gather_sel(idx)
    pos = idx[..., None] * SEL_BLOCK + jnp.arange(SEL_BLOCK, dtype=jnp.int32)
    dist_s = q_pos[None, None, :, None, None] - pos
    gi = jnp.arange(NSA_KV_GROUPS)[None, :, None, None, None]
    bias_s = jnp.transpose(tbl, (1, 0, 2))[gi, _rel_bucket(dist_s)]
    s_s = jnp.einsum("btghd,bgtnkd->bghtnk", q, sel[..., 0, :]).astype(f32) * scale + jnp.moveaxis(bias_s, -1, 2)
    flat = (Bn, NSA_KV_GROUPS, NSA_HPG, Tq, n_sel * SEL_BLOCK)
    mask_s = (dist_s >= 0).reshape(Bn, NSA_KV_GROUPS, 1, Tq, n_sel * SEL_BLOCK)
    p_s = _masked_softmax(s_s.reshape(flat), mask_s).reshape(s_s.shape)
    o_s = jnp.einsum("bghtnk,bgtnkd->btghd", p_s, sel[..., 1, :].astype(f32))
    dist_w = q_pos[:, None] - w_pos[None, :]
    s_w = jnp.einsum("btghd,bsgd->bghts", q, kvw[:, :, 0]).astype(f32) * scale
    s_w = s_w + jnp.transpose(tbl[_rel_bucket(dist_w)], (2, 3, 0, 1))
    p_w = _masked_softmax(s_w, (dist_w >= 0) & (dist_w < WINDOW) & (w_pos[None, :] >= 0))
    o_w = jnp.einsum("bghts,bsgd->btghd", p_w, kvw[:, :, 1].astype(f32))
    gt = jax.nn.sigmoid(gates.astype(f32)).reshape(Bn, Tq, NSA_KV_GROUPS, NSA_HPG, 3)
    o = gt[..., 0:1] * o_c + gt[..., 1:2] * o_s + gt[..., 2:3] * o_w
    return o.reshape(Bn, Tq, NSA_Q_W).astype(q.dtype)


def _nsa_prompt(q, kv_c, kv_s, kv_w, gates, cmp_pe, cmp_w, rel_bias):
    Bn, T = q.shape[:2]
    proj, pe_term = _subblock_proj(kv_c, cmp_pe, cmp_w)
    kvc, c_end = _compress(proj, pe_term)
    ns = -(-T // SEL_BLOCK)
    kvs_blk = jnp.pad(kv_s, [(0, 0), (0, ns * SEL_BLOCK - T), (0, 0), (0, 0), (0, 0)])
    kvs_blk = kvs_blk.reshape(Bn, ns, SEL_BLOCK, 2, NSA_KV_GROUPS, NSA_DH)
    bi = jnp.arange(Bn)[:, None, None, None]
    gi = jnp.arange(NSA_KV_GROUPS)[None, :, None, None]

    def gather_sel(idx):
        return kvs_blk[bi, idx, :, :, gi, :]

    cover = _cover_matrix(kvc.shape[1], ns)
    kvw_pad = jnp.pad(kv_w, [(0, 0), (WINDOW, 0), (0, 0), (0, 0), (0, 0)])

    def one_block(i):
        start = i * Q_BLOCK
        qb = lax.dynamic_slice_in_dim(q, start, Q_BLOCK, 1)
        gb = lax.dynamic_slice_in_dim(gates, start, Q_BLOCK, 1)
        kwb = lax.dynamic_slice_in_dim(kvw_pad, start, WINDOW + Q_BLOCK, 1)
        q_pos = start + jnp.arange(Q_BLOCK, dtype=jnp.int32)
        w_pos = start - WINDOW + jnp.arange(WINDOW + Q_BLOCK, dtype=jnp.int32)
        return _nsa_core(qb, q_pos, kvc, c_end, cover, gather_sel, kwb, w_pos, gb, rel_bias)

    o = lax.map(one_block, jnp.arange(T // Q_BLOCK, dtype=jnp.int32))
    o = jnp.moveaxis(o, 0, 1).reshape(Bn, T, NSA_Q_W)
    return o, kv_w[:, T - min(WINDOW, T):]


def _nsa_sample(q, kv_c, kv_s, kv_w, gates, cache_c, cache_s, page_table, win_buf, cmp_pe, cmp_w, rel_bias):
    Bn, T = q.shape[:2]
    past = page_table.shape[1] * PAGE_SIZE
    past_c = cache_c[page_table].reshape(Bn, past, 2, NSA_KV_GROUPS, NSA_DH)
    proj, pe_term = _subblock_proj(past_c, cmp_pe, cmp_w)
    kvc, c_end = _compress(proj, pe_term)
    bpp = PAGE_SIZE // SEL_BLOCK
    n_past_blk = past // SEL_BLOCK
    n_new_blk = -(-T // SEL_BLOCK)
    pool = cache_s.reshape(cache_s.shape[0] * bpp, SEL_BLOCK, 2, NSA_KV_GROUPS, NSA_DH)
    new_blk = jnp.pad(kv_s, [(0, 0), (0, n_new_blk * SEL_BLOCK - T), (0, 0), (0, 0), (0, 0)])
    new_blk = new_blk.reshape(Bn, n_new_blk, SEL_BLOCK, 2, NSA_KV_GROUPS, NSA_DH)
    bi = jnp.arange(Bn)[:, None, None, None]
    gi = jnp.arange(NSA_KV_GROUPS)[None, :, None, None]

    def gather_sel(idx):
        in_past = idx < n_past_blk
        jp = jnp.minimum(idx, n_past_blk - 1)
        phys = page_table[bi, jp // bpp] * bpp + jp % bpp
        from_pool = pool[phys, :, :, gi, :]
        jn = jnp.clip(idx - n_past_blk, 0, n_new_blk - 1)
        from_new = new_blk[bi, jn, :, :, gi, :]
        return jnp.where(in_past[..., None, None, None], from_pool, from_new.astype(from_pool.dtype))

    cover = _cover_matrix(kvc.shape[1], n_past_blk + n_new_blk)
    w_buf = win_buf.shape[1]
    kvw = jnp.concatenate([win_buf.astype(kv_w.dtype), kv_w], axis=1)
    w_pos = past - w_buf + jnp.arange(w_buf + T, dtype=jnp.int32)
    q_pos = past + jnp.arange(T, dtype=jnp.int32)
    o = _nsa_core(q, q_pos, kvc, c_end, cover, gather_sel, kvw, w_pos, gates, rel_bias)
    return o, kvw[:, T:]


def kernel(x_prompt, x_sample, p_prompt, p_sample, cache_cmp_kv, cache_slc_kv, page_table, state_win_kv, state_gdn, state_conv, g_mix, w_in, gdn_conv_w, gdn_dt_bias, gdn_a_log, gdn_norm, cmp_pe, cmp_w, rel_bias, w_proj_a, w_proj_b, w_out, g_ffn, w_router_group, b_router_group, w_router_expert, b_router_expert, w_gate, w_up, w_down, g_ple, w_ple_gate, w_ple_proj, g_final):
    bp, tp, d = x_prompt.shape
    bs, ts, _ = x_sample.shape
    n_p, n_s = bp * tp, bs * ts
    n_real = n_p + n_s
    mp = -(-n_real // ROW_ALIGN) * ROW_ALIGN
    pad = mp - n_real

    h = jnp.concatenate([x_prompt.reshape(n_p, d), x_sample.reshape(n_s, d), jnp.zeros((pad, d), f32)], 0)
    ple = jnp.concatenate([p_prompt[0].reshape(n_p, -1), p_sample[0].reshape(n_s, -1),
                           jnp.zeros((pad, p_prompt.shape[-1]), f32)], 0).astype(bf16)

    w = w_in[0]
    o_beta = 4 * GDN_QK_W
    o_nq = o_beta + 2 * GDN_HEADS
    o_gate = o_nq + NSA_Q_W + 6 * NSA_KV_W
    o_ga = o_gate + 3 * NSA_HEADS
    w_main = jnp.concatenate([w[:, :o_beta], w[:, o_nq:o_gate], w[:, o_ga:]], axis=1)
    n_small = 2 * GDN_HEADS + 3 * NSA_HEADS
    w_small = jnp.concatenate([w[:, o_beta:o_nq], w[:, o_gate:o_ga], jnp.zeros((d, LANES - n_small), f32)], axis=1)

    a = rmsnorm_rows(h, g_mix[0], bf16)
    z = proj_matmul(a, w_main, f32, TN_DENSE)
    zs = proj_matmul(a, w_small, f32, LANES)

    def rows(x, lo, hi, which):
        if which == "p":
            return x[:n_p, lo:hi].reshape(bp, tp, hi - lo)
        return x[n_p:n_real, lo:hi].reshape(bs, ts, hi - lo)

    lw = (gdn_conv_w[0], gdn_dt_bias[0], gdn_a_log[0], gdn_norm[0])
    outs = {}
    o_a_parts, o_b_parts = [], []
    for which in ("p", "s"):
        bn, t = (bp, tp) if which == "p" else (bs, ts)
        qa, ka, va, za = (rows(z, Z_Q, Z_K, which), rows(z, Z_K, Z_V, which),
                          rows(z, Z_V, Z_ZG, which), rows(z, Z_ZG, Z_NQ, which))
        b_logit = rows(zs, 0, GDN_HEADS, which)
        a_logit = rows(zs, GDN_HEADS, 2 * GDN_HEADS, which)
        gates = rows(zs, 2 * GDN_HEADS, n_small, which).reshape(bn, t, NSA_HEADS, 3)
        q = rows(z, Z_NQ, Z_KV, which).reshape(bn, t, NSA_KV_GROUPS, NSA_HPG, NSA_DH)
        kv = rows(z, Z_KV, Z_GA, which).reshape(bn, t, 3, 2, NSA_KV_GROUPS, NSA_DH)
        kv_c, kv_s, kv_w = kv[:, :, 0], kv[:, :, 1], kv[:, :, 2]
        if which == "p":
            conv0 = jnp.zeros((bn, GDN_CONV - 1, CONV_CH), f32)
            s0 = jnp.zeros((bn, GDN_HEADS, GDN_DK, GDN_DV), f32)
            o_b, win_new = _nsa_prompt(q, kv_c, kv_s, kv_w, gates, cmp_pe[0], cmp_w[0], rel_bias)
        else:
            conv0, s0 = state_conv[0], state_gdn[0]
            o_b, win_new = _nsa_sample(q, kv_c, kv_s, kv_w, gates, cache_cmp_kv[0], cache_slc_kv[0],
                                       page_table, state_win_kv[0], cmp_pe[0], cmp_w[0], rel_bias)
        o_a, conv_new, s_new = _gdn_mixer(qa, ka, va, za, b_logit, a_logit, conv0, s0, *lw)
        o_a_parts.append(o_a.reshape(bn * t, GDN_V_W))
        o_b_parts.append(o_b.reshape(bn * t, NSA_Q_W))
        outs[which] = (kv_c, kv_s, win_new, s_new, conv_new)

    o_a = jnp.concatenate(o_a_parts + [jnp.zeros((pad, GDN_V_W), f32)], 0).astype(bf16)
    o_b = jnp.concatenate(o_b_parts + [jnp.zeros((pad, NSA_Q_W), f32)], 0).astype(bf16)
    merged = merge_matmul(o_a, o_b, z, w_proj_a[0], w_proj_b[0])
    h = resid_matmul(merged, w_out[0], h)
    h = hier_moe(h, n_real, g_ffn[0], w_router_group[0], b_router_group[0], w_router_expert[0],
                 b_router_expert[0], w_gate[0], w_up[0], w_down[0])
    n3 = rmsnorm_rows(h, g_ple[0], bf16)
    h = ple_matmul(n3, w_ple_gate[0], ple, w_ple_proj[0], h)
    y = rmsnorm_rows(h, g_final, f32)
    y_prompt = y[:n_p].reshape(bp, tp, d)
    y_sample = y[n_p:n_real].reshape(bs, ts, d)
    st_p, st_s = outs["p"], outs["s"]
    return (y_prompt, y_sample) + tuple(t[None] for t in st_p) + tuple(t[None] for t in st_s)
```

```python
import functools
import math

import jax
import jax.numpy as jnp
import numpy as np
from jax import lax
from jax.experimental import pallas as pl
from jax.experimental.pallas import tpu as pltpu

D_MODEL = 4096
GDN_HEADS = 16
GDN_DK = 128
GDN_DV = 128
GDN_CONV = 4
GDN_CHUNK = 64
NSA_HEADS = 16
NSA_KV_GROUPS = 4
NSA_HPG = NSA_HEADS // NSA_KV_GROUPS
NSA_DH = 128
CMP_BLOCK = 32
CMP_STRIDE = 16
SEL_BLOCK = 64
SEL_TOPK = 16
WINDOW = 512
Q_BLOCK = 128
FORCE_SCORE = 1.0e4
REL_BUCKETS = 32
REL_MAX_DIST = 1024
PAGE_SIZE = 128
MOE_GROUPS = 4
MOE_PER_GROUP = 8
MOE_EXPERTS = MOE_GROUPS * MOE_PER_GROUP
MOE_TOPK = 2
EXPERT_HIDDEN = 512
EPS = 1e-6

GDN_QK_W = GDN_HEADS * GDN_DK
GDN_V_W = GDN_HEADS * GDN_DV
CONV_CH = 2 * GDN_QK_W + GDN_V_W
NSA_Q_W = NSA_HEADS * NSA_DH
NSA_KV_W = NSA_KV_GROUPS * NSA_DH

LANES = 128
SUBLANES = 8
VMEM_LIMIT = 56 * 1024 * 1024

ROW_ALIGN = 768
TM_DENSE = 768
TN_DENSE = 512
TM_ROWS = 256

Z_Q, Z_K, Z_V, Z_ZG = 0, 2048, 4096, 6144
Z_NQ = 8192
Z_KV = 10240
Z_GA = 13312
Z_GB = 17408
Z_COLS = 21504

bf16 = jnp.bfloat16
f32 = jnp.float32


def _cparams(sem):
    return pltpu.CompilerParams(dimension_semantics=sem, vmem_limit_bytes=VMEM_LIMIT)


def _rmsnorm_body(x_ref, g_ref, o_ref):
    x = x_ref[...]
    y = x * lax.rsqrt(jnp.mean(x * x, -1, keepdims=True) + EPS)
    o_ref[...] = (y * g_ref[...]).astype(o_ref.dtype)


def rmsnorm_rows(x, g, out_dtype):
    m, d = x.shape
    return pl.pallas_call(
        _rmsnorm_body,
        grid=(m // TM_ROWS,),
        in_specs=[pl.BlockSpec((TM_ROWS, d), lambda i: (i, 0)),
                  pl.BlockSpec((1, d), lambda i: (0, 0))],
        out_specs=pl.BlockSpec((TM_ROWS, d), lambda i: (i, 0)),
        out_shape=jax.ShapeDtypeStruct((m, d), out_dtype),
        compiler_params=_cparams(("parallel",)),
        name="rmsnorm_rows",
    )(x, g.reshape(1, d))


def _cast_body(x_ref, o_ref):
    o_ref[...] = x_ref[...].astype(o_ref.dtype)


def cast_bf16(x2d, tr):
    r, c = x2d.shape
    return pl.pallas_call(
        _cast_body,
        grid=(r // tr,),
        in_specs=[pl.BlockSpec((tr, c), lambda i: (i, 0))],
        out_specs=pl.BlockSpec((tr, c), lambda i: (i, 0)),
        out_shape=jax.ShapeDtypeStruct((r, c), bf16),
        compiler_params=_cparams(("parallel",)),
        name="cast_bf16",
    )(x2d)


def _proj_body(a_ref, w_ref, o_ref, wb_ref):
    @pl.when(pl.program_id(1) == 0)
    def _():
        wb_ref[...] = w_ref[...].astype(bf16)
    o_ref[...] = jnp.dot(a_ref[...], wb_ref[...], preferred_element_type=f32).astype(o_ref.dtype)


def proj_matmul(a, w, out_dtype, tn):
    m, k = a.shape
    n = w.shape[1]
    return pl.pallas_call(
        _proj_body,
        grid=(n // tn, m // TM_DENSE),
        in_specs=[pl.BlockSpec((TM_DENSE, k), lambda j, i: (i, 0)),
                  pl.BlockSpec((k, tn), lambda j, i: (0, j))],
        out_specs=pl.BlockSpec((TM_DENSE, tn), lambda j, i: (i, j)),
        out_shape=jax.ShapeDtypeStruct((m, n), out_dtype),
        scratch_shapes=[pltpu.VMEM((k, tn), bf16)],
        compiler_params=_cparams(("arbitrary", "arbitrary")),
        name="proj_matmul",
    )(a, w)


def _merge_body(oa_ref, ob_ref, ga_ref, gb_ref, wa_ref, wb_ref, o_ref, wa_s, wb_s):
    @pl.when(pl.program_id(1) == 0)
    def _():
        wa_s[...] = wa_ref[...].astype(bf16)
        wb_s[...] = wb_ref[...].astype(bf16)
    pa = jnp.dot(oa_ref[...], wa_s[...], preferred_element_type=f32)
    pb = jnp.dot(ob_ref[...], wb_s[...], preferred_element_type=f32)
    o_ref[...] = (jax.nn.sigmoid(ga_ref[...]) * pa + jax.nn.sigmoid(gb_ref[...]) * pb).astype(o_ref.dtype)


def merge_matmul(o_a, o_b, z, w_a, w_b):
    m, ka = o_a.shape
    kb = o_b.shape[1]
    n = w_a.shape[1]
    tn = TN_DENSE
    ja, jb = Z_GA // tn, Z_GB // tn
    return pl.pallas_call(
        _merge_body,
        grid=(n // tn, m // TM_DENSE),
        in_specs=[pl.BlockSpec((TM_DENSE, ka), lambda j, i: (i, 0)),
                  pl.BlockSpec((TM_DENSE, kb), lambda j, i: (i, 0)),
                  pl.BlockSpec((TM_DENSE, tn), lambda j, i: (i, ja + j)),
                  pl.BlockSpec((TM_DENSE, tn), lambda j, i: (i, jb + j)),
                  pl.BlockSpec((ka, tn), lambda j, i: (0, j)),
                  pl.BlockSpec((kb, tn), lambda j, i: (0, j))],
        out_specs=pl.BlockSpec((TM_DENSE, tn), lambda j, i: (i, j)),
        out_shape=jax.ShapeDtypeStruct((m, n), bf16),
        scratch_shapes=[pltpu.VMEM((ka, tn), bf16), pltpu.VMEM((kb, tn), bf16)],
        compiler_params=_cparams(("arbitrary", "arbitrary")),
        name="merge_matmul",
    )(o_a, o_b, z, z, w_a, w_b)


def _resid_body(a_ref, w_ref, h_ref, o_ref, wb_ref):
    @pl.when(pl.program_id(1) == 0)
    def _():
        wb_ref[...] = w_ref[...].astype(bf16)
    o_ref[...] = h_ref[...] + jnp.dot(a_ref[...], wb_ref[...], preferred_element_type=f32)


def resid_matmul(a, w, h):
    m, k = a.shape
    n = w.shape[1]
    tn = TN_DENSE
    return pl.pallas_call(
        _resid_body,
        grid=(n // tn, m // TM_DENSE),
        in_specs=[pl.BlockSpec((TM_DENSE, k), lambda j, i: (i, 0)),
                  pl.BlockSpec((k, tn), lambda j, i: (0, j)),
                  pl.BlockSpec((TM_DENSE, tn), lambda j, i: (i, j))],
        out_specs=pl.BlockSpec((TM_DENSE, tn), lambda j, i: (i, j)),
        out_shape=jax.ShapeDtypeStruct((m, n), f32),
        scratch_shapes=[pltpu.VMEM((k, tn), bf16)],
        compiler_params=_cparams(("arbitrary", "arbitrary")),
        name="resid_matmul",
    )(a, w, h)


def _ple_body(a_ref, w_ref, p_ref, wp_ref, h_ref, o_ref, wb_ref, wpb_ref):
    @pl.when(pl.program_id(1) == 0)
    def _():
        wb_ref[...] = w_ref[...].astype(bf16)
        wpb_ref[...] = wp_ref[...].astype(bf16)
    gate = jax.nn.sigmoid(jnp.dot(a_ref[...], wb_ref[...], preferred_element_type=f32))
    emb = jnp.dot(p_ref[...], wpb_ref[...], preferred_element_type=f32)
    o_ref[...] = h_ref[...] + gate * emb


def ple_matmul(a, w_gate, p, w_proj, h):
    m, k = a.shape
    kp = p.shape[1]
    n = w_gate.shape[1]
    tn = TN_DENSE
    return pl.pallas_call(
        _ple_body,
        grid=(n // tn, m // TM_DENSE),
        in_specs=[pl.BlockSpec((TM_DENSE, k), lambda j, i: (i, 0)),
                  pl.BlockSpec((k, tn), lambda j, i: (0, j)),
                  pl.BlockSpec((TM_DENSE, kp), lambda j, i: (i, 0)),
                  pl.BlockSpec((kp, tn), lambda j, i: (0, j)),
                  pl.BlockSpec((TM_DENSE, tn), lambda j, i: (i, j))],
        out_specs=pl.BlockSpec((TM_DENSE, tn), lambda j, i: (i, j)),
        out_shape=jax.ShapeDtypeStruct((m, n), f32),
        scratch_shapes=[pltpu.VMEM((k, tn), bf16), pltpu.VMEM((kp, tn), bf16)],
        compiler_params=_cparams(("arbitrary", "arbitrary")),
        name="ple_matmul",
    )(a, w_gate, p, w_proj, h)


def _router_body(h_ref, g_ref, wr_ref, br_ref, m_ref, r_ref):
    x = h_ref[...]
    y = x * lax.rsqrt(jnp.mean(x * x, -1, keepdims=True) + EPS) * g_ref[...]
    m_ref[...] = y.astype(bf16)
    logits = jnp.dot(y, wr_ref[...], preferred_element_type=f32,
                     precision=lax.Precision.HIGHEST) + br_ref[...]
    lane = lax.broadcasted_iota(jnp.int32, logits.shape, 1)
    neg = -jnp.inf
    lg = jnp.where(lane < MOE_GROUPS, logits, neg)
    eg = jnp.exp(lg - jnp.max(lg, -1, keepdims=True))
    pg = eg / jnp.sum(eg, -1, keepdims=True)
    pg_top = jnp.max(pg, -1, keepdims=True)
    g_idx = jnp.min(jnp.where(pg == pg_top, lane, LANES), -1, keepdims=True)
    lo = MOE_GROUPS + MOE_PER_GROUP * g_idx
    emask = (lane >= lo) & (lane < lo + MOE_PER_GROUP)
    le = jnp.where(emask, logits, neg)
    ee = jnp.exp(le - jnp.max(le, -1, keepdims=True))
    pe = jnp.where(emask, ee / jnp.sum(ee, -1, keepdims=True), -1.0)
    v1 = jnp.max(pe, -1, keepdims=True)
    i1 = jnp.min(jnp.where(pe == v1, lane, LANES), -1, keepdims=True)
    pe2 = jnp.where(lane == i1, -1.0, pe)
    v2 = jnp.max(pe2, -1, keepdims=True)
    i2 = jnp.min(jnp.where(pe2 == v2, lane, LANES), -1, keepdims=True)
    den = v1 + v2
    w1 = pg_top * v1 / den
    w2 = pg_top * v2 / den
    e1 = (i1 - MOE_GROUPS).astype(f32)
    e2 = (i2 - MOE_GROUPS).astype(f32)
    r_ref[...] = jnp.where(lane == 0, e1, jnp.where(lane == 1, e2,
                           jnp.where(lane == 2, w1, jnp.where(lane == 3, w2, 0.0))))


def moe_router(h, g_ffn, w_router, b_router):
    m, d = h.shape
    return pl.pallas_call(
        _router_body,
        grid=(m // TM_ROWS,),
        in_specs=[pl.BlockSpec((TM_ROWS, d), lambda i: (i, 0)),
                  pl.BlockSpec((1, d), lambda i: (0, 0)),
                  pl.BlockSpec((d, LANES), lambda i: (0, 0)),
                  pl.BlockSpec((1, LANES), lambda i: (0, 0))],
        out_specs=[pl.BlockSpec((TM_ROWS, d), lambda i: (i, 0)),
                   pl.BlockSpec((TM_ROWS, LANES), lambda i: (i, 0))],
        out_shape=[jax.ShapeDtypeStruct((m, d), bf16),
                   jax.ShapeDtypeStruct((m, LANES), f32)],
        compiler_params=_cparams(("parallel",)),
        name="moe_router",
    )(h, g_ffn.reshape(1, d), w_router, b_router)


def _expert_body(te_ref, tv_ref, x_ref, rw_ref, wg_ref, wu_ref, wd_ref, o_ref):
    t = pl.program_id(0)

    @pl.when(tv_ref[t] > 0)
    def _():
        x = x_ref[...]
        gate = jnp.dot(x, wg_ref[...], preferred_element_type=f32)
        up = jnp.dot(x, wu_ref[...], preferred_element_type=f32)
        hid = (jax.nn.silu(gate) * up * rw_ref[...]).astype(bf16)
        o_ref[...] = jnp.dot(hid, wd_ref[...], preferred_element_type=f32)

    @pl.when(tv_ref[t] == 0)
    def _():
        o_ref[...] = jnp.zeros_like(o_ref)


def expert_matmul(tile_expert, tile_valid, xs, row_w, wg, wu, wd):
    r, d = xs.shape
    f = wg.shape[2]
    n_tiles = r // TM_ROWS
    grid_spec = pltpu.PrefetchScalarGridSpec(
        num_scalar_prefetch=2,
        grid=(n_tiles,),
        in_specs=[pl.BlockSpec((TM_ROWS, d), lambda t, te, tv: (t, 0)),
                  pl.BlockSpec((TM_ROWS, 1), lambda t, te, tv: (t, 0)),
                  pl.BlockSpec((None, d, f), lambda t, te, tv: (te[t], 0, 0)),
                  pl.BlockSpec((None, d, f), lambda t, te, tv: (te[t], 0, 0)),
                  pl.BlockSpec((None, f, d), lambda t, te, tv: (te[t], 0, 0))],
        out_specs=pl.BlockSpec((TM_ROWS, d), lambda t, te, tv: (t, 0)),
    )
    return pl.pallas_call(
        _expert_body,
        grid_spec=grid_spec,
        out_shape=jax.ShapeDtypeStruct((r, d), f32),
        compiler_params=_cparams(("arbitrary",)),
        name="expert_matmul",
    )(tile_expert, tile_valid, xs, row_w, wg, wu, wd)


def hier_moe(h, n_real, g_ffn, w_rg, b_rg, w_re, b_re, w_gate, w_up, w_down):
    mp, d = h.shape
    n_route = MOE_GROUPS + MOE_EXPERTS
    w_router = jnp.zeros((d, LANES), f32).at[:, :MOE_GROUPS].set(w_rg).at[:, MOE_GROUPS:n_route].set(w_re)
    b_router = jnp.zeros((1, LANES), f32).at[0, :MOE_GROUPS].set(b_rg).at[0, MOE_GROUPS:n_route].set(b_re)
    m_bf, slab = moe_router(h, g_ffn, w_router, b_router)
    ids = slab[:n_real, 0:2].astype(jnp.int32)
    wts = slab[:n_real, 2:4]

    tm = TM_ROWS
    n_assign = n_real * MOE_TOPK
    n_slots = -(-(n_assign + MOE_EXPERTS * (tm - 1)) // tm) * tm
    e_flat = ids.reshape(-1)
    order = jnp.argsort(e_flat, stable=True)
    e_sorted = e_flat[order]
    counts = jnp.sum(jax.nn.one_hot(e_flat, MOE_EXPERTS, dtype=jnp.int32), axis=0)
    padded = -(-counts // tm) * tm
    start_p = jnp.cumsum(padded) - padded
    start = jnp.cumsum(counts) - counts
    slot_sorted = start_p[e_sorted] + (jnp.arange(n_assign, dtype=jnp.int32) - start[e_sorted])
    slot_of = jnp.zeros((n_assign,), jnp.int32).at[order].set(slot_sorted.astype(jnp.int32))
    src_tok = jnp.zeros((n_slots,), jnp.int32).at[slot_of].set(jnp.arange(n_assign, dtype=jnp.int32) // MOE_TOPK)
    row_w = jnp.zeros((n_slots,), f32).at[slot_of].set(wts.reshape(-1))
    tile_start = jnp.arange(n_slots // tm, dtype=jnp.int32) * tm
    ends = jnp.cumsum(padded)
    tile_expert = jnp.minimum(jnp.searchsorted(ends, tile_start, side="right"), MOE_EXPERTS - 1).astype(jnp.int32)
    tile_valid = (tile_start < ends[-1]).astype(jnp.int32)

    xs = jnp.take(m_bf, src_tok, axis=0)
    wg = cast_bf16(w_gate.reshape(MOE_EXPERTS * d, EXPERT_HIDDEN), 4096).reshape(MOE_EXPERTS, d, EXPERT_HIDDEN)
    wu = cast_bf16(w_up.reshape(MOE_EXPERTS * d, EXPERT_HIDDEN), 4096).reshape(MOE_EXPERTS, d, EXPERT_HIDDEN)
    wd = cast_bf16(w_down.reshape(MOE_EXPERTS * EXPERT_HIDDEN, d), 512).reshape(MOE_EXPERTS, EXPERT_HIDDEN, d)
    ys = expert_matmul(tile_expert, tile_valid, xs, row_w.reshape(n_slots, 1), wg, wu, wd)
    slot2 = slot_of.reshape(n_real, MOE_TOPK)
    y = jnp.take(ys, slot2[:, 0], axis=0) + jnp.take(ys, slot2[:, 1], axis=0)
    return h.at[:n_real].add(y)


NEG_BIG = -1e30
N_BIAS_TILES = 11
_NT = (((1,), (1,)), ((), ()))


def _compress_body(x_ref, w_ref, pe_ref, o_ref):
    ns = o_ref.shape[0]
    acc0 = jnp.zeros((ns, NSA_DH), f32)
    acc1 = jnp.zeros((ns, NSA_DH), f32)
    for s in range(CMP_STRIDE):
        xs = x_ref[pl.ds(s, ns, stride=CMP_STRIDE), :]
        a0 = (xs + pe_ref[s:s + 1, :]).astype(bf16)
        a1 = (xs + pe_ref[CMP_STRIDE + s:CMP_STRIDE + s + 1, :]).astype(bf16)
        acc0 = acc0 + jnp.dot(a0, w_ref[s].astype(bf16), preferred_element_type=f32)
        acc1 = acc1 + jnp.dot(a1, w_ref[CMP_STRIDE + s].astype(bf16), preferred_element_type=f32)
    o_ref[...] = (acc0 + pltpu.roll(acc1, ns - 1, axis=0)).astype(o_ref.dtype)


def compress_prompt(z, cmp_w, cmp_pe, bn, t):
    ns = t // CMP_STRIDE
    col0 = Z_KV // NSA_DH
    return pl.pallas_call(
        _compress_body,
        grid=(bn, 2, NSA_KV_GROUPS),
        in_specs=[pl.BlockSpec((t, NSA_DH), lambda b, x, g: (b, col0 + x * NSA_KV_GROUPS + g)),
                  pl.BlockSpec((None, CMP_BLOCK, NSA_DH, NSA_DH), lambda b, x, g: (x, 0, 0, 0)),
                  pl.BlockSpec((None, CMP_BLOCK, NSA_DH), lambda b, x, g: (x, 0, 0))],
        out_specs=pl.BlockSpec((None, None, None, ns, NSA_DH), lambda b, x, g: (b, x, g, 0, 0)),
        out_shape=jax.ShapeDtypeStruct((bn, 2, NSA_KV_GROUPS, ns, NSA_DH), bf16),
        compiler_params=_cparams(("parallel", "parallel", "parallel")),
        name="compress_prompt",
    )(z, cmp_w, cmp_pe)


def _online_softmax_step(q, k, v, extra, m_ref, l_ref, acc_ref):
    s = lax.dot_general(q, k, _NT, preferred_element_type=f32) * (NSA_DH ** -0.5) + extra
    m_old = m_ref[...]
    m_new = jnp.maximum(m_old, jnp.max(s, -1, keepdims=True))
    alpha = jnp.exp(m_old - m_new)
    p = jnp.exp(s - m_new)
    l_ref[...] = alpha * l_ref[...] + jnp.sum(p, -1, keepdims=True)
    acc_ref[...] = alpha * acc_ref[...] + jnp.dot(p.astype(bf16), v, preferred_element_type=f32)
    m_ref[...] = m_new


def _nsa_prompt_body(q_ref, kc_ref, vc_ref, ks_ref, vs_ref, kw_ref, vw_ref, gate_ref, bc_ref, bt_ref,
                     cov_ref, exp_ref, o_ref, mb_ref, m_ref, l_ref, acc_ref):
    g = pl.program_id(1)
    i = pl.program_id(2)
    qb = Q_BLOCK
    rows = NSA_HPG * qb
    qf = q_ref[...]
    q = jnp.concatenate([qf[:, h * NSA_DH:(h + 1) * NSA_DH] for h in range(NSA_HPG)], axis=0).astype(bf16)

    s = lax.dot_general(q, kc_ref[...], _NT, preferred_element_type=f32) * (NSA_DH ** -0.5) + bc_ref[...]
    m = jnp.max(s, -1, keepdims=True)
    m = jnp.where(m < 0.5 * NEG_BIG, 0.0, m)
    e = jnp.exp(s - m)
    p = e / jnp.maximum(jnp.sum(e, -1, keepdims=True), 1e-30)
    o_c = jnp.dot(p.astype(bf16), vc_ref[...], preferred_element_type=f32)

    psum = p[0:qb]
    for h in range(1, NSA_HPG):
        psum = psum + p[h * qb:(h + 1) * qb]
    imp_t = lax.dot_general(cov_ref[...], psum, _NT, preferred_element_type=f32,
                            precision=lax.Precision.HIGHEST)
    nsb = imp_t.shape[0]
    blk = lax.broadcasted_iota(jnp.int32, (nsb, qb), 0)
    qpos = i * qb + lax.broadcasted_iota(jnp.int32, (nsb, qb), 1)
    cur = qpos // SEL_BLOCK
    forced = (blk == 0) | (blk == cur) | (blk == cur - 1)
    valid = blk * SEL_BLOCK <= qpos
    score = jnp.where(valid, jnp.where(forced, FORCE_SCORE, imp_t), -1.0)
    sel_t = jnp.zeros((nsb, qb), f32)
    for _ in range(min(SEL_TOPK, nsb)):
        mx = jnp.max(score, axis=0, keepdims=True)
        first = jnp.min(jnp.where(score == mx, blk, nsb), axis=0, keepdims=True)
        pick = blk == first
        sel_t = jnp.where(pick, 1.0, sel_t)
        score = jnp.where(pick, -2.0, score)
    unsel = ((sel_t - 1.0) * (-NEG_BIG)).T.astype(bf16)
    mb_ref[...] = jnp.dot(unsel, exp_ref[...], preferred_element_type=f32)

    def reset():
        m_ref[...] = jnp.full(m_ref.shape, NEG_BIG, f32)
        l_ref[...] = jnp.zeros(l_ref.shape, f32)
        acc_ref[...] = jnp.zeros(acc_ref.shape, f32)

    reset()

    def sel_step(j, carry):
        r0 = pl.multiple_of(j * qb, qb)
        k = ks_ref[pl.ds(r0, qb), :].astype(bf16)
        v = vs_ref[pl.ds(r0, qb), :].astype(bf16)
        mb = mb_ref[:, pl.ds(r0, qb)]
        extra = bt_ref[jnp.minimum(i - j, N_BIAS_TILES - 2)] + jnp.concatenate([mb] * NSA_HPG, axis=0)
        _online_softmax_step(q, k, v, extra, m_ref, l_ref, acc_ref)
        return carry

    lax.fori_loop(0, i + 1, sel_step, 0)
    o_s = acc_ref[...] / l_ref[...]

    reset()
    n_win = WINDOW // qb

    def win_step(j, carry):
        r0 = pl.multiple_of(j * qb, qb)
        k = kw_ref[pl.ds(r0, qb), :].astype(bf16)
        v = vw_ref[pl.ds(r0, qb), :].astype(bf16)
        d = i - j
        extra = bt_ref[jnp.where(d == n_win, N_BIAS_TILES - 1, d)]
        _online_softmax_step(q, k, v, extra, m_ref, l_ref, acc_ref)
        return carry

    lax.fori_loop(jnp.maximum(i - n_win, 0), i + 1, win_step, 0)
    o_w = acc_ref[...] / l_ref[...]

    gt = jax.nn.sigmoid(gate_ref[...])
    outs = []
    for h in range(NSA_HPG):
        c = 3 * h
        sl = slice(h * qb, (h + 1) * qb)
        outs.append(gt[:, c:c + 1] * o_c[sl] + gt[:, c + 1:c + 2] * o_s[sl] + gt[:, c + 2:c + 3] * o_w[sl])
    o_ref[...] = jnp.concatenate(outs, axis=1).astype(o_ref.dtype)


def _nsa_bias_tables(rel_bias, t):
    qb = Q_BLOCK
    nq = t // qb
    tbl = rel_bias.astype(f32)
    r = jnp.arange(qb, dtype=jnp.int32)[:, None]
    c = jnp.arange(qb, dtype=jnp.int32)[None, :]
    tiles = []
    for delta in range(N_BIAS_TILES - 1):
        dist = delta * qb + r - c
        tiles.append(jnp.where((dist >= 0)[..., None], tbl[_rel_bucket(dist)], NEG_BIG))
    dist = (WINDOW // qb) * qb + r - c
    tiles.append(jnp.where(((dist >= 0) & (dist < WINDOW))[..., None], tbl[_rel_bucket(dist)], NEG_BIG))
    bt = jnp.stack(tiles, 0)
    bt = bt.reshape(N_BIAS_TILES, qb, qb, NSA_KV_GROUPS, NSA_HPG)
    bt = jnp.transpose(bt, (3, 0, 4, 1, 2)).reshape(NSA_KV_GROUPS, N_BIAS_TILES, NSA_HPG * qb, qb)
    ncp = t // CMP_STRIDE
    c_end = jnp.arange(ncp, dtype=jnp.int32) * CMP_STRIDE + (CMP_BLOCK - 1)
    q_pos = jnp.arange(t, dtype=jnp.int32)
    dist_c = q_pos[:, None] - c_end[None, :]
    bc = jnp.where((dist_c >= 0)[..., None], tbl[_rel_bucket(dist_c)], NEG_BIG)
    bc = bc.reshape(nq, qb, ncp, NSA_KV_GROUPS, NSA_HPG)
    bc = jnp.transpose(bc, (3, 0, 4, 1, 2)).reshape(NSA_KV_GROUPS, nq, NSA_HPG * qb, ncp)
    return bt, bc


def nsa_prompt_attention(z, gates_g, kvc, rel_bias, bn, t):
    qb = Q_BLOCK
    nq = t // qb
    ncp = t // CMP_STRIDE
    nsb = t // SEL_BLOCK
    rows = NSA_HPG * qb
    bt, bc = _nsa_bias_tables(rel_bias, t)
    cover_t = jnp.concatenate([_cover_matrix(ncp - 1, nsb), jnp.zeros((1, nsb), f32)], 0).T
    expand = jnp.asarray(np.repeat(np.eye(nsb, dtype=np.float32), SEL_BLOCK, axis=1), bf16)
    kcol = Z_KV // NSA_DH
    kv_spec = lambda off: pl.BlockSpec((t, NSA_DH), lambda b, g, i, off=off: (b, kcol + off + g))
    return pl.pallas_call(
        _nsa_prompt_body,
        grid=(bn, NSA_KV_GROUPS, nq),
        in_specs=[pl.BlockSpec((qb, NSA_HPG * NSA_DH), lambda b, g, i: (b * nq + i, Z_NQ // (NSA_HPG * NSA_DH) + g)),
                  pl.BlockSpec((None, None, None, ncp, NSA_DH), lambda b, g, i: (b, 0, g, 0, 0)),
                  pl.BlockSpec((None, None, None, ncp, NSA_DH), lambda b, g, i: (b, 1, g, 0, 0)),
                  kv_spec(2 * NSA_KV_GROUPS), kv_spec(3 * NSA_KV_GROUPS),
                  kv_spec(4 * NSA_KV_GROUPS), kv_spec(5 * NSA_KV_GROUPS),
                  pl.BlockSpec((None, qb, 3 * NSA_HPG), lambda b, g, i: (g, b * nq + i, 0)),
                  pl.BlockSpec((None, None, rows, ncp), lambda b, g, i: (g, i, 0, 0)),
                  pl.BlockSpec((None, N_BIAS_TILES, rows, qb), lambda b, g, i: (g, 0, 0, 0)),
                  pl.BlockSpec((nsb, ncp), lambda b, g, i: (0, 0)),
                  pl.BlockSpec((nsb, t), lambda b, g, i: (0, 0))],
        out_specs=pl.BlockSpec((qb, NSA_HPG * NSA_DH), lambda b, g, i: (b * nq + i, g)),
        out_shape=jax.ShapeDtypeStruct((bn * t, NSA_Q_W), bf16),
        scratch_shapes=[pltpu.VMEM((qb, t), f32), pltpu.VMEM((rows, 1), f32), pltpu.VMEM((rows, 1), f32),
                        pltpu.VMEM((rows, NSA_DH), f32)],
        compiler_params=_cparams(("parallel", "parallel", "arbitrary")),
        name="nsa_prompt_attention",
    )(z, kvc, kvc, z, z, z, z, gates_g, bc, bt, cover_t, expand)


def _rmsnorm_j(x, g):
    xf = x.astype(f32)
    y = xf * lax.rsqrt(jnp.mean(xf * xf, -1, keepdims=True) + EPS)
    return (y * g.astype(f32)).astype(x.dtype)


def _l2norm(x):
    return x * lax.rsqrt(jnp.sum(x * x, -1, keepdims=True) + EPS)


def _masked_softmax(s, mask):
    s = jnp.where(mask, s.astype(f32), -jnp.inf)
    m = jnp.max(s, -1, keepdims=True)
    m = jnp.where(jnp.isfinite(m), m, 0.0)
    e = jnp.exp(s - m)
    return e / jnp.maximum(jnp.sum(e, -1, keepdims=True), 1e-30)


def _rel_bucket(dist):
    n = jnp.maximum(dist, 0)
    max_exact = REL_BUCKETS // 2
    nf = jnp.maximum(n, 1).astype(f32)
    large = max_exact + (jnp.log(nf / max_exact) / math.log(REL_MAX_DIST / max_exact)
                         * (REL_BUCKETS - max_exact)).astype(jnp.int32)
    large = jnp.minimum(large, REL_BUCKETS - 1)
    return jnp.where(n < max_exact, n, large)


def _causal_conv(x, buf, w):
    T = x.shape[1]
    xp = jnp.concatenate([buf.astype(x.dtype), x], axis=1)
    y = xp[:, 0:T] * w[0]
    for j in range(1, GDN_CONV):
        y = y + xp[:, j:j + T] * w[j]
    return jax.nn.silu(y), xp[:, T:]


def _gated_delta_rule(q, k, v, g, beta, s0):
    Bn, T, H, dk = q.shape
    dv = v.shape[-1]
    cs = min(GDN_CHUNK, T)
    nc = -(-T // cs)
    pad = nc * cs - T

    def prep(a):
        a = jnp.pad(a, [(0, 0), (0, pad)] + [(0, 0)] * (a.ndim - 2))
        a = a.reshape((Bn, nc, cs) + a.shape[2:])
        return jnp.moveaxis(a, 3, 1)

    q, k, v, g, beta = prep(q), prep(k), prep(v), prep(g), prep(beta)
    gc = jnp.cumsum(g, axis=-1)
    ar = jnp.arange(cs)
    incl = ar[:, None] >= ar[None, :]
    strict = ar[:, None] > ar[None, :]
    decay = jnp.exp(jnp.where(incl, gc[..., :, None] - gc[..., None, :], -jnp.inf))
    kb = k * beta[..., None]
    lmat = jnp.einsum("bhnid,bhnjd->bhnij", kb, k) * jnp.where(strict, decay, 0.0)
    a_mat = jnp.eye(cs, dtype=f32) + lmat
    rhs = jnp.concatenate([kb * jnp.exp(gc)[..., None], v * beta[..., None]], -1)
    wu = lax.linalg.triangular_solve(a_mat, rhs, left_side=True, lower=True, unit_diagonal=True)
    w_c, u_c = wu[..., :dk], wu[..., dk:]
    aqk = jnp.einsum("bhnid,bhnjd->bhnij", q, k) * decay
    qg = q * jnp.exp(gc)[..., None]
    kg = k * jnp.exp(gc[..., -1:] - gc)[..., None]
    glast = jnp.exp(gc[..., -1])
    xs = tuple(jnp.moveaxis(t, 2, 0) for t in (qg, kg, w_c, u_c, aqk, glast))

    def step(s, inp):
        qg_i, kg_i, w_i, u_i, aqk_i, gl_i = inp
        v_new = u_i - jnp.einsum("bhid,bhde->bhie", w_i, s)
        o = jnp.einsum("bhid,bhde->bhie", qg_i, s) + jnp.einsum("bhij,bhje->bhie", aqk_i, v_new)
        s = gl_i[..., None, None] * s + jnp.einsum("bhid,bhie->bhde", kg_i, v_new)
        return s, o

    s_fin, o = lax.scan(step, s0, xs)
    o = jnp.moveaxis(o, 0, 2).reshape(Bn, H, nc * cs, dv)[:, :, :T]
    return jnp.moveaxis(o, 1, 2), s_fin


def _gdn_mixer(qa, ka, va, za, b_logit, a_logit, conv_buf, s0, conv_w, dt_bias, a_log, head_gain):
    Bn, T, _ = qa.shape
    xc, new_buf = _causal_conv(jnp.concatenate([qa, ka, va], -1), conv_buf, conv_w)
    q, k, v = jnp.split(xc.astype(f32), [GDN_QK_W, 2 * GDN_QK_W], -1)
    q = _l2norm(q.reshape(Bn, T, GDN_HEADS, GDN_DK)) * (GDN_DK ** -0.5)
    k = _l2norm(k.reshape(Bn, T, GDN_HEADS, GDN_DK))
    v = v.reshape(Bn, T, GDN_HEADS, GDN_DV)
    g = -jnp.exp(a_log.astype(f32)) * jax.nn.softplus(a_logit.astype(f32) + dt_bias.astype(f32))
    beta = jax.nn.sigmoid(b_logit.astype(f32))
    o, s_new = _gated_delta_rule(q, k, v, g, beta, s0.astype(f32))
    o = _rmsnorm_j(o, head_gain) * jax.nn.silu(za.astype(f32).reshape(Bn, T, GDN_HEADS, GDN_DV))
    return o.reshape(Bn, T, GDN_V_W).astype(qa.dtype), new_buf, s_new


def _subblock_proj(rows, pe, w):
    r = CMP_BLOCK // CMP_STRIDE
    Bn, L = rows.shape[:2]
    ns = L // CMP_STRIDE
    sub = rows[:, :ns * CMP_STRIDE].reshape(Bn, ns, CMP_STRIDE, 2, NSA_KV_GROUPS, NSA_DH)
    w_r = w.reshape(2, r, CMP_STRIDE, NSA_DH, NSA_DH)
    pe_r = pe.reshape(2, r, CMP_STRIDE, NSA_DH)
    proj = jnp.einsum("bnsxgd,xmsde->mbnxge", sub, w_r)
    pe_term = jnp.einsum("xmsd,xmsde->xe", pe_r, w_r)
    return proj, pe_term


def _compress(proj, pe_term):
    r, Bn, ns = proj.shape[:3]
    nc = ns - r + 1
    kvc = proj[0][:, 0:nc]
    for m in range(1, r):
        kvc = kvc + proj[m][:, m:m + nc]
    kvc = kvc + pe_term[None, None, :, None, :]
    c_end = jnp.arange(nc, dtype=jnp.int32) * CMP_STRIDE + (CMP_BLOCK - 1)
    return kvc, c_end


def _cover_matrix(nc, ns):
    cs = np.arange(nc) * CMP_STRIDE
    ss = np.arange(ns) * SEL_BLOCK
    inter = np.minimum(cs[:, None] + CMP_BLOCK, ss[None, :] + SEL_BLOCK) - np.maximum(cs[:, None], ss[None, :])
    return jnp.asarray(np.clip(inter, 0, None) / CMP_BLOCK, dtype=f32)


def _nsa_core(q, q_pos, kvc, c_end, cover, gather_sel, kvw, w_pos, gates, rel_bias):
    Bn, Tq = q.shape[:2]
    scale = NSA_DH ** -0.5
    tbl = rel_bias.astype(f32).reshape(REL_BUCKETS, NSA_KV_GROUPS, NSA_HPG)
    dist_c = q_pos[:, None] - c_end[None, :]
    s_c = jnp.einsum("btghd,bngd->bghtn", q, kvc[:, :, 0]).astype(f32) * scale
    s_c = s_c + jnp.transpose(tbl[_rel_bucket(dist_c)], (2, 3, 0, 1))
    p_c = _masked_softmax(s_c, dist_c >= 0)
    o_c = jnp.einsum("bghtn,bngd->btghd", p_c, kvc[:, :, 1].astype(f32))
    ns = cover.shape[1]
    imp = jnp.einsum("bghtn,nm->bgtm", p_c, cover)
    blk = jnp.arange(ns, dtype=jnp.int32)[None, :]
    cur = (q_pos // SEL_BLOCK)[:, None]
    forced = (blk == 0) | (blk == cur) | (blk == cur - 1)
    valid = blk * SEL_BLOCK <= q_pos[:, None]
    score = jnp.where(valid, jnp.where(forced, FORCE_SCORE, imp), -1.0)
    _, idx = lax.top_k(score, min(SEL_TOPK, ns))
    n_sel = idx.shape[-1]
    sel = gather_sel(idx)
    pos = idx[..., None] * SEL_BLOCK + jnp.arange(SEL_BLOCK, dtype=jnp.int32)
    dist_s = q_pos[None, None, :, None, None] - pos
    gi = jnp.arange(NSA_KV_GROUPS)[None, :, None, None, None]
    bias_s = jnp.transpose(tbl, (1, 0, 2))[gi, _rel_bucket(dist_s)]
    s_s = jnp.einsum("btghd,bgtnkd->bghtnk", q, sel[..., 0, :]).astype(f32) * scale + jnp.moveaxis(bias_s, -1, 2)
    flat = (Bn, NSA_KV_GROUPS, NSA_HPG, Tq, n_sel * SEL_BLOCK)
    mask_s = (dist_s >= 0).reshape(Bn, NSA_KV_GROUPS, 1, Tq, n_sel * SEL_BLOCK)
    p_s = _masked_softmax(s_s.reshape(flat), mask_s).reshape(s_s.shape)
    o_s = jnp.einsum("bghtnk,bgtnkd->btghd", p_s, sel[..., 1, :].astype(f32))
    dist_w = q_pos[:, None] - w_pos[None, :]
    s_w = jnp.einsum("btghd,bsgd->bghts", q, kvw[:, :, 0]).astype(f32) * scale
    s_w = s_w + jnp.transpose(tbl[_rel_bucket(dist_w)], (2, 3, 0, 1))
    p_w = _masked_softmax(s_w, (dist_w >= 0) & (dist_w < WINDOW) & (w_pos[None, :] >= 0))
    o_w = jnp.einsum("bghts,bsgd->btghd", p_w, kvw[:, :, 1].astype(f32))
    gt = jax.nn.sigmoid(gates.astype(f32)).reshape(Bn, Tq, NSA_KV_GROUPS, NSA_HPG, 3)
    o = gt[..., 0:1] * o_c + gt[..., 1:2] * o_s + gt[..., 2:3] * o_w
    return o.reshape(Bn, Tq, NSA_Q_W).astype(q.dtype)


def _nsa_prompt(q, kv_c, kv_s, kv_w, gates, cmp_pe, cmp_w, rel_bias):
    Bn, T = q.shape[:2]
    proj, pe_term = _subblock_proj(kv_c, cmp_pe, cmp_w)
    kvc, c_end = _compress(proj, pe_term)
    ns = -(-T // SEL_BLOCK)
    kvs_blk = jnp.pad(kv_s, [(0, 0), (0, ns * SEL_BLOCK - T), (0, 0), (0, 0), (0, 0)])
    kvs_blk = kvs_blk.reshape(Bn, ns, SEL_BLOCK, 2, NSA_KV_GROUPS, NSA_DH)
    bi = jnp.arange(Bn)[:, None, None, None]
    gi = jnp.arange(NSA_KV_GROUPS)[None, :, None, None]

    def gather_sel(idx):
        return kvs_blk[bi, idx, :, :, gi, :]

    cover = _cover_matrix(kvc.shape[1], ns)
    kvw_pad = jnp.pad(kv_w, [(0, 0), (WINDOW, 0), (0, 0), (0, 0), (0, 0)])

    def one_block(i):
        start = i * Q_BLOCK
        qb = lax.dynamic_slice_in_dim(q, start, Q_BLOCK, 1)
        gb = lax.dynamic_slice_in_dim(gates, start, Q_BLOCK, 1)
        kwb = lax.dynamic_slice_in_dim(kvw_pad, start, WINDOW + Q_BLOCK, 1)
        q_pos = start + jnp.arange(Q_BLOCK, dtype=jnp.int32)
        w_pos = start - WINDOW + jnp.arange(WINDOW + Q_BLOCK, dtype=jnp.int32)
        return _nsa_core(qb, q_pos, kvc, c_end, cover, gather_sel, kwb, w_pos, gb, rel_bias)

    o = lax.map(one_block, jnp.arange(T // Q_BLOCK, dtype=jnp.int32))
    o = jnp.moveaxis(o, 0, 1).reshape(Bn, T, NSA_Q_W)
    return o, kv_w[:, T - min(WINDOW, T):]


def _nsa_sample(q, kv_c, kv_s, kv_w, gates, cache_c, cache_s, page_table, win_buf, cmp_pe, cmp_w, rel_bias):
    Bn, T = q.shape[:2]
    past = page_table.shape[1] * PAGE_SIZE
    past_c = cache_c[page_table].reshape(Bn, past, 2, NSA_KV_GROUPS, NSA_DH)
    proj, pe_term = _subblock_proj(past_c, cmp_pe, cmp_w)
    kvc, c_end = _compress(proj, pe_term)
    bpp = PAGE_SIZE // SEL_BLOCK
    n_past_blk = past // SEL_BLOCK
    n_new_blk = -(-T // SEL_BLOCK)
    pool = cache_s.reshape(cache_s.shape[0] * bpp, SEL_BLOCK, 2, NSA_KV_GROUPS, NSA_DH)
    new_blk = jnp.pad(kv_s, [(0, 0), (0, n_new_blk * SEL_BLOCK - T), (0, 0), (0, 0), (0, 0)])
    new_blk = new_blk.reshape(Bn, n_new_blk, SEL_BLOCK, 2, NSA_KV_GROUPS, NSA_DH)
    bi = jnp.arange(Bn)[:, None, None, None]
    gi = jnp.arange(NSA_KV_GROUPS)[None, :, None, None]

    def gather_sel(idx):
        in_past = idx < n_past_blk
        jp = jnp.minimum(idx, n_past_blk - 1)
        phys = page_table[bi, jp // bpp] * bpp + jp % bpp
        from_pool = pool[phys, :, :, gi, :]
        jn = jnp.clip(idx - n_past_blk, 0, n_new_blk - 1)
        from_new = new_blk[bi, jn, :, :, gi, :]
        return jnp.where(in_past[..., None, None, None], from_pool, from_new.astype(from_pool.dtype))

    cover = _cover_matrix(kvc.shape[1], n_past_blk + n_new_blk)
    w_buf = win_buf.shape[1]
    kvw = jnp.concatenate([win_buf.astype(kv_w.dtype), kv_w], axis=1)
    w_pos = past - w_buf + jnp.arange(w_buf + T, dtype=jnp.int32)
    q_pos = past + jnp.arange(T, dtype=jnp.int32)
    o = _nsa_core(q, q_pos, kvc, c_end, cover, gather_sel, kvw, w_pos, gates, rel_bias)
    return o, kvw[:, T:]


def kernel(x_prompt, x_sample, p_prompt, p_sample, cache_cmp_kv, cache_slc_kv, page_table, state_win_kv, state_gdn, state_conv, g_mix, w_in, gdn_conv_w, gdn_dt_bias, gdn_a_log, gdn_norm, cmp_pe, cmp_w, rel_bias, w_proj_a, w_proj_b, w_out, g_ffn, w_router_group, b_router_group, w_router_expert, b_router_expert, w_gate, w_up, w_down, g_ple, w_ple_gate, w_ple_proj, g_final):
    bp, tp, d = x_prompt.shape
    bs, ts, _ = x_sample.shape
    n_p, n_s = bp * tp, bs * ts
    n_real = n_p + n_s
    mp = -(-n_real // ROW_ALIGN) * ROW_ALIGN
    pad = mp - n_real

    h = jnp.concatenate([x_prompt.reshape(n_p, d), x_sample.reshape(n_s, d), jnp.zeros((pad, d), f32)], 0)
    ple = jnp.concatenate([p_prompt[0].reshape(n_p, -1), p_sample[0].reshape(n_s, -1),
                           jnp.zeros((pad, p_prompt.shape[-1]), f32)], 0).astype(bf16)

    w = w_in[0]
    o_beta = 4 * GDN_QK_W
    o_nq = o_beta + 2 * GDN_HEADS
    o_gate = o_nq + NSA_Q_W + 6 * NSA_KV_W
    o_ga = o_gate + 3 * NSA_HEADS
    w_main = jnp.concatenate([w[:, :o_beta], w[:, o_nq:o_gate], w[:, o_ga:]], axis=1)
    n_small = 2 * GDN_HEADS + 3 * NSA_HEADS
    w_small = jnp.concatenate([w[:, o_beta:o_nq], w[:, o_gate:o_ga], jnp.zeros((d, LANES - n_small), f32)], axis=1)

    a = rmsnorm_rows(h, g_mix[0], bf16)
    z = proj_matmul(a, w_main, f32, TN_DENSE)
    zs = proj_matmul(a, w_small, f32, LANES)

    def rows(x, lo, hi, which):
        if which == "p":
            return x[:n_p, lo:hi].reshape(bp, tp, hi - lo)
        return x[n_p:n_real, lo:hi].reshape(bs, ts, hi - lo)

    lw = (gdn_conv_w[0], gdn_dt_bias[0], gdn_a_log[0], gdn_norm[0])
    outs = {}
    o_a_parts, o_b_parts = [], []
    for which in ("p", "s"):
        bn, t = (bp, tp) if which == "p" else (bs, ts)
        qa, ka, va, za = (rows(z, Z_Q, Z_K, which), rows(z, Z_K, Z_V, which),
                          rows(z, Z_V, Z_ZG, which), rows(z, Z_ZG, Z_NQ, which))
        b_logit = rows(zs, 0, GDN_HEADS, which)
        a_logit = rows(zs, GDN_HEADS, 2 * GDN_HEADS, which)
        gates = rows(zs, 2 * GDN_HEADS, n_small, which).reshape(bn, t, NSA_HEADS, 3)
        q = rows(z, Z_NQ, Z_KV, which).reshape(bn, t, NSA_KV_GROUPS, NSA_HPG, NSA_DH)
        kv = rows(z, Z_KV, Z_GA, which).reshape(bn, t, 3, 2, NSA_KV_GROUPS, NSA_DH)
        kv_c, kv_s, kv_w = kv[:, :, 0], kv[:, :, 1], kv[:, :, 2]
        if which == "p":
            conv0 = jnp.zeros((bn, GDN_CONV - 1, CONV_CH), f32)
            s0 = jnp.zeros((bn, GDN_HEADS, GDN_DK, GDN_DV), f32)
            kvc = compress_prompt(z, cmp_w[0], cmp_pe[0], bn, t)
            gates_g = jnp.transpose(zs[:, 2 * GDN_HEADS:n_small].reshape(mp, NSA_KV_GROUPS, 3 * NSA_HPG), (1, 0, 2))
            o_b = nsa_prompt_attention(z, gates_g, kvc, rel_bias, bn, t)
            win_new = kv_w[:, t - min(WINDOW, t):]
        else:
            conv0, s0 = state_conv[0], state_gdn[0]
            o_b, win_new = _nsa_sample(q, kv_c, kv_s, kv_w, gates, cache_cmp_kv[0], cache_slc_kv[0],
                                       page_table, state_win_kv[0], cmp_pe[0], cmp_w[0], rel_bias)
        o_a, conv_new, s_new = _gdn_mixer(qa, ka, va, za, b_logit, a_logit, conv0, s0, *lw)
        o_a_parts.append(o_a.reshape(bn * t, GDN_V_W))
        o_b_parts.append(o_b.reshape(bn * t, NSA_Q_W))
        outs[which] = (kv_c, kv_s, win_new, s_new, conv_new)

    o_a = jnp.concatenate(o_a_parts + [jnp.zeros((pad, GDN_V_W), f32)], 0).astype(bf16)
    o_b = jnp.concatenate(o_b_parts + [jnp.zeros((pad, NSA_Q_W), f32)], 0).astype(bf16)
    merged = merge_matmul(o_a, o_b, z, w_proj_a[0], w_proj_b[0])
    h = resid_matmul(merged, w_out[0], h)
    h = hier_moe(h, n_real, g_ffn[0], w_router_group[0], b_router_group[0], w_router_expert[0],
                 b_router_expert[0], w_gate[0], w_up[0], w_down[0])
    n3 = rmsnorm_rows(h, g_ple[0], bf16)
    h = ple_matmul(n3, w_ple_gate[0], ple, w_ple_proj[0], h)
    y = rmsnorm_rows(h, g_final, f32)
    y_prompt = y[:n_p].reshape(bp, tp, d)
    y_sample = y[n_p:n_real].reshape(bs, ts, d)
    st_p, st_s = outs["p"], outs["s"]
    return (y_prompt, y_sample) + tuple(t[None] for t in st_p) + tuple(t[None] for t in st_s)
```

```python
import functools
import math

import jax
import jax.numpy as jnp
import numpy as np
from jax import lax
from jax.experimental import pallas as pl
from jax.experimental.pallas import tpu as pltpu

D_MODEL = 4096
GDN_HEADS = 16
GDN_DK = 128
GDN_DV = 128
GDN_CONV = 4
GDN_CHUNK = 64
NSA_HEADS = 16
NSA_KV_GROUPS = 4
NSA_HPG = NSA_HEADS // NSA_KV_GROUPS
NSA_DH = 128
CMP_BLOCK = 32
CMP_STRIDE = 16
SEL_BLOCK = 64
SEL_TOPK = 16
WINDOW = 512
Q_BLOCK = 128
FORCE_SCORE = 1.0e4
REL_BUCKETS = 32
REL_MAX_DIST = 1024
PAGE_SIZE = 128
MOE_GROUPS = 4
MOE_PER_GROUP = 8
MOE_EXPERTS = MOE_GROUPS * MOE_PER_GROUP
MOE_TOPK = 2
EXPERT_HIDDEN = 512
EPS = 1e-6

GDN_QK_W = GDN_HEADS * GDN_DK
GDN_V_W = GDN_HEADS * GDN_DV
CONV_CH = 2 * GDN_QK_W + GDN_V_W
NSA_Q_W = NSA_HEADS * NSA_DH
NSA_KV_W = NSA_KV_GROUPS * NSA_DH

LANES = 128
SUBLANES = 8
VMEM_LIMIT = 56 * 1024 * 1024

ROW_ALIGN = 768
TM_DENSE = 768
TN_DENSE = 512
TM_ROWS = 256

Z_Q, Z_K, Z_V, Z_ZG = 0, 2048, 4096, 6144
Z_NQ = 8192
Z_KV = 10240
Z_GA = 13312
Z_GB = 17408
Z_COLS = 21504

bf16 = jnp.bfloat16
f32 = jnp.float32


def _cparams(sem):
    return pltpu.CompilerParams(dimension_semantics=sem, vmem_limit_bytes=VMEM_LIMIT)


def _rmsnorm_body(x_ref, g_ref, o_ref):
    x = x_ref[...]
    y = x * lax.rsqrt(jnp.mean(x * x, -1, keepdims=True) + EPS)
    o_ref[...] = (y * g_ref[...]).astype(o_ref.dtype)


def rmsnorm_rows(x, g, out_dtype):
    m, d = x.shape
    return pl.pallas_call(
        _rmsnorm_body,
        grid=(m // TM_ROWS,),
        in_specs=[pl.BlockSpec((TM_ROWS, d), lambda i: (i, 0)),
                  pl.BlockSpec((1, d), lambda i: (0, 0))],
        out_specs=pl.BlockSpec((TM_ROWS, d), lambda i: (i, 0)),
        out_shape=jax.ShapeDtypeStruct((m, d), out_dtype),
        compiler_params=_cparams(("parallel",)),
        name="rmsnorm_rows",
    )(x, g.reshape(1, d))


def _cast_body(x_ref, o_ref):
    o_ref[...] = x_ref[...].astype(o_ref.dtype)


def cast_bf16(x2d, tr):
    r, c = x2d.shape
    return pl.pallas_call(
        _cast_body,
        grid=(r // tr,),
        in_specs=[pl.BlockSpec((tr, c), lambda i: (i, 0))],
        out_specs=pl.BlockSpec((tr, c), lambda i: (i, 0)),
        out_shape=jax.ShapeDtypeStruct((r, c), bf16),
        compiler_params=_cparams(("parallel",)),
        name="cast_bf16",
    )(x2d)


def _proj_body(a_ref, w_ref, o_ref, wb_ref):
    @pl.when(pl.program_id(1) == 0)
    def _():
        wb_ref[...] = w_ref[...].astype(bf16)
    o_ref[...] = jnp.dot(a_ref[...], wb_ref[...], preferred_element_type=f32).astype(o_ref.dtype)


def proj_matmul(a, w, out_dtype, tn):
    m, k = a.shape
    n = w.shape[1]
    return pl.pallas_call(
        _proj_body,
        grid=(n // tn, m // TM_DENSE),
        in_specs=[pl.BlockSpec((TM_DENSE, k), lambda j, i: (i, 0)),
                  pl.BlockSpec((k, tn), lambda j, i: (0, j))],
        out_specs=pl.BlockSpec((TM_DENSE, tn), lambda j, i: (i, j)),
        out_shape=jax.ShapeDtypeStruct((m, n), out_dtype),
        scratch_shapes=[pltpu.VMEM((k, tn), bf16)],
        compiler_params=_cparams(("arbitrary", "arbitrary")),
        name="proj_matmul",
    )(a, w)


def _merge_body(oa_ref, ob_ref, ga_ref, gb_ref, wa_ref, wb_ref, o_ref, wa_s, wb_s):
    @pl.when(pl.program_id(1) == 0)
    def _():
        wa_s[...] = wa_ref[...].astype(bf16)
        wb_s[...] = wb_ref[...].astype(bf16)
    pa = jnp.dot(oa_ref[...], wa_s[...], preferred_element_type=f32)
    pb = jnp.dot(ob_ref[...], wb_s[...], preferred_element_type=f32)
    o_ref[...] = (jax.nn.sigmoid(ga_ref[...]) * pa + jax.nn.sigmoid(gb_ref[...]) * pb).astype(o_ref.dtype)


def merge_matmul(o_a, o_b, z, w_a, w_b):
    m, ka = o_a.shape
    kb = o_b.shape[1]
    n = w_a.shape[1]
    tn = TN_DENSE
    ja, jb = Z_GA // tn, Z_GB // tn
    return pl.pallas_call(
        _merge_body,
        grid=(n // tn, m // TM_DENSE),
        in_specs=[pl.BlockSpec((TM_DENSE, ka), lambda j, i: (i, 0)),
                  pl.BlockSpec((TM_DENSE, kb), lambda j, i: (i, 0)),
                  pl.BlockSpec((TM_DENSE, tn), lambda j, i: (i, ja + j)),
                  pl.BlockSpec((TM_DENSE, tn), lambda j, i: (i, jb + j)),
                  pl.BlockSpec((ka, tn), lambda j, i: (0, j)),
                  pl.BlockSpec((kb, tn), lambda j, i: (0, j))],
        out_specs=pl.BlockSpec((TM_DENSE, tn), lambda j, i: (i, j)),
        out_shape=jax.ShapeDtypeStruct((m, n), bf16),
        scratch_shapes=[pltpu.VMEM((ka, tn), bf16), pltpu.VMEM((kb, tn), bf16)],
        compiler_params=_cparams(("arbitrary", "arbitrary")),
        name="merge_matmul",
    )(o_a, o_b, z, z, w_a, w_b)


def _resid_body(a_ref, w_ref, h_ref, o_ref, wb_ref):
    @pl.when(pl.program_id(1) == 0)
    def _():
        wb_ref[...] = w_ref[...].astype(bf16)
    o_ref[...] = h_ref[...] + jnp.dot(a_ref[...], wb_ref[...], preferred_element_type=f32)


def resid_matmul(a, w, h):
    m, k = a.shape
    n = w.shape[1]
    tn = TN_DENSE
    return pl.pallas_call(
        _resid_body,
        grid=(n // tn, m // TM_DENSE),
        in_specs=[pl.BlockSpec((TM_DENSE, k), lambda j, i: (i, 0)),
                  pl.BlockSpec((k, tn), lambda j, i: (0, j)),
                  pl.BlockSpec((TM_DENSE, tn), lambda j, i: (i, j))],
        out_specs=pl.BlockSpec((TM_DENSE, tn), lambda j, i: (i, j)),
        out_shape=jax.ShapeDtypeStruct((m, n), f32),
        scratch_shapes=[pltpu.VMEM((k, tn), bf16)],
        compiler_params=_cparams(("arbitrary", "arbitrary")),
        name="resid_matmul",
    )(a, w, h)


def _ple_body(a_ref, w_ref, p_ref, wp_ref, h_ref, o_ref, wb_ref, wpb_ref):
    @pl.when(pl.program_id(1) == 0)
    def _():
        wb_ref[...] = w_ref[...].astype(bf16)
        wpb_ref[...] = wp_ref[...].astype(bf16)
    gate = jax.nn.sigmoid(jnp.dot(a_ref[...], wb_ref[...], preferred_element_type=f32))
    emb = jnp.dot(p_ref[...], wpb_ref[...], preferred_element_type=f32)
    o_ref[...] = h_ref[...] + gate * emb


def ple_matmul(a, w_gate, p, w_proj, h):
    m, k = a.shape
    kp = p.shape[1]
    n = w_gate.shape[1]
    tn = TN_DENSE
    return pl.pallas_call(
        _ple_body,
        grid=(n // tn, m // TM_DENSE),
        in_specs=[pl.BlockSpec((TM_DENSE, k), lambda j, i: (i, 0)),
                  pl.BlockSpec((k, tn), lambda j, i: (0, j)),
                  pl.BlockSpec((TM_DENSE, kp), lambda j, i: (i, 0)),
                  pl.BlockSpec((kp, tn), lambda j, i: (0, j)),
                  pl.BlockSpec((TM_DENSE, tn), lambda j, i: (i, j))],
        out_specs=pl.BlockSpec((TM_DENSE, tn), lambda j, i: (i, j)),
        out_shape=jax.ShapeDtypeStruct((m, n), f32),
        scratch_shapes=[pltpu.VMEM((k, tn), bf16), pltpu.VMEM((kp, tn), bf16)],
        compiler_params=_cparams(("arbitrary", "arbitrary")),
        name="ple_matmul",
    )(a, w_gate, p, w_proj, h)


def _router_body(h_ref, g_ref, wr_ref, br_ref, m_ref, r_ref):
    x = h_ref[...]
    y = x * lax.rsqrt(jnp.mean(x * x, -1, keepdims=True) + EPS) * g_ref[...]
    m_ref[...] = y.astype(bf16)
    logits = jnp.dot(y, wr_ref[...], preferred_element_type=f32,
                     precision=lax.Precision.HIGHEST) + br_ref[...]
    lane = lax.broadcasted_iota(jnp.int32, logits.shape, 1)
    neg = -jnp.inf
    lg = jnp.where(lane < MOE_GROUPS, logits, neg)
    eg = jnp.exp(lg - jnp.max(lg, -1, keepdims=True))
    pg = eg / jnp.sum(eg, -1, keepdims=True)
    pg_top = jnp.max(pg, -1, keepdims=True)
    g_idx = jnp.min(jnp.where(pg == pg_top, lane, LANES), -1, keepdims=True)
    lo = MOE_GROUPS + MOE_PER_GROUP * g_idx
    emask = (lane >= lo) & (lane < lo + MOE_PER_GROUP)
    le = jnp.where(emask, logits, neg)
    ee = jnp.exp(le - jnp.max(le, -1, keepdims=True))
    pe = jnp.where(emask, ee / jnp.sum(ee, -1, keepdims=True), -1.0)
    v1 = jnp.max(pe, -1, keepdims=True)
    i1 = jnp.min(jnp.where(pe == v1, lane, LANES), -1, keepdims=True)
    pe2 = jnp.where(lane == i1, -1.0, pe)
    v2 = jnp.max(pe2, -1, keepdims=True)
    i2 = jnp.min(jnp.where(pe2 == v2, lane, LANES), -1, keepdims=True)
    den = v1 + v2
    w1 = pg_top * v1 / den
    w2 = pg_top * v2 / den
    e1 = (i1 - MOE_GROUPS).astype(f32)
    e2 = (i2 - MOE_GROUPS).astype(f32)
    r_ref[...] = jnp.where(lane == 0, e1, jnp.where(lane == 1, e2,
                           jnp.where(lane == 2, w1, jnp.where(lane == 3, w2, 0.0))))


def moe_router(h, g_ffn, w_router, b_router):
    m, d = h.shape
    return pl.pallas_call(
        _router_body,
        grid=(m // TM_ROWS,),
        in_specs=[pl.BlockSpec((TM_ROWS, d), lambda i: (i, 0)),
                  pl.BlockSpec((1, d), lambda i: (0, 0)),
                  pl.BlockSpec((d, LANES), lambda i: (0, 0)),
                  pl.BlockSpec((1, LANES), lambda i: (0, 0))],
        out_specs=[pl.BlockSpec((TM_ROWS, d), lambda i: (i, 0)),
                   pl.BlockSpec((TM_ROWS, LANES), lambda i: (i, 0))],
        out_shape=[jax.ShapeDtypeStruct((m, d), bf16),
                   jax.ShapeDtypeStruct((m, LANES), f32)],
        compiler_params=_cparams(("parallel",)),
        name="moe_router",
    )(h, g_ffn.reshape(1, d), w_router, b_router)


def _expert_body(te_ref, tv_ref, x_ref, rw_ref, wg_ref, wu_ref, wd_ref, o_ref):
    t = pl.program_id(0)

    @pl.when(tv_ref[t] > 0)
    def _():
        x = x_ref[...]
        gate = jnp.dot(x, wg_ref[...], preferred_element_type=f32)
        up = jnp.dot(x, wu_ref[...], preferred_element_type=f32)
        hid = (jax.nn.silu(gate) * up * rw_ref[...]).astype(bf16)
        o_ref[...] = jnp.dot(hid, wd_ref[...], preferred_element_type=f32)

    @pl.when(tv_ref[t] == 0)
    def _():
        o_ref[...] = jnp.zeros_like(o_ref)


def expert_matmul(tile_expert, tile_valid, xs, row_w, wg, wu, wd):
    r, d = xs.shape
    f = wg.shape[2]
    n_tiles = r // TM_ROWS
    grid_spec = pltpu.PrefetchScalarGridSpec(
        num_scalar_prefetch=2,
        grid=(n_tiles,),
        in_specs=[pl.BlockSpec((TM_ROWS, d), lambda t, te, tv: (t, 0)),
                  pl.BlockSpec((TM_ROWS, 1), lambda t, te, tv: (t, 0)),
                  pl.BlockSpec((None, d, f), lambda t, te, tv: (te[t], 0, 0)),
                  pl.BlockSpec((None, d, f), lambda t, te, tv: (te[t], 0, 0)),
                  pl.BlockSpec((None, f, d), lambda t, te, tv: (te[t], 0, 0))],
        out_specs=pl.BlockSpec((TM_ROWS, d), lambda t, te, tv: (t, 0)),
    )
    return pl.pallas_call(
        _expert_body,
        grid_spec=grid_spec,
        out_shape=jax.ShapeDtypeStruct((r, d), f32),
        compiler_params=_cparams(("arbitrary",)),
        name="expert_matmul",
    )(tile_expert, tile_valid, xs, row_w, wg, wu, wd)


def hier_moe(h, n_real, g_ffn, w_rg, b_rg, w_re, b_re, w_gate, w_up, w_down):
    mp, d = h.shape
    n_route = MOE_GROUPS + MOE_EXPERTS
    w_router = jnp.zeros((d, LANES), f32).at[:, :MOE_GROUPS].set(w_rg).at[:, MOE_GROUPS:n_route].set(w_re)
    b_router = jnp.zeros((1, LANES), f32).at[0, :MOE_GROUPS].set(b_rg).at[0, MOE_GROUPS:n_route].set(b_re)
    m_bf, slab = moe_router(h, g_ffn, w_router, b_router)
    ids = slab[:n_real, 0:2].astype(jnp.int32)
    wts = slab[:n_real, 2:4]

    tm = TM_ROWS
    n_assign = n_real * MOE_TOPK
    n_slots = -(-(n_assign + MOE_EXPERTS * (tm - 1)) // tm) * tm
    e_flat = ids.reshape(-1)
    order = jnp.argsort(e_flat, stable=True)
    e_sorted = e_flat[order]
    counts = jnp.sum(jax.nn.one_hot(e_flat, MOE_EXPERTS, dtype=jnp.int32), axis=0)
    padded = -(-counts // tm) * tm
    start_p = jnp.cumsum(padded) - padded
    start = jnp.cumsum(counts) - counts
    slot_sorted = start_p[e_sorted] + (jnp.arange(n_assign, dtype=jnp.int32) - start[e_sorted])
    slot_of = jnp.zeros((n_assign,), jnp.int32).at[order].set(slot_sorted.astype(jnp.int32))
    src_tok = jnp.zeros((n_slots,), jnp.int32).at[slot_of].set(jnp.arange(n_assign, dtype=jnp.int32) // MOE_TOPK)
    row_w = jnp.zeros((n_slots,), f32).at[slot_of].set(wts.reshape(-1))
    tile_start = jnp.arange(n_slots // tm, dtype=jnp.int32) * tm
    ends = jnp.cumsum(padded)
    tile_expert = jnp.minimum(jnp.searchsorted(ends, tile_start, side="right"), MOE_EXPERTS - 1).astype(jnp.int32)
    tile_valid = (tile_start < ends[-1]).astype(jnp.int32)

    xs = jnp.take(m_bf, src_tok, axis=0)
    wg = cast_bf16(w_gate.reshape(MOE_EXPERTS * d, EXPERT_HIDDEN), 4096).reshape(MOE_EXPERTS, d, EXPERT_HIDDEN)
    wu = cast_bf16(w_up.reshape(MOE_EXPERTS * d, EXPERT_HIDDEN), 4096).reshape(MOE_EXPERTS, d, EXPERT_HIDDEN)
    wd = cast_bf16(w_down.reshape(MOE_EXPERTS * EXPERT_HIDDEN, d), 512).reshape(MOE_EXPERTS, EXPERT_HIDDEN, d)
    ys = expert_matmul(tile_expert, tile_valid, xs, row_w.reshape(n_slots, 1), wg, wu, wd)
    slot2 = slot_of.reshape(n_real, MOE_TOPK)
    y = jnp.take(ys, slot2[:, 0], axis=0) + jnp.take(ys, slot2[:, 1], axis=0)
    return h.at[:n_real].add(y)


NEG_BIG = -1e30
N_BIAS_TILES = 11
_NT = (((1,), (1,)), ((), ()))


def _compress_body(x_ref, w_ref, pe_ref, o_ref):
    ns = o_ref.shape[0]
    acc0 = jnp.zeros((ns, NSA_DH), f32)
    acc1 = jnp.zeros((ns, NSA_DH), f32)
    for s in range(CMP_STRIDE):
        xs = x_ref[pl.ds(s, ns, stride=CMP_STRIDE), :]
        a0 = (xs + pe_ref[s:s + 1, :]).astype(bf16)
        a1 = (xs + pe_ref[CMP_STRIDE + s:CMP_STRIDE + s + 1, :]).astype(bf16)
        acc0 = acc0 + jnp.dot(a0, w_ref[s].astype(bf16), preferred_element_type=f32)
        acc1 = acc1 + jnp.dot(a1, w_ref[CMP_STRIDE + s].astype(bf16), preferred_element_type=f32)
    o_ref[...] = (acc0 + pltpu.roll(acc1, ns - 1, axis=0)).astype(o_ref.dtype)


def compress_prompt(z, cmp_w, cmp_pe, bn, t):
    ns = t // CMP_STRIDE
    col0 = Z_KV // NSA_DH
    return pl.pallas_call(
        _compress_body,
        grid=(bn, 2, NSA_KV_GROUPS),
        in_specs=[pl.BlockSpec((t, NSA_DH), lambda b, x, g: (b, col0 + x * NSA_KV_GROUPS + g)),
                  pl.BlockSpec((None, CMP_BLOCK, NSA_DH, NSA_DH), lambda b, x, g: (x, 0, 0, 0)),
                  pl.BlockSpec((None, CMP_BLOCK, NSA_DH), lambda b, x, g: (x, 0, 0))],
        out_specs=pl.BlockSpec((None, None, None, ns, NSA_DH), lambda b, x, g: (b, x, g, 0, 0)),
        out_shape=jax.ShapeDtypeStruct((bn, 2, NSA_KV_GROUPS, ns, NSA_DH), bf16),
        compiler_params=_cparams(("parallel", "parallel", "parallel")),
        name="compress_prompt",
    )(z, cmp_w, cmp_pe)


def _online_softmax_step(q, k, v, extra, m_ref, l_ref, acc_ref):
    s = lax.dot_general(q, k, _NT, preferred_element_type=f32) * (NSA_DH ** -0.5) + extra
    m_old = m_ref[...]
    m_new = jnp.maximum(m_old, jnp.max(s, -1, keepdims=True))
    alpha = jnp.exp(m_old - m_new)
    p = jnp.exp(s - m_new)
    l_ref[...] = alpha * l_ref[...] + jnp.sum(p, -1, keepdims=True)
    acc_ref[...] = alpha * acc_ref[...] + jnp.dot(p.astype(bf16), v, preferred_element_type=f32)
    m_ref[...] = m_new


def _nsa_prompt_body(q_ref, kc_ref, vc_ref, ks_ref, vs_ref, kw_ref, vw_ref, gate_ref, bc_ref, bt_ref,
                     cov_ref, exp_ref, o_ref, mb_ref, m_ref, l_ref, acc_ref):
    g = pl.program_id(1)
    i = pl.program_id(2)
    qb = Q_BLOCK
    rows = NSA_HPG * qb
    qf = q_ref[...]
    q = jnp.concatenate([qf[:, h * NSA_DH:(h + 1) * NSA_DH] for h in range(NSA_HPG)], axis=0).astype(bf16)

    ncp = kc_ref.shape[0]
    n_done = (qb // CMP_STRIDE) * (i + 1)
    bias_c = pltpu.roll(bc_ref[...], n_done % ncp, axis=1)
    lane_c = lax.broadcasted_iota(jnp.int32, bias_c.shape, 1)
    bias_c = jnp.where(lane_c < n_done, bias_c, NEG_BIG)
    s = lax.dot_general(q, kc_ref[...], _NT, preferred_element_type=f32) * (NSA_DH ** -0.5) + bias_c
    m = jnp.max(s, -1, keepdims=True)
    m = jnp.where(m < 0.5 * NEG_BIG, 0.0, m)
    e = jnp.exp(s - m)
    p = e / jnp.maximum(jnp.sum(e, -1, keepdims=True), 1e-30)
    o_c = jnp.dot(p.astype(bf16), vc_ref[...], preferred_element_type=f32)

    psum = p[0:qb]
    for h in range(1, NSA_HPG):
        psum = psum + p[h * qb:(h + 1) * qb]
    imp_t = lax.dot_general(cov_ref[...], psum, _NT, preferred_element_type=f32,
                            precision=lax.Precision.HIGHEST)
    nsb = imp_t.shape[0]
    blk = lax.broadcasted_iota(jnp.int32, (nsb, qb), 0)
    qpos = i * qb + lax.broadcasted_iota(jnp.int32, (nsb, qb), 1)
    cur = qpos // SEL_BLOCK
    forced = (blk == 0) | (blk == cur) | (blk == cur - 1)
    valid = blk * SEL_BLOCK <= qpos
    score = jnp.where(valid, jnp.where(forced, FORCE_SCORE, imp_t), -1.0)
    sel_t = jnp.zeros((nsb, qb), f32)
    for _ in range(min(SEL_TOPK, nsb)):
        mx = jnp.max(score, axis=0, keepdims=True)
        first = jnp.min(jnp.where(score == mx, blk, nsb), axis=0, keepdims=True)
        pick = blk == first
        sel_t = jnp.where(pick, 1.0, sel_t)
        score = jnp.where(pick, -2.0, score)
    unsel = ((sel_t - 1.0) * (-NEG_BIG)).T.astype(bf16)
    mb_ref[...] = jnp.dot(unsel, exp_ref[...], preferred_element_type=f32)

    def reset():
        m_ref[...] = jnp.full(m_ref.shape, NEG_BIG, f32)
        l_ref[...] = jnp.zeros(l_ref.shape, f32)
        acc_ref[...] = jnp.zeros(acc_ref.shape, f32)

    reset()

    def sel_step(j, carry):
        r0 = pl.multiple_of(j * qb, qb)
        k = ks_ref[pl.ds(r0, qb), :].astype(bf16)
        v = vs_ref[pl.ds(r0, qb), :].astype(bf16)
        mb = mb_ref[:, pl.ds(r0, qb)]
        extra = bt_ref[jnp.minimum(i - j, N_BIAS_TILES - 2)] + jnp.concatenate([mb] * NSA_HPG, axis=0)
        _online_softmax_step(q, k, v, extra, m_ref, l_ref, acc_ref)
        return carry

    lax.fori_loop(0, i + 1, sel_step, 0)
    o_s = acc_ref[...] / l_ref[...]

    reset()
    n_win = WINDOW // qb

    def win_step(j, carry):
        r0 = pl.multiple_of(j * qb, qb)
        k = kw_ref[pl.ds(r0, qb), :].astype(bf16)
        v = vw_ref[pl.ds(r0, qb), :].astype(bf16)
        d = i - j
        extra = bt_ref[jnp.where(d == n_win, N_BIAS_TILES - 1, d)]
        _online_softmax_step(q, k, v, extra, m_ref, l_ref, acc_ref)
        return carry

    lax.fori_loop(jnp.maximum(i - n_win, 0), i + 1, win_step, 0)
    o_w = acc_ref[...] / l_ref[...]

    gt = jax.nn.sigmoid(gate_ref[...])
    outs = []
    for h in range(NSA_HPG):
        c = 3 * h
        sl = slice(h * qb, (h + 1) * qb)
        outs.append(gt[:, c:c + 1] * o_c[sl] + gt[:, c + 1:c + 2] * o_s[sl] + gt[:, c + 2:c + 3] * o_w[sl])
    o_ref[...] = jnp.concatenate(outs, axis=1).astype(o_ref.dtype)


def _nsa_bias_tables(rel_bias, t):
    qb = Q_BLOCK
    nq = t // qb
    tbl = rel_bias.astype(f32)
    r = jnp.arange(qb, dtype=jnp.int32)[:, None]
    c = jnp.arange(qb, dtype=jnp.int32)[None, :]
    tiles = []
    for delta in range(N_BIAS_TILES - 1):
        dist = delta * qb + r - c
        tiles.append(jnp.where((dist >= 0)[..., None], tbl[_rel_bucket(dist)], NEG_BIG))
    dist = (WINDOW // qb) * qb + r - c
    tiles.append(jnp.where(((dist >= 0) & (dist < WINDOW))[..., None], tbl[_rel_bucket(dist)], NEG_BIG))
    bt = jnp.stack(tiles, 0)
    bt = bt.reshape(N_BIAS_TILES, qb, qb, NSA_KV_GROUPS, NSA_HPG)
    bt = jnp.transpose(bt, (3, 0, 4, 1, 2)).reshape(NSA_KV_GROUPS, N_BIAS_TILES, NSA_HPG * qb, qb)
    ncp = t // CMP_STRIDE
    m_back = (ncp - 1) - jnp.arange(ncp, dtype=jnp.int32)[None, :]
    dist_c = r + CMP_STRIDE * m_back + CMP_STRIDE - (qb + CMP_BLOCK - 1)
    bc = jnp.where((dist_c >= 0)[..., None], tbl[_rel_bucket(dist_c)], NEG_BIG)
    bc = bc.reshape(qb, ncp, NSA_KV_GROUPS, NSA_HPG)
    bc = jnp.transpose(bc, (2, 3, 0, 1)).reshape(NSA_KV_GROUPS, NSA_HPG * qb, ncp)
    return bt, bc


def nsa_prompt_attention(z, gates_g, kvc, rel_bias, bn, t):
    qb = Q_BLOCK
    nq = t // qb
    ncp = t // CMP_STRIDE
    nsb = t // SEL_BLOCK
    rows = NSA_HPG * qb
    bt, bc = _nsa_bias_tables(rel_bias, t)
    cover_t = jnp.concatenate([_cover_matrix(ncp - 1, nsb), jnp.zeros((1, nsb), f32)], 0).T
    expand = jnp.asarray(np.repeat(np.eye(nsb, dtype=np.float32), SEL_BLOCK, axis=1), bf16)
    kcol = Z_KV // NSA_DH
    kv_spec = lambda off: pl.BlockSpec((t, NSA_DH), lambda b, g, i, off=off: (b, kcol + off + g))
    return pl.pallas_call(
        _nsa_prompt_body,
        grid=(bn, NSA_KV_GROUPS, nq),
        in_specs=[pl.BlockSpec((qb, NSA_HPG * NSA_DH), lambda b, g, i: (b * nq + i, Z_NQ // (NSA_HPG * NSA_DH) + g)),
                  pl.BlockSpec((None, None, None, ncp, NSA_DH), lambda b, g, i: (b, 0, g, 0, 0)),
                  pl.BlockSpec((None, None, None, ncp, NSA_DH), lambda b, g, i: (b, 1, g, 0, 0)),
                  kv_spec(2 * NSA_KV_GROUPS), kv_spec(3 * NSA_KV_GROUPS),
                  kv_spec(4 * NSA_KV_GROUPS), kv_spec(5 * NSA_KV_GROUPS),
                  pl.BlockSpec((None, qb, 3 * NSA_HPG), lambda b, g, i: (g, b * nq + i, 0)),
                  pl.BlockSpec((None, rows, ncp), lambda b, g, i: (g, 0, 0)),
                  pl.BlockSpec((None, N_BIAS_TILES, rows, qb), lambda b, g, i: (g, 0, 0, 0)),
                  pl.BlockSpec((nsb, ncp), lambda b, g, i: (0, 0)),
                  pl.BlockSpec((nsb, t), lambda b, g, i: (0, 0))],
        out_specs=pl.BlockSpec((qb, NSA_HPG * NSA_DH), lambda b, g, i: (b * nq + i, g)),
        out_shape=jax.ShapeDtypeStruct((bn * t, NSA_Q_W), bf16),
        scratch_shapes=[pltpu.VMEM((qb, t), f32), pltpu.VMEM((rows, 1), f32), pltpu.VMEM((rows, 1), f32),
                        pltpu.VMEM((rows, NSA_DH), f32)],
        compiler_params=_cparams(("parallel", "parallel", "arbitrary")),
        name="nsa_prompt_attention",
    )(z, kvc, kvc, z, z, z, z, gates_g, bc, bt, cover_t, expand)


GDN_RT = 512
_HI = lax.Precision.HIGHEST


def _gdn_prep_body(q_ref, k_ref, v_ref, qp_ref, kp_ref, vp_ref, gb_ref, cwq_ref, cwk_ref, cwv_ref,
                   w_ref, u_ref, aqk_ref, qg_ref, kg_ref, gl_ref, qs_ref, ks_ref, vs_ref):
    rt = q_ref.shape[0]
    cs = GDN_CHUNK
    first = pl.program_id(2) == 0
    pad = SUBLANES

    def conv_silu(x_ref, prev_ref, scr_ref, cw_ref):
        scr_ref[0:pad, :] = jnp.where(first, 0.0, prev_ref[...])
        scr_ref[pad:pad + rt, :] = x_ref[...]
        y = scr_ref[pl.ds(pad - (GDN_CONV - 1), rt), :] * cw_ref[0:1, :]
        for j in range(1, GDN_CONV):
            y = y + scr_ref[pl.ds(pad - (GDN_CONV - 1) + j, rt), :] * cw_ref[j:j + 1, :]
        return y * jax.nn.sigmoid(y)

    q = conv_silu(q_ref, qp_ref, qs_ref, cwq_ref)
    k = conv_silu(k_ref, kp_ref, ks_ref, cwk_ref)
    v = conv_silu(v_ref, vp_ref, vs_ref, cwv_ref)
    q = q * lax.rsqrt(jnp.sum(q * q, -1, keepdims=True) + EPS) * (GDN_DK ** -0.5)
    k = k * lax.rsqrt(jnp.sum(k * k, -1, keepdims=True) + EPS)

    gb = gb_ref[...]
    x = gb[:, 1:2] + gb[:, 2:3]
    softplus = jnp.maximum(x, 0.0) + jnp.log1p(jnp.exp(-jnp.abs(x)))
    g = jnp.broadcast_to(-jnp.exp(gb[:, 3:4]) * softplus, (rt, GDN_DK))
    beta = jnp.broadcast_to(jax.nn.sigmoid(gb[:, 0:1]), (rt, GDN_DK))

    ri = lax.broadcasted_iota(jnp.int32, (cs, cs), 0)
    ci = lax.broadcasted_iota(jnp.int32, (cs, cs), 1)
    incl = ri >= ci
    strict = ri > ci
    tri = incl.astype(f32)
    eye = (ri == ci).astype(f32)
    gls = []
    for c in range(rt // cs):
        sl = slice(c * cs, (c + 1) * cs)
        qc, kc, vc, gc_, bc_ = q[sl], k[sl], v[sl], g[sl], beta[sl]
        gcum = jnp.dot(tri, gc_, preferred_element_type=f32, precision=_HI)
        dlt = jnp.dot(tri, jnp.where(strict, gc_[:, :cs], 0.0), preferred_element_type=f32, precision=_HI)
        dec = jnp.exp(jnp.where(incl, dlt, 0.0))
        eg = jnp.exp(gcum)
        g_last = gcum[cs - 1:cs, :]
        kb = kc * bc_
        kbf = kc.astype(bf16)
        lmat = lax.dot_general(kb.astype(bf16), kbf, _NT, preferred_element_type=f32) * jnp.where(strict, dec, 0.0)
        aqk = lax.dot_general(qc.astype(bf16), kbf, _NT, preferred_element_type=f32) * jnp.where(incl, dec, 0.0)
        inv = eye - lmat
        pw = lmat
        for _ in range(int(math.log2(cs)) - 1):
            pw = jnp.dot(pw, pw, preferred_element_type=f32, precision=_HI)
            inv = inv + jnp.dot(inv, pw, preferred_element_type=f32, precision=_HI)
        rhs = jnp.concatenate([kb * eg, vc * bc_], axis=1)
        wu = jnp.dot(inv, rhs, preferred_element_type=f32, precision=_HI)
        w_ref[sl, :] = wu[:, :GDN_DK].astype(w_ref.dtype)
        u_ref[sl, :] = wu[:, GDN_DK:]
        aqk_ref[sl, :] = aqk.astype(aqk_ref.dtype)
        qg_ref[sl, :] = (qc * eg).astype(qg_ref.dtype)
        kg_ref[sl, :] = (kc * jnp.exp(g_last - gcum)).astype(kg_ref.dtype)
        gls.append(jnp.exp(g_last))
    gl_ref[...] = jnp.concatenate(gls, axis=0)


def _gdn_scan_body(w_ref, u_ref, aqk_ref, qg_ref, kg_ref, gl_ref, za_ref, gn_ref, o_ref, sfin_ref, s_ref):
    c = pl.program_id(1)
    nc = pl.num_programs(1)
    n_gl = gl_ref.shape[1]

    @pl.when(c == 0)
    def _():
        s_ref[...] = jnp.zeros(s_ref.shape, f32)

    for h in range(GDN_HEADS):
        s = s_ref[h]
        sb = s.astype(bf16)
        v_new = u_ref[h] - jnp.dot(w_ref[h], sb, preferred_element_type=f32)
        vb = v_new.astype(bf16)
        o = (jnp.dot(qg_ref[h], sb, preferred_element_type=f32)
             + jnp.dot(aqk_ref[h], vb, preferred_element_type=f32))
        gl = gl_ref[h, pl.ds(c % n_gl, 1), :]
        s_ref[h] = gl * s + lax.dot_general(kg_ref[h], vb, (((0,), (0,)), ((), ())), preferred_element_type=f32)
        on = o * lax.rsqrt(jnp.mean(o * o, -1, keepdims=True) + EPS) * gn_ref[...]
        za = za_ref[:, h * GDN_DV:(h + 1) * GDN_DV]
        o_ref[:, h * GDN_DV:(h + 1) * GDN_DV] = (on * (za * jax.nn.sigmoid(za))).astype(o_ref.dtype)

    @pl.when(c == nc - 1)
    def _():
        sfin_ref[...] = s_ref[...]


def gdn_prompt(z, gb, conv_w, gdn_norm, bn, t):
    rt, cs = GDN_RT, GDN_CHUNK
    n_rt = t // rt
    m = bn * t
    hcol = GDN_DK // LANES
    qkv = lambda part: pl.BlockSpec((rt, GDN_DK), lambda h, b, r, part=part: (b * n_rt + r, part * GDN_HEADS + h))
    prev = lambda part: pl.BlockSpec(
        (SUBLANES, GDN_DK),
        lambda h, b, r, part=part: (jnp.maximum((b * n_rt + r) * (rt // SUBLANES) - 1, 0), part * GDN_HEADS + h))
    cw = lambda part: pl.BlockSpec((GDN_CONV, GDN_DK), lambda h, b, r, part=part: (0, part * GDN_HEADS + h))
    per_row = lambda width: pl.BlockSpec((None, rt, width), lambda h, b, r: (h, b * n_rt + r, 0))
    w_c, u_c, aqk, qg, kg, gl = pl.pallas_call(
        _gdn_prep_body,
        grid=(GDN_HEADS, bn, n_rt),
        in_specs=[qkv(0), qkv(1), qkv(2), prev(0), prev(1), prev(2), per_row(4), cw(0), cw(1), cw(2)],
        out_specs=[per_row(GDN_DK), per_row(GDN_DV), per_row(cs), per_row(GDN_DK), per_row(GDN_DK),
                   pl.BlockSpec((None, rt // cs, GDN_DK), lambda h, b, r: (h, b * n_rt + r, 0))],
        out_shape=[jax.ShapeDtypeStruct((GDN_HEADS, m, GDN_DK), bf16),
                   jax.ShapeDtypeStruct((GDN_HEADS, m, GDN_DV), f32),
                   jax.ShapeDtypeStruct((GDN_HEADS, m, cs), bf16),
                   jax.ShapeDtypeStruct((GDN_HEADS, m, GDN_DK), bf16),
                   jax.ShapeDtypeStruct((GDN_HEADS, m, GDN_DK), bf16),
                   jax.ShapeDtypeStruct((GDN_HEADS, m // cs, GDN_DK), f32)],
        scratch_shapes=[pltpu.VMEM((rt + SUBLANES, GDN_DK), f32)] * 3,
        compiler_params=_cparams(("parallel", "parallel", "parallel")),
        name="gdn_prep",
    )(z, z, z, z, z, z, gb, conv_w, conv_w, conv_w)

    nc = t // cs
    n_gl = rt // cs
    heads = lambda width: pl.BlockSpec((GDN_HEADS, cs, width), lambda b, c: (0, b * nc + c, 0))
    o, s_fin = pl.pallas_call(
        _gdn_scan_body,
        grid=(bn, nc),
        in_specs=[heads(GDN_DK), heads(GDN_DV), heads(cs), heads(GDN_DK), heads(GDN_DK),
                  pl.BlockSpec((GDN_HEADS, n_gl, GDN_DK), lambda b, c: (0, (b * nc + c) // n_gl, 0)),
                  pl.BlockSpec((cs, GDN_V_W), lambda b, c: (b * nc + c, Z_ZG // GDN_V_W)),
                  pl.BlockSpec((1, GDN_DV), lambda b, c: (0, 0))],
        out_specs=[pl.BlockSpec((cs, GDN_V_W), lambda b, c: (b * nc + c, 0)),
                   pl.BlockSpec((None, GDN_HEADS, GDN_DK, GDN_DV), lambda b, c: (b, 0, 0, 0))],
        out_shape=[jax.ShapeDtypeStruct((m, GDN_V_W), bf16),
                   jax.ShapeDtypeStruct((bn, GDN_HEADS, GDN_DK, GDN_DV), f32)],
        scratch_shapes=[pltpu.VMEM((GDN_HEADS, GDN_DK, GDN_DV), f32)],
        compiler_params=_cparams(("parallel", "arbitrary")),
        name="gdn_scan",
    )(w_c, u_c, aqk, qg, kg, gl, z, gdn_norm.reshape(1, GDN_DV))
    return o, s_fin


GDN_SB = 8


def _gdn_sample_body(x_ref, za_ref, gb_ref, sc_ref, s0_ref, cw_ref, gn_ref, o_ref, s_ref):
    sb = GDN_SB
    x3 = sc_ref[...]
    y = x_ref[...] * cw_ref[GDN_CONV - 1:GDN_CONV, :]
    for j in range(GDN_CONV - 1):
        y = y + x3[:, j, :] * cw_ref[j:j + 1, :]
    xc = y * jax.nn.sigmoid(y)
    for h in range(GDN_HEADS):
        q = xc[:, h * GDN_DK:(h + 1) * GDN_DK]
        k = xc[:, GDN_QK_W + h * GDN_DK:GDN_QK_W + (h + 1) * GDN_DK]
        v = xc[:, 2 * GDN_QK_W + h * GDN_DV:2 * GDN_QK_W + (h + 1) * GDN_DV]
        q = q * lax.rsqrt(jnp.sum(q * q, -1, keepdims=True) + EPS) * (GDN_DK ** -0.5)
        k = k * lax.rsqrt(jnp.sum(k * k, -1, keepdims=True) + EPS)
        gb = gb_ref[h]
        x = gb[:, 1:2] + gb[:, 2:3]
        softplus = jnp.maximum(x, 0.0) + jnp.log1p(jnp.exp(-jnp.abs(x)))
        eg = jnp.broadcast_to(jnp.exp(-jnp.exp(gb[:, 3:4]) * softplus), (sb, GDN_DV))
        beta = jnp.broadcast_to(jax.nn.sigmoid(gb[:, 0:1]), (sb, GDN_DV))
        qk = jnp.broadcast_to(jnp.sum(q * k, -1, keepdims=True), (sb, GDN_DV))
        q_t = q.T
        k_t = k.T
        o_rows = []
        for i in range(sb):
            s = s0_ref[i, h]
            kcol = k_t[:, i:i + 1]
            k_s = jnp.sum(kcol * s, axis=0, keepdims=True)
            q_s = jnp.sum(q_t[:, i:i + 1] * s, axis=0, keepdims=True)
            eg_i = eg[i:i + 1]
            v_new = beta[i:i + 1] * (v[i:i + 1] - eg_i * k_s)
            o_rows.append(eg_i * q_s + qk[i:i + 1] * v_new)
            s_ref[i, h] = eg_i * s + kcol * v_new
        o = jnp.concatenate(o_rows, axis=0)
        on = o * lax.rsqrt(jnp.mean(o * o, -1, keepdims=True) + EPS) * gn_ref[...]
        za = za_ref[:, h * GDN_DV:(h + 1) * GDN_DV]
        o_ref[:, h * GDN_DV:(h + 1) * GDN_DV] = on * (za * jax.nn.sigmoid(za))


def gdn_sample(z, gb, state_conv, state_gdn, conv_w, gdn_norm, row0, bs):
    sb = GDN_SB
    r0 = row0 // sb
    return pl.pallas_call(
        _gdn_sample_body,
        grid=(bs // sb,),
        in_specs=[pl.BlockSpec((sb, CONV_CH), lambda i: (r0 + i, 0)),
                  pl.BlockSpec((sb, GDN_V_W), lambda i: (r0 + i, Z_ZG // GDN_V_W)),
                  pl.BlockSpec((GDN_HEADS, sb, 4), lambda i: (0, r0 + i, 0)),
                  pl.BlockSpec((sb, GDN_CONV - 1, CONV_CH), lambda i: (i, 0, 0)),
                  pl.BlockSpec((sb, GDN_HEADS, GDN_DK, GDN_DV), lambda i: (i, 0, 0, 0)),
                  pl.BlockSpec((GDN_CONV, CONV_CH), lambda i: (0, 0)),
                  pl.BlockSpec((1, GDN_DV), lambda i: (0, 0))],
        out_specs=[pl.BlockSpec((sb, GDN_V_W), lambda i: (i, 0)),
                   pl.BlockSpec((sb, GDN_HEADS, GDN_DK, GDN_DV), lambda i: (i, 0, 0, 0))],
        out_shape=[jax.ShapeDtypeStruct((bs, GDN_V_W), f32),
                   jax.ShapeDtypeStruct((bs, GDN_HEADS, GDN_DK, GDN_DV), f32)],
        compiler_params=_cparams(("parallel",)),
        name="gdn_sample",
    )(z, z, gb, state_conv, state_gdn, conv_w, gdn_norm.reshape(1, GDN_DV))


def _rmsnorm_j(x, g):
    xf = x.astype(f32)
    y = xf * lax.rsqrt(jnp.mean(xf * xf, -1, keepdims=True) + EPS)
    return (y * g.astype(f32)).astype(x.dtype)


def _l2norm(x):
    return x * lax.rsqrt(jnp.sum(x * x, -1, keepdims=True) + EPS)


def _masked_softmax(s, mask):
    s = jnp.where(mask, s.astype(f32), -jnp.inf)
    m = jnp.max(s, -1, keepdims=True)
    m = jnp.where(jnp.isfinite(m), m, 0.0)
    e = jnp.exp(s - m)
    return e / jnp.maximum(jnp.sum(e, -1, keepdims=True), 1e-30)


def _rel_bucket(dist):
    n = jnp.maximum(dist, 0)
    max_exact = REL_BUCKETS // 2
    nf = jnp.maximum(n, 1).astype(f32)
    large = max_exact + (jnp.log(nf / max_exact) / math.log(REL_MAX_DIST / max_exact)
                         * (REL_BUCKETS - max_exact)).astype(jnp.int32)
    large = jnp.minimum(large, REL_BUCKETS - 1)
    return jnp.where(n < max_exact, n, large)


def _causal_conv(x, buf, w):
    T = x.shape[1]
    xp = jnp.concatenate([buf.astype(x.dtype), x], axis=1)
    y = xp[:, 0:T] * w[0]
    for j in range(1, GDN_CONV):
        y = y + xp[:, j:j + T] * w[j]
    return jax.nn.silu(y), xp[:, T:]


def _gated_delta_rule(q, k, v, g, beta, s0):
    Bn, T, H, dk = q.shape
    dv = v.shape[-1]
    cs = min(GDN_CHUNK, T)
    nc = -(-T // cs)
    pad = nc * cs - T

    def prep(a):
        a = jnp.pad(a, [(0, 0), (0, pad)] + [(0, 0)] * (a.ndim - 2))
        a = a.reshape((Bn, nc, cs) + a.shape[2:])
        return jnp.moveaxis(a, 3, 1)

    q, k, v, g, beta = prep(q), prep(k), prep(v), prep(g), prep(beta)
    gc = jnp.cumsum(g, axis=-1)
    ar = jnp.arange(cs)
    incl = ar[:, None] >= ar[None, :]
    strict = ar[:, None] > ar[None, :]
    decay = jnp.exp(jnp.where(incl, gc[..., :, None] - gc[..., None, :], -jnp.inf))
    kb = k * beta[..., None]
    lmat = jnp.einsum("bhnid,bhnjd->bhnij", kb, k) * jnp.where(strict, decay, 0.0)
    a_mat = jnp.eye(cs, dtype=f32) + lmat
    rhs = jnp.concatenate([kb * jnp.exp(gc)[..., None], v * beta[..., None]], -1)
    wu = lax.linalg.triangular_solve(a_mat, rhs, left_side=True, lower=True, unit_diagonal=True)
    w_c, u_c = wu[..., :dk], wu[..., dk:]
    aqk = jnp.einsum("bhnid,bhnjd->bhnij", q, k) * decay
    qg = q * jnp.exp(gc)[..., None]
    kg = k * jnp.exp(gc[..., -1:] - gc)[..., None]
    glast = jnp.exp(gc[..., -1])
    xs = tuple(jnp.moveaxis(t, 2, 0) for t in (qg, kg, w_c, u_c, aqk, glast))

    def step(s, inp):
        qg_i, kg_i, w_i, u_i, aqk_i, gl_i = inp
        v_new = u_i - jnp.einsum("bhid,bhde->bhie", w_i, s)
        o = jnp.einsum("bhid,bhde->bhie", qg_i, s) + jnp.einsum("bhij,bhje->bhie", aqk_i, v_new)
        s = gl_i[..., None, None] * s + jnp.einsum("bhid,bhie->bhde", kg_i, v_new)
        return s, o

    s_fin, o = lax.scan(step, s0, xs)
    o = jnp.moveaxis(o, 0, 2).reshape(Bn, H, nc * cs, dv)[:, :, :T]
    return jnp.moveaxis(o, 1, 2), s_fin


def _gdn_mixer(qa, ka, va, za, b_logit, a_logit, conv_buf, s0, conv_w, dt_bias, a_log, head_gain):
    Bn, T, _ = qa.shape
    xc, new_buf = _causal_conv(jnp.concatenate([qa, ka, va], -1), conv_buf, conv_w)
    q, k, v = jnp.split(xc.astype(f32), [GDN_QK_W, 2 * GDN_QK_W], -1)
    q = _l2norm(q.reshape(Bn, T, GDN_HEADS, GDN_DK)) * (GDN_DK ** -0.5)
    k = _l2norm(k.reshape(Bn, T, GDN_HEADS, GDN_DK))
    v = v.reshape(Bn, T, GDN_HEADS, GDN_DV)
    g = -jnp.exp(a_log.astype(f32)) * jax.nn.softplus(a_logit.astype(f32) + dt_bias.astype(f32))
    beta = jax.nn.sigmoid(b_logit.astype(f32))
    o, s_new = _gated_delta_rule(q, k, v, g, beta, s0.astype(f32))
    o = _rmsnorm_j(o, head_gain) * jax.nn.silu(za.astype(f32).reshape(Bn, T, GDN_HEADS, GDN_DV))
    return o.reshape(Bn, T, GDN_V_W).astype(qa.dtype), new_buf, s_new


def _subblock_proj(rows, pe, w):
    r = CMP_BLOCK // CMP_STRIDE
    Bn, L = rows.shape[:2]
    ns = L // CMP_STRIDE
    sub = rows[:, :ns * CMP_STRIDE].reshape(Bn, ns, CMP_STRIDE, 2, NSA_KV_GROUPS, NSA_DH)
    w_r = w.reshape(2, r, CMP_STRIDE, NSA_DH, NSA_DH)
    pe_r = pe.reshape(2, r, CMP_STRIDE, NSA_DH)
    proj = jnp.einsum("bnsxgd,xmsde->mbnxge", sub, w_r)
    pe_term = jnp.einsum("xmsd,xmsde->xe", pe_r, w_r)
    return proj, pe_term


def _compress(proj, pe_term):
    r, Bn, ns = proj.shape[:3]
    nc = ns - r + 1
    kvc = proj[0][:, 0:nc]
    for m in range(1, r):
        kvc = kvc + proj[m][:, m:m + nc]
    kvc = kvc + pe_term[None, None, :, None, :]
    c_end = jnp.arange(nc, dtype=jnp.int32) * CMP_STRIDE + (CMP_BLOCK - 1)
    return kvc, c_end


def _cover_matrix(nc, ns):
    cs = np.arange(nc) * CMP_STRIDE
    ss = np.arange(ns) * SEL_BLOCK
    inter = np.minimum(cs[:, None] + CMP_BLOCK, ss[None, :] + SEL_BLOCK) - np.maximum(cs[:, None], ss[None, :])
    return jnp.asarray(np.clip(inter, 0, None) / CMP_BLOCK, dtype=f32)


def _nsa_core(q, q_pos, kvc, c_end, cover, gather_sel, kvw, w_pos, gates, rel_bias):
    Bn, Tq = q.shape[:2]
    scale = NSA_DH ** -0.5
    tbl = rel_bias.astype(f32).reshape(REL_BUCKETS, NSA_KV_GROUPS, NSA_HPG)
    dist_c = q_pos[:, None] - c_end[None, :]
    s_c = jnp.einsum("btghd,bngd->bghtn", q, kvc[:, :, 0]).astype(f32) * scale
    s_c = s_c + jnp.transpose(tbl[_rel_bucket(dist_c)], (2, 3, 0, 1))
    p_c = _masked_softmax(s_c, dist_c >= 0)
    o_c = jnp.einsum("bghtn,bngd->btghd", p_c, kvc[:, :, 1].astype(f32))
    ns = cover.shape[1]
    imp = jnp.einsum("bghtn,nm->bgtm", p_c, cover)
    blk = jnp.arange(ns, dtype=jnp.int32)[None, :]
    cur = (q_pos // SEL_BLOCK)[:, None]
    forced = (blk == 0) | (blk == cur) | (blk == cur - 1)
    valid = blk * SEL_BLOCK <= q_pos[:, None]
    score = jnp.where(valid, jnp.where(forced, FORCE_SCORE, imp), -1.0)
    _, idx = lax.top_k(score, min(SEL_TOPK, ns))
    n_sel = idx.shape[-1]
    sel = gather_sel(idx)
    pos = idx[..., None] * SEL_BLOCK + jnp.arange(SEL_BLOCK, dtype=jnp.int32)
    dist_s = q_pos[None, None, :, None, None] - pos
    gi = jnp.arange(NSA_KV_GROUPS)[None, :, None, None, None]
    bias_s = jnp.transpose(tbl, (1, 0, 2))[gi, _rel_bucket(dist_s)]
    s_s = jnp.einsum("btghd,bgtnkd->bghtnk", q, sel[..., 0, :]).astype(f32) * scale + jnp.moveaxis(bias_s, -1, 2)
    flat = (Bn, NSA_KV_GROUPS, NSA_HPG, Tq, n_sel * SEL_BLOCK)
    mask_s = (dist_s >= 0).reshape(Bn, NSA_KV_GROUPS, 1, Tq, n_sel * SEL_BLOCK)
    p_s = _masked_softmax(s_s.reshape(flat), mask_s).reshape(s_s.shape)
    o_s = jnp.einsum("bghtnk,bgtnkd->btghd", p_s, sel[..., 1, :].astype(f32))
    dist_w = q_pos[:, None] - w_pos[None, :]
    s_w = jnp.einsum("btghd,bsgd->bghts", q, kvw[:, :, 0]).astype(f32) * scale
    s_w = s_w + jnp.transpose(tbl[_rel_bucket(dist_w)], (2, 3, 0, 1))
    p_w = _masked_softmax(s_w, (dist_w >= 0) & (dist_w < WINDOW) & (w_pos[None, :] >= 0))
    o_w = jnp.einsum("bghts,bsgd->btghd", p_w, kvw[:, :, 1].astype(f32))
    gt = jax.nn.sigmoid(gates.astype(f32)).reshape(Bn, Tq, NSA_KV_GROUPS, NSA_HPG, 3)
    o = gt[..., 0:1] * o_c + gt[..., 1:2] * o_s + gt[..., 2:3] * o_w
    return o.reshape(Bn, Tq, NSA_Q_W).astype(q.dtype)


def _nsa_prompt(q, kv_c, kv_s, kv_w, gates, cmp_pe, cmp_w, rel_bias):
    Bn, T = q.shape[:2]
    proj, pe_term = _subblock_proj(kv_c, cmp_pe, cmp_w)
    kvc, c_end = _compress(proj, pe_term)
    ns = -(-T // SEL_BLOCK)
    kvs_blk = jnp.pad(kv_s, [(0, 0), (0, ns * SEL_BLOCK - T), (0, 0), (0, 0), (0, 0)])
    kvs_blk = kvs_blk.reshape(Bn, ns, SEL_BLOCK, 2, NSA_KV_GROUPS, NSA_DH)
    bi = jnp.arange(Bn)[:, None, None, None]
    gi = jnp.arange(NSA_KV_GROUPS)[None, :, None, None]

    def gather_sel(idx):
        return kvs_blk[bi, idx, :, :, gi, :]

    cover = _cover_matrix(kvc.shape[1], ns)
    kvw_pad = jnp.pad(kv_w, [(0, 0), (WINDOW, 0), (0, 0), (0, 0), (0, 0)])

    def one_block(i):
        start = i * Q_BLOCK
        qb = lax.dynamic_slice_in_dim(q, start, Q_BLOCK, 1)
        gb = lax.dynamic_slice_in_dim(gates, start, Q_BLOCK, 1)
        kwb = lax.dynamic_slice_in_dim(kvw_pad, start, WINDOW + Q_BLOCK, 1)
        q_pos = start + jnp.arange(Q_BLOCK, dtype=jnp.int32)
        w_pos = start - WINDOW + jnp.arange(WINDOW + Q_BLOCK, dtype=jnp.int32)
        return _nsa_core(qb, q_pos, kvc, c_end, cover, gather_sel, kwb, w_pos, gb, rel_bias)

    o = lax.map(one_block, jnp.arange(T // Q_BLOCK, dtype=jnp.int32))
    o = jnp.moveaxis(o, 0, 1).reshape(Bn, T, NSA_Q_W)
    return o, kv_w[:, T - min(WINDOW, T):]


def _nsa_sample(q, kv_c, kv_s, kv_w, gates, cache_c, cache_s, page_table, win_buf, cmp_pe, cmp_w, rel_bias):
    Bn, T = q.shape[:2]
    past = page_table.shape[1] * PAGE_SIZE
    past_c = cache_c[page_table].reshape(Bn, past, 2, NSA_KV_GROUPS, NSA_DH)
    proj, pe_term = _subblock_proj(past_c, cmp_pe, cmp_w)
    kvc, c_end = _compress(proj, pe_term)
    bpp = PAGE_SIZE // SEL_BLOCK
    n_past_blk = past // SEL_BLOCK
    n_new_blk = -(-T // SEL_BLOCK)
    pool = cache_s.reshape(cache_s.shape[0] * bpp, SEL_BLOCK, 2, NSA_KV_GROUPS, NSA_DH)
    new_blk = jnp.pad(kv_s, [(0, 0), (0, n_new_blk * SEL_BLOCK - T), (0, 0), (0, 0), (0, 0)])
    new_blk = new_blk.reshape(Bn, n_new_blk, SEL_BLOCK, 2, NSA_KV_GROUPS, NSA_DH)
    bi = jnp.arange(Bn)[:, None, None, None]
    gi = jnp.arange(NSA_KV_GROUPS)[None, :, None, None]

    def gather_sel(idx):
        in_past = idx < n_past_blk
        jp = jnp.minimum(idx, n_past_blk - 1)
        phys = page_table[bi, jp // bpp] * bpp + jp % bpp
        from_pool = pool[phys, :, :, gi, :]
        jn = jnp.clip(idx - n_past_blk, 0, n_new_blk - 1)
        from_new = new_blk[bi, jn, :, :, gi, :]
        return jnp.where(in_past[..., None, None, None], from_pool, from_new.astype(from_pool.dtype))

    cover = _cover_matrix(kvc.shape[1], n_past_blk + n_new_blk)
    w_buf = win_buf.shape[1]
    kvw = jnp.concatenate([win_buf.astype(kv_w.dtype), kv_w], axis=1)
    w_pos = past - w_buf + jnp.arange(w_buf + T, dtype=jnp.int32)
    q_pos = past + jnp.arange(T, dtype=jnp.int32)
    o = _nsa_core(q, q_pos, kvc, c_end, cover, gather_sel, kvw, w_pos, gates, rel_bias)
    return o, kvw[:, T:]


def kernel(x_prompt, x_sample, p_prompt, p_sample, cache_cmp_kv, cache_slc_kv, page_table, state_win_kv, state_gdn, state_conv, g_mix, w_in, gdn_conv_w, gdn_dt_bias, gdn_a_log, gdn_norm, cmp_pe, cmp_w, rel_bias, w_proj_a, w_proj_b, w_out, g_ffn, w_router_group, b_router_group, w_router_expert, b_router_expert, w_gate, w_up, w_down, g_ple, w_ple_gate, w_ple_proj, g_final):
    bp, tp, d = x_prompt.shape
    bs, ts, _ = x_sample.shape
    n_p, n_s = bp * tp, bs * ts
    n_real = n_p + n_s
    mp = -(-n_real // ROW_ALIGN) * ROW_ALIGN
    pad = mp - n_real

    h = jnp.concatenate([x_prompt.reshape(n_p, d), x_sample.reshape(n_s, d), jnp.zeros((pad, d), f32)], 0)
    ple = jnp.concatenate([p_prompt[0].reshape(n_p, -1), p_sample[0].reshape(n_s, -1),
                           jnp.zeros((pad, p_prompt.shape[-1]), f32)], 0).astype(bf16)

    w = w_in[0]
    o_beta = 4 * GDN_QK_W
    o_nq = o_beta + 2 * GDN_HEADS
    o_gate = o_nq + NSA_Q_W + 6 * NSA_KV_W
    o_ga = o_gate + 3 * NSA_HEADS
    w_main = jnp.concatenate([w[:, :o_beta], w[:, o_nq:o_gate], w[:, o_ga:]], axis=1)
    n_small = 2 * GDN_HEADS + 3 * NSA_HEADS
    w_small = jnp.concatenate([w[:, o_beta:o_nq], w[:, o_gate:o_ga], jnp.zeros((d, LANES - n_small), f32)], axis=1)

    a = rmsnorm_rows(h, g_mix[0], bf16)
    z = proj_matmul(a, w_main, f32, TN_DENSE)
    zs = proj_matmul(a, w_small, f32, LANES)

    def rows(x, lo, hi, which):
        if which == "p":
            return x[:n_p, lo:hi].reshape(bp, tp, hi - lo)
        return x[n_p:n_real, lo:hi].reshape(bs, ts, hi - lo)

    assert ts == 1 and tp % GDN_RT == 0 and n_p % GDN_SB == 0 and bs % GDN_SB == 0

    gb = jnp.stack([zs[:, 0:GDN_HEADS], zs[:, GDN_HEADS:2 * GDN_HEADS],
                    jnp.broadcast_to(gdn_dt_bias[0], (mp, GDN_HEADS)),
                    jnp.broadcast_to(gdn_a_log[0], (mp, GDN_HEADS))], -1)
    gb = jnp.transpose(gb, (1, 0, 2))
    gates_g = jnp.transpose(zs[:, 2 * GDN_HEADS:n_small].reshape(mp, NSA_KV_GROUPS, 3 * NSA_HPG), (1, 0, 2))

    outs = {}
    kv = rows(z, Z_KV, Z_GA, "p").reshape(bp, tp, 3, 2, NSA_KV_GROUPS, NSA_DH)
    kvc = compress_prompt(z, cmp_w[0], cmp_pe[0], bp, tp)
    o_b_p = nsa_prompt_attention(z, gates_g, kvc, rel_bias, bp, tp)
    o_a_p, s_p = gdn_prompt(z, gb, gdn_conv_w[0], gdn_norm[0], bp, tp)
    conv_p = rows(z, 0, CONV_CH, "p")[:, tp - (GDN_CONV - 1):]
    outs["p"] = (kv[:, :, 0], kv[:, :, 1], kv[:, :, 2][:, tp - min(WINDOW, tp):], s_p, conv_p)
    kv = rows(z, Z_KV, Z_GA, "s").reshape(bs, ts, 3, 2, NSA_KV_GROUPS, NSA_DH)
    q = rows(z, Z_NQ, Z_KV, "s").reshape(bs, ts, NSA_KV_GROUPS, NSA_HPG, NSA_DH)
    gates = rows(zs, 2 * GDN_HEADS, n_small, "s").reshape(bs, ts, NSA_HEADS, 3)
    o_b_s, win_s = _nsa_sample(q, kv[:, :, 0], kv[:, :, 1], kv[:, :, 2], gates, cache_cmp_kv[0], cache_slc_kv[0],
                               page_table, state_win_kv[0], cmp_pe[0], cmp_w[0], rel_bias)
    o_a_s, s_s = gdn_sample(z, gb, state_conv[0], state_gdn[0], gdn_conv_w[0], gdn_norm[0], n_p, bs)
    conv_s = jnp.concatenate([state_conv[0][:, 1:], rows(z, 0, CONV_CH, "s")], axis=1)
    outs["s"] = (kv[:, :, 0], kv[:, :, 1], win_s, s_s, conv_s)

    o_a = jnp.concatenate([o_a_p, o_a_s.astype(bf16), jnp.zeros((pad, GDN_V_W), bf16)], 0)
    o_b = jnp.concatenate([o_b_p, o_b_s.reshape(n_s, NSA_Q_W).astype(bf16), jnp.zeros((pad, NSA_Q_W), bf16)], 0)
    merged = merge_matmul(o_a, o_b, z, w_proj_a[0], w_proj_b[0])
    h = resid_matmul(merged, w_out[0], h)
    h = hier_moe(h, n_real, g_ffn[0], w_router_group[0], b_router_group[0], w_router_expert[0],
                 b_router_expert[0], w_gate[0], w_up[0], w_down[0])
    n3 = rmsnorm_rows(h, g_ple[0], bf16)
    h = ple_matmul(n3, w_ple_gate[0], ple, w_ple_proj[0], h)
    y = rmsnorm_rows(h, g_final, f32)
    y_prompt = y[:n_p].reshape(bp, tp, d)
    y_sample = y[n_p:n_real].reshape(bs, ts, d)
    st_p, st_s = outs["p"], outs["s"]
    return (y_prompt, y_sample) + tuple(t[None] for t in st_p) + tuple(t[None] for t in st_s)
```

```python
import functools
import math

import jax
import jax.numpy as jnp
import numpy as np
from jax import lax
from jax.experimental import pallas as pl
from jax.experimental.pallas import tpu as pltpu

D_MODEL = 4096
GDN_HEADS = 16
GDN_DK = 128
GDN_DV = 128
GDN_CONV = 4
GDN_CHUNK = 64
NSA_HEADS = 16
NSA_KV_GROUPS = 4
NSA_HPG = NSA_HEADS // NSA_KV_GROUPS
NSA_DH = 128
CMP_BLOCK = 32
CMP_STRIDE = 16
SEL_BLOCK = 64
SEL_TOPK = 16
WINDOW = 512
Q_BLOCK = 128
FORCE_SCORE = 1.0e4
REL_BUCKETS = 32
REL_MAX_DIST = 1024
PAGE_SIZE = 128
MOE_GROUPS = 4
MOE_PER_GROUP = 8
MOE_EXPERTS = MOE_GROUPS * MOE_PER_GROUP
MOE_TOPK = 2
EXPERT_HIDDEN = 512
EPS = 1e-6

GDN_QK_W = GDN_HEADS * GDN_DK
GDN_V_W = GDN_HEADS * GDN_DV
CONV_CH = 2 * GDN_QK_W + GDN_V_W
NSA_Q_W = NSA_HEADS * NSA_DH
NSA_KV_W = NSA_KV_GROUPS * NSA_DH

LANES = 128
SUBLANES = 8
VMEM_LIMIT = 56 * 1024 * 1024

ROW_ALIGN = 768
TM_DENSE = 768
TN_DENSE = 512
TM_ROWS = 256

Z_Q, Z_K, Z_V, Z_ZG = 0, 2048, 4096, 6144
Z_NQ = 8192
Z_KV = 10240
Z_GA = 13312
Z_GB = 17408
Z_COLS = 21504

bf16 = jnp.bfloat16
f32 = jnp.float32


def _cparams(sem):
    return pltpu.CompilerParams(dimension_semantics=sem, vmem_limit_bytes=VMEM_LIMIT)


def _rmsnorm_body(x_ref, g_ref, o_ref):
    x = x_ref[...]
    y = x * lax.rsqrt(jnp.mean(x * x, -1, keepdims=True) + EPS)
    o_ref[...] = (y * g_ref[...]).astype(o_ref.dtype)


def rmsnorm_rows(x, g, out_dtype):
    m, d = x.shape
    return pl.pallas_call(
        _rmsnorm_body,
        grid=(m // TM_ROWS,),
        in_specs=[pl.BlockSpec((TM_ROWS, d), lambda i: (i, 0)),
                  pl.BlockSpec((1, d), lambda i: (0, 0))],
        out_specs=pl.BlockSpec((TM_ROWS, d), lambda i: (i, 0)),
        out_shape=jax.ShapeDtypeStruct((m, d), out_dtype),
        compiler_params=_cparams(("parallel",)),
        name="rmsnorm_rows",
    )(x, g.reshape(1, d))


def _cast_body(x_ref, o_ref):
    o_ref[...] = x_ref[...].astype(o_ref.dtype)


def cast_bf16(x2d, tr):
    r, c = x2d.shape
    return pl.pallas_call(
        _cast_body,
        grid=(r // tr,),
        in_specs=[pl.BlockSpec((tr, c), lambda i: (i, 0))],
        out_specs=pl.BlockSpec((tr, c), lambda i: (i, 0)),
        out_shape=jax.ShapeDtypeStruct((r, c), bf16),
        compiler_params=_cparams(("parallel",)),
        name="cast_bf16",
    )(x2d)


def _proj_body(a_ref, w_ref, o_ref, wb_ref):
    @pl.when(pl.program_id(1) == 0)
    def _():
        wb_ref[...] = w_ref[...].astype(bf16)
    o_ref[...] = jnp.dot(a_ref[...], wb_ref[...], preferred_element_type=f32).astype(o_ref.dtype)


def proj_matmul(a, w, out_dtype, tn):
    m, k = a.shape
    n = w.shape[1]
    return pl.pallas_call(
        _proj_body,
        grid=(n // tn, m // TM_DENSE),
        in_specs=[pl.BlockSpec((TM_DENSE, k), lambda j, i: (i, 0)),
                  pl.BlockSpec((k, tn), lambda j, i: (0, j))],
        out_specs=pl.BlockSpec((TM_DENSE, tn), lambda j, i: (i, j)),
        out_shape=jax.ShapeDtypeStruct((m, n), out_dtype),
        scratch_shapes=[pltpu.VMEM((k, tn), bf16)],
        compiler_params=_cparams(("arbitrary", "arbitrary")),
        name="proj_matmul",
    )(a, w)


def _merge_body(oa_ref, ob_ref, ga_ref, gb_ref, wa_ref, wb_ref, o_ref, wa_s, wb_s):
    @pl.when(pl.program_id(1) == 0)
    def _():
        wa_s[...] = wa_ref[...].astype(bf16)
        wb_s[...] = wb_ref[...].astype(bf16)
    pa = jnp.dot(oa_ref[...], wa_s[...], preferred_element_type=f32)
    pb = jnp.dot(ob_ref[...], wb_s[...], preferred_element_type=f32)
    o_ref[...] = (jax.nn.sigmoid(ga_ref[...]) * pa + jax.nn.sigmoid(gb_ref[...]) * pb).astype(o_ref.dtype)


def merge_matmul(o_a, o_b, z, w_a, w_b):
    m, ka = o_a.shape
    kb = o_b.shape[1]
    n = w_a.shape[1]
    tn = TN_DENSE
    ja, jb = Z_GA // tn, Z_GB // tn
    return pl.pallas_call(
        _merge_body,
        grid=(n // tn, m // TM_DENSE),
        in_specs=[pl.BlockSpec((TM_DENSE, ka), lambda j, i: (i, 0)),
                  pl.BlockSpec((TM_DENSE, kb), lambda j, i: (i, 0)),
                  pl.BlockSpec((TM_DENSE, tn), lambda j, i: (i, ja + j)),
                  pl.BlockSpec((TM_DENSE, tn), lambda j, i: (i, jb + j)),
                  pl.BlockSpec((ka, tn), lambda j, i: (0, j)),
                  pl.BlockSpec((kb, tn), lambda j, i: (0, j))],
        out_specs=pl.BlockSpec((TM_DENSE, tn), lambda j, i: (i, j)),
        out_shape=jax.ShapeDtypeStruct((m, n), bf16),
        scratch_shapes=[pltpu.VMEM((ka, tn), bf16), pltpu.VMEM((kb, tn), bf16)],
        compiler_params=_cparams(("arbitrary", "arbitrary")),
        name="merge_matmul",
    )(o_a, o_b, z, z, w_a, w_b)


def _resid_body(a_ref, w_ref, h_ref, o_ref, wb_ref):
    @pl.when(pl.program_id(1) == 0)
    def _():
        wb_ref[...] = w_ref[...].astype(bf16)
    o_ref[...] = h_ref[...] + jnp.dot(a_ref[...], wb_ref[...], preferred_element_type=f32)


def resid_matmul(a, w, h):
    m, k = a.shape
    n = w.shape[1]
    tn = TN_DENSE
    return pl.pallas_call(
        _resid_body,
        grid=(n // tn, m // TM_DENSE),
        in_specs=[pl.BlockSpec((TM_DENSE, k), lambda j, i: (i, 0)),
                  pl.BlockSpec((k, tn), lambda j, i: (0, j)),
                  pl.BlockSpec((TM_DENSE, tn), lambda j, i: (i, j))],
        out_specs=pl.BlockSpec((TM_DENSE, tn), lambda j, i: (i, j)),
        out_shape=jax.ShapeDtypeStruct((m, n), f32),
        scratch_shapes=[pltpu.VMEM((k, tn), bf16)],
        compiler_params=_cparams(("arbitrary", "arbitrary")),
        name="resid_matmul",
    )(a, w, h)


def _ple_body(a_ref, w_ref, p_ref, wp_ref, h_ref, o_ref, wb_ref, wpb_ref):
    @pl.when(pl.program_id(1) == 0)
    def _():
        wb_ref[...] = w_ref[...].astype(bf16)
        wpb_ref[...] = wp_ref[...].astype(bf16)
    gate = jax.nn.sigmoid(jnp.dot(a_ref[...], wb_ref[...], preferred_element_type=f32))
    emb = jnp.dot(p_ref[...], wpb_ref[...], preferred_element_type=f32)
    o_ref[...] = h_ref[...] + gate * emb


def ple_matmul(a, w_gate, p, w_proj, h):
    m, k = a.shape
    kp = p.shape[1]
    n = w_gate.shape[1]
    tn = TN_DENSE
    return pl.pallas_call(
        _ple_body,
        grid=(n // tn, m // TM_DENSE),
        in_specs=[pl.BlockSpec((TM_DENSE, k), lambda j, i: (i, 0)),
                  pl.BlockSpec((k, tn), lambda j, i: (0, j)),
                  pl.BlockSpec((TM_DENSE, kp), lambda j, i: (i, 0)),
                  pl.BlockSpec((kp, tn), lambda j, i: (0, j)),
                  pl.BlockSpec((TM_DENSE, tn), lambda j, i: (i, j))],
        out_specs=pl.BlockSpec((TM_DENSE, tn), lambda j, i: (i, j)),
        out_shape=jax.ShapeDtypeStruct((m, n), f32),
        scratch_shapes=[pltpu.VMEM((k, tn), bf16), pltpu.VMEM((kp, tn), bf16)],
        compiler_params=_cparams(("arbitrary", "arbitrary")),
        name="ple_matmul",
    )(a, w_gate, p, w_proj, h)


def _router_body(h_ref, g_ref, wr_ref, br_ref, m_ref, r_ref):
    x = h_ref[...]
    y = x * lax.rsqrt(jnp.mean(x * x, -1, keepdims=True) + EPS) * g_ref[...]
    m_ref[...] = y.astype(bf16)
    logits = jnp.dot(y, wr_ref[...], preferred_element_type=f32,
                     precision=lax.Precision.HIGHEST) + br_ref[...]
    lane = lax.broadcasted_iota(jnp.int32, logits.shape, 1)
    neg = -jnp.inf
    lg = jnp.where(lane < MOE_GROUPS, logits, neg)
    eg = jnp.exp(lg - jnp.max(lg, -1, keepdims=True))
    pg = eg / jnp.sum(eg, -1, keepdims=True)
    pg_top = jnp.max(pg, -1, keepdims=True)
    g_idx = jnp.min(jnp.where(pg == pg_top, lane, LANES), -1, keepdims=True)
    lo = MOE_GROUPS + MOE_PER_GROUP * g_idx
    emask = (lane >= lo) & (lane < lo + MOE_PER_GROUP)
    le = jnp.where(emask, logits, neg)
    ee = jnp.exp(le - jnp.max(le, -1, keepdims=True))
    pe = jnp.where(emask, ee / jnp.sum(ee, -1, keepdims=True), -1.0)
    v1 = jnp.max(pe, -1, keepdims=True)
    i1 = jnp.min(jnp.where(pe == v1, lane, LANES), -1, keepdims=True)
    pe2 = jnp.where(lane == i1, -1.0, pe)
    v2 = jnp.max(pe2, -1, keepdims=True)
    i2 = jnp.min(jnp.where(pe2 == v2, lane, LANES), -1, keepdims=True)
    den = v1 + v2
    w1 = pg_top * v1 / den
    w2 = pg_top * v2 / den
    e1 = (i1 - MOE_GROUPS).astype(f32)
    e2 = (i2 - MOE_GROUPS).astype(f32)
    r_ref[...] = jnp.where(lane == 0, e1, jnp.where(lane == 1, e2,
                           jnp.where(lane == 2, w1, jnp.where(lane == 3, w2, 0.0))))


def moe_router(h, g_ffn, w_router, b_router):
    m, d = h.shape
    return pl.pallas_call(
        _router_body,
        grid=(m // TM_ROWS,),
        in_specs=[pl.BlockSpec((TM_ROWS, d), lambda i: (i, 0)),
                  pl.BlockSpec((1, d), lambda i: (0, 0)),
                  pl.BlockSpec((d, LANES), lambda i: (0, 0)),
                  pl.BlockSpec((1, LANES), lambda i: (0, 0))],
        out_specs=[pl.BlockSpec((TM_ROWS, d), lambda i: (i, 0)),
                   pl.BlockSpec((TM_ROWS, LANES), lambda i: (i, 0))],
        out_shape=[jax.ShapeDtypeStruct((m, d), bf16),
                   jax.ShapeDtypeStruct((m, LANES), f32)],
        compiler_params=_cparams(("parallel",)),
        name="moe_router",
    )(h, g_ffn.reshape(1, d), w_router, b_router)


def _expert_body(te_ref, tv_ref, x_ref, rw_ref, wg_ref, wu_ref, wd_ref, o_ref):
    t = pl.program_id(0)

    @pl.when(tv_ref[t] > 0)
    def _():
        x = x_ref[...]
        gate = jnp.dot(x, wg_ref[...], preferred_element_type=f32)
        up = jnp.dot(x, wu_ref[...], preferred_element_type=f32)
        hid = (jax.nn.silu(gate) * up * rw_ref[...]).astype(bf16)
        o_ref[...] = jnp.dot(hid, wd_ref[...], preferred_element_type=f32)

    @pl.when(tv_ref[t] == 0)
    def _():
        o_ref[...] = jnp.zeros_like(o_ref)


def expert_matmul(tile_expert, tile_valid, xs, row_w, wg, wu, wd):
    r, d = xs.shape
    f = wg.shape[2]
    n_tiles = r // TM_ROWS
    grid_spec = pltpu.PrefetchScalarGridSpec(
        num_scalar_prefetch=2,
        grid=(n_tiles,),
        in_specs=[pl.BlockSpec((TM_ROWS, d), lambda t, te, tv: (t, 0)),
                  pl.BlockSpec((TM_ROWS, 1), lambda t, te, tv: (t, 0)),
                  pl.BlockSpec((None, d, f), lambda t, te, tv: (te[t], 0, 0)),
                  pl.BlockSpec((None, d, f), lambda t, te, tv: (te[t], 0, 0)),
                  pl.BlockSpec((None, f, d), lambda t, te, tv: (te[t], 0, 0))],
        out_specs=pl.BlockSpec((TM_ROWS, d), lambda t, te, tv: (t, 0)),
    )
    return pl.pallas_call(
        _expert_body,
        grid_spec=grid_spec,
        out_shape=jax.ShapeDtypeStruct((r, d), f32),
        compiler_params=_cparams(("arbitrary",)),
        name="expert_matmul",
    )(tile_expert, tile_valid, xs, row_w, wg, wu, wd)


def hier_moe(h, n_real, g_ffn, w_rg, b_rg, w_re, b_re, w_gate, w_up, w_down):
    mp, d = h.shape
    n_route = MOE_GROUPS + MOE_EXPERTS
    w_router = jnp.zeros((d, LANES), f32).at[:, :MOE_GROUPS].set(w_rg).at[:, MOE_GROUPS:n_route].set(w_re)
    b_router = jnp.zeros((1, LANES), f32).at[0, :MOE_GROUPS].set(b_rg).at[0, MOE_GROUPS:n_route].set(b_re)
    m_bf, slab = moe_router(h, g_ffn, w_router, b_router)
    ids = slab[:n_real, 0:2].astype(jnp.int32)
    wts = slab[:n_real, 2:4]

    tm = TM_ROWS
    n_assign = n_real * MOE_TOPK
    n_slots = -(-(n_assign + MOE_EXPERTS * (tm - 1)) // tm) * tm
    e_flat = ids.reshape(-1)
    order = jnp.argsort(e_flat, stable=True)
    e_sorted = e_flat[order]
    counts = jnp.sum(jax.nn.one_hot(e_flat, MOE_EXPERTS, dtype=jnp.int32), axis=0)
    padded = -(-counts // tm) * tm
    start_p = jnp.cumsum(padded) - padded
    start = jnp.cumsum(counts) - counts
    slot_sorted = start_p[e_sorted] + (jnp.arange(n_assign, dtype=jnp.int32) - start[e_sorted])
    slot_of = jnp.zeros((n_assign,), jnp.int32).at[order].set(slot_sorted.astype(jnp.int32))
    src_tok = jnp.zeros((n_slots,), jnp.int32).at[slot_of].set(jnp.arange(n_assign, dtype=jnp.int32) // MOE_TOPK)
    row_w = jnp.zeros((n_slots,), f32).at[slot_of].set(wts.reshape(-1))
    tile_start = jnp.arange(n_slots // tm, dtype=jnp.int32) * tm
    ends = jnp.cumsum(padded)
    tile_expert = jnp.minimum(jnp.searchsorted(ends, tile_start, side="right"), MOE_EXPERTS - 1).astype(jnp.int32)
    tile_valid = (tile_start < ends[-1]).astype(jnp.int32)

    xs = jnp.take(m_bf, src_tok, axis=0)
    wg = cast_bf16(w_gate.reshape(MOE_EXPERTS * d, EXPERT_HIDDEN), 4096).reshape(MOE_EXPERTS, d, EXPERT_HIDDEN)
    wu = cast_bf16(w_up.reshape(MOE_EXPERTS * d, EXPERT_HIDDEN), 4096).reshape(MOE_EXPERTS, d, EXPERT_HIDDEN)
    wd = cast_bf16(w_down.reshape(MOE_EXPERTS * EXPERT_HIDDEN, d), 512).reshape(MOE_EXPERTS, EXPERT_HIDDEN, d)
    ys = expert_matmul(tile_expert, tile_valid, xs, row_w.reshape(n_slots, 1), wg, wu, wd)
    slot2 = slot_of.reshape(n_real, MOE_TOPK)
    y = jnp.take(ys, slot2[:, 0], axis=0) + jnp.take(ys, slot2[:, 1], axis=0)
    return h.at[:n_real].add(y)


NEG_BIG = -1e30
N_BIAS_TILES = 11
_NT = (((1,), (1,)), ((), ()))


def _compress_body(x_ref, w_ref, pe_ref, o_ref):
    ns = o_ref.shape[0]
    acc0 = jnp.zeros((ns, NSA_DH), f32)
    acc1 = jnp.zeros((ns, NSA_DH), f32)
    for s in range(CMP_STRIDE):
        xs = x_ref[pl.ds(s, ns, stride=CMP_STRIDE), :]
        a0 = (xs + pe_ref[s:s + 1, :]).astype(bf16)
        a1 = (xs + pe_ref[CMP_STRIDE + s:CMP_STRIDE + s + 1, :]).astype(bf16)
        acc0 = acc0 + jnp.dot(a0, w_ref[s].astype(bf16), preferred_element_type=f32)
        acc1 = acc1 + jnp.dot(a1, w_ref[CMP_STRIDE + s].astype(bf16), preferred_element_type=f32)
    o_ref[...] = (acc0 + pltpu.roll(acc1, ns - 1, axis=0)).astype(o_ref.dtype)


def compress_prompt(z, cmp_w, cmp_pe, bn, t):
    ns = t // CMP_STRIDE
    col0 = Z_KV // NSA_DH
    return pl.pallas_call(
        _compress_body,
        grid=(bn, 2, NSA_KV_GROUPS),
        in_specs=[pl.BlockSpec((t, NSA_DH), lambda b, x, g: (b, col0 + x * NSA_KV_GROUPS + g)),
                  pl.BlockSpec((None, CMP_BLOCK, NSA_DH, NSA_DH), lambda b, x, g: (x, 0, 0, 0)),
                  pl.BlockSpec((None, CMP_BLOCK, NSA_DH), lambda b, x, g: (x, 0, 0))],
        out_specs=pl.BlockSpec((None, None, None, ns, NSA_DH), lambda b, x, g: (b, x, g, 0, 0)),
        out_shape=jax.ShapeDtypeStruct((bn, 2, NSA_KV_GROUPS, ns, NSA_DH), bf16),
        compiler_params=_cparams(("parallel", "parallel", "parallel")),
        name="compress_prompt",
    )(z, cmp_w, cmp_pe)


def _online_softmax_step(q, k, v, extra, m_ref, l_ref, acc_ref):
    s = lax.dot_general(q, k, _NT, preferred_element_type=f32) * (NSA_DH ** -0.5) + extra
    m_old = m_ref[...]
    m_new = jnp.maximum(m_old, jnp.max(s, -1, keepdims=True))
    alpha = jnp.exp(m_old - m_new)
    p = jnp.exp(s - m_new)
    l_ref[...] = alpha * l_ref[...] + jnp.sum(p, -1, keepdims=True)
    acc_ref[...] = alpha * acc_ref[...] + jnp.dot(p.astype(bf16), v, preferred_element_type=f32)
    m_ref[...] = m_new


def _nsa_prompt_body(q_ref, kc_ref, vc_ref, ks_ref, vs_ref, kw_ref, vw_ref, gate_ref, bc_ref, bt_ref,
                     cov_ref, exp_ref, o_ref, mb_ref, m_ref, l_ref, acc_ref):
    g = pl.program_id(1)
    i = pl.program_id(2)
    qb = Q_BLOCK
    rows = NSA_HPG * qb
    qf = q_ref[...]
    q = jnp.concatenate([qf[:, h * NSA_DH:(h + 1) * NSA_DH] for h in range(NSA_HPG)], axis=0).astype(bf16)

    ncp = kc_ref.shape[0]
    n_done = (qb // CMP_STRIDE) * (i + 1)
    bias_c = pltpu.roll(bc_ref[...], n_done % ncp, axis=1)
    lane_c = lax.broadcasted_iota(jnp.int32, bias_c.shape, 1)
    bias_c = jnp.where(lane_c < n_done, bias_c, NEG_BIG)
    s = lax.dot_general(q, kc_ref[...], _NT, preferred_element_type=f32) * (NSA_DH ** -0.5) + bias_c
    m = jnp.max(s, -1, keepdims=True)
    m = jnp.where(m < 0.5 * NEG_BIG, 0.0, m)
    e = jnp.exp(s - m)
    p = e / jnp.maximum(jnp.sum(e, -1, keepdims=True), 1e-30)
    o_c = jnp.dot(p.astype(bf16), vc_ref[...], preferred_element_type=f32)

    psum = p[0:qb]
    for h in range(1, NSA_HPG):
        psum = psum + p[h * qb:(h + 1) * qb]
    imp_t = lax.dot_general(cov_ref[...], psum, _NT, preferred_element_type=f32,
                            precision=lax.Precision.HIGHEST)
    nsb = imp_t.shape[0]
    blk = lax.broadcasted_iota(jnp.int32, (nsb, qb), 0)
    qpos = i * qb + lax.broadcasted_iota(jnp.int32, (nsb, qb), 1)
    cur = qpos // SEL_BLOCK
    forced = (blk == 0) | (blk == cur) | (blk == cur - 1)
    valid = blk * SEL_BLOCK <= qpos
    score = jnp.where(valid, jnp.where(forced, FORCE_SCORE, imp_t), -1.0)
    sel_t = jnp.zeros((nsb, qb), f32)
    for _ in range(min(SEL_TOPK, nsb)):
        mx = jnp.max(score, axis=0, keepdims=True)
        first = jnp.min(jnp.where(score == mx, blk, nsb), axis=0, keepdims=True)
        pick = blk == first
        sel_t = jnp.where(pick, 1.0, sel_t)
        score = jnp.where(pick, -2.0, score)
    unsel = ((sel_t - 1.0) * (-NEG_BIG)).T.astype(bf16)
    mb_ref[...] = jnp.dot(unsel, exp_ref[...], preferred_element_type=f32)

    def reset():
        m_ref[...] = jnp.full(m_ref.shape, NEG_BIG, f32)
        l_ref[...] = jnp.zeros(l_ref.shape, f32)
        acc_ref[...] = jnp.zeros(acc_ref.shape, f32)

    reset()

    def sel_step(j, carry):
        r0 = pl.multiple_of(j * qb, qb)
        k = ks_ref[pl.ds(r0, qb), :].astype(bf16)
        v = vs_ref[pl.ds(r0, qb), :].astype(bf16)
        mb = mb_ref[:, pl.ds(r0, qb)]
        extra = bt_ref[jnp.minimum(i - j, N_BIAS_TILES - 2)] + jnp.concatenate([mb] * NSA_HPG, axis=0)
        _online_softmax_step(q, k, v, extra, m_ref, l_ref, acc_ref)
        return carry

    lax.fori_loop(0, i + 1, sel_step, 0)
    o_s = acc_ref[...] / l_ref[...]

    reset()
    n_win = WINDOW // qb

    def win_step(j, carry):
        r0 = pl.multiple_of(j * qb, qb)
        k = kw_ref[pl.ds(r0, qb), :].astype(bf16)
        v = vw_ref[pl.ds(r0, qb), :].astype(bf16)
        d = i - j
        extra = bt_ref[jnp.where(d == n_win, N_BIAS_TILES - 1, d)]
        _online_softmax_step(q, k, v, extra, m_ref, l_ref, acc_ref)
        return carry

    lax.fori_loop(jnp.maximum(i - n_win, 0), i + 1, win_step, 0)
    o_w = acc_ref[...] / l_ref[...]

    gt = jax.nn.sigmoid(gate_ref[...])
    outs = []
    for h in range(NSA_HPG):
        c = 3 * h
        sl = slice(h * qb, (h + 1) * qb)
        outs.append(gt[:, c:c + 1] * o_c[sl] + gt[:, c + 1:c + 2] * o_s[sl] + gt[:, c + 2:c + 3] * o_w[sl])
    o_ref[...] = jnp.concatenate(outs, axis=1).astype(o_ref.dtype)


def _nsa_bias_tables(rel_bias, t):
    qb = Q_BLOCK
    nq = t // qb
    tbl = rel_bias.astype(f32)
    r = jnp.arange(qb, dtype=jnp.int32)[:, None]
    c = jnp.arange(qb, dtype=jnp.int32)[None, :]
    tiles = []
    for delta in range(N_BIAS_TILES - 1):
        dist = delta * qb + r - c
        tiles.append(jnp.where((dist >= 0)[..., None], tbl[_rel_bucket(dist)], NEG_BIG))
    dist = (WINDOW // qb) * qb + r - c
    tiles.append(jnp.where(((dist >= 0) & (dist < WINDOW))[..., None], tbl[_rel_bucket(dist)], NEG_BIG))
    bt = jnp.stack(tiles, 0)
    bt = bt.reshape(N_BIAS_TILES, qb, qb, NSA_KV_GROUPS, NSA_HPG)
    bt = jnp.transpose(bt, (3, 0, 4, 1, 2)).reshape(NSA_KV_GROUPS, N_BIAS_TILES, NSA_HPG * qb, qb)
    ncp = t // CMP_STRIDE
    m_back = (ncp - 1) - jnp.arange(ncp, dtype=jnp.int32)[None, :]
    dist_c = r + CMP_STRIDE * m_back + CMP_STRIDE - (qb + CMP_BLOCK - 1)
    bc = jnp.where((dist_c >= 0)[..., None], tbl[_rel_bucket(dist_c)], NEG_BIG)
    bc = bc.reshape(qb, ncp, NSA_KV_GROUPS, NSA_HPG)
    bc = jnp.transpose(bc, (2, 3, 0, 1)).reshape(NSA_KV_GROUPS, NSA_HPG * qb, ncp)
    return bt, bc


def nsa_prompt_attention(z, gates_g, kvc, rel_bias, bn, t):
    qb = Q_BLOCK
    nq = t // qb
    ncp = t // CMP_STRIDE
    nsb = t // SEL_BLOCK
    rows = NSA_HPG * qb
    bt, bc = _nsa_bias_tables(rel_bias, t)
    cover_t = jnp.concatenate([_cover_matrix(ncp - 1, nsb), jnp.zeros((1, nsb), f32)], 0).T
    expand = jnp.asarray(np.repeat(np.eye(nsb, dtype=np.float32), SEL_BLOCK, axis=1), bf16)
    kcol = Z_KV // NSA_DH
    kv_spec = lambda off: pl.BlockSpec((t, NSA_DH), lambda b, g, i, off=off: (b, kcol + off + g))
    return pl.pallas_call(
        _nsa_prompt_body,
        grid=(bn, NSA_KV_GROUPS, nq),
        in_specs=[pl.BlockSpec((qb, NSA_HPG * NSA_DH), lambda b, g, i: (b * nq + i, Z_NQ // (NSA_HPG * NSA_DH) + g)),
                  pl.BlockSpec((None, None, None, ncp, NSA_DH), lambda b, g, i: (b, 0, g, 0, 0)),
                  pl.BlockSpec((None, None, None, ncp, NSA_DH), lambda b, g, i: (b, 1, g, 0, 0)),
                  kv_spec(2 * NSA_KV_GROUPS), kv_spec(3 * NSA_KV_GROUPS),
                  kv_spec(4 * NSA_KV_GROUPS), kv_spec(5 * NSA_KV_GROUPS),
                  pl.BlockSpec((None, qb, 3 * NSA_HPG), lambda b, g, i: (g, b * nq + i, 0)),
                  pl.BlockSpec((None, rows, ncp), lambda b, g, i: (g, 0, 0)),
                  pl.BlockSpec((None, N_BIAS_TILES, rows, qb), lambda b, g, i: (g, 0, 0, 0)),
                  pl.BlockSpec((nsb, ncp), lambda b, g, i: (0, 0)),
                  pl.BlockSpec((nsb, t), lambda b, g, i: (0, 0))],
        out_specs=pl.BlockSpec((qb, NSA_HPG * NSA_DH), lambda b, g, i: (b * nq + i, g)),
        out_shape=jax.ShapeDtypeStruct((bn * t, NSA_Q_W), bf16),
        scratch_shapes=[pltpu.VMEM((qb, t), f32), pltpu.VMEM((rows, 1), f32), pltpu.VMEM((rows, 1), f32),
                        pltpu.VMEM((rows, NSA_DH), f32)],
        compiler_params=_cparams(("parallel", "parallel", "arbitrary")),
        name="nsa_prompt_attention",
    )(z, kvc, kvc, z, z, z, z, gates_g, bc, bt, cover_t, expand)


GDN_RT = 512
_HI = lax.Precision.HIGHEST


def _gdn_prep_body(q_ref, k_ref, v_ref, qp_ref, kp_ref, vp_ref, gb_ref, cwq_ref, cwk_ref, cwv_ref,
                   w_ref, u_ref, aqk_ref, qg_ref, kg_ref, gl_ref, qs_ref, ks_ref, vs_ref):
    rt = q_ref.shape[0]
    cs = GDN_CHUNK
    first = pl.program_id(2) == 0
    pad = SUBLANES

    def conv_silu(x_ref, prev_ref, scr_ref, cw_ref):
        scr_ref[0:pad, :] = jnp.where(first, 0.0, prev_ref[...])
        scr_ref[pad:pad + rt, :] = x_ref[...]
        y = scr_ref[pl.ds(pad - (GDN_CONV - 1), rt), :] * cw_ref[0:1, :]
        for j in range(1, GDN_CONV):
            y = y + scr_ref[pl.ds(pad - (GDN_CONV - 1) + j, rt), :] * cw_ref[j:j + 1, :]
        return y * jax.nn.sigmoid(y)

    q = conv_silu(q_ref, qp_ref, qs_ref, cwq_ref)
    k = conv_silu(k_ref, kp_ref, ks_ref, cwk_ref)
    v = conv_silu(v_ref, vp_ref, vs_ref, cwv_ref)
    q = q * lax.rsqrt(jnp.sum(q * q, -1, keepdims=True) + EPS) * (GDN_DK ** -0.5)
    k = k * lax.rsqrt(jnp.sum(k * k, -1, keepdims=True) + EPS)

    gb = gb_ref[...]
    x = gb[:, 1:2] + gb[:, 2:3]
    softplus = jnp.maximum(x, 0.0) + jnp.log1p(jnp.exp(-jnp.abs(x)))
    g = jnp.broadcast_to(-jnp.exp(gb[:, 3:4]) * softplus, (rt, GDN_DK))
    beta = jnp.broadcast_to(jax.nn.sigmoid(gb[:, 0:1]), (rt, GDN_DK))

    ri = lax.broadcasted_iota(jnp.int32, (cs, cs), 0)
    ci = lax.broadcasted_iota(jnp.int32, (cs, cs), 1)
    incl = ri >= ci
    strict = ri > ci
    tri = incl.astype(f32)
    eye = (ri == ci).astype(f32)
    gls = []
    for c in range(rt // cs):
        sl = slice(c * cs, (c + 1) * cs)
        qc, kc, vc, gc_, bc_ = q[sl], k[sl], v[sl], g[sl], beta[sl]
        gcum = jnp.dot(tri, gc_, preferred_element_type=f32, precision=_HI)
        dlt = jnp.dot(tri, jnp.where(strict, gc_[:, :cs], 0.0), preferred_element_type=f32, precision=_HI)
        dec = jnp.exp(jnp.where(incl, dlt, 0.0))
        eg = jnp.exp(gcum)
        g_last = gcum[cs - 1:cs, :]
        kb = kc * bc_
        kbf = kc.astype(bf16)
        lmat = lax.dot_general(kb.astype(bf16), kbf, _NT, preferred_element_type=f32) * jnp.where(strict, dec, 0.0)
        aqk = lax.dot_general(qc.astype(bf16), kbf, _NT, preferred_element_type=f32) * jnp.where(incl, dec, 0.0)
        inv = eye - lmat
        pw = lmat
        for _ in range(int(math.log2(cs)) - 1):
            pw = jnp.dot(pw, pw, preferred_element_type=f32, precision=_HI)
            inv = inv + jnp.dot(inv, pw, preferred_element_type=f32, precision=_HI)
        rhs = jnp.concatenate([kb * eg, vc * bc_], axis=1)
        wu = jnp.dot(inv, rhs, preferred_element_type=f32, precision=_HI)
        w_ref[sl, :] = wu[:, :GDN_DK].astype(w_ref.dtype)
        u_ref[sl, :] = wu[:, GDN_DK:]
        aqk_ref[sl, :] = aqk.astype(aqk_ref.dtype)
        qg_ref[sl, :] = (qc * eg).astype(qg_ref.dtype)
        kg_ref[sl, :] = (kc * jnp.exp(g_last - gcum)).astype(kg_ref.dtype)
        gls.append(jnp.exp(g_last))
    gl_ref[...] = jnp.concatenate(gls, axis=0)


def _gdn_scan_body(w_ref, u_ref, aqk_ref, qg_ref, kg_ref, gl_ref, za_ref, gn_ref, o_ref, sfin_ref, s_ref):
    c = pl.program_id(1)
    nc = pl.num_programs(1)
    n_gl = gl_ref.shape[1]

    @pl.when(c == 0)
    def _():
        s_ref[...] = jnp.zeros(s_ref.shape, f32)

    for h in range(GDN_HEADS):
        s = s_ref[h]
        sb = s.astype(bf16)
        v_new = u_ref[h] - jnp.dot(w_ref[h], sb, preferred_element_type=f32)
        vb = v_new.astype(bf16)
        o = (jnp.dot(qg_ref[h], sb, preferred_element_type=f32)
             + jnp.dot(aqk_ref[h], vb, preferred_element_type=f32))
        gl = gl_ref[h, pl.ds(c % n_gl, 1), :]
        s_ref[h] = gl * s + lax.dot_general(kg_ref[h], vb, (((0,), (0,)), ((), ())), preferred_element_type=f32)
        on = o * lax.rsqrt(jnp.mean(o * o, -1, keepdims=True) + EPS) * gn_ref[...]
        za = za_ref[:, h * GDN_DV:(h + 1) * GDN_DV]
        o_ref[:, h * GDN_DV:(h + 1) * GDN_DV] = (on * (za * jax.nn.sigmoid(za))).astype(o_ref.dtype)

    @pl.when(c == nc - 1)
    def _():
        sfin_ref[...] = s_ref[...]


def gdn_prompt(z, gb, conv_w, gdn_norm, bn, t):
    rt, cs = GDN_RT, GDN_CHUNK
    n_rt = t // rt
    m = bn * t
    hcol = GDN_DK // LANES
    qkv = lambda part: pl.BlockSpec((rt, GDN_DK), lambda h, b, r, part=part: (b * n_rt + r, part * GDN_HEADS + h))
    prev = lambda part: pl.BlockSpec(
        (SUBLANES, GDN_DK),
        lambda h, b, r, part=part: (jnp.maximum((b * n_rt + r) * (rt // SUBLANES) - 1, 0), part * GDN_HEADS + h))
    cw = lambda part: pl.BlockSpec((GDN_CONV, GDN_DK), lambda h, b, r, part=part: (0, part * GDN_HEADS + h))
    per_row = lambda width: pl.BlockSpec((None, rt, width), lambda h, b, r: (h, b * n_rt + r, 0))
    w_c, u_c, aqk, qg, kg, gl = pl.pallas_call(
        _gdn_prep_body,
        grid=(GDN_HEADS, bn, n_rt),
        in_specs=[qkv(0), qkv(1), qkv(2), prev(0), prev(1), prev(2), per_row(4), cw(0), cw(1), cw(2)],
        out_specs=[per_row(GDN_DK), per_row(GDN_DV), per_row(cs), per_row(GDN_DK), per_row(GDN_DK),
                   pl.BlockSpec((None, rt // cs, GDN_DK), lambda h, b, r: (h, b * n_rt + r, 0))],
        out_shape=[jax.ShapeDtypeStruct((GDN_HEADS, m, GDN_DK), bf16),
                   jax.ShapeDtypeStruct((GDN_HEADS, m, GDN_DV), f32),
                   jax.ShapeDtypeStruct((GDN_HEADS, m, cs), bf16),
                   jax.ShapeDtypeStruct((GDN_HEADS, m, GDN_DK), bf16),
                   jax.ShapeDtypeStruct((GDN_HEADS, m, GDN_DK), bf16),
                   jax.ShapeDtypeStruct((GDN_HEADS, m // cs, GDN_DK), f32)],
        scratch_shapes=[pltpu.VMEM((rt + SUBLANES, GDN_DK), f32)] * 3,
        compiler_params=_cparams(("parallel", "parallel", "parallel")),
        name="gdn_prep",
    )(z, z, z, z, z, z, gb, conv_w, conv_w, conv_w)

    nc = t // cs
    n_gl = rt // cs
    heads = lambda width: pl.BlockSpec((GDN_HEADS, cs, width), lambda b, c: (0, b * nc + c, 0))
    o, s_fin = pl.pallas_call(
        _gdn_scan_body,
        grid=(bn, nc),
        in_specs=[heads(GDN_DK), heads(GDN_DV), heads(cs), heads(GDN_DK), heads(GDN_DK),
                  pl.BlockSpec((GDN_HEADS, n_gl, GDN_DK), lambda b, c: (0, (b * nc + c) // n_gl, 0)),
                  pl.BlockSpec((cs, GDN_V_W), lambda b, c: (b * nc + c, Z_ZG // GDN_V_W)),
                  pl.BlockSpec((1, GDN_DV), lambda b, c: (0, 0))],
        out_specs=[pl.BlockSpec((cs, GDN_V_W), lambda b, c: (b * nc + c, 0)),
                   pl.BlockSpec((None, GDN_HEADS, GDN_DK, GDN_DV), lambda b, c: (b, 0, 0, 0))],
        out_shape=[jax.ShapeDtypeStruct((m, GDN_V_W), bf16),
                   jax.ShapeDtypeStruct((bn, GDN_HEADS, GDN_DK, GDN_DV), f32)],
        scratch_shapes=[pltpu.VMEM((GDN_HEADS, GDN_DK, GDN_DV), f32)],
        compiler_params=_cparams(("parallel", "arbitrary")),
        name="gdn_scan",
    )(w_c, u_c, aqk, qg, kg, gl, z, gdn_norm.reshape(1, GDN_DV))
    return o, s_fin


GDN_SB = 8


def _gdn_sample_body(x_ref, za_ref, gb_ref, sc_ref, s0_ref, cw_ref, gn_ref, o_ref, s_ref):
    sb = GDN_SB
    x3 = sc_ref[...]
    y = x_ref[...] * cw_ref[GDN_CONV - 1:GDN_CONV, :]
    for j in range(GDN_CONV - 1):
        y = y + x3[:, j, :] * cw_ref[j:j + 1, :]
    xc = y * jax.nn.sigmoid(y)
    for h in range(GDN_HEADS):
        q = xc[:, h * GDN_DK:(h + 1) * GDN_DK]
        k = xc[:, GDN_QK_W + h * GDN_DK:GDN_QK_W + (h + 1) * GDN_DK]
        v = xc[:, 2 * GDN_QK_W + h * GDN_DV:2 * GDN_QK_W + (h + 1) * GDN_DV]
        q = q * lax.rsqrt(jnp.sum(q * q, -1, keepdims=True) + EPS) * (GDN_DK ** -0.5)
        k = k * lax.rsqrt(jnp.sum(k * k, -1, keepdims=True) + EPS)
        gb = gb_ref[h]
        x = gb[:, 1:2] + gb[:, 2:3]
        softplus = jnp.maximum(x, 0.0) + jnp.log1p(jnp.exp(-jnp.abs(x)))
        eg = jnp.broadcast_to(jnp.exp(-jnp.exp(gb[:, 3:4]) * softplus), (sb, GDN_DV))
        beta = jnp.broadcast_to(jax.nn.sigmoid(gb[:, 0:1]), (sb, GDN_DV))
        qk = jnp.broadcast_to(jnp.sum(q * k, -1, keepdims=True), (sb, GDN_DV))
        q_t = q.T
        k_t = k.T
        o_rows = []
        for i in range(sb):
            s = s0_ref[i, h]
            kcol = k_t[:, i:i + 1]
            k_s = jnp.sum(kcol * s, axis=0, keepdims=True)
            q_s = jnp.sum(q_t[:, i:i + 1] * s, axis=0, keepdims=True)
            eg_i = eg[i:i + 1]
            v_new = beta[i:i + 1] * (v[i:i + 1] - eg_i * k_s)
            o_rows.append(eg_i * q_s + qk[i:i + 1] * v_new)
            s_ref[i, h] = eg_i * s + kcol * v_new
        o = jnp.concatenate(o_rows, axis=0)
        on = o * lax.rsqrt(jnp.mean(o * o, -1, keepdims=True) + EPS) * gn_ref[...]
        za = za_ref[:, h * GDN_DV:(h + 1) * GDN_DV]
        o_ref[:, h * GDN_DV:(h + 1) * GDN_DV] = on * (za * jax.nn.sigmoid(za))


def gdn_sample(z, gb, state_conv, state_gdn, conv_w, gdn_norm, row0, bs):
    sb = GDN_SB
    r0 = row0 // sb
    return pl.pallas_call(
        _gdn_sample_body,
        grid=(bs // sb,),
        in_specs=[pl.BlockSpec((sb, CONV_CH), lambda i: (r0 + i, 0)),
                  pl.BlockSpec((sb, GDN_V_W), lambda i: (r0 + i, Z_ZG // GDN_V_W)),
                  pl.BlockSpec((GDN_HEADS, sb, 4), lambda i: (0, r0 + i, 0)),
                  pl.BlockSpec((sb, GDN_CONV - 1, CONV_CH), lambda i: (i, 0, 0)),
                  pl.BlockSpec((sb, GDN_HEADS, GDN_DK, GDN_DV), lambda i: (i, 0, 0, 0)),
                  pl.BlockSpec((GDN_CONV, CONV_CH), lambda i: (0, 0)),
                  pl.BlockSpec((1, GDN_DV), lambda i: (0, 0))],
        out_specs=[pl.BlockSpec((sb, GDN_V_W), lambda i: (i, 0)),
                   pl.BlockSpec((sb, GDN_HEADS, GDN_DK, GDN_DV), lambda i: (i, 0, 0, 0))],
        out_shape=[jax.ShapeDtypeStruct((bs, GDN_V_W), f32),
                   jax.ShapeDtypeStruct((bs, GDN_HEADS, GDN_DK, GDN_DV), f32)],
        compiler_params=_cparams(("parallel",)),
        name="gdn_sample",
    )(z, z, gb, state_conv, state_gdn, conv_w, gdn_norm.reshape(1, GDN_DV))


SMP_PAGES = 8
_TN = (((0,), (0,)), ((), ()))
_KV_ROW = 2 * NSA_KV_GROUPS * NSA_DH


def _page_specs(shape):
    return [pl.BlockSpec((None,) + shape, lambda b, p, pt, kk=kk: (pt[b, p * SMP_PAGES + kk], 0, 0))
            for kk in range(SMP_PAGES)]


def _compress_sample_body(pt_ref, *refs):
    pages = refs[:SMP_PAGES]
    w_ref, o_ref = refs[SMP_PAGES:]
    per = PAGE_SIZE // CMP_STRIDE
    slabs = 2 * NSA_KV_GROUPS
    for x in range(2):
        rows = []
        for kk in range(SMP_PAGES):
            for g in range(NSA_KV_GROUPS):
                slab = x * NSA_KV_GROUPS + g
                parts = [pages[kk][pl.ds(s * slabs + slab, per, stride=CMP_STRIDE * slabs), :]
                         for s in range(CMP_STRIDE)]
                rows.append(jnp.concatenate(parts, axis=1))
        lhs = jnp.concatenate(rows, axis=0).astype(bf16)
        prod = jnp.dot(lhs, w_ref[x], preferred_element_type=f32)
        for kk in range(SMP_PAGES):
            for g in range(NSA_KV_GROUPS):
                r = (kk * NSA_KV_GROUPS + g) * per
                o_ref[x, g, kk * per:(kk + 1) * per, :] = prod[r:r + per]


def compress_sample(page_table, cache, w2):
    bs, n_pages = page_table.shape
    per = PAGE_SIZE // CMP_STRIDE
    n_sub = n_pages * per
    grid_spec = pltpu.PrefetchScalarGridSpec(
        num_scalar_prefetch=1,
        grid=(bs, n_pages // SMP_PAGES),
        in_specs=(_page_specs((PAGE_SIZE * 2 * NSA_KV_GROUPS, NSA_DH))
                  + [pl.BlockSpec((2, CMP_STRIDE * NSA_DH, 2 * NSA_DH), lambda b, p, pt: (0, 0, 0))]),
        out_specs=pl.BlockSpec((None, 2, NSA_KV_GROUPS, SMP_PAGES * per, 2 * NSA_DH), lambda b, p, pt: (b, 0, 0, p, 0)),
    )
    return pl.pallas_call(
        _compress_sample_body,
        grid_spec=grid_spec,
        out_shape=jax.ShapeDtypeStruct((bs, 2, NSA_KV_GROUPS, n_sub, 2 * NSA_DH), f32),
        compiler_params=_cparams(("parallel", "arbitrary")),
        name="compress_sample",
    )(page_table, *([cache] * SMP_PAGES), w2)


def _nsa_sample_select_body(qt_ref, pm_ref, w_ref, pe_ref, bc_ref, cov_ref, win_ref, bw_ref, b0_ref, kn_ref, vn_ref,
                            oc_ref, ow_ref, mask_ref, sn_ref, *, past):
    scale = NSA_DH ** -0.5
    qt = qt_ref[...]
    n_sub = pm_ref.shape[2]
    lane1 = lax.broadcasted_iota(jnp.int32, (1, LANES), 1)
    lane_n = lax.broadcasted_iota(jnp.int32, (n_sub, LANES), 1)
    pe_term = []
    for x in range(2):
        pr = jnp.dot(pe_ref[x], w_ref[x], preferred_element_type=f32)
        pe_term.append(pr[0:1, :NSA_DH] + pr[1:2, NSA_DH:])
    oc_t = jnp.zeros((NSA_DH, LANES), f32)
    ow_t = jnp.zeros((NSA_DH, LANES), f32)
    psum_all = jnp.zeros((n_sub, LANES), f32)
    snew = jnp.zeros((SUBLANES, LANES), f32)
    for g in range(NSA_KV_GROUPS):
        in_g = (lane1 // NSA_HPG) == g
        kvc = []
        for x in range(2):
            pm = pm_ref[x, g]
            kvc.append((pm[:, :NSA_DH] + pltpu.roll(pm[:, NSA_DH:], n_sub - 1, axis=0) + pe_term[x]).astype(bf16))
        s = jnp.dot(kvc[0], qt, preferred_element_type=f32) * scale + bc_ref[...]
        e = jnp.exp(s - jnp.max(s, axis=0, keepdims=True))
        p = jnp.where(in_g, e / jnp.sum(e, axis=0, keepdims=True), 0.0)
        oc_t = oc_t + lax.dot_general(kvc[1], p.astype(bf16), _TN, preferred_element_type=f32)
        psum = jnp.sum(p, axis=1, keepdims=True)
        psum_all = jnp.where((lane_n // NSA_HPG) == g, psum, psum_all)
        kw = win_ref[:, g * NSA_DH:(g + 1) * NSA_DH].astype(bf16)
        vw = win_ref[:, (NSA_KV_GROUPS + g) * NSA_DH:(NSA_KV_GROUPS + g + 1) * NSA_DH].astype(bf16)
        sw = jnp.dot(kw, qt, preferred_element_type=f32) * scale + bw_ref[...]
        sn = jnp.dot(kn_ref[g], qt, preferred_element_type=f32) * scale + b0_ref[...]
        mw = jnp.maximum(jnp.max(sw, axis=0, keepdims=True), sn[1:2])
        ew = jnp.exp(sw - mw)
        en = jnp.exp(sn[1:2] - mw)
        lw = jnp.sum(ew, axis=0, keepdims=True) + en
        pw = jnp.where(in_g, ew / lw, 0.0)
        row = lax.broadcasted_iota(jnp.int32, (SUBLANES, LANES), 0)
        pn = jnp.where((row == 1) & in_g, en / lw, 0.0)
        ow_t = (ow_t + lax.dot_general(vw, pw.astype(bf16), _TN, preferred_element_type=f32)
                + lax.dot_general(vn_ref[g], pn.astype(bf16), _TN, preferred_element_type=f32))
        snew = jnp.where(in_g, sn, snew)
    oc_ref[...] = oc_t
    ow_ref[...] = ow_t
    sn_ref[...] = snew

    imp_t = jnp.dot(cov_ref[...], psum_all, preferred_element_type=f32, precision=_HI)
    nsb = imp_t.shape[0]
    n_blocks = past // SEL_BLOCK + 1
    blk = lax.broadcasted_iota(jnp.int32, (nsb, LANES), 0)
    cur = past // SEL_BLOCK
    forced = (blk == 0) | (blk == cur) | (blk == cur - 1)
    valid = (blk * SEL_BLOCK <= past) & (blk < n_blocks)
    score = jnp.where(valid, jnp.where(forced, FORCE_SCORE, imp_t), -1.0)
    sel_t = jnp.zeros((nsb, LANES), f32)
    for _ in range(min(SEL_TOPK, n_blocks)):
        mx = jnp.max(score, axis=0, keepdims=True)
        first = jnp.min(jnp.where(score == mx, blk, nsb), axis=0, keepdims=True)
        pick = blk == first
        sel_t = jnp.where(pick, 1.0, sel_t)
        score = jnp.where(pick, -2.0, score)
    mask_ref[...] = (sel_t - 1.0) * (-NEG_BIG)


def _nsa_sample_attend_body(pt_ref, *refs):
    pages = refs[:SMP_PAGES]
    (qt_ref, bias_ref, mask_ref, sn_ref, vn_ref, oc_ref, ow_ref, gate_ref, o_ref, m_ref, l_ref, acc_ref) = refs[SMP_PAGES:]
    p = pl.program_id(1)
    grp = lax.broadcasted_iota(jnp.int32, (1, LANES), 1) // NSA_HPG

    @pl.when(p == 0)
    def _():
        m_ref[...] = sn_ref[0:1, :]
        l_ref[...] = jnp.ones(l_ref.shape, f32)
        acc_ref[...] = vn_ref[...]

    qt = qt_ref[...]
    s = None
    for g in range(NSA_KV_GROUPS):
        kg = jnp.concatenate([pages[kk][:, g * NSA_DH:(g + 1) * NSA_DH] for kk in range(SMP_PAGES)], axis=0)
        sg = jnp.dot(kg.astype(bf16), qt, preferred_element_type=f32)
        s = sg if g == 0 else jnp.where(grp == g, sg, s)
    bpp = PAGE_SIZE // SEL_BLOCK
    mrows = []
    for kk in range(SMP_PAGES):
        for half in range(bpp):
            mrow = mask_ref[pl.ds((p * SMP_PAGES + kk) * bpp + half, 1), :]
            mrows.append(jnp.broadcast_to(mrow, (SEL_BLOCK, LANES)))
    s = s * (NSA_DH ** -0.5) + bias_ref[...] + jnp.concatenate(mrows, axis=0)
    m_old = m_ref[...]
    m_new = jnp.maximum(m_old, jnp.max(s, axis=0, keepdims=True))
    alpha = jnp.exp(m_old - m_new)
    pe = jnp.exp(s - m_new)
    l_ref[...] = alpha * l_ref[...] + jnp.sum(pe, axis=0, keepdims=True)
    pb = pe.astype(bf16)
    upd = jnp.zeros(acc_ref.shape, f32)
    for g in range(NSA_KV_GROUPS):
        c0 = (NSA_KV_GROUPS + g) * NSA_DH
        vg = jnp.concatenate([pages[kk][:, c0:c0 + NSA_DH] for kk in range(SMP_PAGES)], axis=0)
        upd = upd + lax.dot_general(vg.astype(bf16), jnp.where(grp == g, pb, jnp.zeros_like(pb)), _TN,
                                    preferred_element_type=f32)
    acc_ref[...] = alpha * acc_ref[...] + upd
    m_ref[...] = m_new

    @pl.when(p == pl.num_programs(1) - 1)
    def _():
        gt = jax.nn.sigmoid(gate_ref[...])
        o_s = acc_ref[...] / l_ref[...]
        o_ref[...] = gt[0:1] * oc_ref[...] + gt[1:2] * o_s + gt[2:3] * ow_ref[...]


def nsa_sample(z, zs, row0, cache_c, cache_s, page_table, win_buf, cmp_pe, cmp_w, rel_bias):
    bs, n_pages = page_table.shape
    past = n_pages * PAGE_SIZE
    n_sub = past // CMP_STRIDE
    n_blocks = past // SEL_BLOCK + 1
    nsb = -(-n_blocks // SUBLANES) * SUBLANES
    w_buf = win_buf.shape[1]
    tbl = rel_bias.astype(f32)
    lane_pad = lambda a: jnp.pad(a, [(0, 0)] * (a.ndim - 1) + [(0, LANES - a.shape[-1])])

    def bias_rows(dist, ok):
        return lane_pad(jnp.where(ok[:, None], tbl[_rel_bucket(dist)], NEG_BIG))

    dist_c = past - (jnp.arange(n_sub, dtype=jnp.int32) * CMP_STRIDE + (CMP_BLOCK - 1))
    bias_c = bias_rows(dist_c, dist_c >= 0)
    w_pos = past - w_buf + jnp.arange(w_buf, dtype=jnp.int32)
    dist_w = past - w_pos
    bias_w = bias_rows(dist_w, (dist_w < WINDOW) & (w_pos >= 0))
    bias_0 = jnp.broadcast_to(lane_pad(tbl[_rel_bucket(jnp.zeros((1,), jnp.int32))]), (SUBLANES, LANES))
    dist_s = past - jnp.arange(past, dtype=jnp.int32)
    bias_s = bias_rows(dist_s, dist_s >= 0)
    cover_t = jnp.pad(_cover_matrix(n_sub - 1, n_blocks), [(0, 1), (0, nsb - n_blocks)]).T

    w2 = cmp_w.reshape(2, CMP_BLOCK // CMP_STRIDE, CMP_STRIDE * NSA_DH, NSA_DH)
    w2 = jnp.transpose(w2, (0, 2, 1, 3)).reshape(2, CMP_STRIDE * NSA_DH, 2 * NSA_DH).astype(bf16)
    pe2 = jnp.pad(cmp_pe.reshape(2, CMP_BLOCK // CMP_STRIDE, CMP_STRIDE * NSA_DH), [(0, 0), (0, SUBLANES - 2), (0, 0)])
    pe2 = pe2.astype(bf16)

    zrow = z[row0:row0 + bs]
    q = zrow[:, Z_NQ:Z_KV].reshape(bs, NSA_HEADS, NSA_DH)
    qt = lane_pad(jnp.transpose(q, (0, 2, 1))).astype(bf16)
    kv = zrow[:, Z_KV:Z_GA].reshape(bs, 3, 2, NSA_KV_GROUPS, NSA_DH)
    zero = jnp.zeros((bs, NSA_KV_GROUPS, NSA_DH), f32)
    pad_rows = lambda r0, r1: jnp.pad(jnp.stack([r0, r1], 2), [(0, 0), (0, 0), (0, SUBLANES - 2), (0, 0)]).astype(bf16)
    k_new = pad_rows(kv[:, 1, 0], kv[:, 2, 0])
    v_new_w = pad_rows(zero, kv[:, 2, 1])
    v_new_s = lane_pad(jnp.transpose(jnp.repeat(kv[:, 1, 1], NSA_HPG, axis=1), (0, 2, 1)))
    gates = zs[row0:row0 + bs, 2 * GDN_HEADS:2 * GDN_HEADS + 3 * NSA_HEADS].reshape(bs, NSA_HEADS, 3)
    gates = jnp.pad(lane_pad(jnp.transpose(gates, (0, 2, 1))), [(0, 0), (0, SUBLANES - 3), (0, 0)])

    pm = compress_sample(page_table, cache_c.reshape(-1, PAGE_SIZE * 2 * NSA_KV_GROUPS, NSA_DH), w2)
    per_seq = lambda *shape: pl.BlockSpec((None,) + shape, lambda b: (b,) + (0,) * len(shape))
    const = lambda *shape: pl.BlockSpec(shape, lambda b: (0,) * len(shape))
    oc_t, ow_t, mask, s_new = pl.pallas_call(
        functools.partial(_nsa_sample_select_body, past=past),
        grid=(bs,),
        in_specs=[per_seq(NSA_DH, LANES), per_seq(2, NSA_KV_GROUPS, n_sub, 2 * NSA_DH),
                  const(2, CMP_STRIDE * NSA_DH, 2 * NSA_DH), const(2, SUBLANES, CMP_STRIDE * NSA_DH),
                  const(n_sub, LANES), const(nsb, n_sub), per_seq(w_buf, _KV_ROW), const(w_buf, LANES),
                  const(SUBLANES, LANES), per_seq(NSA_KV_GROUPS, SUBLANES, NSA_DH),
                  per_seq(NSA_KV_GROUPS, SUBLANES, NSA_DH)],
        out_specs=[per_seq(NSA_DH, LANES), per_seq(NSA_DH, LANES), per_seq(nsb, LANES), per_seq(SUBLANES, LANES)],
        out_shape=[jax.ShapeDtypeStruct((bs, NSA_DH, LANES), f32), jax.ShapeDtypeStruct((bs, NSA_DH, LANES), f32),
                   jax.ShapeDtypeStruct((bs, nsb, LANES), f32), jax.ShapeDtypeStruct((bs, SUBLANES, LANES), f32)],
        compiler_params=_cparams(("parallel",)),
        name="nsa_sample_select",
    )(qt, pm, w2, pe2, bias_c, cover_t, win_buf.reshape(bs, w_buf, _KV_ROW), bias_w, bias_0, k_new, v_new_w)

    seq = lambda *shape: pl.BlockSpec((None,) + shape, lambda b, p, pt: (b,) + (0,) * len(shape))
    grid_spec = pltpu.PrefetchScalarGridSpec(
        num_scalar_prefetch=1,
        grid=(bs, n_pages // SMP_PAGES),
        in_specs=_page_specs((PAGE_SIZE, _KV_ROW)) + [seq(NSA_DH, LANES),
                                  pl.BlockSpec((SMP_PAGES * PAGE_SIZE, LANES), lambda b, p, pt: (p, 0)),
                                  seq(nsb, LANES), seq(SUBLANES, LANES), seq(NSA_DH, LANES), seq(NSA_DH, LANES),
                                  seq(NSA_DH, LANES), seq(SUBLANES, LANES)],
        out_specs=seq(NSA_DH, LANES),
        scratch_shapes=[pltpu.VMEM((1, LANES), f32), pltpu.VMEM((1, LANES), f32), pltpu.VMEM((NSA_DH, LANES), f32)],
    )
    o_t = pl.pallas_call(
        _nsa_sample_attend_body,
        grid_spec=grid_spec,
        out_shape=jax.ShapeDtypeStruct((bs, NSA_DH, LANES), f32),
        compiler_params=_cparams(("parallel", "arbitrary")),
        name="nsa_sample_attend",
    )(page_table, *([cache_s.reshape(-1, PAGE_SIZE, _KV_ROW)] * SMP_PAGES), qt, bias_s, mask, s_new, v_new_s,
      oc_t, ow_t, gates)
    return jnp.transpose(o_t[:, :, :NSA_HEADS], (0, 2, 1)).reshape(bs, NSA_Q_W)


def _rel_bucket(dist):
    n = jnp.maximum(dist, 0)
    max_exact = REL_BUCKETS // 2
    nf = jnp.maximum(n, 1).astype(f32)
    large = max_exact + (jnp.log(nf / max_exact) / math.log(REL_MAX_DIST / max_exact)
                         * (REL_BUCKETS - max_exact)).astype(jnp.int32)
    large = jnp.minimum(large, REL_BUCKETS - 1)
    return jnp.where(n < max_exact, n, large)


def _cover_matrix(nc, ns):
    cs = np.arange(nc) * CMP_STRIDE
    ss = np.arange(ns) * SEL_BLOCK
    inter = np.minimum(cs[:, None] + CMP_BLOCK, ss[None, :] + SEL_BLOCK) - np.maximum(cs[:, None], ss[None, :])
    return jnp.asarray(np.clip(inter, 0, None) / CMP_BLOCK, dtype=f32)


def kernel(x_prompt, x_sample, p_prompt, p_sample, cache_cmp_kv, cache_slc_kv, page_table, state_win_kv, state_gdn, state_conv, g_mix, w_in, gdn_conv_w, gdn_dt_bias, gdn_a_log, gdn_norm, cmp_pe, cmp_w, rel_bias, w_proj_a, w_proj_b, w_out, g_ffn, w_router_group, b_router_group, w_router_expert, b_router_expert, w_gate, w_up, w_down, g_ple, w_ple_gate, w_ple_proj, g_final):
    bp, tp, d = x_prompt.shape
    bs, ts, _ = x_sample.shape
    n_p, n_s = bp * tp, bs * ts
    n_real = n_p + n_s
    mp = -(-n_real // ROW_ALIGN) * ROW_ALIGN
    pad = mp - n_real

    h = jnp.concatenate([x_prompt.reshape(n_p, d), x_sample.reshape(n_s, d), jnp.zeros((pad, d), f32)], 0)
    ple = jnp.concatenate([p_prompt[0].reshape(n_p, -1), p_sample[0].reshape(n_s, -1),
                           jnp.zeros((pad, p_prompt.shape[-1]), f32)], 0).astype(bf16)

    w = w_in[0]
    o_beta = 4 * GDN_QK_W
    o_nq = o_beta + 2 * GDN_HEADS
    o_gate = o_nq + NSA_Q_W + 6 * NSA_KV_W
    o_ga = o_gate + 3 * NSA_HEADS
    w_main = jnp.concatenate([w[:, :o_beta], w[:, o_nq:o_gate], w[:, o_ga:]], axis=1)
    n_small = 2 * GDN_HEADS + 3 * NSA_HEADS
    w_small = jnp.concatenate([w[:, o_beta:o_nq], w[:, o_gate:o_ga], jnp.zeros((d, LANES - n_small), f32)], axis=1)

    a = rmsnorm_rows(h, g_mix[0], bf16)
    z = proj_matmul(a, w_main, f32, TN_DENSE)
    zs = proj_matmul(a, w_small, f32, LANES)

    def rows(x, lo, hi, which):
        if which == "p":
            return x[:n_p, lo:hi].reshape(bp, tp, hi - lo)
        return x[n_p:n_real, lo:hi].reshape(bs, ts, hi - lo)

    assert ts == 1 and tp % GDN_RT == 0 and n_p % GDN_SB == 0 and bs % GDN_SB == 0

    gb = jnp.stack([zs[:, 0:GDN_HEADS], zs[:, GDN_HEADS:2 * GDN_HEADS],
                    jnp.broadcast_to(gdn_dt_bias[0], (mp, GDN_HEADS)),
                    jnp.broadcast_to(gdn_a_log[0], (mp, GDN_HEADS))], -1)
    gb = jnp.transpose(gb, (1, 0, 2))
    gates_g = jnp.transpose(zs[:, 2 * GDN_HEADS:n_small].reshape(mp, NSA_KV_GROUPS, 3 * NSA_HPG), (1, 0, 2))

    outs = {}
    kv = rows(z, Z_KV, Z_GA, "p").reshape(bp, tp, 3, 2, NSA_KV_GROUPS, NSA_DH)
    kvc = compress_prompt(z, cmp_w[0], cmp_pe[0], bp, tp)
    o_b_p = nsa_prompt_attention(z, gates_g, kvc, rel_bias, bp, tp)
    o_a_p, s_p = gdn_prompt(z, gb, gdn_conv_w[0], gdn_norm[0], bp, tp)
    conv_p = rows(z, 0, CONV_CH, "p")[:, tp - (GDN_CONV - 1):]
    outs["p"] = (kv[:, :, 0], kv[:, :, 1], kv[:, :, 2][:, tp - min(WINDOW, tp):], s_p, conv_p)
    kv = rows(z, Z_KV, Z_GA, "s").reshape(bs, ts, 3, 2, NSA_KV_GROUPS, NSA_DH)
    o_b_s = nsa_sample(z, zs, n_p, cache_cmp_kv[0], cache_slc_kv[0], page_table, state_win_kv[0],
                       cmp_pe[0], cmp_w[0], rel_bias)
    win_s = jnp.concatenate([state_win_kv[0], kv[:, :, 2]], axis=1)[:, ts:]
    o_a_s, s_s = gdn_sample(z, gb, state_conv[0], state_gdn[0], gdn_conv_w[0], gdn_norm[0], n_p, bs)
    conv_s = jnp.concatenate([state_conv[0][:, 1:], rows(z, 0, CONV_CH, "s")], axis=1)
    outs["s"] = (kv[:, :, 0], kv[:, :, 1], win_s, s_s, conv_s)

    o_a = jnp.concatenate([o_a_p, o_a_s.astype(bf16), jnp.zeros((pad, GDN_V_W), bf16)], 0)
    o_b = jnp.concatenate([o_b_p, o_b_s.reshape(n_s, NSA_Q_W).astype(bf16), jnp.zeros((pad, NSA_Q_W), bf16)], 0)
    merged = merge_matmul(o_a, o_b, z, w_proj_a[0], w_proj_b[0])
    h = resid_matmul(merged, w_out[0], h)
    h = hier_moe(h, n_real, g_ffn[0], w_router_group[0], b_router_group[0], w_router_expert[0],
                 b_router_expert[0], w_gate[0], w_up[0], w_down[0])
    n3 = rmsnorm_rows(h, g_ple[0], bf16)
    h = ple_matmul(n3, w_ple_gate[0], ple, w_ple_proj[0], h)
    y = rmsnorm_rows(h, g_final, f32)
    y_prompt = y[:n_p].reshape(bp, tp, d)
    y_sample = y[n_p:n_real].reshape(bs, ts, d)
    st_p, st_s = outs["p"], outs["s"]
    return (y_prompt, y_sample) + tuple(t[None] for t in st_p) + tuple(t[None] for t in st_s)
```

```python
import functools
import math

import jax
import jax.numpy as jnp
import numpy as np
from jax import lax
from jax.experimental import pallas as pl
from jax.experimental.pallas import tpu as pltpu

D_MODEL = 4096
GDN_HEADS = 16
GDN_DK = 128
GDN_DV = 128
GDN_CONV = 4
GDN_CHUNK = 64
NSA_HEADS = 16
NSA_KV_GROUPS = 4
NSA_HPG = NSA_HEADS // NSA_KV_GROUPS
NSA_DH = 128
CMP_BLOCK = 32
CMP_STRIDE = 16
SEL_BLOCK = 64
SEL_TOPK = 16
WINDOW = 512
Q_BLOCK = 128
FORCE_SCORE = 1.0e4
REL_BUCKETS = 32
REL_MAX_DIST = 1024
PAGE_SIZE = 128
MOE_GROUPS = 4
MOE_PER_GROUP = 8
MOE_EXPERTS = MOE_GROUPS * MOE_PER_GROUP
MOE_TOPK = 2
EXPERT_HIDDEN = 512
EPS = 1e-6

GDN_QK_W = GDN_HEADS * GDN_DK
GDN_V_W = GDN_HEADS * GDN_DV
CONV_CH = 2 * GDN_QK_W + GDN_V_W
NSA_Q_W = NSA_HEADS * NSA_DH
NSA_KV_W = NSA_KV_GROUPS * NSA_DH

LANES = 128
SUBLANES = 8
VMEM_LIMIT = 56 * 1024 * 1024

ROW_ALIGN = 768
TM_DENSE = 768
TN_DENSE = 512
TM_ROWS = 256

Z_Q, Z_K, Z_V, Z_ZG = 0, 2048, 4096, 6144
Z_NQ = 8192
Z_KV = 10240
Z_GA = 13312
Z_GB = 17408
Z_COLS = 21504

bf16 = jnp.bfloat16
f32 = jnp.float32


def _cparams(sem):
    return pltpu.CompilerParams(dimension_semantics=sem, vmem_limit_bytes=VMEM_LIMIT)


def _rmsnorm_body(x_ref, g_ref, o_ref):
    x = x_ref[...]
    y = x * lax.rsqrt(jnp.mean(x * x, -1, keepdims=True) + EPS)
    o_ref[...] = (y * g_ref[...]).astype(o_ref.dtype)


def rmsnorm_rows(x, g, out_dtype):
    m, d = x.shape
    return pl.pallas_call(
        _rmsnorm_body,
        grid=(m // TM_ROWS,),
        in_specs=[pl.BlockSpec((TM_ROWS, d), lambda i: (i, 0)),
                  pl.BlockSpec((1, d), lambda i: (0, 0))],
        out_specs=pl.BlockSpec((TM_ROWS, d), lambda i: (i, 0)),
        out_shape=jax.ShapeDtypeStruct((m, d), out_dtype),
        compiler_params=_cparams(("parallel",)),
        name="rmsnorm_rows",
    )(x, g.reshape(1, d))


def _cast_body(x_ref, o_ref):
    o_ref[...] = x_ref[...].astype(o_ref.dtype)


def cast_bf16(x2d, tr):
    r, c = x2d.shape
    return pl.pallas_call(
        _cast_body,
        grid=(r // tr,),
        in_specs=[pl.BlockSpec((tr, c), lambda i: (i, 0))],
        out_specs=pl.BlockSpec((tr, c), lambda i: (i, 0)),
        out_shape=jax.ShapeDtypeStruct((r, c), bf16),
        compiler_params=_cparams(("parallel",)),
        name="cast_bf16",
    )(x2d)


def _proj_body(a_ref, w_ref, o_ref, wb_ref):
    @pl.when(pl.program_id(1) == 0)
    def _():
        wb_ref[...] = w_ref[...].astype(bf16)
    o_ref[...] = jnp.dot(a_ref[...], wb_ref[...], preferred_element_type=f32).astype(o_ref.dtype)


def proj_matmul(a, w, out_dtype, tn):
    m, k = a.shape
    n = w.shape[1]
    return pl.pallas_call(
        _proj_body,
        grid=(n // tn, m // TM_DENSE),
        in_specs=[pl.BlockSpec((TM_DENSE, k), lambda j, i: (i, 0)),
                  pl.BlockSpec((k, tn), lambda j, i: (0, j))],
        out_specs=pl.BlockSpec((TM_DENSE, tn), lambda j, i: (i, j)),
        out_shape=jax.ShapeDtypeStruct((m, n), out_dtype),
        scratch_shapes=[pltpu.VMEM((k, tn), bf16)],
        compiler_params=_cparams(("arbitrary", "arbitrary")),
        name="proj_matmul",
    )(a, w)


def _merge_body(oa_ref, ob_ref, ga_ref, gb_ref, wa_ref, wb_ref, o_ref, wa_s, wb_s):
    @pl.when(pl.program_id(1) == 0)
    def _():
        wa_s[...] = wa_ref[...].astype(bf16)
        wb_s[...] = wb_ref[...].astype(bf16)
    pa = jnp.dot(oa_ref[...], wa_s[...], preferred_element_type=f32)
    pb = jnp.dot(ob_ref[...], wb_s[...], preferred_element_type=f32)
    o_ref[...] = (jax.nn.sigmoid(ga_ref[...]) * pa + jax.nn.sigmoid(gb_ref[...]) * pb).astype(o_ref.dtype)


def merge_matmul(o_a, o_b, z, w_a, w_b):
    m, ka = o_a.shape
    kb = o_b.shape[1]
    n = w_a.shape[1]
    tn = TN_DENSE
    ja, jb = Z_GA // tn, Z_GB // tn
    return pl.pallas_call(
        _merge_body,
        grid=(n // tn, m // TM_DENSE),
        in_specs=[pl.BlockSpec((TM_DENSE, ka), lambda j, i: (i, 0)),
                  pl.BlockSpec((TM_DENSE, kb), lambda j, i: (i, 0)),
                  pl.BlockSpec((TM_DENSE, tn), lambda j, i: (i, ja + j)),
                  pl.BlockSpec((TM_DENSE, tn), lambda j, i: (i, jb + j)),
                  pl.BlockSpec((ka, tn), lambda j, i: (0, j)),
                  pl.BlockSpec((kb, tn), lambda j, i: (0, j))],
        out_specs=pl.BlockSpec((TM_DENSE, tn), lambda j, i: (i, j)),
        out_shape=jax.ShapeDtypeStruct((m, n), bf16),
        scratch_shapes=[pltpu.VMEM((ka, tn), bf16), pltpu.VMEM((kb, tn), bf16)],
        compiler_params=_cparams(("arbitrary", "arbitrary")),
        name="merge_matmul",
    )(o_a, o_b, z, z, w_a, w_b)


def _resid_body(a_ref, w_ref, h_ref, o_ref, wb_ref):
    @pl.when(pl.program_id(1) == 0)
    def _():
        wb_ref[...] = w_ref[...].astype(bf16)
    o_ref[...] = h_ref[...] + jnp.dot(a_ref[...], wb_ref[...], preferred_element_type=f32)


def resid_matmul(a, w, h):
    m, k = a.shape
    n = w.shape[1]
    tn = TN_DENSE
    return pl.pallas_call(
        _resid_body,
        grid=(n // tn, m // TM_DENSE),
        in_specs=[pl.BlockSpec((TM_DENSE, k), lambda j, i: (i, 0)),
                  pl.BlockSpec((k, tn), lambda j, i: (0, j)),
                  pl.BlockSpec((TM_DENSE, tn), lambda j, i: (i, j))],
        out_specs=pl.BlockSpec((TM_DENSE, tn), lambda j, i: (i, j)),
        out_shape=jax.ShapeDtypeStruct((m, n), f32),
        scratch_shapes=[pltpu.VMEM((k, tn), bf16)],
        compiler_params=_cparams(("arbitrary", "arbitrary")),
        name="resid_matmul",
    )(a, w, h)


def _ple_body(a_ref, w_ref, p_ref, wp_ref, h_ref, o_ref, wb_ref, wpb_ref):
    @pl.when(pl.program_id(1) == 0)
    def _():
        wb_ref[...] = w_ref[...].astype(bf16)
        wpb_ref[...] = wp_ref[...].astype(bf16)
    gate = jax.nn.sigmoid(jnp.dot(a_ref[...], wb_ref[...], preferred_element_type=f32))
    emb = jnp.dot(p_ref[...], wpb_ref[...], preferred_element_type=f32)
    o_ref[...] = h_ref[...] + gate * emb


def ple_matmul(a, w_gate, p, w_proj, h):
    m, k = a.shape
    kp = p.shape[1]
    n = w_gate.shape[1]
    tn = TN_DENSE
    return pl.pallas_call(
        _ple_body,
        grid=(n // tn, m // TM_DENSE),
        in_specs=[pl.BlockSpec((TM_DENSE, k), lambda j, i: (i, 0)),
                  pl.BlockSpec((k, tn), lambda j, i: (0, j)),
                  pl.BlockSpec((TM_DENSE, kp), lambda j, i: (i, 0)),
                  pl.BlockSpec((kp, tn), lambda j, i: (0, j)),
                  pl.BlockSpec((TM_DENSE, tn), lambda j, i: (i, j))],
        out_specs=pl.BlockSpec((TM_DENSE, tn), lambda j, i: (i, j)),
        out_shape=jax.ShapeDtypeStruct((m, n), f32),
        scratch_shapes=[pltpu.VMEM((k, tn), bf16), pltpu.VMEM((kp, tn), bf16)],
        compiler_params=_cparams(("arbitrary", "arbitrary")),
        name="ple_matmul",
    )(a, w_gate, p, w_proj, h)


def _router_body(h_ref, g_ref, wr_ref, br_ref, m_ref, r_ref):
    x = h_ref[...]
    y = x * lax.rsqrt(jnp.mean(x * x, -1, keepdims=True) + EPS) * g_ref[...]
    m_ref[...] = y.astype(bf16)
    logits = jnp.dot(y, wr_ref[...], preferred_element_type=f32,
                     precision=lax.Precision.HIGHEST) + br_ref[...]
    lane = lax.broadcasted_iota(jnp.int32, logits.shape, 1)
    neg = -jnp.inf
    lg = jnp.where(lane < MOE_GROUPS, logits, neg)
    eg = jnp.exp(lg - jnp.max(lg, -1, keepdims=True))
    pg = eg / jnp.sum(eg, -1, keepdims=True)
    pg_top = jnp.max(pg, -1, keepdims=True)
    g_idx = jnp.min(jnp.where(pg == pg_top, lane, LANES), -1, keepdims=True)
    lo = MOE_GROUPS + MOE_PER_GROUP * g_idx
    emask = (lane >= lo) & (lane < lo + MOE_PER_GROUP)
    le = jnp.where(emask, logits, neg)
    ee = jnp.exp(le - jnp.max(le, -1, keepdims=True))
    pe = jnp.where(emask, ee / jnp.sum(ee, -1, keepdims=True), -1.0)
    v1 = jnp.max(pe, -1, keepdims=True)
    i1 = jnp.min(jnp.where(pe == v1, lane, LANES), -1, keepdims=True)
    pe2 = jnp.where(lane == i1, -1.0, pe)
    v2 = jnp.max(pe2, -1, keepdims=True)
    i2 = jnp.min(jnp.where(pe2 == v2, lane, LANES), -1, keepdims=True)
    den = v1 + v2
    w1 = pg_top * v1 / den
    w2 = pg_top * v2 / den
    e1 = (i1 - MOE_GROUPS).astype(f32)
    e2 = (i2 - MOE_GROUPS).astype(f32)
    r_ref[...] = jnp.where(lane == 0, e1, jnp.where(lane == 1, e2,
                           jnp.where(lane == 2, w1, jnp.where(lane == 3, w2, 0.0))))


def moe_router(h, g_ffn, w_router, b_router):
    m, d = h.shape
    return pl.pallas_call(
        _router_body,
        grid=(m // TM_ROWS,),
        in_specs=[pl.BlockSpec((TM_ROWS, d), lambda i: (i, 0)),
                  pl.BlockSpec((1, d), lambda i: (0, 0)),
                  pl.BlockSpec((d, LANES), lambda i: (0, 0)),
                  pl.BlockSpec((1, LANES), lambda i: (0, 0))],
        out_specs=[pl.BlockSpec((TM_ROWS, d), lambda i: (i, 0)),
                   pl.BlockSpec((TM_ROWS, LANES), lambda i: (i, 0))],
        out_shape=[jax.ShapeDtypeStruct((m, d), bf16),
                   jax.ShapeDtypeStruct((m, LANES), f32)],
        compiler_params=_cparams(("parallel",)),
        name="moe_router",
    )(h, g_ffn.reshape(1, d), w_router, b_router)


def _expert_body(te_ref, tv_ref, x_ref, rw_ref, wg_ref, wu_ref, wd_ref, o_ref):
    t = pl.program_id(0)

    @pl.when(tv_ref[t] > 0)
    def _():
        x = x_ref[...]
        gate = jnp.dot(x, wg_ref[...], preferred_element_type=f32)
        up = jnp.dot(x, wu_ref[...], preferred_element_type=f32)
        hid = (jax.nn.silu(gate) * up * rw_ref[...]).astype(bf16)
        o_ref[...] = jnp.dot(hid, wd_ref[...], preferred_element_type=f32)

    @pl.when(tv_ref[t] == 0)
    def _():
        o_ref[...] = jnp.zeros_like(o_ref)


def expert_matmul(tile_expert, tile_valid, xs, row_w, wg, wu, wd):
    r, d = xs.shape
    f = wg.shape[2]
    n_tiles = r // TM_ROWS
    grid_spec = pltpu.PrefetchScalarGridSpec(
        num_scalar_prefetch=2,
        grid=(n_tiles,),
        in_specs=[pl.BlockSpec((TM_ROWS, d), lambda t, te, tv: (t, 0)),
                  pl.BlockSpec((TM_ROWS, 1), lambda t, te, tv: (t, 0)),
                  pl.BlockSpec((None, d, f), lambda t, te, tv: (te[t], 0, 0)),
                  pl.BlockSpec((None, d, f), lambda t, te, tv: (te[t], 0, 0)),
                  pl.BlockSpec((None, f, d), lambda t, te, tv: (te[t], 0, 0))],
        out_specs=pl.BlockSpec((TM_ROWS, d), lambda t, te, tv: (t, 0)),
    )
    return pl.pallas_call(
        _expert_body,
        grid_spec=grid_spec,
        out_shape=jax.ShapeDtypeStruct((r, d), f32),
        compiler_params=_cparams(("arbitrary",)),
        name="expert_matmul",
    )(tile_expert, tile_valid, xs, row_w, wg, wu, wd)


def hier_moe(h, n_real, g_ffn, w_rg, b_rg, w_re, b_re, w_gate, w_up, w_down):
    mp, d = h.shape
    n_route = MOE_GROUPS + MOE_EXPERTS
    w_router = jnp.zeros((d, LANES), f32).at[:, :MOE_GROUPS].set(w_rg).at[:, MOE_GROUPS:n_route].set(w_re)
    b_router = jnp.zeros((1, LANES), f32).at[0, :MOE_GROUPS].set(b_rg).at[0, MOE_GROUPS:n_route].set(b_re)
    m_bf, slab = moe_router(h, g_ffn, w_router, b_router)
    ids = slab[:n_real, 0:2].astype(jnp.int32)
    wts = slab[:n_real, 2:4]

    tm = TM_ROWS
    n_assign = n_real * MOE_TOPK
    n_slots = -(-(n_assign + MOE_EXPERTS * (tm - 1)) // tm) * tm
    e_flat = ids.reshape(-1)
    order = jnp.argsort(e_flat, stable=True)
    e_sorted = e_flat[order]
    counts = jnp.sum(jax.nn.one_hot(e_flat, MOE_EXPERTS, dtype=jnp.int32), axis=0)
    padded = -(-counts // tm) * tm
    start_p = jnp.cumsum(padded) - padded
    start = jnp.cumsum(counts) - counts
    slot_sorted = start_p[e_sorted] + (jnp.arange(n_assign, dtype=jnp.int32) - start[e_sorted])
    slot_of = jnp.zeros((n_assign,), jnp.int32).at[order].set(slot_sorted.astype(jnp.int32))
    src_tok = jnp.zeros((n_slots,), jnp.int32).at[slot_of].set(jnp.arange(n_assign, dtype=jnp.int32) // MOE_TOPK)
    row_w = jnp.zeros((n_slots,), f32).at[slot_of].set(wts.reshape(-1))
    tile_start = jnp.arange(n_slots // tm, dtype=jnp.int32) * tm
    ends = jnp.cumsum(padded)
    tile_expert = jnp.minimum(jnp.searchsorted(ends, tile_start, side="right"), MOE_EXPERTS - 1).astype(jnp.int32)
    tile_valid = (tile_start < ends[-1]).astype(jnp.int32)

    xs = jnp.take(m_bf, src_tok, axis=0)
    wg = cast_bf16(w_gate.reshape(MOE_EXPERTS * d, EXPERT_HIDDEN), 4096).reshape(MOE_EXPERTS, d, EXPERT_HIDDEN)
    wu = cast_bf16(w_up.reshape(MOE_EXPERTS * d, EXPERT_HIDDEN), 4096).reshape(MOE_EXPERTS, d, EXPERT_HIDDEN)
    wd = cast_bf16(w_down.reshape(MOE_EXPERTS * EXPERT_HIDDEN, d), 512).reshape(MOE_EXPERTS, EXPERT_HIDDEN, d)
    ys = expert_matmul(tile_expert, tile_valid, xs, row_w.reshape(n_slots, 1), wg, wu, wd)
    slot2 = slot_of.reshape(n_real, MOE_TOPK)
    y = jnp.take(ys, slot2[:, 0], axis=0) + jnp.take(ys, slot2[:, 1], axis=0)
    return h.at[:n_real].add(y)


NEG_BIG = -1e30
TILE_FAR = REL_MAX_DIST // Q_BLOCK + 1
TILE_WIN_EDGE = TILE_FAR + 1
TILE_NONE = TILE_FAR + 2
N_BIAS_TILES = TILE_FAR + 3
NSA_KT = 4
_NT = (((1,), (1,)), ((), ()))


def _compress_body(x_ref, w_ref, pe_ref, o_ref):
    ns = o_ref.shape[0]
    acc0 = jnp.zeros((ns, NSA_DH), f32)
    acc1 = jnp.zeros((ns, NSA_DH), f32)
    for s in range(CMP_STRIDE):
        xs = x_ref[pl.ds(s, ns, stride=CMP_STRIDE), :]
        a0 = (xs + pe_ref[s:s + 1, :]).astype(bf16)
        a1 = (xs + pe_ref[CMP_STRIDE + s:CMP_STRIDE + s + 1, :]).astype(bf16)
        acc0 = acc0 + jnp.dot(a0, w_ref[s].astype(bf16), preferred_element_type=f32)
        acc1 = acc1 + jnp.dot(a1, w_ref[CMP_STRIDE + s].astype(bf16), preferred_element_type=f32)
    o_ref[...] = (acc0 + pltpu.roll(acc1, ns - 1, axis=0)).astype(o_ref.dtype)


def compress_prompt(z, cmp_w, cmp_pe, bn, t):
    ns = t // CMP_STRIDE
    col0 = Z_KV // NSA_DH
    return pl.pallas_call(
        _compress_body,
        grid=(bn, 2, NSA_KV_GROUPS),
        in_specs=[pl.BlockSpec((t, NSA_DH), lambda b, x, g: (b, col0 + x * NSA_KV_GROUPS + g)),
                  pl.BlockSpec((None, CMP_BLOCK, NSA_DH, NSA_DH), lambda b, x, g: (x, 0, 0, 0)),
                  pl.BlockSpec((None, CMP_BLOCK, NSA_DH), lambda b, x, g: (x, 0, 0))],
        out_specs=pl.BlockSpec((None, None, None, ns, NSA_DH), lambda b, x, g: (b, x, g, 0, 0)),
        out_shape=jax.ShapeDtypeStruct((bn, 2, NSA_KV_GROUPS, ns, NSA_DH), bf16),
        compiler_params=_cparams(("parallel", "parallel", "parallel")),
        name="compress_prompt",
    )(z, cmp_w, cmp_pe)


def _nsa_prompt_body(q_ref, kc_ref, vc_ref, ks_ref, vs_ref, kw_ref, vw_ref, gate_ref, bc_ref, bt_ref,
                     cov_ref, exp_ref, o_ref, mb_ref, s_ref, mx_ref, m_ref, l_ref, acc_ref):
    g = pl.program_id(1)
    i = pl.program_id(2)
    qb = Q_BLOCK
    rows = NSA_HPG * qb
    qf = q_ref[...]
    q = jnp.concatenate([qf[:, h * NSA_DH:(h + 1) * NSA_DH] for h in range(NSA_HPG)], axis=0).astype(bf16)

    ncp = kc_ref.shape[0]
    n_done = (qb // CMP_STRIDE) * (i + 1)
    bias_c = pltpu.roll(bc_ref[...], n_done % ncp, axis=1)
    lane_c = lax.broadcasted_iota(jnp.int32, bias_c.shape, 1)
    bias_c = jnp.where(lane_c < n_done, bias_c, NEG_BIG)
    s = lax.dot_general(q, kc_ref[...], _NT, preferred_element_type=f32) * (NSA_DH ** -0.5) + bias_c
    m = jnp.max(s, -1, keepdims=True)
    m = jnp.where(m < 0.5 * NEG_BIG, 0.0, m)
    e = jnp.exp(s - m)
    p = e / jnp.maximum(jnp.sum(e, -1, keepdims=True), 1e-30)
    o_c = jnp.dot(p.astype(bf16), vc_ref[...], preferred_element_type=f32)

    psum = p[0:qb]
    for h in range(1, NSA_HPG):
        psum = psum + p[h * qb:(h + 1) * qb]
    imp_t = lax.dot_general(cov_ref[...], psum, _NT, preferred_element_type=f32,
                            precision=lax.Precision.HIGHEST)
    nsb = imp_t.shape[0]
    blk = lax.broadcasted_iota(jnp.int32, (nsb, qb), 0)
    qpos = i * qb + lax.broadcasted_iota(jnp.int32, (nsb, qb), 1)
    cur = qpos // SEL_BLOCK
    forced = (blk == 0) | (blk == cur) | (blk == cur - 1)
    valid = blk * SEL_BLOCK <= qpos
    score = jnp.where(valid, jnp.where(forced, FORCE_SCORE, imp_t), -1.0)
    sel_t = jnp.zeros((nsb, qb), f32)
    for _ in range(min(SEL_TOPK, nsb)):
        mx = jnp.max(score, axis=0, keepdims=True)
        first = jnp.min(jnp.where(score == mx, blk, nsb), axis=0, keepdims=True)
        pick = blk == first
        sel_t = jnp.where(pick, 1.0, sel_t)
        score = jnp.where(pick, -2.0, score)
    unsel = ((sel_t - 1.0) * (-NEG_BIG)).T.astype(bf16)
    mb_ref[...] = jnp.dot(unsel, exp_ref[...], preferred_element_type=f32)

    scale = NSA_DH ** -0.5

    mx_ref[...] = jnp.full(mx_ref.shape, NEG_BIG, f32)

    kw_ = NSA_KT * qb
    n_steps = (i + NSA_KT) // NSA_KT

    def score_step(jj, carry):
        r0 = pl.multiple_of(jj * kw_, kw_)
        k = ks_ref[pl.ds(r0, kw_), :].astype(bf16)
        biases = []
        for tt in range(NSA_KT):
            d = i - (jj * NSA_KT + tt)
            biases.append(bt_ref[jnp.where(d < 0, TILE_NONE, jnp.minimum(d, TILE_FAR))])
        mb = mb_ref[:, pl.ds(r0, kw_)]
        s = (lax.dot_general(q, k, _NT, preferred_element_type=f32) * scale
             + jnp.concatenate(biases, axis=1) + jnp.concatenate([mb] * NSA_HPG, axis=0))
        s_ref[:, pl.ds(r0, kw_)] = s
        mx = mx_ref[...]
        for tt in range(NSA_KT):
            mx = jnp.maximum(mx, s[:, tt * qb:(tt + 1) * qb])
        mx_ref[...] = mx
        return carry

    lax.fori_loop(0, n_steps, score_step, 0)
    m_ref[...] = jnp.broadcast_to(jnp.max(mx_ref[...], -1, keepdims=True), m_ref.shape)
    l_ref[...] = jnp.zeros(l_ref.shape, f32)
    acc_ref[...] = jnp.zeros(acc_ref.shape, f32)

    def pv_step(jj, carry):
        r0 = pl.multiple_of(jj * kw_, kw_)
        p = jnp.exp(s_ref[:, pl.ds(r0, kw_)] - jnp.concatenate([m_ref[...]] * NSA_KT, axis=1))
        v = vs_ref[pl.ds(r0, kw_), :].astype(bf16)
        lsum = l_ref[...]
        for tt in range(NSA_KT):
            lsum = lsum + p[:, tt * qb:(tt + 1) * qb]
        l_ref[...] = lsum
        acc_ref[...] = acc_ref[...] + jnp.dot(p.astype(bf16), v, preferred_element_type=f32)
        return carry

    lax.fori_loop(0, n_steps, pv_step, 0)
    o_s = acc_ref[...] / jnp.sum(l_ref[...], -1, keepdims=True)

    n_win = WINDOW // qb
    j0 = jnp.maximum(i - n_win, 0)
    s_w = []
    for tt in range(n_win + 1):
        d = i - (j0 + tt)
        r0 = pl.multiple_of((j0 + tt) * qb, qb)
        k = kw_ref[pl.ds(r0, qb), :].astype(bf16)
        tile = jnp.where(d < 0, TILE_NONE, jnp.where(d == n_win, TILE_WIN_EDGE, d))
        s_w.append(lax.dot_general(q, k, _NT, preferred_element_type=f32) * scale + bt_ref[tile])
    mw = s_w[0]
    for s in s_w[1:]:
        mw = jnp.maximum(mw, s)
    mw = jnp.max(mw, -1, keepdims=True)
    lw = jnp.zeros((rows, qb), f32)
    o_w = jnp.zeros((rows, NSA_DH), f32)
    for tt in range(n_win + 1):
        r0 = pl.multiple_of((j0 + tt) * qb, qb)
        p = jnp.exp(s_w[tt] - mw)
        lw = lw + p
        o_w = o_w + jnp.dot(p.astype(bf16), vw_ref[pl.ds(r0, qb), :].astype(bf16), preferred_element_type=f32)
    o_w = o_w / jnp.sum(lw, -1, keepdims=True)

    gt = jax.nn.sigmoid(gate_ref[...])
    outs = []
    for h in range(NSA_HPG):
        c = 3 * h
        sl = slice(h * qb, (h + 1) * qb)
        outs.append(gt[:, c:c + 1] * o_c[sl] + gt[:, c + 1:c + 2] * o_s[sl] + gt[:, c + 2:c + 3] * o_w[sl])
    o_ref[...] = jnp.concatenate(outs, axis=1).astype(o_ref.dtype)


def _nsa_bias_tables(rel_bias, t):
    qb = Q_BLOCK
    nq = t // qb
    tbl = rel_bias.astype(f32)
    r = jnp.arange(qb, dtype=jnp.int32)[:, None]
    c = jnp.arange(qb, dtype=jnp.int32)[None, :]
    tiles = []
    for delta in range(TILE_FAR + 1):
        dist = delta * qb + r - c
        tiles.append(jnp.where((dist >= 0)[..., None], tbl[_rel_bucket(dist)], NEG_BIG))
    dist = (WINDOW // qb) * qb + r - c
    tiles.append(jnp.where(((dist >= 0) & (dist < WINDOW))[..., None], tbl[_rel_bucket(dist)], NEG_BIG))
    tiles.append(jnp.full((qb, qb, NSA_HEADS), NEG_BIG, f32))
    bt = jnp.stack(tiles, 0)
    bt = bt.reshape(N_BIAS_TILES, qb, qb, NSA_KV_GROUPS, NSA_HPG)
    bt = jnp.transpose(bt, (3, 0, 4, 1, 2)).reshape(NSA_KV_GROUPS, N_BIAS_TILES, NSA_HPG * qb, qb)
    ncp = t // CMP_STRIDE
    m_back = (ncp - 1) - jnp.arange(ncp, dtype=jnp.int32)[None, :]
    dist_c = r + CMP_STRIDE * m_back + CMP_STRIDE - (qb + CMP_BLOCK - 1)
    bc = jnp.where((dist_c >= 0)[..., None], tbl[_rel_bucket(dist_c)], NEG_BIG)
    bc = bc.reshape(qb, ncp, NSA_KV_GROUPS, NSA_HPG)
    bc = jnp.transpose(bc, (2, 3, 0, 1)).reshape(NSA_KV_GROUPS, NSA_HPG * qb, ncp)
    return bt, bc


def nsa_prompt_attention(z, gates_g, kvc, rel_bias, bn, t):
    qb = Q_BLOCK
    nq = t // qb
    ncp = t // CMP_STRIDE
    nsb = t // SEL_BLOCK
    rows = NSA_HPG * qb
    bt, bc = _nsa_bias_tables(rel_bias, t)
    cover_t = jnp.concatenate([_cover_matrix(ncp - 1, nsb), jnp.zeros((1, nsb), f32)], 0).T
    expand = jnp.asarray(np.repeat(np.eye(nsb, dtype=np.float32), SEL_BLOCK, axis=1), bf16)
    kcol = Z_KV // NSA_DH
    kv_spec = lambda off: pl.BlockSpec((t, NSA_DH), lambda b, g, i, off=off: (b, kcol + off + g))
    return pl.pallas_call(
        _nsa_prompt_body,
        grid=(bn, NSA_KV_GROUPS, nq),
        in_specs=[pl.BlockSpec((qb, NSA_HPG * NSA_DH), lambda b, g, i: (b * nq + i, Z_NQ // (NSA_HPG * NSA_DH) + g)),
                  pl.BlockSpec((None, None, None, ncp, NSA_DH), lambda b, g, i: (b, 0, g, 0, 0)),
                  pl.BlockSpec((None, None, None, ncp, NSA_DH), lambda b, g, i: (b, 1, g, 0, 0)),
                  kv_spec(2 * NSA_KV_GROUPS), kv_spec(3 * NSA_KV_GROUPS),
                  kv_spec(4 * NSA_KV_GROUPS), kv_spec(5 * NSA_KV_GROUPS),
                  pl.BlockSpec((None, qb, 3 * NSA_HPG), lambda b, g, i: (g, b * nq + i, 0)),
                  pl.BlockSpec((None, rows, ncp), lambda b, g, i: (g, 0, 0)),
                  pl.BlockSpec((None, N_BIAS_TILES, rows, qb), lambda b, g, i: (g, 0, 0, 0)),
                  pl.BlockSpec((nsb, ncp), lambda b, g, i: (0, 0)),
                  pl.BlockSpec((nsb, t), lambda b, g, i: (0, 0))],
        out_specs=pl.BlockSpec((qb, NSA_HPG * NSA_DH), lambda b, g, i: (b * nq + i, g)),
        out_shape=jax.ShapeDtypeStruct((bn * t, NSA_Q_W), bf16),
        scratch_shapes=[pltpu.VMEM((qb, t), f32), pltpu.VMEM((rows, t), f32), pltpu.VMEM((rows, qb), f32),
                        pltpu.VMEM((rows, qb), f32), pltpu.VMEM((rows, qb), f32), pltpu.VMEM((rows, NSA_DH), f32)],
        compiler_params=_cparams(("parallel", "parallel", "arbitrary")),
        name="nsa_prompt_attention",
    )(z, kvc, kvc, z, z, z, z, gates_g, bc, bt, cover_t, expand)


GDN_RT = 512
GDN_GROUP = 4
_HI = lax.Precision.HIGHEST


def _mm1(a, b):
    return jnp.dot(a.astype(bf16), b.astype(bf16), preferred_element_type=f32)


def _mm3(a, b):
    ah, bh = a.astype(bf16), b.astype(bf16)
    al = (a - ah.astype(f32)).astype(bf16)
    bl = (b - bh.astype(f32)).astype(bf16)
    return (jnp.dot(ah, bh, preferred_element_type=f32) + jnp.dot(ah, bl, preferred_element_type=f32)
            + jnp.dot(al, bh, preferred_element_type=f32))


def _gdn_prep_body(q_ref, k_ref, v_ref, qp_ref, kp_ref, vp_ref, gb_ref, cwq_ref, cwk_ref, cwv_ref,
                   w_ref, u_ref, aqk_ref, qg_ref, kg_ref, gl_ref, qs_ref, ks_ref, vs_ref):
    rt = q_ref.shape[0]
    cs = GDN_CHUNK
    first = pl.program_id(2) == 0
    pad = SUBLANES

    def conv_silu(x_ref, prev_ref, scr_ref, cw_ref):
        scr_ref[0:pad, :] = jnp.where(first, 0.0, prev_ref[...])
        scr_ref[pad:pad + rt, :] = x_ref[...]
        y = scr_ref[pl.ds(pad - (GDN_CONV - 1), rt), :] * cw_ref[0:1, :]
        for j in range(1, GDN_CONV):
            y = y + scr_ref[pl.ds(pad - (GDN_CONV - 1) + j, rt), :] * cw_ref[j:j + 1, :]
        return y * jax.nn.sigmoid(y)

    q = conv_silu(q_ref, qp_ref, qs_ref, cwq_ref)
    k = conv_silu(k_ref, kp_ref, ks_ref, cwk_ref)
    v = conv_silu(v_ref, vp_ref, vs_ref, cwv_ref)
    q = q * lax.rsqrt(jnp.sum(q * q, -1, keepdims=True) + EPS) * (GDN_DK ** -0.5)
    k = k * lax.rsqrt(jnp.sum(k * k, -1, keepdims=True) + EPS)

    gb = gb_ref[...]
    x = gb[:, 1:2] + gb[:, 2:3]
    softplus = jnp.maximum(x, 0.0) + jnp.log1p(jnp.exp(-jnp.abs(x)))
    g = jnp.broadcast_to(-jnp.exp(gb[:, 3:4]) * softplus, (rt, GDN_DK))
    beta = jnp.broadcast_to(jax.nn.sigmoid(gb[:, 0:1]), (rt, GDN_DK))

    row_in_chunk = lax.broadcasted_iota(jnp.int32, (rt, GDN_DK), 0) % cs
    gcum_all = g
    step = 1
    while step < cs:
        gcum_all = gcum_all + jnp.where(row_in_chunk >= step, pltpu.roll(gcum_all, step, axis=0), 0.0)
        step *= 2

    gr = GDN_GROUP * cs
    ri = lax.broadcasted_iota(jnp.int32, (gr, gr), 0)
    ci = lax.broadcasted_iota(jnp.int32, (gr, gr), 1)
    same_chunk = (ri // cs) == (ci // cs)
    incl = same_chunk & (ri >= ci)
    strict = same_chunk & (ri > ci)
    eye = (ri == ci).astype(f32)
    gls = []
    for grp in range(rt // gr):
        sl = slice(grp * gr, (grp + 1) * gr)
        qc, kc, vc, bc_, gcum = q[sl], k[sl], v[sl], beta[sl], gcum_all[sl]
        g_col = jnp.concatenate([gcum] * (gr // GDN_DK), axis=1)
        g_row = jnp.broadcast_to(gcum.T[0:1, :], (gr, gr))
        dec = jnp.exp(jnp.where(incl, g_col - g_row, 0.0))
        eg = jnp.exp(gcum)
        lasts = [gcum[(c + 1) * cs - 1:(c + 1) * cs, :] for c in range(GDN_GROUP)]
        g_last = jnp.concatenate([jnp.broadcast_to(r, (cs, GDN_DK)) for r in lasts], axis=0)
        kb = kc * bc_
        kbf = kc.astype(bf16)
        lmat = lax.dot_general(kb.astype(bf16), kbf, _NT, preferred_element_type=f32) * jnp.where(strict, dec, 0.0)
        aqk = lax.dot_general(qc.astype(bf16), kbf, _NT, preferred_element_type=f32) * jnp.where(incl, dec, 0.0)
        inv = eye - lmat
        pw = lmat
        for _ in range(int(math.log2(cs)) - 1):
            pw = _mm1(pw, pw)
            inv = inv + _mm1(inv, pw)
        for _ in range(2):
            inv = inv + _mm1(inv, eye - inv - _mm3(lmat, inv))
        rhs = jnp.concatenate([kb * eg, vc * bc_], axis=1)
        wu = _mm3(inv, rhs)
        w_ref[sl, :] = wu[:, :GDN_DK].astype(w_ref.dtype)
        u_ref[sl, :] = wu[:, GDN_DK:]
        for c in range(GDN_GROUP):
            blk = slice(c * cs, (c + 1) * cs)
            aqk_ref[grp * gr + c * cs:grp * gr + (c + 1) * cs, :] = aqk[blk, blk].astype(aqk_ref.dtype)
        qg_ref[sl, :] = (qc * eg).astype(qg_ref.dtype)
        kg_ref[sl, :] = (kc * jnp.exp(g_last - gcum)).astype(kg_ref.dtype)
        gls.extend(jnp.exp(r) for r in lasts)
    gl_ref[...] = jnp.concatenate(gls, axis=0)


def _gdn_scan_body(w_ref, u_ref, aqk_ref, qg_ref, kg_ref, gl_ref, za_ref, gn_ref, o_ref, sfin_ref, s_ref):
    c = pl.program_id(1)
    nc = pl.num_programs(1)
    n_gl = gl_ref.shape[1]

    @pl.when(c == 0)
    def _():
        s_ref[...] = jnp.zeros(s_ref.shape, f32)

    for h in range(GDN_HEADS):
        s = s_ref[h]
        sb = s.astype(bf16)
        v_new = u_ref[h] - jnp.dot(w_ref[h], sb, preferred_element_type=f32)
        vb = v_new.astype(bf16)
        o = (jnp.dot(qg_ref[h], sb, preferred_element_type=f32)
             + jnp.dot(aqk_ref[h], vb, preferred_element_type=f32))
        gl = gl_ref[h, pl.ds(c % n_gl, 1), :]
        s_ref[h] = gl * s + lax.dot_general(kg_ref[h], vb, (((0,), (0,)), ((), ())), preferred_element_type=f32)
        on = o * lax.rsqrt(jnp.mean(o * o, -1, keepdims=True) + EPS) * gn_ref[...]
        za = za_ref[:, h * GDN_DV:(h + 1) * GDN_DV]
        o_ref[:, h * GDN_DV:(h + 1) * GDN_DV] = (on * (za * jax.nn.sigmoid(za))).astype(o_ref.dtype)

    @pl.when(c == nc - 1)
    def _():
        sfin_ref[...] = s_ref[...]


def gdn_prompt(z, gb, conv_w, gdn_norm, bn, t):
    rt, cs = GDN_RT, GDN_CHUNK
    n_rt = t // rt
    m = bn * t
    hcol = GDN_DK // LANES
    qkv = lambda part: pl.BlockSpec((rt, GDN_DK), lambda h, b, r, part=part: (b * n_rt + r, part * GDN_HEADS + h))
    prev = lambda part: pl.BlockSpec(
        (SUBLANES, GDN_DK),
        lambda h, b, r, part=part: (jnp.maximum((b * n_rt + r) * (rt // SUBLANES) - 1, 0), part * GDN_HEADS + h))
    cw = lambda part: pl.BlockSpec((GDN_CONV, GDN_DK), lambda h, b, r, part=part: (0, part * GDN_HEADS + h))
    per_row = lambda width: pl.BlockSpec((None, rt, width), lambda h, b, r: (h, b * n_rt + r, 0))
    w_c, u_c, aqk, qg, kg, gl = pl.pallas_call(
        _gdn_prep_body,
        grid=(GDN_HEADS, bn, n_rt),
        in_specs=[qkv(0), qkv(1), qkv(2), prev(0), prev(1), prev(2), per_row(4), cw(0), cw(1), cw(2)],
        out_specs=[per_row(GDN_DK), per_row(GDN_DV), per_row(cs), per_row(GDN_DK), per_row(GDN_DK),
                   pl.BlockSpec((None, rt // cs, GDN_DK), lambda h, b, r: (h, b * n_rt + r, 0))],
        out_shape=[jax.ShapeDtypeStruct((GDN_HEADS, m, GDN_DK), bf16),
                   jax.ShapeDtypeStruct((GDN_HEADS, m, GDN_DV), f32),
                   jax.ShapeDtypeStruct((GDN_HEADS, m, cs), bf16),
                   jax.ShapeDtypeStruct((GDN_HEADS, m, GDN_DK), bf16),
                   jax.ShapeDtypeStruct((GDN_HEADS, m, GDN_DK), bf16),
                   jax.ShapeDtypeStruct((GDN_HEADS, m // cs, GDN_DK), f32)],
        scratch_shapes=[pltpu.VMEM((rt + SUBLANES, GDN_DK), f32)] * 3,
        compiler_params=_cparams(("parallel", "parallel", "parallel")),
        name="gdn_prep",
    )(z, z, z, z, z, z, gb, conv_w, conv_w, conv_w)

    nc = t // cs
    n_gl = rt // cs
    heads = lambda width: pl.BlockSpec((GDN_HEADS, cs, width), lambda b, c: (0, b * nc + c, 0))
    o, s_fin = pl.pallas_call(
        _gdn_scan_body,
        grid=(bn, nc),
        in_specs=[heads(GDN_DK), heads(GDN_DV), heads(cs), heads(GDN_DK), heads(GDN_DK),
                  pl.BlockSpec((GDN_HEADS, n_gl, GDN_DK), lambda b, c: (0, (b * nc + c) // n_gl, 0)),
                  pl.BlockSpec((cs, GDN_V_W), lambda b, c: (b * nc + c, Z_ZG // GDN_V_W)),
                  pl.BlockSpec((1, GDN_DV), lambda b, c: (0, 0))],
        out_specs=[pl.BlockSpec((cs, GDN_V_W), lambda b, c: (b * nc + c, 0)),
                   pl.BlockSpec((None, GDN_HEADS, GDN_DK, GDN_DV), lambda b, c: (b, 0, 0, 0))],
        out_shape=[jax.ShapeDtypeStruct((m, GDN_V_W), bf16),
                   jax.ShapeDtypeStruct((bn, GDN_HEADS, GDN_DK, GDN_DV), f32)],
        scratch_shapes=[pltpu.VMEM((GDN_HEADS, GDN_DK, GDN_DV), f32)],
        compiler_params=_cparams(("parallel", "arbitrary")),
        name="gdn_scan",
    )(w_c, u_c, aqk, qg, kg, gl, z, gdn_norm.reshape(1, GDN_DV))
    return o, s_fin


GDN_SB = 8


def _gdn_sample_body(x_ref, za_ref, gb_ref, sc_ref, s0_ref, cw_ref, gn_ref, o_ref, s_ref):
    sb = GDN_SB
    x3 = sc_ref[...]
    y = x_ref[...] * cw_ref[GDN_CONV - 1:GDN_CONV, :]
    for j in range(GDN_CONV - 1):
        y = y + x3[:, j, :] * cw_ref[j:j + 1, :]
    xc = y * jax.nn.sigmoid(y)
    for h in range(GDN_HEADS):
        q = xc[:, h * GDN_DK:(h + 1) * GDN_DK]
        k = xc[:, GDN_QK_W + h * GDN_DK:GDN_QK_W + (h + 1) * GDN_DK]
        v = xc[:, 2 * GDN_QK_W + h * GDN_DV:2 * GDN_QK_W + (h + 1) * GDN_DV]
        q = q * lax.rsqrt(jnp.sum(q * q, -1, keepdims=True) + EPS) * (GDN_DK ** -0.5)
        k = k * lax.rsqrt(jnp.sum(k * k, -1, keepdims=True) + EPS)
        gb = gb_ref[h]
        x = gb[:, 1:2] + gb[:, 2:3]
        softplus = jnp.maximum(x, 0.0) + jnp.log1p(jnp.exp(-jnp.abs(x)))
        eg = jnp.broadcast_to(jnp.exp(-jnp.exp(gb[:, 3:4]) * softplus), (sb, GDN_DV))
        beta = jnp.broadcast_to(jax.nn.sigmoid(gb[:, 0:1]), (sb, GDN_DV))
        qk = jnp.broadcast_to(jnp.sum(q * k, -1, keepdims=True), (sb, GDN_DV))
        q_t = q.T
        k_t = k.T
        o_rows = []
        for i in range(sb):
            s = s0_ref[i, h]
            kcol = k_t[:, i:i + 1]
            k_s = jnp.sum(kcol * s, axis=0, keepdims=True)
            q_s = jnp.sum(q_t[:, i:i + 1] * s, axis=0, keepdims=True)
            eg_i = eg[i:i + 1]
            v_new = beta[i:i + 1] * (v[i:i + 1] - eg_i * k_s)
            o_rows.append(eg_i * q_s + qk[i:i + 1] * v_new)
            s_ref[i, h] = eg_i * s + kcol * v_new
        o = jnp.concatenate(o_rows, axis=0)
        on = o * lax.rsqrt(jnp.mean(o * o, -1, keepdims=True) + EPS) * gn_ref[...]
        za = za_ref[:, h * GDN_DV:(h + 1) * GDN_DV]
        o_ref[:, h * GDN_DV:(h + 1) * GDN_DV] = on * (za * jax.nn.sigmoid(za))


def gdn_sample(z, gb, state_conv, state_gdn, conv_w, gdn_norm, row0, bs):
    sb = GDN_SB
    r0 = row0 // sb
    return pl.pallas_call(
        _gdn_sample_body,
        grid=(bs // sb,),
        in_specs=[pl.BlockSpec((sb, CONV_CH), lambda i: (r0 + i, 0)),
                  pl.BlockSpec((sb, GDN_V_W), lambda i: (r0 + i, Z_ZG // GDN_V_W)),
                  pl.BlockSpec((GDN_HEADS, sb, 4), lambda i: (0, r0 + i, 0)),
                  pl.BlockSpec((sb, GDN_CONV - 1, CONV_CH), lambda i: (i, 0, 0)),
                  pl.BlockSpec((sb, GDN_HEADS, GDN_DK, GDN_DV), lambda i: (i, 0, 0, 0)),
                  pl.BlockSpec((GDN_CONV, CONV_CH), lambda i: (0, 0)),
                  pl.BlockSpec((1, GDN_DV), lambda i: (0, 0))],
        out_specs=[pl.BlockSpec((sb, GDN_V_W), lambda i: (i, 0)),
                   pl.BlockSpec((sb, GDN_HEADS, GDN_DK, GDN_DV), lambda i: (i, 0, 0, 0))],
        out_shape=[jax.ShapeDtypeStruct((bs, GDN_V_W), f32),
                   jax.ShapeDtypeStruct((bs, GDN_HEADS, GDN_DK, GDN_DV), f32)],
        compiler_params=_cparams(("parallel",)),
        name="gdn_sample",
    )(z, z, gb, state_conv, state_gdn, conv_w, gdn_norm.reshape(1, GDN_DV))


SMP_PAGES = 8
_TN = (((0,), (0,)), ((), ()))
_KV_ROW = 2 * NSA_KV_GROUPS * NSA_DH


def _page_specs(shape):
    return [pl.BlockSpec((None,) + shape, lambda b, p, pt, kk=kk: (pt[b, p * SMP_PAGES + kk], 0, 0))
            for kk in range(SMP_PAGES)]


def _compress_sample_body(pt_ref, *refs):
    pages = refs[:SMP_PAGES]
    w_ref, o_ref = refs[SMP_PAGES:]
    per = PAGE_SIZE // CMP_STRIDE
    slabs = 2 * NSA_KV_GROUPS
    for x in range(2):
        rows = []
        for kk in range(SMP_PAGES):
            for g in range(NSA_KV_GROUPS):
                slab = x * NSA_KV_GROUPS + g
                parts = [pages[kk][pl.ds(s * slabs + slab, per, stride=CMP_STRIDE * slabs), :]
                         for s in range(CMP_STRIDE)]
                rows.append(jnp.concatenate(parts, axis=1))
        lhs = jnp.concatenate(rows, axis=0).astype(bf16)
        prod = jnp.dot(lhs, w_ref[x], preferred_element_type=f32)
        for kk in range(SMP_PAGES):
            for g in range(NSA_KV_GROUPS):
                r = (kk * NSA_KV_GROUPS + g) * per
                o_ref[x, g, kk * per:(kk + 1) * per, :] = prod[r:r + per]


def compress_sample(page_table, cache, w2):
    bs, n_pages = page_table.shape
    per = PAGE_SIZE // CMP_STRIDE
    n_sub = n_pages * per
    grid_spec = pltpu.PrefetchScalarGridSpec(
        num_scalar_prefetch=1,
        grid=(bs, n_pages // SMP_PAGES),
        in_specs=(_page_specs((PAGE_SIZE * 2 * NSA_KV_GROUPS, NSA_DH))
                  + [pl.BlockSpec((2, CMP_STRIDE * NSA_DH, 2 * NSA_DH), lambda b, p, pt: (0, 0, 0))]),
        out_specs=pl.BlockSpec((None, 2, NSA_KV_GROUPS, SMP_PAGES * per, 2 * NSA_DH), lambda b, p, pt: (b, 0, 0, p, 0)),
    )
    return pl.pallas_call(
        _compress_sample_body,
        grid_spec=grid_spec,
        out_shape=jax.ShapeDtypeStruct((bs, 2, NSA_KV_GROUPS, n_sub, 2 * NSA_DH), f32),
        compiler_params=_cparams(("parallel", "arbitrary")),
        name="compress_sample",
    )(page_table, *([cache] * SMP_PAGES), w2)


def _nsa_sample_select_body(qt_ref, pm_ref, w_ref, pe_ref, bc_ref, cov_ref, win_ref, bw_ref, b0_ref, kn_ref, vn_ref,
                            oc_ref, ow_ref, mask_ref, sn_ref, *, past):
    scale = NSA_DH ** -0.5
    qt = qt_ref[...]
    n_sub = pm_ref.shape[2]
    lane1 = lax.broadcasted_iota(jnp.int32, (1, LANES), 1)
    lane_n = lax.broadcasted_iota(jnp.int32, (n_sub, LANES), 1)
    pe_term = []
    for x in range(2):
        pr = jnp.dot(pe_ref[x], w_ref[x], preferred_element_type=f32)
        pe_term.append(pr[0:1, :NSA_DH] + pr[1:2, NSA_DH:])
    oc_t = jnp.zeros((NSA_DH, LANES), f32)
    ow_t = jnp.zeros((NSA_DH, LANES), f32)
    psum_all = jnp.zeros((n_sub, LANES), f32)
    snew = jnp.zeros((SUBLANES, LANES), f32)
    for g in range(NSA_KV_GROUPS):
        in_g = (lane1 // NSA_HPG) == g
        kvc = []
        for x in range(2):
            pm = pm_ref[x, g]
            kvc.append((pm[:, :NSA_DH] + pltpu.roll(pm[:, NSA_DH:], n_sub - 1, axis=0) + pe_term[x]).astype(bf16))
        s = jnp.dot(kvc[0], qt, preferred_element_type=f32) * scale + bc_ref[...]
        e = jnp.exp(s - jnp.max(s, axis=0, keepdims=True))
        p = jnp.where(in_g, e / jnp.sum(e, axis=0, keepdims=True), 0.0)
        oc_t = oc_t + lax.dot_general(kvc[1], p.astype(bf16), _TN, preferred_element_type=f32)
        psum = jnp.sum(p, axis=1, keepdims=True)
        psum_all = jnp.where((lane_n // NSA_HPG) == g, psum, psum_all)
        slabs = 2 * NSA_KV_GROUPS
        w_buf = win_ref.shape[0] // slabs
        kw = win_ref[pl.ds(g, w_buf, stride=slabs), :].astype(bf16)
        vw = win_ref[pl.ds(NSA_KV_GROUPS + g, w_buf, stride=slabs), :].astype(bf16)
        sw = jnp.dot(kw, qt, preferred_element_type=f32) * scale + bw_ref[...]
        sn = jnp.dot(kn_ref[g], qt, preferred_element_type=f32) * scale + b0_ref[...]
        mw = jnp.maximum(jnp.max(sw, axis=0, keepdims=True), sn[1:2])
        ew = jnp.exp(sw - mw)
        en = jnp.exp(sn[1:2] - mw)
        lw = jnp.sum(ew, axis=0, keepdims=True) + en
        pw = jnp.where(in_g, ew / lw, 0.0)
        row = lax.broadcasted_iota(jnp.int32, (SUBLANES, LANES), 0)
        pn = jnp.where((row == 1) & in_g, en / lw, 0.0)
        ow_t = (ow_t + lax.dot_general(vw, pw.astype(bf16), _TN, preferred_element_type=f32)
                + lax.dot_general(vn_ref[g], pn.astype(bf16), _TN, preferred_element_type=f32))
        snew = jnp.where(in_g, sn, snew)
    oc_ref[...] = oc_t
    ow_ref[...] = ow_t
    sn_ref[...] = snew

    imp_t = jnp.dot(cov_ref[...], psum_all, preferred_element_type=f32, precision=_HI)
    nsb = imp_t.shape[0]
    n_blocks = past // SEL_BLOCK + 1
    blk = lax.broadcasted_iota(jnp.int32, (nsb, LANES), 0)
    cur = past // SEL_BLOCK
    forced = (blk == 0) | (blk == cur) | (blk == cur - 1)
    valid = (blk * SEL_BLOCK <= past) & (blk < n_blocks)
    score = jnp.where(valid, jnp.where(forced, FORCE_SCORE, imp_t), -1.0)
    sel_t = jnp.zeros((nsb, LANES), f32)
    for _ in range(min(SEL_TOPK, n_blocks)):
        mx = jnp.max(score, axis=0, keepdims=True)
        first = jnp.min(jnp.where(score == mx, blk, nsb), axis=0, keepdims=True)
        pick = blk == first
        sel_t = jnp.where(pick, 1.0, sel_t)
        score = jnp.where(pick, -2.0, score)
    mask_ref[...] = (sel_t - 1.0) * (-NEG_BIG)


def _nsa_sample_attend_body(pt_ref, *refs):
    pages = refs[:SMP_PAGES]
    (qt_ref, bias_ref, mask_ref, sn_ref, vn_ref, oc_ref, ow_ref, gate_ref, o_ref, m_ref, l_ref, acc_ref) = refs[SMP_PAGES:]
    p = pl.program_id(1)
    grp = lax.broadcasted_iota(jnp.int32, (1, LANES), 1) // NSA_HPG

    @pl.when(p == 0)
    def _():
        m_ref[...] = sn_ref[0:1, :]
        l_ref[...] = jnp.ones(l_ref.shape, f32)
        acc_ref[...] = vn_ref[...]

    qt = qt_ref[...]
    slabs = 2 * NSA_KV_GROUPS
    s = None
    for g in range(NSA_KV_GROUPS):
        kg = jnp.concatenate([pages[kk][pl.ds(g, PAGE_SIZE, stride=slabs), :] for kk in range(SMP_PAGES)], axis=0)
        sg = jnp.dot(kg.astype(bf16), qt, preferred_element_type=f32)
        s = sg if g == 0 else jnp.where(grp == g, sg, s)
    bpp = PAGE_SIZE // SEL_BLOCK
    mrows = []
    for kk in range(SMP_PAGES):
        for half in range(bpp):
            mrow = mask_ref[pl.ds((p * SMP_PAGES + kk) * bpp + half, 1), :]
            mrows.append(jnp.broadcast_to(mrow, (SEL_BLOCK, LANES)))
    s = s * (NSA_DH ** -0.5) + bias_ref[...] + jnp.concatenate(mrows, axis=0)
    m_old = m_ref[...]
    m_new = jnp.maximum(m_old, jnp.max(s, axis=0, keepdims=True))
    alpha = jnp.exp(m_old - m_new)
    pe = jnp.exp(s - m_new)
    l_ref[...] = alpha * l_ref[...] + jnp.sum(pe, axis=0, keepdims=True)
    pb = pe.astype(bf16)
    upd = jnp.zeros(acc_ref.shape, f32)
    for g in range(NSA_KV_GROUPS):
        vg = jnp.concatenate([pages[kk][pl.ds(NSA_KV_GROUPS + g, PAGE_SIZE, stride=slabs), :]
                              for kk in range(SMP_PAGES)], axis=0)
        upd = upd + lax.dot_general(vg.astype(bf16), jnp.where(grp == g, pb, jnp.zeros_like(pb)), _TN,
                                    preferred_element_type=f32)
    acc_ref[...] = alpha * acc_ref[...] + upd
    m_ref[...] = m_new

    @pl.when(p == pl.num_programs(1) - 1)
    def _():
        gt = jax.nn.sigmoid(gate_ref[...])
        o_s = acc_ref[...] / l_ref[...]
        o_ref[...] = gt[0:1] * oc_ref[...] + gt[1:2] * o_s + gt[2:3] * ow_ref[...]


def nsa_sample(z, zs, row0, cache_c, cache_s, page_table, win_buf, cmp_pe, cmp_w, rel_bias):
    bs, n_pages = page_table.shape
    past = n_pages * PAGE_SIZE
    n_sub = past // CMP_STRIDE
    n_blocks = past // SEL_BLOCK + 1
    nsb = -(-n_blocks // SUBLANES) * SUBLANES
    w_buf = win_buf.shape[1]
    tbl = rel_bias.astype(f32)
    lane_pad = lambda a: jnp.pad(a, [(0, 0)] * (a.ndim - 1) + [(0, LANES - a.shape[-1])])

    def bias_rows(dist, ok):
        return lane_pad(jnp.where(ok[:, None], tbl[_rel_bucket(dist)], NEG_BIG))

    dist_c = past - (jnp.arange(n_sub, dtype=jnp.int32) * CMP_STRIDE + (CMP_BLOCK - 1))
    bias_c = bias_rows(dist_c, dist_c >= 0)
    w_pos = past - w_buf + jnp.arange(w_buf, dtype=jnp.int32)
    dist_w = past - w_pos
    bias_w = bias_rows(dist_w, (dist_w < WINDOW) & (w_pos >= 0))
    bias_0 = jnp.broadcast_to(lane_pad(tbl[_rel_bucket(jnp.zeros((1,), jnp.int32))]), (SUBLANES, LANES))
    dist_s = past - jnp.arange(past, dtype=jnp.int32)
    bias_s = bias_rows(dist_s, dist_s >= 0)
    cover_t = jnp.pad(_cover_matrix(n_sub - 1, n_blocks), [(0, 1), (0, nsb - n_blocks)]).T

    w2 = cmp_w.reshape(2, CMP_BLOCK // CMP_STRIDE, CMP_STRIDE * NSA_DH, NSA_DH)
    w2 = jnp.transpose(w2, (0, 2, 1, 3)).reshape(2, CMP_STRIDE * NSA_DH, 2 * NSA_DH).astype(bf16)
    pe2 = jnp.pad(cmp_pe.reshape(2, CMP_BLOCK // CMP_STRIDE, CMP_STRIDE * NSA_DH), [(0, 0), (0, SUBLANES - 2), (0, 0)])
    pe2 = pe2.astype(bf16)

    zrow = z[row0:row0 + bs]
    q = zrow[:, Z_NQ:Z_KV].reshape(bs, NSA_HEADS, NSA_DH)
    qt = lane_pad(jnp.transpose(q, (0, 2, 1))).astype(bf16)
    kv = zrow[:, Z_KV:Z_GA].reshape(bs, 3, 2, NSA_KV_GROUPS, NSA_DH)
    zero = jnp.zeros((bs, NSA_KV_GROUPS, NSA_DH), f32)
    pad_rows = lambda r0, r1: jnp.pad(jnp.stack([r0, r1], 2), [(0, 0), (0, 0), (0, SUBLANES - 2), (0, 0)]).astype(bf16)
    k_new = pad_rows(kv[:, 1, 0], kv[:, 2, 0])
    v_new_w = pad_rows(zero, kv[:, 2, 1])
    v_new_s = lane_pad(jnp.transpose(jnp.repeat(kv[:, 1, 1], NSA_HPG, axis=1), (0, 2, 1)))
    gates = zs[row0:row0 + bs, 2 * GDN_HEADS:2 * GDN_HEADS + 3 * NSA_HEADS].reshape(bs, NSA_HEADS, 3)
    gates = jnp.pad(lane_pad(jnp.transpose(gates, (0, 2, 1))), [(0, 0), (0, SUBLANES - 3), (0, 0)])

    pm = compress_sample(page_table, cache_c.reshape(-1, PAGE_SIZE * 2 * NSA_KV_GROUPS, NSA_DH), w2)
    per_seq = lambda *shape: pl.BlockSpec((None,) + shape, lambda b: (b,) + (0,) * len(shape))
    const = lambda *shape: pl.BlockSpec(shape, lambda b: (0,) * len(shape))
    oc_t, ow_t, mask, s_new = pl.pallas_call(
        functools.partial(_nsa_sample_select_body, past=past),
        grid=(bs,),
        in_specs=[per_seq(NSA_DH, LANES), per_seq(2, NSA_KV_GROUPS, n_sub, 2 * NSA_DH),
                  const(2, CMP_STRIDE * NSA_DH, 2 * NSA_DH), const(2, SUBLANES, CMP_STRIDE * NSA_DH),
                  const(n_sub, LANES), const(nsb, n_sub), per_seq(w_buf * 2 * NSA_KV_GROUPS, NSA_DH),
                  const(w_buf, LANES),
                  const(SUBLANES, LANES), per_seq(NSA_KV_GROUPS, SUBLANES, NSA_DH),
                  per_seq(NSA_KV_GROUPS, SUBLANES, NSA_DH)],
        out_specs=[per_seq(NSA_DH, LANES), per_seq(NSA_DH, LANES), per_seq(nsb, LANES), per_seq(SUBLANES, LANES)],
        out_shape=[jax.ShapeDtypeStruct((bs, NSA_DH, LANES), f32), jax.ShapeDtypeStruct((bs, NSA_DH, LANES), f32),
                   jax.ShapeDtypeStruct((bs, nsb, LANES), f32), jax.ShapeDtypeStruct((bs, SUBLANES, LANES), f32)],
        compiler_params=_cparams(("parallel",)),
        name="nsa_sample_select",
    )(qt, pm, w2, pe2, bias_c, cover_t, win_buf.reshape(bs, w_buf * 2 * NSA_KV_GROUPS, NSA_DH), bias_w, bias_0,
      k_new, v_new_w)

    seq = lambda *shape: pl.BlockSpec((None,) + shape, lambda b, p, pt: (b,) + (0,) * len(shape))
    grid_spec = pltpu.PrefetchScalarGridSpec(
        num_scalar_prefetch=1,
        grid=(bs, n_pages // SMP_PAGES),
        in_specs=_page_specs((PAGE_SIZE * 2 * NSA_KV_GROUPS, NSA_DH)) + [seq(NSA_DH, LANES),
                                  pl.BlockSpec((SMP_PAGES * PAGE_SIZE, LANES), lambda b, p, pt: (p, 0)),
                                  seq(nsb, LANES), seq(SUBLANES, LANES), seq(NSA_DH, LANES), seq(NSA_DH, LANES),
                                  seq(NSA_DH, LANES), seq(SUBLANES, LANES)],
        out_specs=seq(NSA_DH, LANES),
        scratch_shapes=[pltpu.VMEM((1, LANES), f32), pltpu.VMEM((1, LANES), f32), pltpu.VMEM((NSA_DH, LANES), f32)],
    )
    o_t = pl.pallas_call(
        _nsa_sample_attend_body,
        grid_spec=grid_spec,
        out_shape=jax.ShapeDtypeStruct((bs, NSA_DH, LANES), f32),
        compiler_params=_cparams(("parallel", "arbitrary")),
        name="nsa_sample_attend",
    )(page_table, *([cache_s.reshape(-1, PAGE_SIZE * 2 * NSA_KV_GROUPS, NSA_DH)] * SMP_PAGES), qt, bias_s, mask, s_new, v_new_s,
      oc_t, ow_t, gates)
    return jnp.transpose(o_t[:, :, :NSA_HEADS], (0, 2, 1)).reshape(bs, NSA_Q_W)


def _rel_bucket(dist):
    n = jnp.maximum(dist, 0)
    max_exact = REL_BUCKETS // 2
    nf = jnp.maximum(n, 1).astype(f32)
    large = max_exact + (jnp.log(nf / max_exact) / math.log(REL_MAX_DIST / max_exact)
                         * (REL_BUCKETS - max_exact)).astype(jnp.int32)
    large = jnp.minimum(large, REL_BUCKETS - 1)
    return jnp.where(n < max_exact, n, large)


def _cover_matrix(nc, ns):
    cs = np.arange(nc) * CMP_STRIDE
    ss = np.arange(ns) * SEL_BLOCK
    inter = np.minimum(cs[:, None] + CMP_BLOCK, ss[None, :] + SEL_BLOCK) - np.maximum(cs[:, None], ss[None, :])
    return jnp.asarray(np.clip(inter, 0, None) / CMP_BLOCK, dtype=f32)


def kernel(x_prompt, x_sample, p_prompt, p_sample, cache_cmp_kv, cache_slc_kv, page_table, state_win_kv, state_gdn, state_conv, g_mix, w_in, gdn_conv_w, gdn_dt_bias, gdn_a_log, gdn_norm, cmp_pe, cmp_w, rel_bias, w_proj_a, w_proj_b, w_out, g_ffn, w_router_group, b_router_group, w_router_expert, b_router_expert, w_gate, w_up, w_down, g_ple, w_ple_gate, w_ple_proj, g_final):
    bp, tp, d = x_prompt.shape
    bs, ts, _ = x_sample.shape
    n_p, n_s = bp * tp, bs * ts
    n_real = n_p + n_s
    mp = -(-n_real // ROW_ALIGN) * ROW_ALIGN
    pad = mp - n_real

    h = jnp.concatenate([x_prompt.reshape(n_p, d), x_sample.reshape(n_s, d), jnp.zeros((pad, d), f32)], 0)
    ple = jnp.concatenate([p_prompt[0].reshape(n_p, -1), p_sample[0].reshape(n_s, -1),
                           jnp.zeros((pad, p_prompt.shape[-1]), f32)], 0).astype(bf16)

    w = w_in[0]
    o_beta = 4 * GDN_QK_W
    o_nq = o_beta + 2 * GDN_HEADS
    o_gate = o_nq + NSA_Q_W + 6 * NSA_KV_W
    o_ga = o_gate + 3 * NSA_HEADS
    w_main = jnp.concatenate([w[:, :o_beta], w[:, o_nq:o_gate], w[:, o_ga:]], axis=1)
    n_small = 2 * GDN_HEADS + 3 * NSA_HEADS
    w_small = jnp.concatenate([w[:, o_beta:o_nq], w[:, o_gate:o_ga], jnp.zeros((d, LANES - n_small), f32)], axis=1)

    a = rmsnorm_rows(h, g_mix[0], bf16)
    z = proj_matmul(a, w_main, f32, TN_DENSE)
    zs = proj_matmul(a, w_small, f32, LANES)

    def rows(x, lo, hi, which):
        if which == "p":
            return x[:n_p, lo:hi].reshape(bp, tp, hi - lo)
        return x[n_p:n_real, lo:hi].reshape(bs, ts, hi - lo)

    assert ts == 1 and tp % GDN_RT == 0 and n_p % GDN_SB == 0 and bs % GDN_SB == 0

    gb = jnp.stack([zs[:, 0:GDN_HEADS], zs[:, GDN_HEADS:2 * GDN_HEADS],
                    jnp.broadcast_to(gdn_dt_bias[0], (mp, GDN_HEADS)),
                    jnp.broadcast_to(gdn_a_log[0], (mp, GDN_HEADS))], -1)
    gb = jnp.transpose(gb, (1, 0, 2))
    gates_g = jnp.transpose(zs[:, 2 * GDN_HEADS:n_small].reshape(mp, NSA_KV_GROUPS, 3 * NSA_HPG), (1, 0, 2))

    outs = {}
    kv = rows(z, Z_KV, Z_GA, "p").reshape(bp, tp, 3, 2, NSA_KV_GROUPS, NSA_DH)
    kvc = compress_prompt(z, cmp_w[0], cmp_pe[0], bp, tp)
    o_b_p = nsa_prompt_attention(z, gates_g, kvc, rel_bias, bp, tp)
    o_a_p, s_p = gdn_prompt(z, gb, gdn_conv_w[0], gdn_norm[0], bp, tp)
    conv_p = jnp.stack([z[(b + 1) * tp - (GDN_CONV - 1):(b + 1) * tp, :CONV_CH] for b in range(bp)], 0)
    outs["p"] = (kv[:, :, 0], kv[:, :, 1], kv[:, :, 2][:, tp - min(WINDOW, tp):], s_p, conv_p)
    kv = rows(z, Z_KV, Z_GA, "s").reshape(bs, ts, 3, 2, NSA_KV_GROUPS, NSA_DH)
    o_b_s = nsa_sample(z, zs, n_p, cache_cmp_kv[0], cache_slc_kv[0], page_table, state_win_kv[0],
                       cmp_pe[0], cmp_w[0], rel_bias)
    win_s = jnp.concatenate([state_win_kv[0], kv[:, :, 2]], axis=1)[:, ts:]
    o_a_s, s_s = gdn_sample(z, gb, state_conv[0], state_gdn[0], gdn_conv_w[0], gdn_norm[0], n_p, bs)
    conv_s = jnp.concatenate([state_conv[0][:, 1:], rows(z, 0, CONV_CH, "s")], axis=1)
    outs["s"] = (kv[:, :, 0], kv[:, :, 1], win_s, s_s, conv_s)

    o_a = jnp.concatenate([o_a_p, o_a_s.astype(bf16), jnp.zeros((pad, GDN_V_W), bf16)], 0)
    o_b = jnp.concatenate([o_b_p, o_b_s.reshape(n_s, NSA_Q_W).astype(bf16), jnp.zeros((pad, NSA_Q_W), bf16)], 0)
    merged = merge_matmul(o_a, o_b, z, w_proj_a[0], w_proj_b[0])
    h = resid_matmul(merged, w_out[0], h)
    h = hier_moe(h, n_real, g_ffn[0], w_router_group[0], b_router_group[0], w_router_expert[0],
                 b_router_expert[0], w_gate[0], w_up[0], w_down[0])
    n3 = rmsnorm_rows(h, g_ple[0], bf16)
    h = ple_matmul(n3, w_ple_gate[0], ple, w_ple_proj[0], h)
    y = rmsnorm_rows(h, g_final, f32)
    y_prompt = y[:n_p].reshape(bp, tp, d)
    y_sample = y[n_p:n_real].reshape(bs, ts, d)
    st_p, st_s = outs["p"], outs["s"]
    return (y_prompt, y_sample) + tuple(t[None] for t in st_p) + tuple(t[None] for t in st_s)
```

```python
import functools
import math

import jax
import jax.numpy as jnp
import numpy as np
from jax import lax
from jax.experimental import pallas as pl
from jax.experimental.pallas import tpu as pltpu

D_MODEL = 4096
GDN_HEADS = 16
GDN_DK = 128
GDN_DV = 128
GDN_CONV = 4
GDN_CHUNK = 64
NSA_HEADS = 16
NSA_KV_GROUPS = 4
NSA_HPG = NSA_HEADS // NSA_KV_GROUPS
NSA_DH = 128
CMP_BLOCK = 32
CMP_STRIDE = 16
SEL_BLOCK = 64
SEL_TOPK = 16
WINDOW = 512
Q_BLOCK = 128
FORCE_SCORE = 1.0e4
REL_BUCKETS = 32
REL_MAX_DIST = 1024
PAGE_SIZE = 128
MOE_GROUPS = 4
MOE_PER_GROUP = 8
MOE_EXPERTS = MOE_GROUPS * MOE_PER_GROUP
MOE_TOPK = 2
EXPERT_HIDDEN = 512
EPS = 1e-6

GDN_QK_W = GDN_HEADS * GDN_DK
GDN_V_W = GDN_HEADS * GDN_DV
CONV_CH = 2 * GDN_QK_W + GDN_V_W
NSA_Q_W = NSA_HEADS * NSA_DH
NSA_KV_W = NSA_KV_GROUPS * NSA_DH

LANES = 128
SUBLANES = 8
VMEM_LIMIT = 56 * 1024 * 1024

ROW_ALIGN = 768
TM_DENSE = 768
TN_DENSE = 512
TM_ROWS = 256

Z_Q, Z_K, Z_V, Z_ZG = 0, 2048, 4096, 6144
Z_NQ = 8192
Z_KV = 10240
Z_GA = 13312
Z_GB = 17408
Z_COLS = 21504

bf16 = jnp.bfloat16
f32 = jnp.float32


def _cparams(sem):
    return pltpu.CompilerParams(dimension_semantics=sem, vmem_limit_bytes=VMEM_LIMIT)


def _rmsnorm_body(x_ref, g_ref, o_ref):
    x = x_ref[...]
    y = x * lax.rsqrt(jnp.mean(x * x, -1, keepdims=True) + EPS)
    o_ref[...] = (y * g_ref[...]).astype(o_ref.dtype)


def rmsnorm_rows(x, g, out_dtype, row0=0, n_rows=None):
    m, d = x.shape
    n_rows = m if n_rows is None else n_rows
    t0 = row0 // TM_ROWS
    return pl.pallas_call(
        _rmsnorm_body,
        grid=(n_rows // TM_ROWS,),
        in_specs=[pl.BlockSpec((TM_ROWS, d), lambda i: (t0 + i, 0)),
                  pl.BlockSpec((1, d), lambda i: (0, 0))],
        out_specs=pl.BlockSpec((TM_ROWS, d), lambda i: (i, 0)),
        out_shape=jax.ShapeDtypeStruct((n_rows, d), out_dtype),
        compiler_params=_cparams(("parallel",)),
        name="rmsnorm_rows",
    )(x, g.reshape(1, d))


def _proj_body(a_ref, wa_ref, wb_ref, *rest, shift):
    o_ref, ws_ref = rest[-2:]

    @pl.when(pl.program_id(1) == 0)
    def _():
        if shift:
            w = jnp.concatenate([wa_ref[:, shift:], wb_ref[:, :shift]], axis=1)
        else:
            w = wa_ref[...]
        ws_ref[...] = w.astype(bf16)
    o_ref[...] = jnp.dot(a_ref[...], ws_ref[...], preferred_element_type=f32)


def proj_columns(a, w, z, src_col, n_cols, dst_col):
    m, k = a.shape
    tn = TN_DENSE
    j_src, shift = divmod(src_col, tn)
    j_dst = dst_col // tn
    in_specs = [pl.BlockSpec((TM_DENSE, k), lambda j, i: (i, 0)),
                pl.BlockSpec((k, tn), lambda j, i: (0, j_src + j)),
                pl.BlockSpec((k, tn), lambda j, i: (0, j_src + j + (1 if shift else 0)))]
    args = [a, w, w]
    aliases = {}
    if z is not None:
        in_specs.append(pl.BlockSpec(memory_space=pl.ANY))
        args.append(z)
        aliases = {3: 0}
    return pl.pallas_call(
        functools.partial(_proj_body, shift=shift),
        grid=(n_cols // tn, m // TM_DENSE),
        in_specs=in_specs,
        out_specs=pl.BlockSpec((TM_DENSE, tn), lambda j, i: (i, j_dst + j)),
        out_shape=jax.ShapeDtypeStruct((m, Z_COLS), f32),
        scratch_shapes=[pltpu.VMEM((k, tn), bf16)],
        input_output_aliases=aliases,
        compiler_params=_cparams(("arbitrary", "arbitrary")),
        name="proj_columns",
    )(*args)


def _proj_small_body(a_ref, wa_ref, wb_ref, o_ref, ws_ref, *, lo_a, n_a, lo_b, n_b):
    @pl.when(pl.program_id(0) == 0)
    def _():
        pad = jnp.zeros((wa_ref.shape[0], LANES - n_a - n_b), f32)
        w = jnp.concatenate([wa_ref[:, lo_a:lo_a + n_a], wb_ref[:, lo_b:lo_b + n_b], pad], axis=1)
        ws_ref[...] = w.astype(bf16)
    o_ref[...] = jnp.dot(a_ref[...], ws_ref[...], preferred_element_type=f32)


def proj_small(a, w, col_a, n_a, col_b, n_b):
    m, k = a.shape
    tn = TN_DENSE
    ja, lo_a = divmod(col_a, tn)
    jb, lo_b = divmod(col_b, tn)
    assert lo_a + n_a <= tn and lo_b + n_b <= tn and n_a + n_b <= LANES
    return pl.pallas_call(
        functools.partial(_proj_small_body, lo_a=lo_a, n_a=n_a, lo_b=lo_b, n_b=n_b),
        grid=(m // TM_DENSE,),
        in_specs=[pl.BlockSpec((TM_DENSE, k), lambda i: (i, 0)),
                  pl.BlockSpec((k, tn), lambda i: (0, ja)),
                  pl.BlockSpec((k, tn), lambda i: (0, jb))],
        out_specs=pl.BlockSpec((TM_DENSE, LANES), lambda i: (i, 0)),
        out_shape=jax.ShapeDtypeStruct((m, LANES), f32),
        scratch_shapes=[pltpu.VMEM((k, LANES), bf16)],
        compiler_params=_cparams(("arbitrary",)),
        name="proj_small",
    )(a, w, w)


def _merge_body(oa_ref, ob_ref, ga_ref, gb_ref, wa_ref, wb_ref, o_ref, wa_s, wb_s):
    @pl.when(pl.program_id(1) == 0)
    def _():
        wa_s[...] = wa_ref[...].astype(bf16)
        wb_s[...] = wb_ref[...].astype(bf16)
    pa = jnp.dot(oa_ref[...], wa_s[...], preferred_element_type=f32)
    pb = jnp.dot(ob_ref[...], wb_s[...], preferred_element_type=f32)
    o_ref[...] = (jax.nn.sigmoid(ga_ref[...]) * pa + jax.nn.sigmoid(gb_ref[...]) * pb).astype(o_ref.dtype)


def merge_matmul(o_a, o_b, z, w_a, w_b):
    m, ka = o_a.shape
    kb = o_b.shape[1]
    n = w_a.shape[1]
    tn = TN_DENSE
    ja, jb = Z_GA // tn, Z_GB // tn
    return pl.pallas_call(
        _merge_body,
        grid=(n // tn, m // TM_DENSE),
        in_specs=[pl.BlockSpec((TM_DENSE, ka), lambda j, i: (i, 0)),
                  pl.BlockSpec((TM_DENSE, kb), lambda j, i: (i, 0)),
                  pl.BlockSpec((TM_DENSE, tn), lambda j, i: (i, ja + j)),
                  pl.BlockSpec((TM_DENSE, tn), lambda j, i: (i, jb + j)),
                  pl.BlockSpec((ka, tn), lambda j, i: (0, j)),
                  pl.BlockSpec((kb, tn), lambda j, i: (0, j))],
        out_specs=pl.BlockSpec((TM_DENSE, tn), lambda j, i: (i, j)),
        out_shape=jax.ShapeDtypeStruct((m, n), bf16),
        scratch_shapes=[pltpu.VMEM((ka, tn), bf16), pltpu.VMEM((kb, tn), bf16)],
        compiler_params=_cparams(("arbitrary", "arbitrary")),
        name="merge_matmul",
    )(o_a, o_b, z, z, w_a, w_b)


def _resid_body(a_ref, w_ref, h_ref, o_ref, wb_ref):
    @pl.when(pl.program_id(1) == 0)
    def _():
        wb_ref[...] = w_ref[...].astype(bf16)
    o_ref[...] = h_ref[...] + jnp.dot(a_ref[...], wb_ref[...], preferred_element_type=f32)


def resid_matmul(a, w, h):
    m, k = a.shape
    n = w.shape[1]
    tn = TN_DENSE
    return pl.pallas_call(
        _resid_body,
        grid=(n // tn, m // TM_DENSE),
        in_specs=[pl.BlockSpec((TM_DENSE, k), lambda j, i: (i, 0)),
                  pl.BlockSpec((k, tn), lambda j, i: (0, j)),
                  pl.BlockSpec((TM_DENSE, tn), lambda j, i: (i, j))],
        out_specs=pl.BlockSpec((TM_DENSE, tn), lambda j, i: (i, j)),
        out_shape=jax.ShapeDtypeStruct((m, n), f32),
        scratch_shapes=[pltpu.VMEM((k, tn), bf16)],
        compiler_params=_cparams(("arbitrary", "arbitrary")),
        name="resid_matmul",
    )(a, w, h)


def _ple_body(a_ref, w_ref, p_ref, wp_ref, h_ref, o_ref, wb_ref, wpb_ref):
    @pl.when(pl.program_id(1) == 0)
    def _():
        wb_ref[...] = w_ref[...].astype(bf16)
        wpb_ref[...] = wp_ref[...].astype(bf16)
    gate = jax.nn.sigmoid(jnp.dot(a_ref[...], wb_ref[...], preferred_element_type=f32))
    emb = jnp.dot(p_ref[...], wpb_ref[...], preferred_element_type=f32)
    o_ref[...] = h_ref[...] + gate * emb


def ple_matmul(a, w_gate, p, w_proj, h):
    m, k = a.shape
    kp = p.shape[1]
    n = w_gate.shape[1]
    tn = TN_DENSE
    return pl.pallas_call(
        _ple_body,
        grid=(n // tn, m // TM_DENSE),
        in_specs=[pl.BlockSpec((TM_DENSE, k), lambda j, i: (i, 0)),
                  pl.BlockSpec((k, tn), lambda j, i: (0, j)),
                  pl.BlockSpec((TM_DENSE, kp), lambda j, i: (i, 0)),
                  pl.BlockSpec((kp, tn), lambda j, i: (0, j)),
                  pl.BlockSpec((TM_DENSE, tn), lambda j, i: (i, j))],
        out_specs=pl.BlockSpec((TM_DENSE, tn), lambda j, i: (i, j)),
        out_shape=jax.ShapeDtypeStruct((m, n), f32),
        scratch_shapes=[pltpu.VMEM((k, tn), bf16), pltpu.VMEM((kp, tn), bf16)],
        compiler_params=_cparams(("arbitrary", "arbitrary")),
        name="ple_matmul",
    )(a, w_gate, p, w_proj, h)


def _router_body(h_ref, g_ref, wr_ref, br_ref, m_ref, r_ref):
    x = h_ref[...]
    y = x * lax.rsqrt(jnp.mean(x * x, -1, keepdims=True) + EPS) * g_ref[...]
    m_ref[...] = y.astype(bf16)
    logits = jnp.dot(y, wr_ref[...], preferred_element_type=f32,
                     precision=lax.Precision.HIGHEST) + br_ref[...]
    lane = lax.broadcasted_iota(jnp.int32, logits.shape, 1)
    neg = -jnp.inf
    lg = jnp.where(lane < MOE_GROUPS, logits, neg)
    eg = jnp.exp(lg - jnp.max(lg, -1, keepdims=True))
    pg = eg / jnp.sum(eg, -1, keepdims=True)
    pg_top = jnp.max(pg, -1, keepdims=True)
    g_idx = jnp.min(jnp.where(pg == pg_top, lane, LANES), -1, keepdims=True)
    lo = MOE_GROUPS + MOE_PER_GROUP * g_idx
    emask = (lane >= lo) & (lane < lo + MOE_PER_GROUP)
    le = jnp.where(emask, logits, neg)
    ee = jnp.exp(le - jnp.max(le, -1, keepdims=True))
    pe = jnp.where(emask, ee / jnp.sum(ee, -1, keepdims=True), -1.0)
    v1 = jnp.max(pe, -1, keepdims=True)
    i1 = jnp.min(jnp.where(pe == v1, lane, LANES), -1, keepdims=True)
    pe2 = jnp.where(lane == i1, -1.0, pe)
    v2 = jnp.max(pe2, -1, keepdims=True)
    i2 = jnp.min(jnp.where(pe2 == v2, lane, LANES), -1, keepdims=True)
    den = v1 + v2
    w1 = pg_top * v1 / den
    w2 = pg_top * v2 / den
    e1 = (i1 - MOE_GROUPS).astype(f32)
    e2 = (i2 - MOE_GROUPS).astype(f32)
    r_ref[...] = jnp.where(lane == 0, e1, jnp.where(lane == 1, e2,
                           jnp.where(lane == 2, w1, jnp.where(lane == 3, w2, 0.0))))


def moe_router(h, g_ffn, w_router, b_router):
    m, d = h.shape
    return pl.pallas_call(
        _router_body,
        grid=(m // TM_ROWS,),
        in_specs=[pl.BlockSpec((TM_ROWS, d), lambda i: (i, 0)),
                  pl.BlockSpec((1, d), lambda i: (0, 0)),
                  pl.BlockSpec((d, LANES), lambda i: (0, 0)),
                  pl.BlockSpec((1, LANES), lambda i: (0, 0))],
        out_specs=[pl.BlockSpec((TM_ROWS, d), lambda i: (i, 0)),
                   pl.BlockSpec((TM_ROWS, LANES), lambda i: (i, 0))],
        out_shape=[jax.ShapeDtypeStruct((m, d), bf16),
                   jax.ShapeDtypeStruct((m, LANES), f32)],
        compiler_params=_cparams(("parallel",)),
        name="moe_router",
    )(h, g_ffn.reshape(1, d), w_router, b_router)


def _expert_body(te_ref, tv_ref, x_ref, rw_ref, wg_ref, wu_ref, wd_ref, o_ref, wg_s, wu_s, wd_s):
    t = pl.program_id(0)
    new_expert = (t == 0) | (te_ref[t] != te_ref[jnp.maximum(t - 1, 0)])

    @pl.when(new_expert)
    def _():
        wg_s[...] = wg_ref[...].astype(bf16)
        wu_s[...] = wu_ref[...].astype(bf16)
        wd_s[...] = wd_ref[...].astype(bf16)

    @pl.when(tv_ref[t] > 0)
    def _():
        x = x_ref[...]
        gate = jnp.dot(x, wg_s[...], preferred_element_type=f32)
        up = jnp.dot(x, wu_s[...], preferred_element_type=f32)
        hid = (jax.nn.silu(gate) * up * rw_ref[...]).astype(bf16)
        o_ref[...] = jnp.dot(hid, wd_s[...], preferred_element_type=f32)

    @pl.when(tv_ref[t] == 0)
    def _():
        o_ref[...] = jnp.zeros_like(o_ref)


def expert_matmul(tile_expert, tile_valid, xs, row_w, wg, wu, wd):
    r, d = xs.shape
    f = wg.shape[2]
    n_tiles = r // TM_ROWS
    grid_spec = pltpu.PrefetchScalarGridSpec(
        num_scalar_prefetch=2,
        grid=(n_tiles,),
        in_specs=[pl.BlockSpec((TM_ROWS, d), lambda t, te, tv: (t, 0)),
                  pl.BlockSpec((TM_ROWS, 1), lambda t, te, tv: (t, 0)),
                  pl.BlockSpec((None, d, f), lambda t, te, tv: (te[t], 0, 0), pipeline_mode=pl.Buffered(1)),
                  pl.BlockSpec((None, d, f), lambda t, te, tv: (te[t], 0, 0), pipeline_mode=pl.Buffered(1)),
                  pl.BlockSpec((None, f, d), lambda t, te, tv: (te[t], 0, 0), pipeline_mode=pl.Buffered(1))],
        out_specs=pl.BlockSpec((TM_ROWS, d), lambda t, te, tv: (t, 0)),
        scratch_shapes=[pltpu.VMEM((d, f), bf16), pltpu.VMEM((d, f), bf16), pltpu.VMEM((f, d), bf16)],
    )
    return pl.pallas_call(
        _expert_body,
        grid_spec=grid_spec,
        out_shape=jax.ShapeDtypeStruct((r, d), f32),
        compiler_params=_cparams(("arbitrary",)),
        name="expert_matmul",
    )(tile_expert, tile_valid, xs, row_w, wg, wu, wd)


def hier_moe(h, n_real, g_ffn, w_rg, b_rg, w_re, b_re, w_gate, w_up, w_down):
    mp, d = h.shape
    n_route = MOE_GROUPS + MOE_EXPERTS
    w_router = jnp.zeros((d, LANES), f32).at[:, :MOE_GROUPS].set(w_rg).at[:, MOE_GROUPS:n_route].set(w_re)
    b_router = jnp.zeros((1, LANES), f32).at[0, :MOE_GROUPS].set(b_rg).at[0, MOE_GROUPS:n_route].set(b_re)
    m_bf, slab = moe_router(h, g_ffn, w_router, b_router)
    ids = slab[:n_real, 0:2].astype(jnp.int32)
    wts = slab[:n_real, 2:4]

    tm = TM_ROWS
    n_assign = n_real * MOE_TOPK
    n_slots = -(-(n_assign + MOE_EXPERTS * (tm - 1)) // tm) * tm
    e_flat = ids.reshape(-1)
    order = jnp.argsort(e_flat, stable=True)
    e_sorted = e_flat[order]
    counts = jnp.sum(jax.nn.one_hot(e_flat, MOE_EXPERTS, dtype=jnp.int32), axis=0)
    padded = -(-counts // tm) * tm
    start_p = jnp.cumsum(padded) - padded
    start = jnp.cumsum(counts) - counts
    slot_sorted = start_p[e_sorted] + (jnp.arange(n_assign, dtype=jnp.int32) - start[e_sorted])
    slot_of = jnp.zeros((n_assign,), jnp.int32).at[order].set(slot_sorted.astype(jnp.int32))
    src_tok = jnp.zeros((n_slots,), jnp.int32).at[slot_of].set(jnp.arange(n_assign, dtype=jnp.int32) // MOE_TOPK)
    row_w = jnp.zeros((n_slots,), f32).at[slot_of].set(wts.reshape(-1))
    tile_start = jnp.arange(n_slots // tm, dtype=jnp.int32) * tm
    ends = jnp.cumsum(padded)
    tile_expert = jnp.minimum(jnp.searchsorted(ends, tile_start, side="right"), MOE_EXPERTS - 1).astype(jnp.int32)
    tile_valid = (tile_start < ends[-1]).astype(jnp.int32)

    xs = jnp.take(m_bf, src_tok, axis=0)
    ys = expert_matmul(tile_expert, tile_valid, xs, row_w.reshape(n_slots, 1), w_gate, w_up, w_down)
    slot2 = slot_of.reshape(n_real, MOE_TOPK)
    y = jnp.take(ys, slot2[:, 0], axis=0) + jnp.take(ys, slot2[:, 1], axis=0)
    return h.at[:n_real].add(y)


NEG_BIG = -1e30
TILE_FAR = REL_MAX_DIST // Q_BLOCK + 1
TILE_WIN_EDGE = TILE_FAR + 1
TILE_NONE = TILE_FAR + 2
N_BIAS_TILES = TILE_FAR + 3
NSA_KT = 4
_NT = (((1,), (1,)), ((), ()))


def _compress_body(x_ref, w_ref, pe_ref, o_ref):
    ns = o_ref.shape[0]
    acc0 = jnp.zeros((ns, NSA_DH), f32)
    acc1 = jnp.zeros((ns, NSA_DH), f32)
    for s in range(CMP_STRIDE):
        xs = x_ref[pl.ds(s, ns, stride=CMP_STRIDE), :]
        a0 = (xs + pe_ref[s:s + 1, :]).astype(bf16)
        a1 = (xs + pe_ref[CMP_STRIDE + s:CMP_STRIDE + s + 1, :]).astype(bf16)
        acc0 = acc0 + jnp.dot(a0, w_ref[s].astype(bf16), preferred_element_type=f32)
        acc1 = acc1 + jnp.dot(a1, w_ref[CMP_STRIDE + s].astype(bf16), preferred_element_type=f32)
    o_ref[...] = (acc0 + pltpu.roll(acc1, ns - 1, axis=0)).astype(o_ref.dtype)


def compress_prompt(z, cmp_w, cmp_pe, bn, t):
    ns = t // CMP_STRIDE
    col0 = Z_KV // NSA_DH
    return pl.pallas_call(
        _compress_body,
        grid=(bn, 2, NSA_KV_GROUPS),
        in_specs=[pl.BlockSpec((t, NSA_DH), lambda b, x, g: (b, col0 + x * NSA_KV_GROUPS + g)),
                  pl.BlockSpec((None, CMP_BLOCK, NSA_DH, NSA_DH), lambda b, x, g: (x, 0, 0, 0)),
                  pl.BlockSpec((None, CMP_BLOCK, NSA_DH), lambda b, x, g: (x, 0, 0))],
        out_specs=pl.BlockSpec((None, None, None, ns, NSA_DH), lambda b, x, g: (b, x, g, 0, 0)),
        out_shape=jax.ShapeDtypeStruct((bn, 2, NSA_KV_GROUPS, ns, NSA_DH), bf16),
        compiler_params=_cparams(("parallel", "parallel", "parallel")),
        name="compress_prompt",
    )(z, cmp_w, cmp_pe)


def _nsa_prompt_body(q_ref, kc_ref, vc_ref, ks_ref, vs_ref, kw_ref, vw_ref, gate_ref, bc_ref, bt_ref,
                     cov_ref, exp_ref, o_ref, mb_ref, s_ref, mx_ref, m_ref, l_ref, acc_ref):
    g = pl.program_id(1)
    i = pl.program_id(2)
    qb = Q_BLOCK
    rows = NSA_HPG * qb
    qf = q_ref[...]
    q = jnp.concatenate([qf[:, h * NSA_DH:(h + 1) * NSA_DH] for h in range(NSA_HPG)], axis=0).astype(bf16)

    ncp = kc_ref.shape[0]
    n_done = (qb // CMP_STRIDE) * (i + 1)
    bias_c = pltpu.roll(bc_ref[...], n_done % ncp, axis=1)
    lane_c = lax.broadcasted_iota(jnp.int32, bias_c.shape, 1)
    bias_c = jnp.where(lane_c < n_done, bias_c, NEG_BIG)
    s = lax.dot_general(q, kc_ref[...], _NT, preferred_element_type=f32) * (NSA_DH ** -0.5) + bias_c
    m = jnp.max(s, -1, keepdims=True)
    m = jnp.where(m < 0.5 * NEG_BIG, 0.0, m)
    e = jnp.exp(s - m)
    p = e / jnp.maximum(jnp.sum(e, -1, keepdims=True), 1e-30)
    o_c = jnp.dot(p.astype(bf16), vc_ref[...], preferred_element_type=f32)

    psum = p[0:qb]
    for h in range(1, NSA_HPG):
        psum = psum + p[h * qb:(h + 1) * qb]
    imp_t = lax.dot_general(cov_ref[...], psum, _NT, preferred_element_type=f32,
                            precision=lax.Precision.HIGHEST)
    nsb = imp_t.shape[0]
    blk = lax.broadcasted_iota(jnp.int32, (nsb, qb), 0)
    qpos = i * qb + lax.broadcasted_iota(jnp.int32, (nsb, qb), 1)
    cur = qpos // SEL_BLOCK
    forced = (blk == 0) | (blk == cur) | (blk == cur - 1)
    valid = blk * SEL_BLOCK <= qpos
    score = jnp.where(valid, jnp.where(forced, FORCE_SCORE, imp_t), -1.0)
    sel_t = jnp.zeros((nsb, qb), f32)
    for _ in range(min(SEL_TOPK, nsb)):
        mx = jnp.max(score, axis=0, keepdims=True)
        first = jnp.min(jnp.where(score == mx, blk, nsb), axis=0, keepdims=True)
        pick = blk == first
        sel_t = jnp.where(pick, 1.0, sel_t)
        score = jnp.where(pick, -2.0, score)
    unsel = ((sel_t - 1.0) * (-NEG_BIG)).T.astype(bf16)
    mb_ref[...] = jnp.dot(unsel, exp_ref[...], preferred_element_type=f32)

    scale = NSA_DH ** -0.5

    mx_ref[...] = jnp.full(mx_ref.shape, NEG_BIG, f32)

    kw_ = NSA_KT * qb
    n_steps = (i + NSA_KT) // NSA_KT

    def score_step(jj, carry):
        r0 = pl.multiple_of(jj * kw_, kw_)
        k = ks_ref[pl.ds(r0, kw_), :].astype(bf16)
        biases = []
        for tt in range(NSA_KT):
            d = i - (jj * NSA_KT + tt)
            biases.append(bt_ref[jnp.where(d < 0, TILE_NONE, jnp.minimum(d, TILE_FAR))])
        mb = mb_ref[:, pl.ds(r0, kw_)]
        s = (lax.dot_general(q, k, _NT, preferred_element_type=f32) * scale
             + jnp.concatenate(biases, axis=1) + jnp.concatenate([mb] * NSA_HPG, axis=0))
        s_ref[:, pl.ds(r0, kw_)] = s
        mx = mx_ref[...]
        for tt in range(NSA_KT):
            mx = jnp.maximum(mx, s[:, tt * qb:(tt + 1) * qb])
        mx_ref[...] = mx
        return carry

    lax.fori_loop(0, n_steps, score_step, 0)
    m_ref[...] = jnp.broadcast_to(jnp.max(mx_ref[...], -1, keepdims=True), m_ref.shape)
    l_ref[...] = jnp.zeros(l_ref.shape, f32)
    acc_ref[...] = jnp.zeros(acc_ref.shape, f32)

    def pv_step(jj, carry):
        r0 = pl.multiple_of(jj * kw_, kw_)
        p = jnp.exp(s_ref[:, pl.ds(r0, kw_)] - jnp.concatenate([m_ref[...]] * NSA_KT, axis=1))
        v = vs_ref[pl.ds(r0, kw_), :].astype(bf16)
        lsum = l_ref[...]
        for tt in range(NSA_KT):
            lsum = lsum + p[:, tt * qb:(tt + 1) * qb]
        l_ref[...] = lsum
        acc_ref[...] = acc_ref[...] + jnp.dot(p.astype(bf16), v, preferred_element_type=f32)
        return carry

    lax.fori_loop(0, n_steps, pv_step, 0)
    o_s = acc_ref[...] / jnp.sum(l_ref[...], -1, keepdims=True)

    n_win = WINDOW // qb
    j0 = jnp.maximum(i - n_win, 0)
    s_w = []
    for tt in range(n_win + 1):
        d = i - (j0 + tt)
        r0 = pl.multiple_of((j0 + tt) * qb, qb)
        k = kw_ref[pl.ds(r0, qb), :].astype(bf16)
        tile = jnp.where(d < 0, TILE_NONE, jnp.where(d == n_win, TILE_WIN_EDGE, d))
        s_w.append(lax.dot_general(q, k, _NT, preferred_element_type=f32) * scale + bt_ref[tile])
    mw = s_w[0]
    for s in s_w[1:]:
        mw = jnp.maximum(mw, s)
    mw = jnp.max(mw, -1, keepdims=True)
    lw = jnp.zeros((rows, qb), f32)
    o_w = jnp.zeros((rows, NSA_DH), f32)
    for tt in range(n_win + 1):
        r0 = pl.multiple_of((j0 + tt) * qb, qb)
        p = jnp.exp(s_w[tt] - mw)
        lw = lw + p
        o_w = o_w + jnp.dot(p.astype(bf16), vw_ref[pl.ds(r0, qb), :].astype(bf16), preferred_element_type=f32)
    o_w = o_w / jnp.sum(lw, -1, keepdims=True)

    gt = jax.nn.sigmoid(gate_ref[...])
    outs = []
    for h in range(NSA_HPG):
        c = 3 * h
        sl = slice(h * qb, (h + 1) * qb)
        outs.append(gt[:, c:c + 1] * o_c[sl] + gt[:, c + 1:c + 2] * o_s[sl] + gt[:, c + 2:c + 3] * o_w[sl])
    o_ref[...] = jnp.concatenate(outs, axis=1).astype(o_ref.dtype)


def _nsa_bias_tables(rel_bias, t):
    qb = Q_BLOCK
    nq = t // qb
    tbl = rel_bias.astype(f32)
    r = jnp.arange(qb, dtype=jnp.int32)[:, None]
    c = jnp.arange(qb, dtype=jnp.int32)[None, :]
    tiles = []
    for delta in range(TILE_FAR + 1):
        dist = delta * qb + r - c
        tiles.append(jnp.where((dist >= 0)[..., None], _rel_bias_of(tbl, dist), NEG_BIG))
    dist = (WINDOW // qb) * qb + r - c
    tiles.append(jnp.where(((dist >= 0) & (dist < WINDOW))[..., None], _rel_bias_of(tbl, dist), NEG_BIG))
    tiles.append(jnp.full((qb, qb, NSA_HEADS), NEG_BIG, f32))
    bt = jnp.stack(tiles, 0)
    bt = bt.reshape(N_BIAS_TILES, qb, qb, NSA_KV_GROUPS, NSA_HPG)
    bt = jnp.transpose(bt, (3, 0, 4, 1, 2)).reshape(NSA_KV_GROUPS, N_BIAS_TILES, NSA_HPG * qb, qb)
    ncp = t // CMP_STRIDE
    m_back = (ncp - 1) - jnp.arange(ncp, dtype=jnp.int32)[None, :]
    dist_c = r + CMP_STRIDE * m_back + CMP_STRIDE - (qb + CMP_BLOCK - 1)
    bc = jnp.where((dist_c >= 0)[..., None], _rel_bias_of(tbl, dist_c), NEG_BIG)
    bc = bc.reshape(qb, ncp, NSA_KV_GROUPS, NSA_HPG)
    bc = jnp.transpose(bc, (2, 3, 0, 1)).reshape(NSA_KV_GROUPS, NSA_HPG * qb, ncp)
    return bt, bc


def nsa_prompt_attention(z, gates_g, kvc, rel_bias, bn, t):
    qb = Q_BLOCK
    nq = t // qb
    ncp = t // CMP_STRIDE
    nsb = t // SEL_BLOCK
    rows = NSA_HPG * qb
    bt, bc = _nsa_bias_tables(rel_bias, t)
    cover_t = jnp.concatenate([_cover_matrix(ncp - 1, nsb), jnp.zeros((1, nsb), f32)], 0).T
    expand = jnp.asarray(np.repeat(np.eye(nsb, dtype=np.float32), SEL_BLOCK, axis=1), bf16)
    kcol = Z_KV // NSA_DH
    kv_spec = lambda off: pl.BlockSpec((t, NSA_DH), lambda b, g, i, off=off: (b, kcol + off + g))
    return pl.pallas_call(
        _nsa_prompt_body,
        grid=(bn, NSA_KV_GROUPS, nq),
        in_specs=[pl.BlockSpec((qb, NSA_HPG * NSA_DH), lambda b, g, i: (b * nq + i, Z_NQ // (NSA_HPG * NSA_DH) + g)),
                  pl.BlockSpec((None, None, None, ncp, NSA_DH), lambda b, g, i: (b, 0, g, 0, 0)),
                  pl.BlockSpec((None, None, None, ncp, NSA_DH), lambda b, g, i: (b, 1, g, 0, 0)),
                  kv_spec(2 * NSA_KV_GROUPS), kv_spec(3 * NSA_KV_GROUPS),
                  kv_spec(4 * NSA_KV_GROUPS), kv_spec(5 * NSA_KV_GROUPS),
                  pl.BlockSpec((None, qb, 3 * NSA_HPG), lambda b, g, i: (g, b * nq + i, 0)),
                  pl.BlockSpec((None, rows, ncp), lambda b, g, i: (g, 0, 0)),
                  pl.BlockSpec((None, N_BIAS_TILES, rows, qb), lambda b, g, i: (g, 0, 0, 0)),
                  pl.BlockSpec((nsb, ncp), lambda b, g, i: (0, 0)),
                  pl.BlockSpec((nsb, t), lambda b, g, i: (0, 0))],
        out_specs=pl.BlockSpec((qb, NSA_HPG * NSA_DH), lambda b, g, i: (b * nq + i, g)),
        out_shape=jax.ShapeDtypeStruct((bn * t, NSA_Q_W), bf16),
        scratch_shapes=[pltpu.VMEM((qb, t), f32), pltpu.VMEM((rows, t), f32), pltpu.VMEM((rows, qb), f32),
                        pltpu.VMEM((rows, qb), f32), pltpu.VMEM((rows, qb), f32), pltpu.VMEM((rows, NSA_DH), f32)],
        compiler_params=_cparams(("parallel", "parallel", "arbitrary")),
        name="nsa_prompt_attention",
    )(z, kvc, kvc, z, z, z, z, gates_g, bc, bt, cover_t, expand)


GDN_RT = 512
GDN_GROUP = 4
_HI = lax.Precision.HIGHEST


def _mm1(a, b):
    return jnp.dot(a.astype(bf16), b.astype(bf16), preferred_element_type=f32)


def _mm3(a, b):
    ah, bh = a.astype(bf16), b.astype(bf16)
    al = (a - ah.astype(f32)).astype(bf16)
    bl = (b - bh.astype(f32)).astype(bf16)
    return (jnp.dot(ah, bh, preferred_element_type=f32) + jnp.dot(ah, bl, preferred_element_type=f32)
            + jnp.dot(al, bh, preferred_element_type=f32))


def _gdn_prep_body(q_ref, k_ref, v_ref, qp_ref, kp_ref, vp_ref, gb_ref, cwq_ref, cwk_ref, cwv_ref,
                   w_ref, u_ref, aqk_ref, qg_ref, kg_ref, gl_ref, qs_ref, ks_ref, vs_ref):
    rt = q_ref.shape[0]
    cs = GDN_CHUNK
    first = pl.program_id(2) == 0
    pad = SUBLANES

    def conv_silu(x_ref, prev_ref, scr_ref, cw_ref):
        scr_ref[0:pad, :] = jnp.where(first, 0.0, prev_ref[...])
        scr_ref[pad:pad + rt, :] = x_ref[...]
        y = scr_ref[pl.ds(pad - (GDN_CONV - 1), rt), :] * cw_ref[0:1, :]
        for j in range(1, GDN_CONV):
            y = y + scr_ref[pl.ds(pad - (GDN_CONV - 1) + j, rt), :] * cw_ref[j:j + 1, :]
        return y * jax.nn.sigmoid(y)

    q = conv_silu(q_ref, qp_ref, qs_ref, cwq_ref)
    k = conv_silu(k_ref, kp_ref, ks_ref, cwk_ref)
    v = conv_silu(v_ref, vp_ref, vs_ref, cwv_ref)
    q = q * lax.rsqrt(jnp.sum(q * q, -1, keepdims=True) + EPS) * (GDN_DK ** -0.5)
    k = k * lax.rsqrt(jnp.sum(k * k, -1, keepdims=True) + EPS)

    gb = gb_ref[...]
    x = gb[:, 1:2] + gb[:, 2:3]
    softplus = jnp.maximum(x, 0.0) + jnp.log1p(jnp.exp(-jnp.abs(x)))
    g = jnp.broadcast_to(-jnp.exp(gb[:, 3:4]) * softplus, (rt, GDN_DK))
    beta = jnp.broadcast_to(jax.nn.sigmoid(gb[:, 0:1]), (rt, GDN_DK))

    row_in_chunk = lax.broadcasted_iota(jnp.int32, (rt, GDN_DK), 0) % cs
    gcum_all = g
    step = 1
    while step < cs:
        gcum_all = gcum_all + jnp.where(row_in_chunk >= step, pltpu.roll(gcum_all, step, axis=0), 0.0)
        step *= 2

    gr = GDN_GROUP * cs
    ri = lax.broadcasted_iota(jnp.int32, (gr, gr), 0)
    ci = lax.broadcasted_iota(jnp.int32, (gr, gr), 1)
    same_chunk = (ri // cs) == (ci // cs)
    incl = same_chunk & (ri >= ci)
    strict = same_chunk & (ri > ci)
    eye = (ri == ci).astype(f32)
    gls = []
    for grp in range(rt // gr):
        sl = slice(grp * gr, (grp + 1) * gr)
        qc, kc, vc, bc_, gcum = q[sl], k[sl], v[sl], beta[sl], gcum_all[sl]
        g_col = jnp.concatenate([gcum] * (gr // GDN_DK), axis=1)
        g_row = jnp.broadcast_to(gcum.T[0:1, :], (gr, gr))
        dec = jnp.exp(jnp.where(incl, g_col - g_row, 0.0))
        eg = jnp.exp(gcum)
        lasts = [gcum[(c + 1) * cs - 1:(c + 1) * cs, :] for c in range(GDN_GROUP)]
        g_last = jnp.concatenate([jnp.broadcast_to(r, (cs, GDN_DK)) for r in lasts], axis=0)
        kb = kc * bc_
        kbf = kc.astype(bf16)
        lmat = lax.dot_general(kb.astype(bf16), kbf, _NT, preferred_element_type=f32) * jnp.where(strict, dec, 0.0)
        aqk = lax.dot_general(qc.astype(bf16), kbf, _NT, preferred_element_type=f32) * jnp.where(incl, dec, 0.0)
        inv = eye - lmat
        pw = lmat
        for _ in range(int(math.log2(cs)) - 1):
            pw = _mm1(pw, pw)
            inv = inv + _mm1(inv, pw)
        for _ in range(2):
            inv = inv + _mm1(inv, eye - inv - _mm3(lmat, inv))
        rhs = jnp.concatenate([kb * eg, vc * bc_], axis=1)
        wu = _mm3(inv, rhs)
        w_ref[sl, :] = wu[:, :GDN_DK].astype(w_ref.dtype)
        u_ref[sl, :] = wu[:, GDN_DK:]
        for c in range(GDN_GROUP):
            blk = slice(c * cs, (c + 1) * cs)
            aqk_ref[grp * gr + c * cs:grp * gr + (c + 1) * cs, :] = aqk[blk, blk].astype(aqk_ref.dtype)
        qg_ref[sl, :] = (qc * eg).astype(qg_ref.dtype)
        kg_ref[sl, :] = (kc * jnp.exp(g_last - gcum)).astype(kg_ref.dtype)
        gls.extend(jnp.exp(r) for r in lasts)
    gl_ref[...] = jnp.concatenate(gls, axis=0)


def _gdn_scan_body(w_ref, u_ref, aqk_ref, qg_ref, kg_ref, gl_ref, za_ref, gn_ref, o_ref, sfin_ref, s_ref):
    c = pl.program_id(1)
    nc = pl.num_programs(1)
    n_gl = gl_ref.shape[1]

    @pl.when(c == 0)
    def _():
        s_ref[...] = jnp.zeros(s_ref.shape, f32)

    for h in range(GDN_HEADS):
        s = s_ref[h]
        sb = s.astype(bf16)
        v_new = u_ref[h] - jnp.dot(w_ref[h], sb, preferred_element_type=f32)
        vb = v_new.astype(bf16)
        o = (jnp.dot(qg_ref[h], sb, preferred_element_type=f32)
             + jnp.dot(aqk_ref[h], vb, preferred_element_type=f32))
        gl = gl_ref[h, pl.ds(c % n_gl, 1), :]
        s_ref[h] = gl * s + lax.dot_general(kg_ref[h], vb, (((0,), (0,)), ((), ())), preferred_element_type=f32)
        on = o * lax.rsqrt(jnp.mean(o * o, -1, keepdims=True) + EPS) * gn_ref[...]
        za = za_ref[:, h * GDN_DV:(h + 1) * GDN_DV]
        o_ref[:, h * GDN_DV:(h + 1) * GDN_DV] = (on * (za * jax.nn.sigmoid(za))).astype(o_ref.dtype)

    @pl.when(c == nc - 1)
    def _():
        sfin_ref[...] = s_ref[...]


def gdn_prompt(z, gb, conv_w, gdn_norm, bn, t):
    rt, cs = GDN_RT, GDN_CHUNK
    n_rt = t // rt
    m = bn * t
    hcol = GDN_DK // LANES
    qkv = lambda part: pl.BlockSpec((rt, GDN_DK), lambda h, b, r, part=part: (b * n_rt + r, part * GDN_HEADS + h))
    prev = lambda part: pl.BlockSpec(
        (SUBLANES, GDN_DK),
        lambda h, b, r, part=part: (jnp.maximum((b * n_rt + r) * (rt // SUBLANES) - 1, 0), part * GDN_HEADS + h))
    cw = lambda part: pl.BlockSpec((GDN_CONV, GDN_DK), lambda h, b, r, part=part: (0, part * GDN_HEADS + h))
    per_row = lambda width: pl.BlockSpec((None, rt, width), lambda h, b, r: (h, b * n_rt + r, 0))
    w_c, u_c, aqk, qg, kg, gl = pl.pallas_call(
        _gdn_prep_body,
        grid=(GDN_HEADS, bn, n_rt),
        in_specs=[qkv(0), qkv(1), qkv(2), prev(0), prev(1), prev(2), per_row(4), cw(0), cw(1), cw(2)],
        out_specs=[per_row(GDN_DK), per_row(GDN_DV), per_row(cs), per_row(GDN_DK), per_row(GDN_DK),
                   pl.BlockSpec((None, rt // cs, GDN_DK), lambda h, b, r: (h, b * n_rt + r, 0))],
        out_shape=[jax.ShapeDtypeStruct((GDN_HEADS, m, GDN_DK), bf16),
                   jax.ShapeDtypeStruct((GDN_HEADS, m, GDN_DV), f32),
                   jax.ShapeDtypeStruct((GDN_HEADS, m, cs), bf16),
                   jax.ShapeDtypeStruct((GDN_HEADS, m, GDN_DK), bf16),
                   jax.ShapeDtypeStruct((GDN_HEADS, m, GDN_DK), bf16),
                   jax.ShapeDtypeStruct((GDN_HEADS, m // cs, GDN_DK), f32)],
        scratch_shapes=[pltpu.VMEM((rt + SUBLANES, GDN_DK), f32)] * 3,
        compiler_params=_cparams(("parallel", "parallel", "parallel")),
        name="gdn_prep",
    )(z, z, z, z, z, z, gb, conv_w, conv_w, conv_w)

    nc = t // cs
    n_gl = rt // cs
    heads = lambda width: pl.BlockSpec((GDN_HEADS, cs, width), lambda b, c: (0, b * nc + c, 0))
    o, s_fin = pl.pallas_call(
        _gdn_scan_body,
        grid=(bn, nc),
        in_specs=[heads(GDN_DK), heads(GDN_DV), heads(cs), heads(GDN_DK), heads(GDN_DK),
                  pl.BlockSpec((GDN_HEADS, n_gl, GDN_DK), lambda b, c: (0, (b * nc + c) // n_gl, 0)),
                  pl.BlockSpec((cs, GDN_V_W), lambda b, c: (b * nc + c, Z_ZG // GDN_V_W)),
                  pl.BlockSpec((1, GDN_DV), lambda b, c: (0, 0))],
        out_specs=[pl.BlockSpec((cs, GDN_V_W), lambda b, c: (b * nc + c, 0)),
                   pl.BlockSpec((None, GDN_HEADS, GDN_DK, GDN_DV), lambda b, c: (b, 0, 0, 0))],
        out_shape=[jax.ShapeDtypeStruct((m, GDN_V_W), bf16),
                   jax.ShapeDtypeStruct((bn, GDN_HEADS, GDN_DK, GDN_DV), f32)],
        scratch_shapes=[pltpu.VMEM((GDN_HEADS, GDN_DK, GDN_DV), f32)],
        compiler_params=_cparams(("parallel", "arbitrary")),
        name="gdn_scan",
    )(w_c, u_c, aqk, qg, kg, gl, z, gdn_norm.reshape(1, GDN_DV))
    return o, s_fin


GDN_SB = 8


def _gdn_sample_body(x_ref, za_ref, gb_ref, sc_ref, s0_ref, cw_ref, gn_ref, o_ref, s_ref):
    sb = GDN_SB
    x3 = sc_ref[...]
    y = x_ref[...] * cw_ref[GDN_CONV - 1:GDN_CONV, :]
    for j in range(GDN_CONV - 1):
        y = y + x3[:, j, :] * cw_ref[j:j + 1, :]
    xc = y * jax.nn.sigmoid(y)
    for h in range(GDN_HEADS):
        q = xc[:, h * GDN_DK:(h + 1) * GDN_DK]
        k = xc[:, GDN_QK_W + h * GDN_DK:GDN_QK_W + (h + 1) * GDN_DK]
        v = xc[:, 2 * GDN_QK_W + h * GDN_DV:2 * GDN_QK_W + (h + 1) * GDN_DV]
        q = q * lax.rsqrt(jnp.sum(q * q, -1, keepdims=True) + EPS) * (GDN_DK ** -0.5)
        k = k * lax.rsqrt(jnp.sum(k * k, -1, keepdims=True) + EPS)
        gb = gb_ref[h]
        x = gb[:, 1:2] + gb[:, 2:3]
        softplus = jnp.maximum(x, 0.0) + jnp.log1p(jnp.exp(-jnp.abs(x)))
        eg = jnp.broadcast_to(jnp.exp(-jnp.exp(gb[:, 3:4]) * softplus), (sb, GDN_DV))
        beta = jnp.broadcast_to(jax.nn.sigmoid(gb[:, 0:1]), (sb, GDN_DV))
        qk = jnp.broadcast_to(jnp.sum(q * k, -1, keepdims=True), (sb, GDN_DV))
        q_t = q.T
        k_t = k.T
        o_rows = []
        for i in range(sb):
            s = s0_ref[i, h]
            kcol = k_t[:, i:i + 1]
            k_s = jnp.sum(kcol * s, axis=0, keepdims=True)
            q_s = jnp.sum(q_t[:, i:i + 1] * s, axis=0, keepdims=True)
            eg_i = eg[i:i + 1]
            v_new = beta[i:i + 1] * (v[i:i + 1] - eg_i * k_s)
            o_rows.append(eg_i * q_s + qk[i:i + 1] * v_new)
            s_ref[i, h] = eg_i * s + kcol * v_new
        o = jnp.concatenate(o_rows, axis=0)
        on = o * lax.rsqrt(jnp.mean(o * o, -1, keepdims=True) + EPS) * gn_ref[...]
        za = za_ref[:, h * GDN_DV:(h + 1) * GDN_DV]
        o_ref[:, h * GDN_DV:(h + 1) * GDN_DV] = on * (za * jax.nn.sigmoid(za))


def gdn_sample(z, gb, state_conv, state_gdn, conv_w, gdn_norm, row0, bs):
    sb = GDN_SB
    r0 = row0 // sb
    return pl.pallas_call(
        _gdn_sample_body,
        grid=(bs // sb,),
        in_specs=[pl.BlockSpec((sb, CONV_CH), lambda i: (r0 + i, 0)),
                  pl.BlockSpec((sb, GDN_V_W), lambda i: (r0 + i, Z_ZG // GDN_V_W)),
                  pl.BlockSpec((GDN_HEADS, sb, 4), lambda i: (0, r0 + i, 0)),
                  pl.BlockSpec((sb, GDN_CONV - 1, CONV_CH), lambda i: (i, 0, 0)),
                  pl.BlockSpec((sb, GDN_HEADS, GDN_DK, GDN_DV), lambda i: (i, 0, 0, 0)),
                  pl.BlockSpec((GDN_CONV, CONV_CH), lambda i: (0, 0)),
                  pl.BlockSpec((1, GDN_DV), lambda i: (0, 0))],
        out_specs=[pl.BlockSpec((sb, GDN_V_W), lambda i: (i, 0)),
                   pl.BlockSpec((sb, GDN_HEADS, GDN_DK, GDN_DV), lambda i: (i, 0, 0, 0))],
        out_shape=[jax.ShapeDtypeStruct((bs, GDN_V_W), f32),
                   jax.ShapeDtypeStruct((bs, GDN_HEADS, GDN_DK, GDN_DV), f32)],
        compiler_params=_cparams(("parallel",)),
        name="gdn_sample",
    )(z, z, gb, state_conv, state_gdn, conv_w, gdn_norm.reshape(1, GDN_DV))


SMP_PAGES = 8
_TN = (((0,), (0,)), ((), ()))
_KV_ROW = 2 * NSA_KV_GROUPS * NSA_DH


def _page_specs(shape):
    return [pl.BlockSpec((None,) + shape, lambda b, p, pt, kk=kk: (pt[b, p * SMP_PAGES + kk], 0, 0))
            for kk in range(SMP_PAGES)]


def _compress_sample_body(pt_ref, *refs):
    pages = refs[:SMP_PAGES]
    w_ref, o_ref = refs[SMP_PAGES:]
    per = PAGE_SIZE // CMP_STRIDE
    slabs = 2 * NSA_KV_GROUPS
    for x in range(2):
        rows = []
        for kk in range(SMP_PAGES):
            for g in range(NSA_KV_GROUPS):
                slab = x * NSA_KV_GROUPS + g
                parts = [pages[kk][pl.ds(s * slabs + slab, per, stride=CMP_STRIDE * slabs), :]
                         for s in range(CMP_STRIDE)]
                rows.append(jnp.concatenate(parts, axis=1))
        lhs = jnp.concatenate(rows, axis=0).astype(bf16)
        prod = jnp.dot(lhs, w_ref[x], preferred_element_type=f32)
        for kk in range(SMP_PAGES):
            for g in range(NSA_KV_GROUPS):
                r = (kk * NSA_KV_GROUPS + g) * per
                o_ref[x, g, kk * per:(kk + 1) * per, :] = prod[r:r + per]


def compress_sample(page_table, cache, w2):
    bs, n_pages = page_table.shape
    per = PAGE_SIZE // CMP_STRIDE
    n_sub = n_pages * per
    grid_spec = pltpu.PrefetchScalarGridSpec(
        num_scalar_prefetch=1,
        grid=(bs, n_pages // SMP_PAGES),
        in_specs=(_page_specs((PAGE_SIZE * 2 * NSA_KV_GROUPS, NSA_DH))
                  + [pl.BlockSpec((2, CMP_STRIDE * NSA_DH, 2 * NSA_DH), lambda b, p, pt: (0, 0, 0))]),
        out_specs=pl.BlockSpec((None, 2, NSA_KV_GROUPS, SMP_PAGES * per, 2 * NSA_DH), lambda b, p, pt: (b, 0, 0, p, 0)),
    )
    return pl.pallas_call(
        _compress_sample_body,
        grid_spec=grid_spec,
        out_shape=jax.ShapeDtypeStruct((bs, 2, NSA_KV_GROUPS, n_sub, 2 * NSA_DH), f32),
        compiler_params=_cparams(("parallel", "arbitrary")),
        name="compress_sample",
    )(page_table, *([cache] * SMP_PAGES), w2)


def _nsa_sample_select_body(qt_ref, pm_ref, w_ref, pe_ref, bc_ref, cov_ref, win_ref, bw_ref, b0_ref, kn_ref, vn_ref,
                            oc_ref, ow_ref, mask_ref, sn_ref, *, past):
    scale = NSA_DH ** -0.5
    qt = qt_ref[...]
    n_sub = pm_ref.shape[2]
    lane1 = lax.broadcasted_iota(jnp.int32, (1, LANES), 1)
    lane_n = lax.broadcasted_iota(jnp.int32, (n_sub, LANES), 1)
    pe_term = []
    for x in range(2):
        pr = jnp.dot(pe_ref[x], w_ref[x], preferred_element_type=f32)
        pe_term.append(pr[0:1, :NSA_DH] + pr[1:2, NSA_DH:])
    oc_t = jnp.zeros((NSA_DH, LANES), f32)
    ow_t = jnp.zeros((NSA_DH, LANES), f32)
    psum_all = jnp.zeros((n_sub, LANES), f32)
    snew = jnp.zeros((SUBLANES, LANES), f32)
    for g in range(NSA_KV_GROUPS):
        in_g = (lane1 // NSA_HPG) == g
        kvc = []
        for x in range(2):
            pm = pm_ref[x, g]
            kvc.append((pm[:, :NSA_DH] + pltpu.roll(pm[:, NSA_DH:], n_sub - 1, axis=0) + pe_term[x]).astype(bf16))
        s = jnp.dot(kvc[0], qt, preferred_element_type=f32) * scale + bc_ref[...]
        e = jnp.exp(s - jnp.max(s, axis=0, keepdims=True))
        p = jnp.where(in_g, e / jnp.sum(e, axis=0, keepdims=True), 0.0)
        oc_t = oc_t + lax.dot_general(kvc[1], p.astype(bf16), _TN, preferred_element_type=f32)
        psum = jnp.sum(p, axis=1, keepdims=True)
        psum_all = jnp.where((lane_n // NSA_HPG) == g, psum, psum_all)
        slabs = 2 * NSA_KV_GROUPS
        w_buf = win_ref.shape[0] // slabs
        kw = win_ref[pl.ds(g, w_buf, stride=slabs), :].astype(bf16)
        vw = win_ref[pl.ds(NSA_KV_GROUPS + g, w_buf, stride=slabs), :].astype(bf16)
        sw = jnp.dot(kw, qt, preferred_element_type=f32) * scale + bw_ref[...]
        sn = jnp.dot(kn_ref[g], qt, preferred_element_type=f32) * scale + b0_ref[...]
        mw = jnp.maximum(jnp.max(sw, axis=0, keepdims=True), sn[1:2])
        ew = jnp.exp(sw - mw)
        en = jnp.exp(sn[1:2] - mw)
        lw = jnp.sum(ew, axis=0, keepdims=True) + en
        pw = jnp.where(in_g, ew / lw, 0.0)
        row = lax.broadcasted_iota(jnp.int32, (SUBLANES, LANES), 0)
        pn = jnp.where((row == 1) & in_g, en / lw, 0.0)
        ow_t = (ow_t + lax.dot_general(vw, pw.astype(bf16), _TN, preferred_element_type=f32)
                + lax.dot_general(vn_ref[g], pn.astype(bf16), _TN, preferred_element_type=f32))
        snew = jnp.where(in_g, sn, snew)
    oc_ref[...] = oc_t
    ow_ref[...] = ow_t
    sn_ref[...] = snew

    imp_t = jnp.dot(cov_ref[...], psum_all, preferred_element_type=f32, precision=_HI)
    nsb = imp_t.shape[0]
    n_blocks = past // SEL_BLOCK + 1
    blk = lax.broadcasted_iota(jnp.int32, (nsb, LANES), 0)
    cur = past // SEL_BLOCK
    forced = (blk == 0) | (blk == cur) | (blk == cur - 1)
    valid = (blk * SEL_BLOCK <= past) & (blk < n_blocks)
    score = jnp.where(valid, jnp.where(forced, FORCE_SCORE, imp_t), -1.0)
    sel_t = jnp.zeros((nsb, LANES), f32)
    for _ in range(min(SEL_TOPK, n_blocks)):
        mx = jnp.max(score, axis=0, keepdims=True)
        first = jnp.min(jnp.where(score == mx, blk, nsb), axis=0, keepdims=True)
        pick = blk == first
        sel_t = jnp.where(pick, 1.0, sel_t)
        score = jnp.where(pick, -2.0, score)
    mask_ref[...] = (sel_t - 1.0) * (-NEG_BIG)


def _nsa_sample_attend_body(pt_ref, *refs):
    pages = refs[:SMP_PAGES]
    (qt_ref, bias_ref, mask_ref, sn_ref, vn_ref, oc_ref, ow_ref, gate_ref, o_ref, m_ref, l_ref, acc_ref) = refs[SMP_PAGES:]
    p = pl.program_id(1)
    grp = lax.broadcasted_iota(jnp.int32, (1, LANES), 1) // NSA_HPG

    @pl.when(p == 0)
    def _():
        m_ref[...] = sn_ref[0:1, :]
        l_ref[...] = jnp.ones(l_ref.shape, f32)
        acc_ref[...] = vn_ref[...]

    qt = qt_ref[...]
    slabs = 2 * NSA_KV_GROUPS
    s = None
    for g in range(NSA_KV_GROUPS):
        kg = jnp.concatenate([pages[kk][pl.ds(g, PAGE_SIZE, stride=slabs), :] for kk in range(SMP_PAGES)], axis=0)
        sg = jnp.dot(kg.astype(bf16), qt, preferred_element_type=f32)
        s = sg if g == 0 else jnp.where(grp == g, sg, s)
    bpp = PAGE_SIZE // SEL_BLOCK
    mrows = []
    for kk in range(SMP_PAGES):
        for half in range(bpp):
            mrow = mask_ref[pl.ds((p * SMP_PAGES + kk) * bpp + half, 1), :]
            mrows.append(jnp.broadcast_to(mrow, (SEL_BLOCK, LANES)))
    s = s * (NSA_DH ** -0.5) + bias_ref[...] + jnp.concatenate(mrows, axis=0)
    m_old = m_ref[...]
    m_new = jnp.maximum(m_old, jnp.max(s, axis=0, keepdims=True))
    alpha = jnp.exp(m_old - m_new)
    pe = jnp.exp(s - m_new)
    l_ref[...] = alpha * l_ref[...] + jnp.sum(pe, axis=0, keepdims=True)
    pb = pe.astype(bf16)
    upd = jnp.zeros(acc_ref.shape, f32)
    for g in range(NSA_KV_GROUPS):
        vg = jnp.concatenate([pages[kk][pl.ds(NSA_KV_GROUPS + g, PAGE_SIZE, stride=slabs), :]
                              for kk in range(SMP_PAGES)], axis=0)
        upd = upd + lax.dot_general(vg.astype(bf16), jnp.where(grp == g, pb, jnp.zeros_like(pb)), _TN,
                                    preferred_element_type=f32)
    acc_ref[...] = alpha * acc_ref[...] + upd
    m_ref[...] = m_new

    @pl.when(p == pl.num_programs(1) - 1)
    def _():
        gt = jax.nn.sigmoid(gate_ref[...])
        o_s = acc_ref[...] / l_ref[...]
        o_ref[...] = gt[0:1] * oc_ref[...] + gt[1:2] * o_s + gt[2:3] * ow_ref[...]


def nsa_sample(z, zs, row0, cache_c, cache_s, page_table, win_buf, cmp_pe, cmp_w, rel_bias):
    bs, n_pages = page_table.shape
    past = n_pages * PAGE_SIZE
    n_sub = past // CMP_STRIDE
    n_blocks = past // SEL_BLOCK + 1
    nsb = -(-n_blocks // SUBLANES) * SUBLANES
    w_buf = win_buf.shape[1]
    tbl = rel_bias.astype(f32)
    lane_pad = lambda a: jnp.pad(a, [(0, 0)] * (a.ndim - 1) + [(0, LANES - a.shape[-1])])

    def bias_rows(dist, ok):
        return lane_pad(jnp.where(ok[:, None], _rel_bias_of(tbl, dist), NEG_BIG))

    dist_c = past - (jnp.arange(n_sub, dtype=jnp.int32) * CMP_STRIDE + (CMP_BLOCK - 1))
    bias_c = bias_rows(dist_c, dist_c >= 0)
    w_pos = past - w_buf + jnp.arange(w_buf, dtype=jnp.int32)
    dist_w = past - w_pos
    bias_w = bias_rows(dist_w, (dist_w < WINDOW) & (w_pos >= 0))
    bias_0 = jnp.broadcast_to(lane_pad(_rel_bias_of(tbl, jnp.zeros((1,), jnp.int32))), (SUBLANES, LANES))
    dist_s = past - jnp.arange(past, dtype=jnp.int32)
    bias_s = bias_rows(dist_s, dist_s >= 0)
    cover_t = jnp.pad(_cover_matrix(n_sub - 1, n_blocks), [(0, 1), (0, nsb - n_blocks)]).T

    w2 = cmp_w.reshape(2, CMP_BLOCK // CMP_STRIDE, CMP_STRIDE * NSA_DH, NSA_DH)
    w2 = jnp.transpose(w2, (0, 2, 1, 3)).reshape(2, CMP_STRIDE * NSA_DH, 2 * NSA_DH).astype(bf16)
    pe2 = jnp.pad(cmp_pe.reshape(2, CMP_BLOCK // CMP_STRIDE, CMP_STRIDE * NSA_DH), [(0, 0), (0, SUBLANES - 2), (0, 0)])
    pe2 = pe2.astype(bf16)

    zrow = z[row0:row0 + bs]
    q = zrow[:, Z_NQ:Z_KV].reshape(bs, NSA_HEADS, NSA_DH)
    qt = lane_pad(jnp.transpose(q, (0, 2, 1))).astype(bf16)
    kv = zrow[:, Z_KV:Z_GA].reshape(bs, 3, 2, NSA_KV_GROUPS, NSA_DH)
    zero = jnp.zeros((bs, NSA_KV_GROUPS, NSA_DH), f32)
    pad_rows = lambda r0, r1: jnp.pad(jnp.stack([r0, r1], 2), [(0, 0), (0, 0), (0, SUBLANES - 2), (0, 0)]).astype(bf16)
    k_new = pad_rows(kv[:, 1, 0], kv[:, 2, 0])
    v_new_w = pad_rows(zero, kv[:, 2, 1])
    v_new_s = lane_pad(jnp.transpose(jnp.repeat(kv[:, 1, 1], NSA_HPG, axis=1), (0, 2, 1)))
    gates = zs[row0:row0 + bs, 2 * GDN_HEADS:2 * GDN_HEADS + 3 * NSA_HEADS].reshape(bs, NSA_HEADS, 3)
    gates = jnp.pad(lane_pad(jnp.transpose(gates, (0, 2, 1))), [(0, 0), (0, SUBLANES - 3), (0, 0)])

    pm = compress_sample(page_table, cache_c.reshape(-1, PAGE_SIZE * 2 * NSA_KV_GROUPS, NSA_DH), w2)
    per_seq = lambda *shape: pl.BlockSpec((None,) + shape, lambda b: (b,) + (0,) * len(shape))
    const = lambda *shape: pl.BlockSpec(shape, lambda b: (0,) * len(shape))
    oc_t, ow_t, mask, s_new = pl.pallas_call(
        functools.partial(_nsa_sample_select_body, past=past),
        grid=(bs,),
        in_specs=[per_seq(NSA_DH, LANES), per_seq(2, NSA_KV_GROUPS, n_sub, 2 * NSA_DH),
                  const(2, CMP_STRIDE * NSA_DH, 2 * NSA_DH), const(2, SUBLANES, CMP_STRIDE * NSA_DH),
                  const(n_sub, LANES), const(nsb, n_sub), per_seq(w_buf * 2 * NSA_KV_GROUPS, NSA_DH),
                  const(w_buf, LANES),
                  const(SUBLANES, LANES), per_seq(NSA_KV_GROUPS, SUBLANES, NSA_DH),
                  per_seq(NSA_KV_GROUPS, SUBLANES, NSA_DH)],
        out_specs=[per_seq(NSA_DH, LANES), per_seq(NSA_DH, LANES), per_seq(nsb, LANES), per_seq(SUBLANES, LANES)],
        out_shape=[jax.ShapeDtypeStruct((bs, NSA_DH, LANES), f32), jax.ShapeDtypeStruct((bs, NSA_DH, LANES), f32),
                   jax.ShapeDtypeStruct((bs, nsb, LANES), f32), jax.ShapeDtypeStruct((bs, SUBLANES, LANES), f32)],
        compiler_params=_cparams(("parallel",)),
        name="nsa_sample_select",
    )(qt, pm, w2, pe2, bias_c, cover_t, win_buf.reshape(bs, w_buf * 2 * NSA_KV_GROUPS, NSA_DH), bias_w, bias_0,
      k_new, v_new_w)

    seq = lambda *shape: pl.BlockSpec((None,) + shape, lambda b, p, pt: (b,) + (0,) * len(shape))
    grid_spec = pltpu.PrefetchScalarGridSpec(
        num_scalar_prefetch=1,
        grid=(bs, n_pages // SMP_PAGES),
        in_specs=_page_specs((PAGE_SIZE * 2 * NSA_KV_GROUPS, NSA_DH)) + [seq(NSA_DH, LANES),
                                  pl.BlockSpec((SMP_PAGES * PAGE_SIZE, LANES), lambda b, p, pt: (p, 0)),
                                  seq(nsb, LANES), seq(SUBLANES, LANES), seq(NSA_DH, LANES), seq(NSA_DH, LANES),
                                  seq(NSA_DH, LANES), seq(SUBLANES, LANES)],
        out_specs=seq(NSA_DH, LANES),
        scratch_shapes=[pltpu.VMEM((1, LANES), f32), pltpu.VMEM((1, LANES), f32), pltpu.VMEM((NSA_DH, LANES), f32)],
    )
    o_t = pl.pallas_call(
        _nsa_sample_attend_body,
        grid_spec=grid_spec,
        out_shape=jax.ShapeDtypeStruct((bs, NSA_DH, LANES), f32),
        compiler_params=_cparams(("parallel", "arbitrary")),
        name="nsa_sample_attend",
    )(page_table, *([cache_s.reshape(-1, PAGE_SIZE * 2 * NSA_KV_GROUPS, NSA_DH)] * SMP_PAGES), qt, bias_s, mask, s_new, v_new_s,
      oc_t, ow_t, gates)
    return jnp.transpose(o_t[:, :, :NSA_HEADS], (0, 2, 1)).reshape(bs, NSA_Q_W)


def _rel_bucket(dist):
    n = jnp.maximum(dist, 0)
    max_exact = REL_BUCKETS // 2
    nf = jnp.maximum(n, 1).astype(f32)
    large = max_exact + (jnp.log(nf / max_exact) / math.log(REL_MAX_DIST / max_exact)
                         * (REL_BUCKETS - max_exact)).astype(jnp.int32)
    large = jnp.minimum(large, REL_BUCKETS - 1)
    return jnp.where(n < max_exact, n, large)


def _rel_bias_of(tbl, dist):
    one_hot = jax.nn.one_hot(_rel_bucket(dist), REL_BUCKETS, dtype=f32)
    return jnp.dot(one_hot, tbl, precision=lax.Precision.HIGHEST)


def _cover_matrix(nc, ns):
    cs = np.arange(nc) * CMP_STRIDE
    ss = np.arange(ns) * SEL_BLOCK
    inter = np.minimum(cs[:, None] + CMP_BLOCK, ss[None, :] + SEL_BLOCK) - np.maximum(cs[:, None], ss[None, :])
    return jnp.asarray(np.clip(inter, 0, None) / CMP_BLOCK, dtype=f32)


def kernel(x_prompt, x_sample, p_prompt, p_sample, cache_cmp_kv, cache_slc_kv, page_table, state_win_kv, state_gdn, state_conv, g_mix, w_in, gdn_conv_w, gdn_dt_bias, gdn_a_log, gdn_norm, cmp_pe, cmp_w, rel_bias, w_proj_a, w_proj_b, w_out, g_ffn, w_router_group, b_router_group, w_router_expert, b_router_expert, w_gate, w_up, w_down, g_ple, w_ple_gate, w_ple_proj, g_final):
    bp, tp, d = x_prompt.shape
    bs, ts, _ = x_sample.shape
    n_p, n_s = bp * tp, bs * ts
    n_real = n_p + n_s
    mp = -(-n_real // ROW_ALIGN) * ROW_ALIGN
    pad = mp - n_real

    h = jnp.concatenate([x_prompt.reshape(n_p, d), x_sample.reshape(n_s, d), jnp.zeros((pad, d), f32)], 0)
    ple = jnp.concatenate([p_prompt[0].reshape(n_p, -1), p_sample[0].reshape(n_s, -1),
                           jnp.zeros((pad, p_prompt.shape[-1]), f32)], 0).astype(bf16)

    w = w_in[0]
    o_beta = 4 * GDN_QK_W
    o_nq = o_beta + 2 * GDN_HEADS
    o_gate = o_nq + NSA_Q_W + 6 * NSA_KV_W
    o_ga = o_gate + 3 * NSA_HEADS
    n_small = 2 * GDN_HEADS + 3 * NSA_HEADS

    a = rmsnorm_rows(h, g_mix[0], bf16)
    z = proj_columns(a, w, None, 0, o_beta, 0)
    z = proj_columns(a, w, z, o_nq, o_gate - o_nq, Z_NQ)
    z = proj_columns(a, w, z, o_ga, Z_COLS - Z_GA, Z_GA)
    zs = proj_small(a, w, o_beta, 2 * GDN_HEADS, o_gate, 3 * NSA_HEADS)

    def rows(x, lo, hi, which):
        if which == "p":
            return x[:n_p, lo:hi].reshape(bp, tp, hi - lo)
        return x[n_p:n_real, lo:hi].reshape(bs, ts, hi - lo)

    assert ts == 1 and tp % GDN_RT == 0 and n_p % GDN_SB == 0 and bs % GDN_SB == 0
    assert n_p % TM_ROWS == 0 and n_s <= TM_ROWS <= mp - n_p and tp % (NSA_KT * Q_BLOCK) == 0

    gb = jnp.stack([zs[:, 0:GDN_HEADS], zs[:, GDN_HEADS:2 * GDN_HEADS],
                    jnp.broadcast_to(gdn_dt_bias[0], (mp, GDN_HEADS)),
                    jnp.broadcast_to(gdn_a_log[0], (mp, GDN_HEADS))], -1)
    gb = jnp.transpose(gb, (1, 0, 2))
    gates_g = jnp.transpose(zs[:, 2 * GDN_HEADS:n_small].reshape(mp, NSA_KV_GROUPS, 3 * NSA_HPG), (1, 0, 2))

    outs = {}
    kv_shape = (2, NSA_KV_GROUPS, NSA_DH)
    kv_c_p = z[:n_p, Z_KV:Z_KV + _KV_ROW].reshape((bp, tp) + kv_shape)
    kv_s_p = z[:n_p, Z_KV + _KV_ROW:Z_KV + 2 * _KV_ROW].reshape((bp, tp) + kv_shape)
    w_keep = min(WINDOW, tp)
    win_p = jnp.stack([z[(b + 1) * tp - w_keep:(b + 1) * tp, Z_KV + 2 * _KV_ROW:Z_GA] for b in range(bp)], 0)
    win_p = win_p.reshape((bp, w_keep) + kv_shape)
    kvc = compress_prompt(z, cmp_w[0], cmp_pe[0], bp, tp)
    o_b_p = nsa_prompt_attention(z, gates_g, kvc, rel_bias, bp, tp)
    o_a_p, s_p = gdn_prompt(z, gb, gdn_conv_w[0], gdn_norm[0], bp, tp)
    conv_p = jnp.stack([z[(b + 1) * tp - (GDN_CONV - 1):(b + 1) * tp, :CONV_CH] for b in range(bp)], 0)
    outs["p"] = (kv_c_p, kv_s_p, win_p, s_p, conv_p)
    kv = rows(z, Z_KV, Z_GA, "s").reshape(bs, ts, 3, 2, NSA_KV_GROUPS, NSA_DH)
    o_b_s = nsa_sample(z, zs, n_p, cache_cmp_kv[0], cache_slc_kv[0], page_table, state_win_kv[0],
                       cmp_pe[0], cmp_w[0], rel_bias)
    win_s = jnp.concatenate([state_win_kv[0], kv[:, :, 2]], axis=1)[:, ts:]
    o_a_s, s_s = gdn_sample(z, gb, state_conv[0], state_gdn[0], gdn_conv_w[0], gdn_norm[0], n_p, bs)
    conv_s = jnp.concatenate([state_conv[0][:, 1:], rows(z, 0, CONV_CH, "s")], axis=1)
    outs["s"] = (kv[:, :, 0], kv[:, :, 1], win_s, s_s, conv_s)

    o_a = jnp.concatenate([o_a_p, o_a_s.astype(bf16), jnp.zeros((pad, GDN_V_W), bf16)], 0)
    o_b = jnp.concatenate([o_b_p, o_b_s.reshape(n_s, NSA_Q_W).astype(bf16), jnp.zeros((pad, NSA_Q_W), bf16)], 0)
    merged = merge_matmul(o_a, o_b, z, w_proj_a[0], w_proj_b[0])
    h = resid_matmul(merged, w_out[0], h)
    h = hier_moe(h, n_real, g_ffn[0], w_router_group[0], b_router_group[0], w_router_expert[0],
                 b_router_expert[0], w_gate[0], w_up[0], w_down[0])
    n3 = rmsnorm_rows(h, g_ple[0], bf16)
    h = ple_matmul(n3, w_ple_gate[0], ple, w_ple_proj[0], h)
    y_prompt = rmsnorm_rows(h, g_final, f32, 0, n_p).reshape(bp, tp, d)
    y_sample = rmsnorm_rows(h, g_final, f32, n_p, TM_ROWS)[:n_s].reshape(bs, ts, d)
    st_p, st_s = outs["p"], outs["s"]
    return (y_prompt, y_sample) + tuple(t[None] for t in st_p) + tuple(t[None] for t in st_s)
```

```python
import functools
import math

import jax
import jax.numpy as jnp
import numpy as np
from jax import lax
from jax.experimental import pallas as pl
from jax.experimental.pallas import tpu as pltpu

D_MODEL = 4096
GDN_HEADS = 16
GDN_DK = 128
GDN_DV = 128
GDN_CONV = 4
GDN_CHUNK = 64
NSA_HEADS = 16
NSA_KV_GROUPS = 4
NSA_HPG = NSA_HEADS // NSA_KV_GROUPS
NSA_DH = 128
CMP_BLOCK = 32
CMP_STRIDE = 16
SEL_BLOCK = 64
SEL_TOPK = 16
WINDOW = 512
Q_BLOCK = 128
FORCE_SCORE = 1.0e4
REL_BUCKETS = 32
REL_MAX_DIST = 1024
PAGE_SIZE = 128
MOE_GROUPS = 4
MOE_PER_GROUP = 8
MOE_EXPERTS = MOE_GROUPS * MOE_PER_GROUP
MOE_TOPK = 2
EXPERT_HIDDEN = 512
EPS = 1e-6

GDN_QK_W = GDN_HEADS * GDN_DK
GDN_V_W = GDN_HEADS * GDN_DV
CONV_CH = 2 * GDN_QK_W + GDN_V_W
NSA_Q_W = NSA_HEADS * NSA_DH
NSA_KV_W = NSA_KV_GROUPS * NSA_DH

LANES = 128
SUBLANES = 8
VMEM_LIMIT = 56 * 1024 * 1024

ROW_ALIGN = 768
TM_DENSE = 768
TN_DENSE = 512
TM_ROWS = 256
MOE_CHUNKS = 3

Z_Q, Z_K, Z_V, Z_ZG = 0, 2048, 4096, 6144
Z_NQ = 8192
Z_KV = 10240
Z_GA = 13312
Z_GB = 17408
Z_COLS = 21504

bf16 = jnp.bfloat16
f32 = jnp.float32


def _cparams(sem):
    return pltpu.CompilerParams(dimension_semantics=sem, vmem_limit_bytes=VMEM_LIMIT)


def _rmsnorm_body(x_ref, g_ref, o_ref):
    x = x_ref[...]
    y = x * lax.rsqrt(jnp.mean(x * x, -1, keepdims=True) + EPS)
    o_ref[...] = (y * g_ref[...]).astype(o_ref.dtype)


def rmsnorm_rows(x, g, out_dtype, row0=0, n_rows=None):
    m, d = x.shape
    n_rows = m if n_rows is None else n_rows
    t0 = row0 // TM_ROWS
    return pl.pallas_call(
        _rmsnorm_body,
        grid=(n_rows // TM_ROWS,),
        in_specs=[pl.BlockSpec((TM_ROWS, d), lambda i: (t0 + i, 0)),
                  pl.BlockSpec((1, d), lambda i: (0, 0))],
        out_specs=pl.BlockSpec((TM_ROWS, d), lambda i: (i, 0)),
        out_shape=jax.ShapeDtypeStruct((n_rows, d), out_dtype),
        compiler_params=_cparams(("parallel",)),
        name="rmsnorm_rows",
    )(x, g.reshape(1, d))


def _proj_body(a_ref, wa_ref, wb_ref, *rest, shift):
    o_ref, ws_ref = rest[-2:]

    @pl.when(pl.program_id(1) == 0)
    def _():
        if shift:
            w = jnp.concatenate([wa_ref[:, shift:], wb_ref[:, :shift]], axis=1)
        else:
            w = wa_ref[...]
        ws_ref[...] = w.astype(bf16)
    o_ref[...] = jnp.dot(a_ref[...], ws_ref[...], preferred_element_type=f32)


def proj_columns(a, w, z, src_col, n_cols, dst_col):
    m, k = a.shape
    tn = TN_DENSE
    j_src, shift = divmod(src_col, tn)
    j_dst = dst_col // tn
    in_specs = [pl.BlockSpec((TM_DENSE, k), lambda j, i: (i, 0)),
                pl.BlockSpec((k, tn), lambda j, i: (0, j_src + j)),
                pl.BlockSpec((k, tn), lambda j, i: (0, j_src + j + (1 if shift else 0)))]
    args = [a, w, w]
    aliases = {}
    if z is not None:
        in_specs.append(pl.BlockSpec(memory_space=pl.ANY))
        args.append(z)
        aliases = {3: 0}
    return pl.pallas_call(
        functools.partial(_proj_body, shift=shift),
        grid=(n_cols // tn, m // TM_DENSE),
        in_specs=in_specs,
        out_specs=pl.BlockSpec((TM_DENSE, tn), lambda j, i: (i, j_dst + j)),
        out_shape=jax.ShapeDtypeStruct((m, Z_COLS), f32),
        scratch_shapes=[pltpu.VMEM((k, tn), bf16)],
        input_output_aliases=aliases,
        compiler_params=_cparams(("arbitrary", "arbitrary")),
        name="proj_columns",
    )(*args)


def _proj_small_body(a_ref, wa_ref, wb_ref, o_ref, ws_ref, *, lo_a, n_a, lo_b, n_b):
    @pl.when(pl.program_id(0) == 0)
    def _():
        pad = jnp.zeros((wa_ref.shape[0], LANES - n_a - n_b), f32)
        w = jnp.concatenate([wa_ref[:, lo_a:lo_a + n_a], wb_ref[:, lo_b:lo_b + n_b], pad], axis=1)
        ws_ref[...] = w.astype(bf16)
    o_ref[...] = jnp.dot(a_ref[...], ws_ref[...], preferred_element_type=f32)


def proj_small(a, w, col_a, n_a, col_b, n_b):
    m, k = a.shape
    tn = TN_DENSE
    ja, lo_a = divmod(col_a, tn)
    jb, lo_b = divmod(col_b, tn)
    assert lo_a + n_a <= tn and lo_b + n_b <= tn and n_a + n_b <= LANES
    return pl.pallas_call(
        functools.partial(_proj_small_body, lo_a=lo_a, n_a=n_a, lo_b=lo_b, n_b=n_b),
        grid=(m // TM_DENSE,),
        in_specs=[pl.BlockSpec((TM_DENSE, k), lambda i: (i, 0)),
                  pl.BlockSpec((k, tn), lambda i: (0, ja)),
                  pl.BlockSpec((k, tn), lambda i: (0, jb))],
        out_specs=pl.BlockSpec((TM_DENSE, LANES), lambda i: (i, 0)),
        out_shape=jax.ShapeDtypeStruct((m, LANES), f32),
        scratch_shapes=[pltpu.VMEM((k, LANES), bf16)],
        compiler_params=_cparams(("arbitrary",)),
        name="proj_small",
    )(a, w, w)


def _merge_body(oa_ref, ob_ref, ga_ref, gb_ref, wa_ref, wb_ref, o_ref, wa_s, wb_s):
    @pl.when(pl.program_id(1) == 0)
    def _():
        wa_s[...] = wa_ref[...].astype(bf16)
        wb_s[...] = wb_ref[...].astype(bf16)
    pa = jnp.dot(oa_ref[...], wa_s[...], preferred_element_type=f32)
    pb = jnp.dot(ob_ref[...], wb_s[...], preferred_element_type=f32)
    o_ref[...] = (jax.nn.sigmoid(ga_ref[...]) * pa + jax.nn.sigmoid(gb_ref[...]) * pb).astype(o_ref.dtype)


def merge_matmul(o_a, o_b, z, w_a, w_b):
    m, ka = o_a.shape
    kb = o_b.shape[1]
    n = w_a.shape[1]
    tn = TN_DENSE
    ja, jb = Z_GA // tn, Z_GB // tn
    return pl.pallas_call(
        _merge_body,
        grid=(n // tn, m // TM_DENSE),
        in_specs=[pl.BlockSpec((TM_DENSE, ka), lambda j, i: (i, 0)),
                  pl.BlockSpec((TM_DENSE, kb), lambda j, i: (i, 0)),
                  pl.BlockSpec((TM_DENSE, tn), lambda j, i: (i, ja + j)),
                  pl.BlockSpec((TM_DENSE, tn), lambda j, i: (i, jb + j)),
                  pl.BlockSpec((ka, tn), lambda j, i: (0, j)),
                  pl.BlockSpec((kb, tn), lambda j, i: (0, j))],
        out_specs=pl.BlockSpec((TM_DENSE, tn), lambda j, i: (i, j)),
        out_shape=jax.ShapeDtypeStruct((m, n), bf16),
        scratch_shapes=[pltpu.VMEM((ka, tn), bf16), pltpu.VMEM((kb, tn), bf16)],
        compiler_params=_cparams(("arbitrary", "arbitrary")),
        name="merge_matmul",
    )(o_a, o_b, z, z, w_a, w_b)


def _resid_body(a_ref, w_ref, h_ref, o_ref, wb_ref):
    @pl.when(pl.program_id(1) == 0)
    def _():
        wb_ref[...] = w_ref[...].astype(bf16)
    o_ref[...] = h_ref[...] + jnp.dot(a_ref[...], wb_ref[...], preferred_element_type=f32)


def resid_matmul(a, w, h):
    m, k = a.shape
    n = w.shape[1]
    tn = TN_DENSE
    return pl.pallas_call(
        _resid_body,
        grid=(n // tn, m // TM_DENSE),
        in_specs=[pl.BlockSpec((TM_DENSE, k), lambda j, i: (i, 0)),
                  pl.BlockSpec((k, tn), lambda j, i: (0, j)),
                  pl.BlockSpec((TM_DENSE, tn), lambda j, i: (i, j))],
        out_specs=pl.BlockSpec((TM_DENSE, tn), lambda j, i: (i, j)),
        out_shape=jax.ShapeDtypeStruct((m, n), f32),
        scratch_shapes=[pltpu.VMEM((k, tn), bf16)],
        compiler_params=_cparams(("arbitrary", "arbitrary")),
        name="resid_matmul",
    )(a, w, h)


def _ple_body(a_ref, w_ref, p_ref, wp_ref, h_ref, o_ref, wb_ref, wpb_ref):
    @pl.when(pl.program_id(1) == 0)
    def _():
        wb_ref[...] = w_ref[...].astype(bf16)
        wpb_ref[...] = wp_ref[...].astype(bf16)
    gate = jax.nn.sigmoid(jnp.dot(a_ref[...], wb_ref[...], preferred_element_type=f32))
    emb = jnp.dot(p_ref[...], wpb_ref[...], preferred_element_type=f32)
    o_ref[...] = h_ref[...] + gate * emb


def ple_matmul(a, w_gate, p, w_proj, h):
    m, k = a.shape
    kp = p.shape[1]
    n = w_gate.shape[1]
    tn = TN_DENSE
    return pl.pallas_call(
        _ple_body,
        grid=(n // tn, m // TM_DENSE),
        in_specs=[pl.BlockSpec((TM_DENSE, k), lambda j, i: (i, 0)),
                  pl.BlockSpec((k, tn), lambda j, i: (0, j)),
                  pl.BlockSpec((TM_DENSE, kp), lambda j, i: (i, 0)),
                  pl.BlockSpec((kp, tn), lambda j, i: (0, j)),
                  pl.BlockSpec((TM_DENSE, tn), lambda j, i: (i, j))],
        out_specs=pl.BlockSpec((TM_DENSE, tn), lambda j, i: (i, j)),
        out_shape=jax.ShapeDtypeStruct((m, n), f32),
        scratch_shapes=[pltpu.VMEM((k, tn), bf16), pltpu.VMEM((kp, tn), bf16)],
        compiler_params=_cparams(("arbitrary", "arbitrary")),
        name="ple_matmul",
    )(a, w_gate, p, w_proj, h)


def _router_body(h_ref, g_ref, wr_ref, br_ref, m_ref, r_ref):
    x = h_ref[...]
    y = x * lax.rsqrt(jnp.mean(x * x, -1, keepdims=True) + EPS) * g_ref[...]
    m_ref[...] = y.astype(bf16)
    logits = jnp.dot(y, wr_ref[...], preferred_element_type=f32,
                     precision=lax.Precision.HIGHEST) + br_ref[...]
    lane = lax.broadcasted_iota(jnp.int32, logits.shape, 1)
    neg = -jnp.inf
    lg = jnp.where(lane < MOE_GROUPS, logits, neg)
    eg = jnp.exp(lg - jnp.max(lg, -1, keepdims=True))
    pg = eg / jnp.sum(eg, -1, keepdims=True)
    pg_top = jnp.max(pg, -1, keepdims=True)
    g_idx = jnp.min(jnp.where(pg == pg_top, lane, LANES), -1, keepdims=True)
    lo = MOE_GROUPS + MOE_PER_GROUP * g_idx
    emask = (lane >= lo) & (lane < lo + MOE_PER_GROUP)
    le = jnp.where(emask, logits, neg)
    ee = jnp.exp(le - jnp.max(le, -1, keepdims=True))
    pe = jnp.where(emask, ee / jnp.sum(ee, -1, keepdims=True), -1.0)
    v1 = jnp.max(pe, -1, keepdims=True)
    i1 = jnp.min(jnp.where(pe == v1, lane, LANES), -1, keepdims=True)
    pe2 = jnp.where(lane == i1, -1.0, pe)
    v2 = jnp.max(pe2, -1, keepdims=True)
    i2 = jnp.min(jnp.where(pe2 == v2, lane, LANES), -1, keepdims=True)
    den = v1 + v2
    w1 = pg_top * v1 / den
    w2 = pg_top * v2 / den
    e1 = (i1 - MOE_GROUPS).astype(f32)
    e2 = (i2 - MOE_GROUPS).astype(f32)
    r_ref[...] = jnp.where(lane == 0, e1, jnp.where(lane == 1, e2,
                           jnp.where(lane == 2, w1, jnp.where(lane == 3, w2, 0.0))))


def moe_router(h, g_ffn, w_router, b_router):
    m, d = h.shape
    return pl.pallas_call(
        _router_body,
        grid=(m // TM_ROWS,),
        in_specs=[pl.BlockSpec((TM_ROWS, d), lambda i: (i, 0)),
                  pl.BlockSpec((1, d), lambda i: (0, 0)),
                  pl.BlockSpec((d, LANES), lambda i: (0, 0)),
                  pl.BlockSpec((1, LANES), lambda i: (0, 0))],
        out_specs=[pl.BlockSpec((TM_ROWS, d), lambda i: (i, 0)),
                   pl.BlockSpec((TM_ROWS, LANES), lambda i: (i, 0))],
        out_shape=[jax.ShapeDtypeStruct((m, d), bf16),
                   jax.ShapeDtypeStruct((m, LANES), f32)],
        compiler_params=_cparams(("parallel",)),
        name="moe_router",
    )(h, g_ffn.reshape(1, d), w_router, b_router)


def _expert_body(te_ref, tv_ref, x_ref, rw_ref, wg_ref, wu_ref, wd_ref, *rest, tile0):
    o_ref, wg_s, wu_s, wd_s = rest[-4:]
    first = pl.program_id(0) == 0
    t = tile0 + pl.program_id(0)
    new_expert = first | (te_ref[t] != te_ref[jnp.maximum(t - 1, 0)])

    @pl.when(new_expert)
    def _():
        wg_s[...] = wg_ref[...].astype(bf16)
        wu_s[...] = wu_ref[...].astype(bf16)
        wd_s[...] = wd_ref[...].astype(bf16)

    @pl.when(tv_ref[t] > 0)
    def _():
        x = x_ref[...]
        gate = jnp.dot(x, wg_s[...], preferred_element_type=f32)
        up = jnp.dot(x, wu_s[...], preferred_element_type=f32)
        hid = (jax.nn.silu(gate) * up * rw_ref[...]).astype(bf16)
        o_ref[...] = jnp.dot(hid, wd_s[...], preferred_element_type=f32)

    @pl.when(tv_ref[t] == 0)
    def _():
        o_ref[...] = jnp.zeros_like(o_ref)


def expert_matmul(tile_expert, tile_valid, xs, row_w, wg, wu, wd, ys, tile0, n_slots):
    r, d = xs.shape
    f = wg.shape[2]
    weights = lambda shape: pl.BlockSpec((None,) + shape, lambda t, te, tv: (te[tile0 + t], 0, 0),
                                         pipeline_mode=pl.Buffered(1))
    in_specs = [pl.BlockSpec((TM_ROWS, d), lambda t, te, tv: (t, 0)),
                pl.BlockSpec((TM_ROWS, 1), lambda t, te, tv: (t, 0)),
                weights((d, f)), weights((d, f)), weights((f, d))]
    args = [tile_expert, tile_valid, xs, row_w, wg, wu, wd]
    aliases = {}
    if ys is not None:
        in_specs.append(pl.BlockSpec(memory_space=pl.ANY))
        args.append(ys)
        aliases = {len(args) - 1: 0}
    grid_spec = pltpu.PrefetchScalarGridSpec(
        num_scalar_prefetch=2,
        grid=(r // TM_ROWS,),
        in_specs=in_specs,
        out_specs=pl.BlockSpec((TM_ROWS, d), lambda t, te, tv: (tile0 + t, 0)),
        scratch_shapes=[pltpu.VMEM((d, f), bf16), pltpu.VMEM((d, f), bf16), pltpu.VMEM((f, d), bf16)],
    )
    return pl.pallas_call(
        functools.partial(_expert_body, tile0=tile0),
        grid_spec=grid_spec,
        out_shape=jax.ShapeDtypeStruct((n_slots, d), f32),
        input_output_aliases=aliases,
        compiler_params=_cparams(("arbitrary",)),
        name="expert_matmul",
    )(*args)


def hier_moe(h, n_real, g_ffn, w_rg, b_rg, w_re, b_re, w_gate, w_up, w_down):
    mp, d = h.shape
    n_route = MOE_GROUPS + MOE_EXPERTS
    w_router = jnp.zeros((d, LANES), f32).at[:, :MOE_GROUPS].set(w_rg).at[:, MOE_GROUPS:n_route].set(w_re)
    b_router = jnp.zeros((1, LANES), f32).at[0, :MOE_GROUPS].set(b_rg).at[0, MOE_GROUPS:n_route].set(b_re)
    m_bf, slab = moe_router(h, g_ffn, w_router, b_router)
    ids = slab[:n_real, 0:2].astype(jnp.int32)
    wts = slab[:n_real, 2:4]

    tm = TM_ROWS
    n_assign = n_real * MOE_TOPK
    n_slots = -(-(n_assign + MOE_EXPERTS * (tm - 1)) // tm) * tm
    e_flat = ids.reshape(-1)
    order = jnp.argsort(e_flat, stable=True)
    e_sorted = e_flat[order]
    counts = jnp.sum(jax.nn.one_hot(e_flat, MOE_EXPERTS, dtype=jnp.int32), axis=0)
    padded = -(-counts // tm) * tm
    start_p = jnp.cumsum(padded) - padded
    start = jnp.cumsum(counts) - counts
    slot_sorted = (start_p[e_sorted] + (jnp.arange(n_assign, dtype=jnp.int32) - start[e_sorted])).astype(jnp.int32)
    slot_of = slot_sorted[jnp.argsort(order)]
    n_tiles = n_slots // tm
    tile_start = jnp.arange(n_tiles, dtype=jnp.int32) * tm
    ends = jnp.cumsum(padded)
    tile_expert = jnp.minimum(jnp.searchsorted(ends, tile_start, side="right"), MOE_EXPERTS - 1).astype(jnp.int32)
    tile_valid = (tile_start < ends[-1]).astype(jnp.int32)
    slot = jnp.arange(n_slots, dtype=jnp.int32)
    slot_e = jnp.repeat(tile_expert, tm)
    off = slot - start_p[slot_e]
    used = (off < counts[slot_e]) & (jnp.repeat(tile_valid, tm) > 0)
    assign = order[jnp.clip(start[slot_e] + off, 0, n_assign - 1)]
    src_tok = jnp.where(used, assign // MOE_TOPK, 0).astype(jnp.int32)
    row_w = jnp.where(used, wts.reshape(-1)[assign], 0.0)

    ys = None
    for c in range(MOE_CHUNKS):
        t0, t1 = (c * n_tiles) // MOE_CHUNKS, ((c + 1) * n_tiles) // MOE_CHUNKS
        xs = m_bf.at[src_tok[t0 * tm:t1 * tm]].get(mode="promise_in_bounds")
        ys = expert_matmul(tile_expert, tile_valid, xs, row_w[t0 * tm:t1 * tm].reshape(-1, 1),
                           w_gate, w_up, w_down, ys, t0, n_slots)
    slot2 = slot_of.reshape(n_real, MOE_TOPK)
    y = ys.at[slot2[:, 0]].get(mode="promise_in_bounds") + ys.at[slot2[:, 1]].get(mode="promise_in_bounds")
    return h.at[:n_real].add(y)


NEG_BIG = -1e30
TILE_FAR = REL_MAX_DIST // Q_BLOCK + 1
TILE_WIN_EDGE = TILE_FAR + 1
TILE_NONE = TILE_FAR + 2
N_BIAS_TILES = TILE_FAR + 3
NSA_KT = 4
_NT = (((1,), (1,)), ((), ()))


def _compress_body(x_ref, w_ref, pe_ref, o_ref):
    ns = o_ref.shape[0]
    acc0 = jnp.zeros((ns, NSA_DH), f32)
    acc1 = jnp.zeros((ns, NSA_DH), f32)
    for s in range(CMP_STRIDE):
        xs = x_ref[pl.ds(s, ns, stride=CMP_STRIDE), :]
        a0 = (xs + pe_ref[s:s + 1, :]).astype(bf16)
        a1 = (xs + pe_ref[CMP_STRIDE + s:CMP_STRIDE + s + 1, :]).astype(bf16)
        acc0 = acc0 + jnp.dot(a0, w_ref[s].astype(bf16), preferred_element_type=f32)
        acc1 = acc1 + jnp.dot(a1, w_ref[CMP_STRIDE + s].astype(bf16), preferred_element_type=f32)
    o_ref[...] = (acc0 + pltpu.roll(acc1, ns - 1, axis=0)).astype(o_ref.dtype)


def compress_prompt(z, cmp_w, cmp_pe, bn, t):
    ns = t // CMP_STRIDE
    col0 = Z_KV // NSA_DH
    return pl.pallas_call(
        _compress_body,
        grid=(bn, 2, NSA_KV_GROUPS),
        in_specs=[pl.BlockSpec((t, NSA_DH), lambda b, x, g: (b, col0 + x * NSA_KV_GROUPS + g)),
                  pl.BlockSpec((None, CMP_BLOCK, NSA_DH, NSA_DH), lambda b, x, g: (x, 0, 0, 0)),
                  pl.BlockSpec((None, CMP_BLOCK, NSA_DH), lambda b, x, g: (x, 0, 0))],
        out_specs=pl.BlockSpec((None, None, None, ns, NSA_DH), lambda b, x, g: (b, x, g, 0, 0)),
        out_shape=jax.ShapeDtypeStruct((bn, 2, NSA_KV_GROUPS, ns, NSA_DH), bf16),
        compiler_params=_cparams(("parallel", "parallel", "parallel")),
        name="compress_prompt",
    )(z, cmp_w, cmp_pe)


def _nsa_prompt_body(q_ref, kc_ref, vc_ref, ks_ref, vs_ref, kw_ref, vw_ref, gate_ref, bc_ref, bt_ref,
                     cov_ref, exp_ref, o_ref, mb_ref, s_ref, mx_ref, m_ref, l_ref, acc_ref):
    g = pl.program_id(1)
    i = pl.program_id(2)
    qb = Q_BLOCK
    rows = NSA_HPG * qb
    qf = q_ref[...]
    q = jnp.concatenate([qf[:, h * NSA_DH:(h + 1) * NSA_DH] for h in range(NSA_HPG)], axis=0).astype(bf16)

    ncp = kc_ref.shape[0]
    n_done = (qb // CMP_STRIDE) * (i + 1)
    bias_c = pltpu.roll(bc_ref[...], n_done % ncp, axis=1)
    lane_c = lax.broadcasted_iota(jnp.int32, bias_c.shape, 1)
    bias_c = jnp.where(lane_c < n_done, bias_c, NEG_BIG)
    s = lax.dot_general(q, kc_ref[...], _NT, preferred_element_type=f32) * (NSA_DH ** -0.5) + bias_c
    m = jnp.max(s, -1, keepdims=True)
    m = jnp.where(m < 0.5 * NEG_BIG, 0.0, m)
    e = jnp.exp(s - m)
    p = e / jnp.maximum(jnp.sum(e, -1, keepdims=True), 1e-30)
    o_c = jnp.dot(p.astype(bf16), vc_ref[...], preferred_element_type=f32)

    psum = p[0:qb]
    for h in range(1, NSA_HPG):
        psum = psum + p[h * qb:(h + 1) * qb]
    imp_t = lax.dot_general(cov_ref[...], psum, _NT, preferred_element_type=f32,
                            precision=lax.Precision.HIGHEST)
    nsb = imp_t.shape[0]
    blk = lax.broadcasted_iota(jnp.int32, (nsb, qb), 0)
    qpos = i * qb + lax.broadcasted_iota(jnp.int32, (nsb, qb), 1)
    cur = qpos // SEL_BLOCK
    forced = (blk == 0) | (blk == cur) | (blk == cur - 1)
    valid = blk * SEL_BLOCK <= qpos
    score = jnp.where(valid, jnp.where(forced, FORCE_SCORE, imp_t), -1.0)
    sel_t = jnp.zeros((nsb, qb), f32)
    for _ in range(min(SEL_TOPK, nsb)):
        mx = jnp.max(score, axis=0, keepdims=True)
        first = jnp.min(jnp.where(score == mx, blk, nsb), axis=0, keepdims=True)
        pick = blk == first
        sel_t = jnp.where(pick, 1.0, sel_t)
        score = jnp.where(pick, -2.0, score)
    unsel = ((sel_t - 1.0) * (-NEG_BIG)).T.astype(bf16)
    mb_ref[...] = jnp.dot(unsel, exp_ref[...], preferred_element_type=f32)

    scale = NSA_DH ** -0.5

    mx_ref[...] = jnp.full(mx_ref.shape, NEG_BIG, f32)

    kw_ = NSA_KT * qb
    n_steps = (i + NSA_KT) // NSA_KT

    def score_step(jj, carry):
        r0 = pl.multiple_of(jj * kw_, kw_)
        k = ks_ref[pl.ds(r0, kw_), :].astype(bf16)
        biases = []
        for tt in range(NSA_KT):
            d = i - (jj * NSA_KT + tt)
            biases.append(bt_ref[jnp.where(d < 0, TILE_NONE, jnp.minimum(d, TILE_FAR))])
        mb = mb_ref[:, pl.ds(r0, kw_)]
        s = (lax.dot_general(q, k, _NT, preferred_element_type=f32) * scale
             + jnp.concatenate(biases, axis=1) + jnp.concatenate([mb] * NSA_HPG, axis=0))
        s_ref[:, pl.ds(r0, kw_)] = s
        mx = mx_ref[...]
        for tt in range(NSA_KT):
            mx = jnp.maximum(mx, s[:, tt * qb:(tt + 1) * qb])
        mx_ref[...] = mx
        return carry

    lax.fori_loop(0, n_steps, score_step, 0)
    m_ref[...] = jnp.broadcast_to(jnp.max(mx_ref[...], -1, keepdims=True), m_ref.shape)
    l_ref[...] = jnp.zeros(l_ref.shape, f32)
    acc_ref[...] = jnp.zeros(acc_ref.shape, f32)

    def pv_step(jj, carry):
        r0 = pl.multiple_of(jj * kw_, kw_)
        p = jnp.exp(s_ref[:, pl.ds(r0, kw_)] - jnp.concatenate([m_ref[...]] * NSA_KT, axis=1))
        v = vs_ref[pl.ds(r0, kw_), :].astype(bf16)
        lsum = l_ref[...]
        for tt in range(NSA_KT):
            lsum = lsum + p[:, tt * qb:(tt + 1) * qb]
        l_ref[...] = lsum
        acc_ref[...] = acc_ref[...] + jnp.dot(p.astype(bf16), v, preferred_element_type=f32)
        return carry

    lax.fori_loop(0, n_steps, pv_step, 0)
    o_s = acc_ref[...] / jnp.sum(l_ref[...], -1, keepdims=True)

    n_win = WINDOW // qb
    j0 = jnp.maximum(i - n_win, 0)
    s_w = []
    for tt in range(n_win + 1):
        d = i - (j0 + tt)
        r0 = pl.multiple_of((j0 + tt) * qb, qb)
        k = kw_ref[pl.ds(r0, qb), :].astype(bf16)
        tile = jnp.where(d < 0, TILE_NONE, jnp.where(d == n_win, TILE_WIN_EDGE, d))
        s_w.append(lax.dot_general(q, k, _NT, preferred_element_type=f32) * scale + bt_ref[tile])
    mw = s_w[0]
    for s in s_w[1:]:
        mw = jnp.maximum(mw, s)
    mw = jnp.max(mw, -1, keepdims=True)
    lw = jnp.zeros((rows, qb), f32)
    o_w = jnp.zeros((rows, NSA_DH), f32)
    for tt in range(n_win + 1):
        r0 = pl.multiple_of((j0 + tt) * qb, qb)
        p = jnp.exp(s_w[tt] - mw)
        lw = lw + p
        o_w = o_w + jnp.dot(p.astype(bf16), vw_ref[pl.ds(r0, qb), :].astype(bf16), preferred_element_type=f32)
    o_w = o_w / jnp.sum(lw, -1, keepdims=True)

    gt = jax.nn.sigmoid(gate_ref[...])
    outs = []
    for h in range(NSA_HPG):
        c = 3 * h
        sl = slice(h * qb, (h + 1) * qb)
        outs.append(gt[:, c:c + 1] * o_c[sl] + gt[:, c + 1:c + 2] * o_s[sl] + gt[:, c + 2:c + 3] * o_w[sl])
    o_ref[...] = jnp.concatenate(outs, axis=1).astype(o_ref.dtype)


def _nsa_bias_tables(rel_bias, t):
    qb = Q_BLOCK
    nq = t // qb
    tbl = rel_bias.astype(f32)
    r = jnp.arange(qb, dtype=jnp.int32)[:, None]
    c = jnp.arange(qb, dtype=jnp.int32)[None, :]
    dists = [delta * qb + r - c for delta in range(TILE_FAR + 1)]
    oks = [d_ >= 0 for d_ in dists]
    d_win = (WINDOW // qb) * qb + r - c
    dists += [d_win, d_win]
    oks += [(d_win >= 0) & (d_win < WINDOW), jnp.zeros((qb, qb), bool)]
    bt = jnp.where(jnp.stack(oks, 0)[None], _rel_bias_heads(tbl, jnp.stack(dists, 0)), NEG_BIG)
    bt = bt.reshape(NSA_KV_GROUPS, NSA_HPG, N_BIAS_TILES, qb, qb)
    bt = jnp.transpose(bt, (0, 2, 1, 3, 4)).reshape(NSA_KV_GROUPS, N_BIAS_TILES, NSA_HPG * qb, qb)
    ncp = t // CMP_STRIDE
    m_back = (ncp - 1) - jnp.arange(ncp, dtype=jnp.int32)[None, :]
    dist_c = r + CMP_STRIDE * m_back + CMP_STRIDE - (qb + CMP_BLOCK - 1)
    bc = jnp.where((dist_c >= 0)[None], _rel_bias_heads(tbl, dist_c), NEG_BIG)
    return bt, bc.reshape(NSA_KV_GROUPS, NSA_HPG * qb, ncp)


def nsa_prompt_attention(z, gates_g, kvc, rel_bias, bn, t):
    qb = Q_BLOCK
    nq = t // qb
    ncp = t // CMP_STRIDE
    nsb = t // SEL_BLOCK
    rows = NSA_HPG * qb
    bt, bc = _nsa_bias_tables(rel_bias, t)
    cover_t = jnp.concatenate([_cover_matrix(ncp - 1, nsb), jnp.zeros((1, nsb), f32)], 0).T
    expand = jnp.asarray(np.repeat(np.eye(nsb, dtype=np.float32), SEL_BLOCK, axis=1), bf16)
    kcol = Z_KV // NSA_DH
    kv_spec = lambda off: pl.BlockSpec((t, NSA_DH), lambda b, g, i, off=off: (b, kcol + off + g))
    return pl.pallas_call(
        _nsa_prompt_body,
        grid=(bn, NSA_KV_GROUPS, nq),
        in_specs=[pl.BlockSpec((qb, NSA_HPG * NSA_DH), lambda b, g, i: (b * nq + i, Z_NQ // (NSA_HPG * NSA_DH) + g)),
                  pl.BlockSpec((None, None, None, ncp, NSA_DH), lambda b, g, i: (b, 0, g, 0, 0)),
                  pl.BlockSpec((None, None, None, ncp, NSA_DH), lambda b, g, i: (b, 1, g, 0, 0)),
                  kv_spec(2 * NSA_KV_GROUPS), kv_spec(3 * NSA_KV_GROUPS),
                  kv_spec(4 * NSA_KV_GROUPS), kv_spec(5 * NSA_KV_GROUPS),
                  pl.BlockSpec((None, qb, 3 * NSA_HPG), lambda b, g, i: (g, b * nq + i, 0)),
                  pl.BlockSpec((None, rows, ncp), lambda b, g, i: (g, 0, 0)),
                  pl.BlockSpec((None, N_BIAS_TILES, rows, qb), lambda b, g, i: (g, 0, 0, 0)),
                  pl.BlockSpec((nsb, ncp), lambda b, g, i: (0, 0)),
                  pl.BlockSpec((nsb, t), lambda b, g, i: (0, 0))],
        out_specs=pl.BlockSpec((qb, NSA_HPG * NSA_DH), lambda b, g, i: (b * nq + i, g)),
        out_shape=jax.ShapeDtypeStruct((bn * t, NSA_Q_W), bf16),
        scratch_shapes=[pltpu.VMEM((qb, t), f32), pltpu.VMEM((rows, t), f32), pltpu.VMEM((rows, qb), f32),
                        pltpu.VMEM((rows, qb), f32), pltpu.VMEM((rows, qb), f32), pltpu.VMEM((rows, NSA_DH), f32)],
        compiler_params=_cparams(("parallel", "parallel", "arbitrary")),
        name="nsa_prompt_attention",
    )(z, kvc, kvc, z, z, z, z, gates_g, bc, bt, cover_t, expand)


GDN_RT = 1024
GDN_GROUP = 4
_HI = lax.Precision.HIGHEST


def _mm1(a, b):
    return jnp.dot(a.astype(bf16), b.astype(bf16), preferred_element_type=f32)


def _mm3(a, b):
    ah, bh = a.astype(bf16), b.astype(bf16)
    al = (a - ah.astype(f32)).astype(bf16)
    bl = (b - bh.astype(f32)).astype(bf16)
    return (jnp.dot(ah, bh, preferred_element_type=f32) + jnp.dot(ah, bl, preferred_element_type=f32)
            + jnp.dot(al, bh, preferred_element_type=f32))


def _gdn_prep_body(q_ref, k_ref, v_ref, qp_ref, kp_ref, vp_ref, gb_ref, cwq_ref, cwk_ref, cwv_ref,
                   w_ref, u_ref, aqk_ref, qg_ref, kg_ref, gl_ref, qs_ref, ks_ref, vs_ref):
    rt = q_ref.shape[0]
    cs = GDN_CHUNK
    first = pl.program_id(2) == 0
    pad = SUBLANES

    def conv_silu(x_ref, prev_ref, scr_ref, cw_ref):
        scr_ref[0:pad, :] = jnp.where(first, 0.0, prev_ref[...])
        scr_ref[pad:pad + rt, :] = x_ref[...]
        y = scr_ref[pl.ds(pad - (GDN_CONV - 1), rt), :] * cw_ref[0:1, :]
        for j in range(1, GDN_CONV):
            y = y + scr_ref[pl.ds(pad - (GDN_CONV - 1) + j, rt), :] * cw_ref[j:j + 1, :]
        return y * jax.nn.sigmoid(y)

    q = conv_silu(q_ref, qp_ref, qs_ref, cwq_ref)
    k = conv_silu(k_ref, kp_ref, ks_ref, cwk_ref)
    v = conv_silu(v_ref, vp_ref, vs_ref, cwv_ref)
    q = q * lax.rsqrt(jnp.sum(q * q, -1, keepdims=True) + EPS) * (GDN_DK ** -0.5)
    k = k * lax.rsqrt(jnp.sum(k * k, -1, keepdims=True) + EPS)

    gb = gb_ref[...]
    x = gb[:, 1:2] + gb[:, 2:3]
    softplus = jnp.maximum(x, 0.0) + jnp.log1p(jnp.exp(-jnp.abs(x)))
    g = jnp.broadcast_to(-jnp.exp(gb[:, 3:4]) * softplus, (rt, GDN_DK))
    beta = jnp.broadcast_to(jax.nn.sigmoid(gb[:, 0:1]), (rt, GDN_DK))

    row_in_chunk = lax.broadcasted_iota(jnp.int32, (rt, GDN_DK), 0) % cs
    gcum_all = g
    step = 1
    while step < cs:
        gcum_all = gcum_all + jnp.where(row_in_chunk >= step, pltpu.roll(gcum_all, step, axis=0), 0.0)
        step *= 2

    gr = GDN_GROUP * cs
    ri = lax.broadcasted_iota(jnp.int32, (gr, gr), 0)
    ci = lax.broadcasted_iota(jnp.int32, (gr, gr), 1)
    same_chunk = (ri // cs) == (ci // cs)
    incl = same_chunk & (ri >= ci)
    strict = same_chunk & (ri > ci)
    eye = (ri == ci).astype(f32)
    gls = []
    for grp in range(rt // gr):
        sl = slice(grp * gr, (grp + 1) * gr)
        qc, kc, vc, bc_, gcum = q[sl], k[sl], v[sl], beta[sl], gcum_all[sl]
        g_col = jnp.concatenate([gcum] * (gr // GDN_DK), axis=1)
        g_row = jnp.broadcast_to(gcum.T[0:1, :], (gr, gr))
        dec = jnp.exp(jnp.where(incl, g_col - g_row, 0.0))
        eg = jnp.exp(gcum)
        lasts = [gcum[(c + 1) * cs - 1:(c + 1) * cs, :] for c in range(GDN_GROUP)]
        g_last = jnp.concatenate([jnp.broadcast_to(r, (cs, GDN_DK)) for r in lasts], axis=0)
        kb = kc * bc_
        kbf = kc.astype(bf16)
        lmat = lax.dot_general(kb.astype(bf16), kbf, _NT, preferred_element_type=f32) * jnp.where(strict, dec, 0.0)
        aqk = lax.dot_general(qc.astype(bf16), kbf, _NT, preferred_element_type=f32) * jnp.where(incl, dec, 0.0)
        inv = eye - lmat
        pw = lmat
        for _ in range(int(math.log2(cs)) - 1):
            pw = _mm1(pw, pw)
            inv = inv + _mm1(inv, pw)
        for _ in range(2):
            inv = inv + _mm1(inv, eye - inv - _mm3(lmat, inv))
        rhs = jnp.concatenate([kb * eg, vc * bc_], axis=1)
        wu = _mm3(inv, rhs)
        w_ref[sl, :] = wu[:, :GDN_DK].astype(w_ref.dtype)
        u_ref[sl, :] = wu[:, GDN_DK:]
        for c in range(GDN_GROUP):
            blk = slice(c * cs, (c + 1) * cs)
            aqk_ref[grp * gr + c * cs:grp * gr + (c + 1) * cs, :] = aqk[blk, blk].astype(aqk_ref.dtype)
        qg_ref[sl, :] = (qc * eg).astype(qg_ref.dtype)
        kg_ref[sl, :] = (kc * jnp.exp(g_last - gcum)).astype(kg_ref.dtype)
        gls.extend(jnp.exp(r) for r in lasts)
    gl_ref[...] = jnp.concatenate(gls, axis=0)


def _gdn_scan_body(w_ref, u_ref, aqk_ref, qg_ref, kg_ref, gl_ref, za_ref, gn_ref, o_ref, sfin_ref, s_ref):
    c = pl.program_id(1)
    nc = pl.num_programs(1)
    n_gl = gl_ref.shape[1]

    @pl.when(c == 0)
    def _():
        s_ref[...] = jnp.zeros(s_ref.shape, f32)

    for h in range(GDN_HEADS):
        s = s_ref[h]
        sb = s.astype(bf16)
        v_new = u_ref[h] - jnp.dot(w_ref[h], sb, preferred_element_type=f32)
        vb = v_new.astype(bf16)
        o = (jnp.dot(qg_ref[h], sb, preferred_element_type=f32)
             + jnp.dot(aqk_ref[h], vb, preferred_element_type=f32))
        gl = gl_ref[h, pl.ds(c % n_gl, 1), :]
        s_ref[h] = gl * s + lax.dot_general(kg_ref[h], vb, (((0,), (0,)), ((), ())), preferred_element_type=f32)
        on = o * lax.rsqrt(jnp.mean(o * o, -1, keepdims=True) + EPS) * gn_ref[...]
        za = za_ref[:, h * GDN_DV:(h + 1) * GDN_DV]
        o_ref[:, h * GDN_DV:(h + 1) * GDN_DV] = (on * (za * jax.nn.sigmoid(za))).astype(o_ref.dtype)

    @pl.when(c == nc - 1)
    def _():
        sfin_ref[...] = s_ref[...]


def gdn_prompt(z, gb, conv_w, gdn_norm, bn, t):
    rt, cs = GDN_RT, GDN_CHUNK
    n_rt = t // rt
    m = bn * t
    hcol = GDN_DK // LANES
    qkv = lambda part: pl.BlockSpec((rt, GDN_DK), lambda h, b, r, part=part: (b * n_rt + r, part * GDN_HEADS + h))
    prev = lambda part: pl.BlockSpec(
        (SUBLANES, GDN_DK),
        lambda h, b, r, part=part: (jnp.maximum((b * n_rt + r) * (rt // SUBLANES) - 1, 0), part * GDN_HEADS + h))
    cw = lambda part: pl.BlockSpec((GDN_CONV, GDN_DK), lambda h, b, r, part=part: (0, part * GDN_HEADS + h))
    per_row = lambda width: pl.BlockSpec((None, rt, width), lambda h, b, r: (h, b * n_rt + r, 0))
    w_c, u_c, aqk, qg, kg, gl = pl.pallas_call(
        _gdn_prep_body,
        grid=(GDN_HEADS, bn, n_rt),
        in_specs=[qkv(0), qkv(1), qkv(2), prev(0), prev(1), prev(2), per_row(4), cw(0), cw(1), cw(2)],
        out_specs=[per_row(GDN_DK), per_row(GDN_DV), per_row(cs), per_row(GDN_DK), per_row(GDN_DK),
                   pl.BlockSpec((None, rt // cs, GDN_DK), lambda h, b, r: (h, b * n_rt + r, 0))],
        out_shape=[jax.ShapeDtypeStruct((GDN_HEADS, m, GDN_DK), bf16),
                   jax.ShapeDtypeStruct((GDN_HEADS, m, GDN_DV), f32),
                   jax.ShapeDtypeStruct((GDN_HEADS, m, cs), bf16),
                   jax.ShapeDtypeStruct((GDN_HEADS, m, GDN_DK), bf16),
                   jax.ShapeDtypeStruct((GDN_HEADS, m, GDN_DK), bf16),
                   jax.ShapeDtypeStruct((GDN_HEADS, m // cs, GDN_DK), f32)],
        scratch_shapes=[pltpu.VMEM((rt + SUBLANES, GDN_DK), f32)] * 3,
        compiler_params=_cparams(("parallel", "parallel", "parallel")),
        name="gdn_prep",
    )(z, z, z, z, z, z, gb, conv_w, conv_w, conv_w)

    nc = t // cs
    n_gl = rt // cs
    heads = lambda width: pl.BlockSpec((GDN_HEADS, cs, width), lambda b, c: (0, b * nc + c, 0))
    o, s_fin = pl.pallas_call(
        _gdn_scan_body,
        grid=(bn, nc),
        in_specs=[heads(GDN_DK), heads(GDN_DV), heads(cs), heads(GDN_DK), heads(GDN_DK),
                  pl.BlockSpec((GDN_HEADS, n_gl, GDN_DK), lambda b, c: (0, (b * nc + c) // n_gl, 0)),
                  pl.BlockSpec((cs, GDN_V_W), lambda b, c: (b * nc + c, Z_ZG // GDN_V_W)),
                  pl.BlockSpec((1, GDN_DV), lambda b, c: (0, 0))],
        out_specs=[pl.BlockSpec((cs, GDN_V_W), lambda b, c: (b * nc + c, 0)),
                   pl.BlockSpec((None, GDN_HEADS, GDN_DK, GDN_DV), lambda b, c: (b, 0, 0, 0))],
        out_shape=[jax.ShapeDtypeStruct((m, GDN_V_W), bf16),
                   jax.ShapeDtypeStruct((bn, GDN_HEADS, GDN_DK, GDN_DV), f32)],
        scratch_shapes=[pltpu.VMEM((GDN_HEADS, GDN_DK, GDN_DV), f32)],
        compiler_params=_cparams(("parallel", "arbitrary")),
        name="gdn_scan",
    )(w_c, u_c, aqk, qg, kg, gl, z, gdn_norm.reshape(1, GDN_DV))
    return o, s_fin


GDN_SB = 8


def _gdn_sample_body(x_ref, za_ref, gb_ref, sc_ref, s0_ref, cw_ref, gn_ref, o_ref, s_ref):
    sb = GDN_SB
    x3 = sc_ref[...]
    y = x_ref[...] * cw_ref[GDN_CONV - 1:GDN_CONV, :]
    for j in range(GDN_CONV - 1):
        y = y + x3[:, j, :] * cw_ref[j:j + 1, :]
    xc = y * jax.nn.sigmoid(y)
    for h in range(GDN_HEADS):
        q = xc[:, h * GDN_DK:(h + 1) * GDN_DK]
        k = xc[:, GDN_QK_W + h * GDN_DK:GDN_QK_W + (h + 1) * GDN_DK]
        v = xc[:, 2 * GDN_QK_W + h * GDN_DV:2 * GDN_QK_W + (h + 1) * GDN_DV]
        q = q * lax.rsqrt(jnp.sum(q * q, -1, keepdims=True) + EPS) * (GDN_DK ** -0.5)
        k = k * lax.rsqrt(jnp.sum(k * k, -1, keepdims=True) + EPS)
        gb = gb_ref[h]
        x = gb[:, 1:2] + gb[:, 2:3]
        softplus = jnp.maximum(x, 0.0) + jnp.log1p(jnp.exp(-jnp.abs(x)))
        eg = jnp.broadcast_to(jnp.exp(-jnp.exp(gb[:, 3:4]) * softplus), (sb, GDN_DV))
        beta = jnp.broadcast_to(jax.nn.sigmoid(gb[:, 0:1]), (sb, GDN_DV))
        qk = jnp.broadcast_to(jnp.sum(q * k, -1, keepdims=True), (sb, GDN_DV))
        q_t = q.T
        k_t = k.T
        o_rows = []
        for i in range(sb):
            s = s0_ref[i, h]
            kcol = k_t[:, i:i + 1]
            k_s = jnp.sum(kcol * s, axis=0, keepdims=True)
            q_s = jnp.sum(q_t[:, i:i + 1] * s, axis=0, keepdims=True)
            eg_i = eg[i:i + 1]
            v_new = beta[i:i + 1] * (v[i:i + 1] - eg_i * k_s)
            o_rows.append(eg_i * q_s + qk[i:i + 1] * v_new)
            s_ref[i, h] = eg_i * s + kcol * v_new
        o = jnp.concatenate(o_rows, axis=0)
        on = o * lax.rsqrt(jnp.mean(o * o, -1, keepdims=True) + EPS) * gn_ref[...]
        za = za_ref[:, h * GDN_DV:(h + 1) * GDN_DV]
        o_ref[:, h * GDN_DV:(h + 1) * GDN_DV] = on * (za * jax.nn.sigmoid(za))


def gdn_sample(z, gb, state_conv, state_gdn, conv_w, gdn_norm, row0, bs):
    sb = GDN_SB
    r0 = row0 // sb
    return pl.pallas_call(
        _gdn_sample_body,
        grid=(bs // sb,),
        in_specs=[pl.BlockSpec((sb, CONV_CH), lambda i: (r0 + i, 0)),
                  pl.BlockSpec((sb, GDN_V_W), lambda i: (r0 + i, Z_ZG // GDN_V_W)),
                  pl.BlockSpec((GDN_HEADS, sb, 4), lambda i: (0, r0 + i, 0)),
                  pl.BlockSpec((sb, GDN_CONV - 1, CONV_CH), lambda i: (i, 0, 0)),
                  pl.BlockSpec((sb, GDN_HEADS, GDN_DK, GDN_DV), lambda i: (i, 0, 0, 0)),
                  pl.BlockSpec((GDN_CONV, CONV_CH), lambda i: (0, 0)),
                  pl.BlockSpec((1, GDN_DV), lambda i: (0, 0))],
        out_specs=[pl.BlockSpec((sb, GDN_V_W), lambda i: (i, 0)),
                   pl.BlockSpec((sb, GDN_HEADS, GDN_DK, GDN_DV), lambda i: (i, 0, 0, 0))],
        out_shape=[jax.ShapeDtypeStruct((bs, GDN_V_W), f32),
                   jax.ShapeDtypeStruct((bs, GDN_HEADS, GDN_DK, GDN_DV), f32)],
        compiler_params=_cparams(("parallel",)),
        name="gdn_sample",
    )(z, z, gb, state_conv, state_gdn, conv_w, gdn_norm.reshape(1, GDN_DV))


SMP_PAGES = 8
_TN = (((0,), (0,)), ((), ()))
_KV_ROW = 2 * NSA_KV_GROUPS * NSA_DH


def _page_specs(shape):
    return [pl.BlockSpec((None,) + shape, lambda b, p, pt, kk=kk: (pt[b, p * SMP_PAGES + kk], 0, 0))
            for kk in range(SMP_PAGES)]


def _compress_sample_body(pt_ref, *refs):
    pages = refs[:SMP_PAGES]
    w_ref, o_ref = refs[SMP_PAGES:]
    per = PAGE_SIZE // CMP_STRIDE
    slabs = 2 * NSA_KV_GROUPS
    for x in range(2):
        rows = []
        for kk in range(SMP_PAGES):
            for g in range(NSA_KV_GROUPS):
                slab = x * NSA_KV_GROUPS + g
                parts = [pages[kk][pl.ds(s * slabs + slab, per, stride=CMP_STRIDE * slabs), :]
                         for s in range(CMP_STRIDE)]
                rows.append(jnp.concatenate(parts, axis=1))
        lhs = jnp.concatenate(rows, axis=0).astype(bf16)
        prod = jnp.dot(lhs, w_ref[x], preferred_element_type=f32)
        for kk in range(SMP_PAGES):
            for g in range(NSA_KV_GROUPS):
                r = (kk * NSA_KV_GROUPS + g) * per
                o_ref[x, g, kk * per:(kk + 1) * per, :] = prod[r:r + per]


def compress_sample(page_table, cache, w2):
    bs, n_pages = page_table.shape
    per = PAGE_SIZE // CMP_STRIDE
    n_sub = n_pages * per
    grid_spec = pltpu.PrefetchScalarGridSpec(
        num_scalar_prefetch=1,
        grid=(bs, n_pages // SMP_PAGES),
        in_specs=(_page_specs((PAGE_SIZE * 2 * NSA_KV_GROUPS, NSA_DH))
                  + [pl.BlockSpec((2, CMP_STRIDE * NSA_DH, 2 * NSA_DH), lambda b, p, pt: (0, 0, 0))]),
        out_specs=pl.BlockSpec((None, 2, NSA_KV_GROUPS, SMP_PAGES * per, 2 * NSA_DH), lambda b, p, pt: (b, 0, 0, p, 0)),
    )
    return pl.pallas_call(
        _compress_sample_body,
        grid_spec=grid_spec,
        out_shape=jax.ShapeDtypeStruct((bs, 2, NSA_KV_GROUPS, n_sub, 2 * NSA_DH), f32),
        compiler_params=_cparams(("parallel", "arbitrary")),
        name="compress_sample",
    )(page_table, *([cache] * SMP_PAGES), w2)


def _nsa_sample_select_body(qt_ref, pm_ref, w_ref, pe_ref, bc_ref, cov_ref, win_ref, bw_ref, b0_ref, kn_ref, vn_ref,
                            oc_ref, ow_ref, mask_ref, sn_ref, *, past):
    scale = NSA_DH ** -0.5
    qt = qt_ref[...]
    n_sub = pm_ref.shape[2]
    lane1 = lax.broadcasted_iota(jnp.int32, (1, LANES), 1)
    lane_n = lax.broadcasted_iota(jnp.int32, (n_sub, LANES), 1)
    pe_term = []
    for x in range(2):
        pr = jnp.dot(pe_ref[x], w_ref[x], preferred_element_type=f32)
        pe_term.append(pr[0:1, :NSA_DH] + pr[1:2, NSA_DH:])
    oc_t = jnp.zeros((NSA_DH, LANES), f32)
    ow_t = jnp.zeros((NSA_DH, LANES), f32)
    psum_all = jnp.zeros((n_sub, LANES), f32)
    snew = jnp.zeros((SUBLANES, LANES), f32)
    for g in range(NSA_KV_GROUPS):
        in_g = (lane1 // NSA_HPG) == g
        kvc = []
        for x in range(2):
            pm = pm_ref[x, g]
            kvc.append((pm[:, :NSA_DH] + pltpu.roll(pm[:, NSA_DH:], n_sub - 1, axis=0) + pe_term[x]).astype(bf16))
        s = jnp.dot(kvc[0], qt, preferred_element_type=f32) * scale + bc_ref[...]
        e = jnp.exp(s - jnp.max(s, axis=0, keepdims=True))
        p = jnp.where(in_g, e / jnp.sum(e, axis=0, keepdims=True), 0.0)
        oc_t = oc_t + lax.dot_general(kvc[1], p.astype(bf16), _TN, preferred_element_type=f32)
        psum = jnp.sum(p, axis=1, keepdims=True)
        psum_all = jnp.where((lane_n // NSA_HPG) == g, psum, psum_all)
        slabs = 2 * NSA_KV_GROUPS
        w_buf = win_ref.shape[0] // slabs
        kw = win_ref[pl.ds(g, w_buf, stride=slabs), :].astype(bf16)
        vw = win_ref[pl.ds(NSA_KV_GROUPS + g, w_buf, stride=slabs), :].astype(bf16)
        sw = jnp.dot(kw, qt, preferred_element_type=f32) * scale + bw_ref[...]
        sn = jnp.dot(kn_ref[g], qt, preferred_element_type=f32) * scale + b0_ref[...]
        mw = jnp.maximum(jnp.max(sw, axis=0, keepdims=True), sn[1:2])
        ew = jnp.exp(sw - mw)
        en = jnp.exp(sn[1:2] - mw)
        lw = jnp.sum(ew, axis=0, keepdims=True) + en
        pw = jnp.where(in_g, ew / lw, 0.0)
        row = lax.broadcasted_iota(jnp.int32, (SUBLANES, LANES), 0)
        pn = jnp.where((row == 1) & in_g, en / lw, 0.0)
        ow_t = (ow_t + lax.dot_general(vw, pw.astype(bf16), _TN, preferred_element_type=f32)
                + lax.dot_general(vn_ref[g], pn.astype(bf16), _TN, preferred_element_type=f32))
        snew = jnp.where(in_g, sn, snew)
    oc_ref[...] = oc_t
    ow_ref[...] = ow_t
    sn_ref[...] = snew

    imp_t = jnp.dot(cov_ref[...], psum_all, preferred_element_type=f32, precision=_HI)
    nsb = imp_t.shape[0]
    n_blocks = past // SEL_BLOCK + 1
    blk = lax.broadcasted_iota(jnp.int32, (nsb, LANES), 0)
    cur = past // SEL_BLOCK
    forced = (blk == 0) | (blk == cur) | (blk == cur - 1)
    valid = (blk * SEL_BLOCK <= past) & (blk < n_blocks)
    score = jnp.where(valid, jnp.where(forced, FORCE_SCORE, imp_t), -1.0)
    sel_t = jnp.zeros((nsb, LANES), f32)
    for _ in range(min(SEL_TOPK, n_blocks)):
        mx = jnp.max(score, axis=0, keepdims=True)
        first = jnp.min(jnp.where(score == mx, blk, nsb), axis=0, keepdims=True)
        pick = blk == first
        sel_t = jnp.where(pick, 1.0, sel_t)
        score = jnp.where(pick, -2.0, score)
    mask_ref[...] = (sel_t - 1.0) * (-NEG_BIG)


def _nsa_sample_attend_body(pt_ref, *refs):
    pages = refs[:SMP_PAGES]
    (qt_ref, bias_ref, mask_ref, sn_ref, vn_ref, oc_ref, ow_ref, gate_ref, o_ref, m_ref, l_ref, acc_ref) = refs[SMP_PAGES:]
    p = pl.program_id(1)
    grp = lax.broadcasted_iota(jnp.int32, (1, LANES), 1) // NSA_HPG

    @pl.when(p == 0)
    def _():
        m_ref[...] = sn_ref[0:1, :]
        l_ref[...] = jnp.ones(l_ref.shape, f32)
        acc_ref[...] = vn_ref[...]

    qt = qt_ref[...]
    slabs = 2 * NSA_KV_GROUPS
    s = None
    for g in range(NSA_KV_GROUPS):
        kg = jnp.concatenate([pages[kk][pl.ds(g, PAGE_SIZE, stride=slabs), :] for kk in range(SMP_PAGES)], axis=0)
        sg = jnp.dot(kg.astype(bf16), qt, preferred_element_type=f32)
        s = sg if g == 0 else jnp.where(grp == g, sg, s)
    bpp = PAGE_SIZE // SEL_BLOCK
    mrows = []
    for kk in range(SMP_PAGES):
        for half in range(bpp):
            mrow = mask_ref[pl.ds((p * SMP_PAGES + kk) * bpp + half, 1), :]
            mrows.append(jnp.broadcast_to(mrow, (SEL_BLOCK, LANES)))
    s = s * (NSA_DH ** -0.5) + bias_ref[...] + jnp.concatenate(mrows, axis=0)
    m_old = m_ref[...]
    m_new = jnp.maximum(m_old, jnp.max(s, axis=0, keepdims=True))
    alpha = jnp.exp(m_old - m_new)
    pe = jnp.exp(s - m_new)
    l_ref[...] = alpha * l_ref[...] + jnp.sum(pe, axis=0, keepdims=True)
    pb = pe.astype(bf16)
    upd = jnp.zeros(acc_ref.shape, f32)
    for g in range(NSA_KV_GROUPS):
        vg = jnp.concatenate([pages[kk][pl.ds(NSA_KV_GROUPS + g, PAGE_SIZE, stride=slabs), :]
                              for kk in range(SMP_PAGES)], axis=0)
        upd = upd + lax.dot_general(vg.astype(bf16), jnp.where(grp == g, pb, jnp.zeros_like(pb)), _TN,
                                    preferred_element_type=f32)
    acc_ref[...] = alpha * acc_ref[...] + upd
    m_ref[...] = m_new

    @pl.when(p == pl.num_programs(1) - 1)
    def _():
        gt = jax.nn.sigmoid(gate_ref[...])
        o_s = acc_ref[...] / l_ref[...]
        o_ref[...] = gt[0:1] * oc_ref[...] + gt[1:2] * o_s + gt[2:3] * ow_ref[...]


def nsa_sample(z, zs, row0, cache_c, cache_s, page_table, win_buf, cmp_pe, cmp_w, rel_bias):
    bs, n_pages = page_table.shape
    past = n_pages * PAGE_SIZE
    n_sub = past // CMP_STRIDE
    n_blocks = past // SEL_BLOCK + 1
    nsb = -(-n_blocks // SUBLANES) * SUBLANES
    w_buf = win_buf.shape[1]
    tbl = rel_bias.astype(f32)
    lane_pad = lambda a: jnp.pad(a, [(0, 0)] * (a.ndim - 1) + [(0, LANES - a.shape[-1])])

    tbl_lanes = lane_pad(tbl)

    def bias_rows(dist, ok):
        return jnp.where(ok[:, None], _rel_bias_of(tbl_lanes, dist), NEG_BIG)

    dist_c = past - (jnp.arange(n_sub, dtype=jnp.int32) * CMP_STRIDE + (CMP_BLOCK - 1))
    bias_c = bias_rows(dist_c, dist_c >= 0)
    w_pos = past - w_buf + jnp.arange(w_buf, dtype=jnp.int32)
    dist_w = past - w_pos
    bias_w = bias_rows(dist_w, (dist_w < WINDOW) & (w_pos >= 0))
    bias_0 = jnp.broadcast_to(_rel_bias_of(tbl_lanes, jnp.zeros((1,), jnp.int32)), (SUBLANES, LANES))
    dist_s = past - jnp.arange(past, dtype=jnp.int32)
    bias_s = bias_rows(dist_s, dist_s >= 0)
    cover_t = jnp.pad(_cover_matrix(n_sub - 1, n_blocks), [(0, 1), (0, nsb - n_blocks)]).T

    w2 = cmp_w.reshape(2, CMP_BLOCK // CMP_STRIDE, CMP_STRIDE * NSA_DH, NSA_DH)
    w2 = jnp.transpose(w2, (0, 2, 1, 3)).reshape(2, CMP_STRIDE * NSA_DH, 2 * NSA_DH).astype(bf16)
    pe2 = jnp.pad(cmp_pe.reshape(2, CMP_BLOCK // CMP_STRIDE, CMP_STRIDE * NSA_DH), [(0, 0), (0, SUBLANES - 2), (0, 0)])
    pe2 = pe2.astype(bf16)

    zrow = z[row0:row0 + bs]
    q = zrow[:, Z_NQ:Z_KV].reshape(bs, NSA_HEADS, NSA_DH)
    qt = lane_pad(jnp.transpose(q, (0, 2, 1))).astype(bf16)
    kv = zrow[:, Z_KV:Z_GA].reshape(bs, 3, 2, NSA_KV_GROUPS, NSA_DH)
    zero = jnp.zeros((bs, NSA_KV_GROUPS, NSA_DH), f32)
    pad_rows = lambda r0, r1: jnp.pad(jnp.stack([r0, r1], 2), [(0, 0), (0, 0), (0, SUBLANES - 2), (0, 0)]).astype(bf16)
    k_new = pad_rows(kv[:, 1, 0], kv[:, 2, 0])
    v_new_w = pad_rows(zero, kv[:, 2, 1])
    v_new_s = lane_pad(jnp.transpose(jnp.repeat(kv[:, 1, 1], NSA_HPG, axis=1), (0, 2, 1)))
    gates = zs[row0:row0 + bs, 2 * GDN_HEADS:2 * GDN_HEADS + 3 * NSA_HEADS].reshape(bs, NSA_HEADS, 3)
    gates = jnp.pad(lane_pad(jnp.transpose(gates, (0, 2, 1))), [(0, 0), (0, SUBLANES - 3), (0, 0)])

    pm = compress_sample(page_table, cache_c.reshape(-1, PAGE_SIZE * 2 * NSA_KV_GROUPS, NSA_DH), w2)
    per_seq = lambda *shape: pl.BlockSpec((None,) + shape, lambda b: (b,) + (0,) * len(shape))
    const = lambda *shape: pl.BlockSpec(shape, lambda b: (0,) * len(shape))
    oc_t, ow_t, mask, s_new = pl.pallas_call(
        functools.partial(_nsa_sample_select_body, past=past),
        grid=(bs,),
        in_specs=[per_seq(NSA_DH, LANES), per_seq(2, NSA_KV_GROUPS, n_sub, 2 * NSA_DH),
                  const(2, CMP_STRIDE * NSA_DH, 2 * NSA_DH), const(2, SUBLANES, CMP_STRIDE * NSA_DH),
                  const(n_sub, LANES), const(nsb, n_sub), per_seq(w_buf * 2 * NSA_KV_GROUPS, NSA_DH),
                  const(w_buf, LANES),
                  const(SUBLANES, LANES), per_seq(NSA_KV_GROUPS, SUBLANES, NSA_DH),
                  per_seq(NSA_KV_GROUPS, SUBLANES, NSA_DH)],
        out_specs=[per_seq(NSA_DH, LANES), per_seq(NSA_DH, LANES), per_seq(nsb, LANES), per_seq(SUBLANES, LANES)],
        out_shape=[jax.ShapeDtypeStruct((bs, NSA_DH, LANES), f32), jax.ShapeDtypeStruct((bs, NSA_DH, LANES), f32),
                   jax.ShapeDtypeStruct((bs, nsb, LANES), f32), jax.ShapeDtypeStruct((bs, SUBLANES, LANES), f32)],
        compiler_params=_cparams(("parallel",)),
        name="nsa_sample_select",
    )(qt, pm, w2, pe2, bias_c, cover_t, win_buf.reshape(bs, w_buf * 2 * NSA_KV_GROUPS, NSA_DH), bias_w, bias_0,
      k_new, v_new_w)

    seq = lambda *shape: pl.BlockSpec((None,) + shape, lambda b, p, pt: (b,) + (0,) * len(shape))
    grid_spec = pltpu.PrefetchScalarGridSpec(
        num_scalar_prefetch=1,
        grid=(bs, n_pages // SMP_PAGES),
        in_specs=_page_specs((PAGE_SIZE * 2 * NSA_KV_GROUPS, NSA_DH)) + [seq(NSA_DH, LANES),
                                  pl.BlockSpec((SMP_PAGES * PAGE_SIZE, LANES), lambda b, p, pt: (p, 0)),
                                  seq(nsb, LANES), seq(SUBLANES, LANES), seq(NSA_DH, LANES), seq(NSA_DH, LANES),
                                  seq(NSA_DH, LANES), seq(SUBLANES, LANES)],
        out_specs=seq(NSA_DH, LANES),
        scratch_shapes=[pltpu.VMEM((1, LANES), f32), pltpu.VMEM((1, LANES), f32), pltpu.VMEM((NSA_DH, LANES), f32)],
    )
    o_t = pl.pallas_call(
        _nsa_sample_attend_body,
        grid_spec=grid_spec,
        out_shape=jax.ShapeDtypeStruct((bs, NSA_DH, LANES), f32),
        compiler_params=_cparams(("parallel", "arbitrary")),
        name="nsa_sample_attend",
    )(page_table, *([cache_s.reshape(-1, PAGE_SIZE * 2 * NSA_KV_GROUPS, NSA_DH)] * SMP_PAGES), qt, bias_s, mask, s_new, v_new_s,
      oc_t, ow_t, gates)
    return jnp.transpose(o_t[:, :, :NSA_HEADS], (0, 2, 1)).reshape(bs, NSA_Q_W)


def _rel_bucket(dist):
    n = jnp.maximum(dist, 0)
    max_exact = REL_BUCKETS // 2
    nf = jnp.maximum(n, 1).astype(f32)
    large = max_exact + (jnp.log(nf / max_exact) / math.log(REL_MAX_DIST / max_exact)
                         * (REL_BUCKETS - max_exact)).astype(jnp.int32)
    large = jnp.minimum(large, REL_BUCKETS - 1)
    return jnp.where(n < max_exact, n, large)


def _rel_bias_of(tbl, dist):
    one_hot = jax.nn.one_hot(_rel_bucket(dist), REL_BUCKETS, dtype=f32)
    return jnp.dot(one_hot, tbl, precision=lax.Precision.HIGHEST)


def _rel_bias_heads(tbl, dist):
    buckets = jnp.arange(REL_BUCKETS, dtype=jnp.int32).reshape((REL_BUCKETS,) + (1,) * dist.ndim)
    one_hot = (_rel_bucket(dist)[None] == buckets).astype(f32)
    return jnp.tensordot(tbl.T, one_hot, axes=1, precision=lax.Precision.HIGHEST)


def _cover_matrix(nc, ns):
    cs = np.arange(nc) * CMP_STRIDE
    ss = np.arange(ns) * SEL_BLOCK
    inter = np.minimum(cs[:, None] + CMP_BLOCK, ss[None, :] + SEL_BLOCK) - np.maximum(cs[:, None], ss[None, :])
    return jnp.asarray(np.clip(inter, 0, None) / CMP_BLOCK, dtype=f32)


def kernel(x_prompt, x_sample, p_prompt, p_sample, cache_cmp_kv, cache_slc_kv, page_table, state_win_kv, state_gdn, state_conv, g_mix, w_in, gdn_conv_w, gdn_dt_bias, gdn_a_log, gdn_norm, cmp_pe, cmp_w, rel_bias, w_proj_a, w_proj_b, w_out, g_ffn, w_router_group, b_router_group, w_router_expert, b_router_expert, w_gate, w_up, w_down, g_ple, w_ple_gate, w_ple_proj, g_final):
    bp, tp, d = x_prompt.shape
    bs, ts, _ = x_sample.shape
    n_p, n_s = bp * tp, bs * ts
    n_real = n_p + n_s
    mp = -(-n_real // ROW_ALIGN) * ROW_ALIGN
    pad = mp - n_real

    h = jnp.concatenate([x_prompt.reshape(n_p, d), x_sample.reshape(n_s, d), jnp.zeros((pad, d), f32)], 0)
    ple = jnp.concatenate([p_prompt[0].reshape(n_p, -1), p_sample[0].reshape(n_s, -1),
                           jnp.zeros((pad, p_prompt.shape[-1]), f32)], 0).astype(bf16)

    w = w_in[0]
    o_beta = 4 * GDN_QK_W
    o_nq = o_beta + 2 * GDN_HEADS
    o_gate = o_nq + NSA_Q_W + 6 * NSA_KV_W
    o_ga = o_gate + 3 * NSA_HEADS
    n_small = 2 * GDN_HEADS + 3 * NSA_HEADS

    a = rmsnorm_rows(h, g_mix[0], bf16)
    z = proj_columns(a, w, None, 0, o_beta, 0)
    z = proj_columns(a, w, z, o_nq, o_gate - o_nq, Z_NQ)
    z = proj_columns(a, w, z, o_ga, Z_COLS - Z_GA, Z_GA)
    zs = proj_small(a, w, o_beta, 2 * GDN_HEADS, o_gate, 3 * NSA_HEADS)

    def rows(x, lo, hi, which):
        if which == "p":
            return x[:n_p, lo:hi].reshape(bp, tp, hi - lo)
        return x[n_p:n_real, lo:hi].reshape(bs, ts, hi - lo)

    assert ts == 1 and tp % GDN_RT == 0 and n_p % GDN_SB == 0 and bs % GDN_SB == 0
    assert n_p % TM_ROWS == 0 and n_s <= TM_ROWS <= mp - n_p and tp % (NSA_KT * Q_BLOCK) == 0

    gb = jnp.stack([zs[:, 0:GDN_HEADS], zs[:, GDN_HEADS:2 * GDN_HEADS],
                    jnp.broadcast_to(gdn_dt_bias[0], (mp, GDN_HEADS)),
                    jnp.broadcast_to(gdn_a_log[0], (mp, GDN_HEADS))], -1)
    gb = jnp.transpose(gb, (1, 0, 2))
    gates_g = jnp.transpose(zs[:, 2 * GDN_HEADS:n_small].reshape(mp, NSA_KV_GROUPS, 3 * NSA_HPG), (1, 0, 2))

    outs = {}
    kv_shape = (2, NSA_KV_GROUPS, NSA_DH)
    kv_c_p = z[:n_p, Z_KV:Z_KV + _KV_ROW].reshape((bp, tp) + kv_shape)
    kv_s_p = z[:n_p, Z_KV + _KV_ROW:Z_KV + 2 * _KV_ROW].reshape((bp, tp) + kv_shape)
    w_keep = min(WINDOW, tp)
    win_p = jnp.stack([z[(b + 1) * tp - w_keep:(b + 1) * tp, Z_KV + 2 * _KV_ROW:Z_GA] for b in range(bp)], 0)
    win_p = win_p.reshape((bp, w_keep) + kv_shape)
    kvc = compress_prompt(z, cmp_w[0], cmp_pe[0], bp, tp)
    o_b_p = nsa_prompt_attention(z, gates_g, kvc, rel_bias, bp, tp)
    o_a_p, s_p = gdn_prompt(z, gb, gdn_conv_w[0], gdn_norm[0], bp, tp)
    conv_p = jnp.stack([z[(b + 1) * tp - (GDN_CONV - 1):(b + 1) * tp, :CONV_CH] for b in range(bp)], 0)
    outs["p"] = (kv_c_p, kv_s_p, win_p, s_p, conv_p)
    kv = rows(z, Z_KV, Z_GA, "s").reshape(bs, ts, 3, 2, NSA_KV_GROUPS, NSA_DH)
    o_b_s = nsa_sample(z, zs, n_p, cache_cmp_kv[0], cache_slc_kv[0], page_table, state_win_kv[0],
                       cmp_pe[0], cmp_w[0], rel_bias)
    win_s = jnp.concatenate([state_win_kv[0], kv[:, :, 2]], axis=1)[:, ts:]
    o_a_s, s_s = gdn_sample(z, gb, state_conv[0], state_gdn[0], gdn_conv_w[0], gdn_norm[0], n_p, bs)
    conv_s = jnp.concatenate([state_conv[0][:, 1:], rows(z, 0, CONV_CH, "s")], axis=1)
    outs["s"] = (kv[:, :, 0], kv[:, :, 1], win_s, s_s, conv_s)

    o_a = jnp.concatenate([o_a_p, o_a_s.astype(bf16), jnp.zeros((pad, GDN_V_W), bf16)], 0)
    o_b = jnp.concatenate([o_b_p, o_b_s.reshape(n_s, NSA_Q_W).astype(bf16), jnp.zeros((pad, NSA_Q_W), bf16)], 0)
    merged = merge_matmul(o_a, o_b, z, w_proj_a[0], w_proj_b[0])
    h = resid_matmul(merged, w_out[0], h)
    h = hier_moe(h, n_real, g_ffn[0], w_router_group[0], b_router_group[0], w_router_expert[0],
                 b_router_expert[0], w_gate[0], w_up[0], w_down[0])
    n3 = rmsnorm_rows(h, g_ple[0], bf16)
    h = ple_matmul(n3, w_ple_gate[0], ple, w_ple_proj[0], h)
    y_prompt = rmsnorm_rows(h, g_final, f32, 0, n_p).reshape(bp, tp, d)
    y_sample = rmsnorm_rows(h, g_final, f32, n_p, TM_ROWS)[:n_s].reshape(bs, ts, d)
    st_p, st_s = outs["p"], outs["s"]
    return (y_prompt, y_sample) + tuple(t[None] for t in st_p) + tuple(t[None] for t in st_s)
```

```python
import functools
import math

import jax
import jax.numpy as jnp
import numpy as np
from jax import lax
from jax.experimental import pallas as pl
from jax.experimental.pallas import tpu as pltpu

D_MODEL = 4096
GDN_HEADS = 16
GDN_DK = 128
GDN_DV = 128
GDN_CONV = 4
GDN_CHUNK = 64
NSA_HEADS = 16
NSA_KV_GROUPS = 4
NSA_HPG = NSA_HEADS // NSA_KV_GROUPS
NSA_DH = 128
CMP_BLOCK = 32
CMP_STRIDE = 16
SEL_BLOCK = 64
SEL_TOPK = 16
WINDOW = 512
Q_BLOCK = 128
FORCE_SCORE = 1.0e4
REL_BUCKETS = 32
REL_MAX_DIST = 1024
PAGE_SIZE = 128
MOE_GROUPS = 4
MOE_PER_GROUP = 8
MOE_EXPERTS = MOE_GROUPS * MOE_PER_GROUP
MOE_TOPK = 2
EXPERT_HIDDEN = 512
EPS = 1e-6

GDN_QK_W = GDN_HEADS * GDN_DK
GDN_V_W = GDN_HEADS * GDN_DV
CONV_CH = 2 * GDN_QK_W + GDN_V_W
NSA_Q_W = NSA_HEADS * NSA_DH
NSA_KV_W = NSA_KV_GROUPS * NSA_DH

LANES = 128
SUBLANES = 8
VMEM_LIMIT = 56 * 1024 * 1024

ROW_ALIGN = 768
TM_DENSE = 768
TN_DENSE = 512
TM_ROWS = 256
MOE_CHUNKS = 1

Z_Q, Z_K, Z_V, Z_ZG = 0, 2048, 4096, 6144
Z_NQ = 8192
Z_KV = 10240
Z_GA = 13312
Z_GB = 17408
Z_COLS = 21504

bf16 = jnp.bfloat16
f32 = jnp.float32


def _cparams(sem):
    return pltpu.CompilerParams(dimension_semantics=sem, vmem_limit_bytes=VMEM_LIMIT)


def _rmsnorm_body(x_ref, g_ref, o_ref):
    x = x_ref[...]
    y = x * lax.rsqrt(jnp.mean(x * x, -1, keepdims=True) + EPS)
    o_ref[...] = (y * g_ref[...]).astype(o_ref.dtype)


def rmsnorm_rows(x, g, out_dtype, row0=0, n_rows=None):
    m, d = x.shape
    n_rows = m if n_rows is None else n_rows
    t0 = row0 // TM_ROWS
    return pl.pallas_call(
        _rmsnorm_body,
        grid=(n_rows // TM_ROWS,),
        in_specs=[pl.BlockSpec((TM_ROWS, d), lambda i: (t0 + i, 0)),
                  pl.BlockSpec((1, d), lambda i: (0, 0))],
        out_specs=pl.BlockSpec((TM_ROWS, d), lambda i: (i, 0)),
        out_shape=jax.ShapeDtypeStruct((n_rows, d), out_dtype),
        compiler_params=_cparams(("parallel",)),
        name="rmsnorm_rows",
    )(x, g.reshape(1, d))


_NT = (((1,), (1,)), ((), ()))
PROJ_HEAD = 128


def _proj_body(a_ref, wa_ref, wb_ref, o_ref, ws_ref, *, parts):
    j = pl.program_id(0)

    @pl.when(pl.program_id(1) == 0)
    def _():
        for j_lo, j_hi, shift in parts:
            @pl.when((j >= j_lo) & (j < j_hi))
            def _(shift=shift):
                if shift:
                    w = jnp.concatenate([wa_ref[shift:, :], wb_ref[:shift, :]], axis=0)
                else:
                    w = wa_ref[...]
                ws_ref[...] = w.astype(bf16)
    o_ref[...] = lax.dot_general(a_ref[...], ws_ref[...], _NT, preferred_element_type=f32)


def proj_matmul(a, wt, groups):
    m, k = a.shape
    tn = TN_DENSE
    parts, j0 = [], 0
    for src, n in groups:
        shift = src - j0 * tn
        assert n % tn == 0 and 0 <= shift <= PROJ_HEAD and shift % SUBLANES == 0
        parts.append((j0, j0 + n // tn, shift))
        j0 += n // tn
    return pl.pallas_call(
        functools.partial(_proj_body, parts=tuple(parts)),
        grid=(j0, m // TM_DENSE),
        in_specs=[pl.BlockSpec((TM_DENSE, k), lambda j, i: (i, 0)),
                  pl.BlockSpec((tn, k), lambda j, i: (j, 0)),
                  pl.BlockSpec((PROJ_HEAD, k), lambda j, i: ((j + 1) * (tn // PROJ_HEAD), 0))],
        out_specs=pl.BlockSpec((TM_DENSE, tn), lambda j, i: (i, j)),
        out_shape=jax.ShapeDtypeStruct((m, j0 * tn), f32),
        scratch_shapes=[pltpu.VMEM((tn, k), bf16)],
        compiler_params=_cparams(("arbitrary", "arbitrary")),
        name="proj_matmul",
    )(a, wt, wt)


def _proj_small_body(a_ref, wa_ref, wb_ref, o_ref, ws_ref, *, lo_a, n_a, lo_b, n_b):
    @pl.when(pl.program_id(0) == 0)
    def _():
        pad = jnp.zeros((LANES - n_a - n_b, wa_ref.shape[1]), f32)
        w = jnp.concatenate([wa_ref[lo_a:lo_a + n_a, :], wb_ref[lo_b:lo_b + n_b, :], pad], axis=0)
        ws_ref[...] = w.astype(bf16)
    o_ref[...] = lax.dot_general(a_ref[...], ws_ref[...], _NT, preferred_element_type=f32)


def proj_small(a, wt, row_a, n_a, row_b, n_b):
    m, k = a.shape
    blk = LANES
    ja, lo_a = divmod(row_a, blk)
    jb, lo_b = divmod(row_b, blk)
    assert lo_a + n_a <= blk and lo_b + n_b <= blk and n_a + n_b <= LANES
    assert lo_a % SUBLANES == 0 and lo_b % SUBLANES == 0 and n_a % SUBLANES == 0 and n_b % SUBLANES == 0
    return pl.pallas_call(
        functools.partial(_proj_small_body, lo_a=lo_a, n_a=n_a, lo_b=lo_b, n_b=n_b),
        grid=(m // TM_DENSE,),
        in_specs=[pl.BlockSpec((TM_DENSE, k), lambda i: (i, 0)),
                  pl.BlockSpec((blk, k), lambda i: (ja, 0)),
                  pl.BlockSpec((blk, k), lambda i: (jb, 0))],
        out_specs=pl.BlockSpec((TM_DENSE, LANES), lambda i: (i, 0)),
        out_shape=jax.ShapeDtypeStruct((m, LANES), f32),
        scratch_shapes=[pltpu.VMEM((LANES, k), bf16)],
        compiler_params=_cparams(("arbitrary",)),
        name="proj_small",
    )(a, wt, wt)


def _merge_body(oa_ref, ob_ref, ga_ref, gb_ref, wa_ref, wb_ref, o_ref, wa_s, wb_s):
    @pl.when(pl.program_id(1) == 0)
    def _():
        wa_s[...] = wa_ref[...].astype(bf16)
        wb_s[...] = wb_ref[...].astype(bf16)
    pa = jnp.dot(oa_ref[...], wa_s[...], preferred_element_type=f32)
    pb = jnp.dot(ob_ref[...], wb_s[...], preferred_element_type=f32)
    o_ref[...] = (jax.nn.sigmoid(ga_ref[...]) * pa + jax.nn.sigmoid(gb_ref[...]) * pb).astype(o_ref.dtype)


def merge_matmul(o_a, o_b, z, w_a, w_b):
    m, ka = o_a.shape
    kb = o_b.shape[1]
    n = w_a.shape[1]
    tn = TN_DENSE
    ja, jb = Z_GA // tn, Z_GB // tn
    return pl.pallas_call(
        _merge_body,
        grid=(n // tn, m // TM_DENSE),
        in_specs=[pl.BlockSpec((TM_DENSE, ka), lambda j, i: (i, 0)),
                  pl.BlockSpec((TM_DENSE, kb), lambda j, i: (i, 0)),
                  pl.BlockSpec((TM_DENSE, tn), lambda j, i: (i, ja + j)),
                  pl.BlockSpec((TM_DENSE, tn), lambda j, i: (i, jb + j)),
                  pl.BlockSpec((ka, tn), lambda j, i: (0, j)),
                  pl.BlockSpec((kb, tn), lambda j, i: (0, j))],
        out_specs=pl.BlockSpec((TM_DENSE, tn), lambda j, i: (i, j)),
        out_shape=jax.ShapeDtypeStruct((m, n), bf16),
        scratch_shapes=[pltpu.VMEM((ka, tn), bf16), pltpu.VMEM((kb, tn), bf16)],
        compiler_params=_cparams(("arbitrary", "arbitrary")),
        name="merge_matmul",
    )(o_a, o_b, z, z, w_a, w_b)


def _resid_body(a_ref, w_ref, h_ref, o_ref, wb_ref):
    @pl.when(pl.program_id(1) == 0)
    def _():
        wb_ref[...] = w_ref[...].astype(bf16)
    o_ref[...] = h_ref[...] + jnp.dot(a_ref[...], wb_ref[...], preferred_element_type=f32)


def resid_matmul(a, w, h):
    m, k = a.shape
    n = w.shape[1]
    tn = TN_DENSE
    return pl.pallas_call(
        _resid_body,
        grid=(n // tn, m // TM_DENSE),
        in_specs=[pl.BlockSpec((TM_DENSE, k), lambda j, i: (i, 0)),
                  pl.BlockSpec((k, tn), lambda j, i: (0, j)),
                  pl.BlockSpec((TM_DENSE, tn), lambda j, i: (i, j))],
        out_specs=pl.BlockSpec((TM_DENSE, tn), lambda j, i: (i, j)),
        out_shape=jax.ShapeDtypeStruct((m, n), f32),
        scratch_shapes=[pltpu.VMEM((k, tn), bf16)],
        compiler_params=_cparams(("arbitrary", "arbitrary")),
        name="resid_matmul",
    )(a, w, h)


def _ple_body(a_ref, w_ref, p_ref, wp_ref, h_ref, o_ref, wb_ref, wpb_ref):
    @pl.when(pl.program_id(1) == 0)
    def _():
        wb_ref[...] = w_ref[...].astype(bf16)
        wpb_ref[...] = wp_ref[...].astype(bf16)
    gate = jax.nn.sigmoid(jnp.dot(a_ref[...], wb_ref[...], preferred_element_type=f32))
    emb = jnp.dot(p_ref[...], wpb_ref[...], preferred_element_type=f32)
    o_ref[...] = h_ref[...] + gate * emb


def ple_matmul(a, w_gate, p, w_proj, h):
    m, k = a.shape
    kp = p.shape[1]
    n = w_gate.shape[1]
    tn = TN_DENSE
    return pl.pallas_call(
        _ple_body,
        grid=(n // tn, m // TM_DENSE),
        in_specs=[pl.BlockSpec((TM_DENSE, k), lambda j, i: (i, 0)),
                  pl.BlockSpec((k, tn), lambda j, i: (0, j)),
                  pl.BlockSpec((TM_DENSE, kp), lambda j, i: (i, 0)),
                  pl.BlockSpec((kp, tn), lambda j, i: (0, j)),
                  pl.BlockSpec((TM_DENSE, tn), lambda j, i: (i, j))],
        out_specs=pl.BlockSpec((TM_DENSE, tn), lambda j, i: (i, j)),
        out_shape=jax.ShapeDtypeStruct((m, n), f32),
        scratch_shapes=[pltpu.VMEM((k, tn), bf16), pltpu.VMEM((kp, tn), bf16)],
        compiler_params=_cparams(("arbitrary", "arbitrary")),
        name="ple_matmul",
    )(a, w_gate, p, w_proj, h)


def _router_body(h_ref, g_ref, wr_ref, br_ref, m_ref, r_ref):
    x = h_ref[...]
    y = x * lax.rsqrt(jnp.mean(x * x, -1, keepdims=True) + EPS) * g_ref[...]
    m_ref[...] = y.astype(bf16)
    logits = jnp.dot(y, wr_ref[...], preferred_element_type=f32,
                     precision=lax.Precision.HIGHEST) + br_ref[...]
    lane = lax.broadcasted_iota(jnp.int32, logits.shape, 1)
    neg = -jnp.inf
    lg = jnp.where(lane < MOE_GROUPS, logits, neg)
    eg = jnp.exp(lg - jnp.max(lg, -1, keepdims=True))
    pg = eg / jnp.sum(eg, -1, keepdims=True)
    pg_top = jnp.max(pg, -1, keepdims=True)
    g_idx = jnp.min(jnp.where(pg == pg_top, lane, LANES), -1, keepdims=True)
    lo = MOE_GROUPS + MOE_PER_GROUP * g_idx
    emask = (lane >= lo) & (lane < lo + MOE_PER_GROUP)
    le = jnp.where(emask, logits, neg)
    ee = jnp.exp(le - jnp.max(le, -1, keepdims=True))
    pe = jnp.where(emask, ee / jnp.sum(ee, -1, keepdims=True), -1.0)
    v1 = jnp.max(pe, -1, keepdims=True)
    i1 = jnp.min(jnp.where(pe == v1, lane, LANES), -1, keepdims=True)
    pe2 = jnp.where(lane == i1, -1.0, pe)
    v2 = jnp.max(pe2, -1, keepdims=True)
    i2 = jnp.min(jnp.where(pe2 == v2, lane, LANES), -1, keepdims=True)
    den = v1 + v2
    w1 = pg_top * v1 / den
    w2 = pg_top * v2 / den
    e1 = (i1 - MOE_GROUPS).astype(f32)
    e2 = (i2 - MOE_GROUPS).astype(f32)
    r_ref[...] = jnp.where(lane == 0, e1, jnp.where(lane == 1, e2,
                           jnp.where(lane == 2, w1, jnp.where(lane == 3, w2, 0.0))))


def moe_router(h, g_ffn, w_router, b_router):
    m, d = h.shape
    return pl.pallas_call(
        _router_body,
        grid=(m // TM_ROWS,),
        in_specs=[pl.BlockSpec((TM_ROWS, d), lambda i: (i, 0)),
                  pl.BlockSpec((1, d), lambda i: (0, 0)),
                  pl.BlockSpec((d, LANES), lambda i: (0, 0)),
                  pl.BlockSpec((1, LANES), lambda i: (0, 0))],
        out_specs=[pl.BlockSpec((TM_ROWS, d), lambda i: (i, 0)),
                   pl.BlockSpec((TM_ROWS, LANES), lambda i: (i, 0))],
        out_shape=[jax.ShapeDtypeStruct((m, d), bf16),
                   jax.ShapeDtypeStruct((m, LANES), f32)],
        compiler_params=_cparams(("parallel",)),
        name="moe_router",
    )(h, g_ffn.reshape(1, d), w_router, b_router)


def _expert_body(te_ref, tv_ref, x_ref, rw_ref, wg_ref, wu_ref, wd_ref, *rest, tile0):
    o_ref, wg_s, wu_s, wd_s = rest[-4:]
    first = pl.program_id(0) == 0
    t = tile0 + pl.program_id(0)
    new_expert = first | (te_ref[t] != te_ref[jnp.maximum(t - 1, 0)])

    @pl.when(new_expert)
    def _():
        wg_s[...] = wg_ref[...].astype(bf16)
        wu_s[...] = wu_ref[...].astype(bf16)
        wd_s[...] = wd_ref[...].astype(bf16)

    @pl.when(tv_ref[t] > 0)
    def _():
        x = x_ref[...]
        gate = jnp.dot(x, wg_s[...], preferred_element_type=f32)
        up = jnp.dot(x, wu_s[...], preferred_element_type=f32)
        hid = (jax.nn.silu(gate) * up * rw_ref[...]).astype(bf16)
        o_ref[...] = jnp.dot(hid, wd_s[...], preferred_element_type=f32)

    @pl.when(tv_ref[t] == 0)
    def _():
        o_ref[...] = jnp.zeros_like(o_ref)


def expert_matmul(tile_expert, tile_valid, xs, row_w, wg, wu, wd, ys, tile0, n_slots):
    r, d = xs.shape
    f = wg.shape[2]
    weights = lambda shape: pl.BlockSpec((None,) + shape, lambda t, te, tv: (te[tile0 + t], 0, 0),
                                         pipeline_mode=pl.Buffered(1))
    in_specs = [pl.BlockSpec((TM_ROWS, d), lambda t, te, tv: (t, 0)),
                pl.BlockSpec((TM_ROWS, 1), lambda t, te, tv: (t, 0)),
                weights((d, f)), weights((d, f)), weights((f, d))]
    args = [tile_expert, tile_valid, xs, row_w, wg, wu, wd]
    aliases = {}
    if ys is not None:
        in_specs.append(pl.BlockSpec(memory_space=pl.ANY))
        args.append(ys)
        aliases = {len(args) - 1: 0}
    grid_spec = pltpu.PrefetchScalarGridSpec(
        num_scalar_prefetch=2,
        grid=(r // TM_ROWS,),
        in_specs=in_specs,
        out_specs=pl.BlockSpec((TM_ROWS, d), lambda t, te, tv: (tile0 + t, 0)),
        scratch_shapes=[pltpu.VMEM((d, f), bf16), pltpu.VMEM((d, f), bf16), pltpu.VMEM((f, d), bf16)],
    )
    return pl.pallas_call(
        functools.partial(_expert_body, tile0=tile0),
        grid_spec=grid_spec,
        out_shape=jax.ShapeDtypeStruct((n_slots, d), f32),
        input_output_aliases=aliases,
        compiler_params=_cparams(("arbitrary",)),
        name="expert_matmul",
    )(*args)


def hier_moe(h, n_real, g_ffn, w_rg, b_rg, w_re, b_re, w_gate, w_up, w_down):
    mp, d = h.shape
    n_route = MOE_GROUPS + MOE_EXPERTS
    w_router = jnp.zeros((d, LANES), f32).at[:, :MOE_GROUPS].set(w_rg).at[:, MOE_GROUPS:n_route].set(w_re)
    b_router = jnp.zeros((1, LANES), f32).at[0, :MOE_GROUPS].set(b_rg).at[0, MOE_GROUPS:n_route].set(b_re)
    m_bf, slab = moe_router(h, g_ffn, w_router, b_router)
    ids = slab[:n_real, 0:2].astype(jnp.int32)
    wts = slab[:n_real, 2:4]

    tm = TM_ROWS
    n_assign = n_real * MOE_TOPK
    n_slots = -(-(n_assign + MOE_EXPERTS * (tm - 1)) // tm) * tm
    e_flat = ids.reshape(-1)
    order = jnp.argsort(e_flat, stable=True)
    e_sorted = e_flat[order]
    counts = jnp.sum(jax.nn.one_hot(e_flat, MOE_EXPERTS, dtype=jnp.int32), axis=0)
    padded = -(-counts // tm) * tm
    start_p = jnp.cumsum(padded) - padded
    start = jnp.cumsum(counts) - counts
    slot_sorted = (start_p[e_sorted] + (jnp.arange(n_assign, dtype=jnp.int32) - start[e_sorted])).astype(jnp.int32)
    slot_of = slot_sorted[jnp.argsort(order)]
    n_tiles = n_slots // tm
    tile_start = jnp.arange(n_tiles, dtype=jnp.int32) * tm
    ends = jnp.cumsum(padded)
    tile_expert = jnp.minimum(jnp.searchsorted(ends, tile_start, side="right"), MOE_EXPERTS - 1).astype(jnp.int32)
    tile_valid = (tile_start < ends[-1]).astype(jnp.int32)
    slot = jnp.arange(n_slots, dtype=jnp.int32)
    slot_e = jnp.repeat(tile_expert, tm)
    off = slot - start_p[slot_e]
    used = (off < counts[slot_e]) & (jnp.repeat(tile_valid, tm) > 0)
    assign = order[jnp.clip(start[slot_e] + off, 0, n_assign - 1)]
    src_tok = jnp.where(used, assign // MOE_TOPK, 0).astype(jnp.int32)
    row_w = jnp.where(used, wts.reshape(-1)[assign], 0.0)

    ys = None
    for c in range(MOE_CHUNKS):
        t0, t1 = (c * n_tiles) // MOE_CHUNKS, ((c + 1) * n_tiles) // MOE_CHUNKS
        xs = m_bf.at[src_tok[t0 * tm:t1 * tm]].get(mode="promise_in_bounds")
        ys = expert_matmul(tile_expert, tile_valid, xs, row_w[t0 * tm:t1 * tm].reshape(-1, 1),
                           w_gate, w_up, w_down, ys, t0, n_slots)
    slot2 = slot_of.reshape(n_real, MOE_TOPK)
    y = ys.at[slot2[:, 0]].get(mode="promise_in_bounds") + ys.at[slot2[:, 1]].get(mode="promise_in_bounds")
    return h.at[:n_real].add(y)


NEG_BIG = -1e30
TILE_FAR = REL_MAX_DIST // Q_BLOCK + 1
TILE_WIN_EDGE = TILE_FAR + 1
TILE_NONE = TILE_FAR + 2
N_BIAS_TILES = TILE_FAR + 3
NSA_KT = 4


def _compress_body(x_ref, w_ref, pe_ref, o_ref):
    ns = o_ref.shape[0]
    acc0 = jnp.zeros((ns, NSA_DH), f32)
    acc1 = jnp.zeros((ns, NSA_DH), f32)
    for s in range(CMP_STRIDE):
        xs = x_ref[pl.ds(s, ns, stride=CMP_STRIDE), :]
        a0 = (xs + pe_ref[s:s + 1, :]).astype(bf16)
        a1 = (xs + pe_ref[CMP_STRIDE + s:CMP_STRIDE + s + 1, :]).astype(bf16)
        acc0 = acc0 + jnp.dot(a0, w_ref[s].astype(bf16), preferred_element_type=f32)
        acc1 = acc1 + jnp.dot(a1, w_ref[CMP_STRIDE + s].astype(bf16), preferred_element_type=f32)
    o_ref[...] = (acc0 + pltpu.roll(acc1, ns - 1, axis=0)).astype(o_ref.dtype)


def compress_prompt(z, cmp_w, cmp_pe, bn, t):
    ns = t // CMP_STRIDE
    col0 = Z_KV // NSA_DH
    return pl.pallas_call(
        _compress_body,
        grid=(bn, 2, NSA_KV_GROUPS),
        in_specs=[pl.BlockSpec((t, NSA_DH), lambda b, x, g: (b, col0 + x * NSA_KV_GROUPS + g)),
                  pl.BlockSpec((None, CMP_BLOCK, NSA_DH, NSA_DH), lambda b, x, g: (x, 0, 0, 0)),
                  pl.BlockSpec((None, CMP_BLOCK, NSA_DH), lambda b, x, g: (x, 0, 0))],
        out_specs=pl.BlockSpec((None, None, None, ns, NSA_DH), lambda b, x, g: (b, x, g, 0, 0)),
        out_shape=jax.ShapeDtypeStruct((bn, 2, NSA_KV_GROUPS, ns, NSA_DH), bf16),
        compiler_params=_cparams(("parallel", "parallel", "parallel")),
        name="compress_prompt",
    )(z, cmp_w, cmp_pe)


def _nsa_prompt_body(q_ref, kc_ref, vc_ref, ks_ref, vs_ref, kw_ref, vw_ref, gate_ref, bc_ref, bt_ref,
                     cov_ref, exp_ref, o_ref, mb_ref, s_ref, mx_ref, m_ref, l_ref, acc_ref):
    g = pl.program_id(1)
    i = pl.program_id(2)
    qb = Q_BLOCK
    rows = NSA_HPG * qb
    qf = q_ref[...]
    q = jnp.concatenate([qf[:, h * NSA_DH:(h + 1) * NSA_DH] for h in range(NSA_HPG)], axis=0).astype(bf16)

    ncp = kc_ref.shape[0]
    n_done = (qb // CMP_STRIDE) * (i + 1)
    bias_c = pltpu.roll(bc_ref[...], n_done % ncp, axis=1)
    lane_c = lax.broadcasted_iota(jnp.int32, bias_c.shape, 1)
    bias_c = jnp.where(lane_c < n_done, bias_c, NEG_BIG)
    s = lax.dot_general(q, kc_ref[...], _NT, preferred_element_type=f32) * (NSA_DH ** -0.5) + bias_c
    m = jnp.max(s, -1, keepdims=True)
    m = jnp.where(m < 0.5 * NEG_BIG, 0.0, m)
    e = jnp.exp(s - m)
    p = e / jnp.maximum(jnp.sum(e, -1, keepdims=True), 1e-30)
    o_c = jnp.dot(p.astype(bf16), vc_ref[...], preferred_element_type=f32)

    psum = p[0:qb]
    for h in range(1, NSA_HPG):
        psum = psum + p[h * qb:(h + 1) * qb]
    imp_t = lax.dot_general(cov_ref[...], psum, _NT, preferred_element_type=f32,
                            precision=lax.Precision.HIGHEST)
    nsb = imp_t.shape[0]
    blk = lax.broadcasted_iota(jnp.int32, (nsb, qb), 0)
    qpos = i * qb + lax.broadcasted_iota(jnp.int32, (nsb, qb), 1)
    cur = qpos // SEL_BLOCK
    forced = (blk == 0) | (blk == cur) | (blk == cur - 1)
    valid = blk * SEL_BLOCK <= qpos
    score = jnp.where(valid, jnp.where(forced, FORCE_SCORE, imp_t), -1.0)
    sel_t = jnp.zeros((nsb, qb), f32)
    for _ in range(min(SEL_TOPK, nsb)):
        mx = jnp.max(score, axis=0, keepdims=True)
        first = jnp.min(jnp.where(score == mx, blk, nsb), axis=0, keepdims=True)
        pick = blk == first
        sel_t = jnp.where(pick, 1.0, sel_t)
        score = jnp.where(pick, -2.0, score)
    unsel = ((sel_t - 1.0) * (-NEG_BIG)).T.astype(bf16)
    mb_ref[...] = jnp.dot(unsel, exp_ref[...], preferred_element_type=f32)

    scale = NSA_DH ** -0.5

    mx_ref[...] = jnp.full(mx_ref.shape, NEG_BIG, f32)

    kw_ = NSA_KT * qb
    n_steps = (i + NSA_KT) // NSA_KT

    def score_step(jj, carry):
        r0 = pl.multiple_of(jj * kw_, kw_)
        k = ks_ref[pl.ds(r0, kw_), :].astype(bf16)
        biases = []
        for tt in range(NSA_KT):
            d = i - (jj * NSA_KT + tt)
            biases.append(bt_ref[jnp.where(d < 0, TILE_NONE, jnp.minimum(d, TILE_FAR))])
        mb = mb_ref[:, pl.ds(r0, kw_)]
        s = (lax.dot_general(q, k, _NT, preferred_element_type=f32) * scale
             + jnp.concatenate(biases, axis=1) + jnp.concatenate([mb] * NSA_HPG, axis=0))
        s_ref[:, pl.ds(r0, kw_)] = s
        mx = mx_ref[...]
        for tt in range(NSA_KT):
            mx = jnp.maximum(mx, s[:, tt * qb:(tt + 1) * qb])
        mx_ref[...] = mx
        return carry

    lax.fori_loop(0, n_steps, score_step, 0)
    m_ref[...] = jnp.broadcast_to(jnp.max(mx_ref[...], -1, keepdims=True), m_ref.shape)
    l_ref[...] = jnp.zeros(l_ref.shape, f32)
    acc_ref[...] = jnp.zeros(acc_ref.shape, f32)

    def pv_step(jj, carry):
        r0 = pl.multiple_of(jj * kw_, kw_)
        p = jnp.exp(s_ref[:, pl.ds(r0, kw_)] - jnp.concatenate([m_ref[...]] * NSA_KT, axis=1))
        v = vs_ref[pl.ds(r0, kw_), :].astype(bf16)
        lsum = l_ref[...]
        for tt in range(NSA_KT):
            lsum = lsum + p[:, tt * qb:(tt + 1) * qb]
        l_ref[...] = lsum
        acc_ref[...] = acc_ref[...] + jnp.dot(p.astype(bf16), v, preferred_element_type=f32)
        return carry

    lax.fori_loop(0, n_steps, pv_step, 0)
    o_s = acc_ref[...] / jnp.sum(l_ref[...], -1, keepdims=True)

    n_win = WINDOW // qb
    j0 = jnp.maximum(i - n_win, 0)
    s_w = []
    for tt in range(n_win + 1):
        d = i - (j0 + tt)
        r0 = pl.multiple_of((j0 + tt) * qb, qb)
        k = kw_ref[pl.ds(r0, qb), :].astype(bf16)
        tile = jnp.where(d < 0, TILE_NONE, jnp.where(d == n_win, TILE_WIN_EDGE, d))
        s_w.append(lax.dot_general(q, k, _NT, preferred_element_type=f32) * scale + bt_ref[tile])
    mw = s_w[0]
    for s in s_w[1:]:
        mw = jnp.maximum(mw, s)
    mw = jnp.max(mw, -1, keepdims=True)
    lw = jnp.zeros((rows, qb), f32)
    o_w = jnp.zeros((rows, NSA_DH), f32)
    for tt in range(n_win + 1):
        r0 = pl.multiple_of((j0 + tt) * qb, qb)
        p = jnp.exp(s_w[tt] - mw)
        lw = lw + p
        o_w = o_w + jnp.dot(p.astype(bf16), vw_ref[pl.ds(r0, qb), :].astype(bf16), preferred_element_type=f32)
    o_w = o_w / jnp.sum(lw, -1, keepdims=True)

    gt = jax.nn.sigmoid(gate_ref[...])
    outs = []
    for h in range(NSA_HPG):
        c = 3 * h
        sl = slice(h * qb, (h + 1) * qb)
        outs.append(gt[:, c:c + 1] * o_c[sl] + gt[:, c + 1:c + 2] * o_s[sl] + gt[:, c + 2:c + 3] * o_w[sl])
    o_ref[...] = jnp.concatenate(outs, axis=1).astype(o_ref.dtype)


def _nsa_bias_tables(rel_bias, t):
    qb = Q_BLOCK
    nq = t // qb
    tbl = rel_bias.astype(f32)
    r = jnp.arange(qb, dtype=jnp.int32)[:, None]
    c = jnp.arange(qb, dtype=jnp.int32)[None, :]
    dists = [delta * qb + r - c for delta in range(TILE_FAR + 1)]
    oks = [d_ >= 0 for d_ in dists]
    d_win = (WINDOW // qb) * qb + r - c
    dists += [d_win, d_win]
    oks += [(d_win >= 0) & (d_win < WINDOW), jnp.zeros((qb, qb), bool)]
    bt = jnp.where(jnp.stack(oks, 0)[None], _rel_bias_heads(tbl, jnp.stack(dists, 0)), NEG_BIG)
    bt = bt.reshape(NSA_KV_GROUPS, NSA_HPG, N_BIAS_TILES, qb, qb)
    bt = jnp.transpose(bt, (0, 2, 1, 3, 4)).reshape(NSA_KV_GROUPS, N_BIAS_TILES, NSA_HPG * qb, qb)
    ncp = t // CMP_STRIDE
    m_back = (ncp - 1) - jnp.arange(ncp, dtype=jnp.int32)[None, :]
    dist_c = r + CMP_STRIDE * m_back + CMP_STRIDE - (qb + CMP_BLOCK - 1)
    bc = jnp.where((dist_c >= 0)[None], _rel_bias_heads(tbl, dist_c), NEG_BIG)
    return bt, bc.reshape(NSA_KV_GROUPS, NSA_HPG * qb, ncp)


def nsa_prompt_attention(z, gates_g, kvc, rel_bias, bn, t):
    qb = Q_BLOCK
    nq = t // qb
    ncp = t // CMP_STRIDE
    nsb = t // SEL_BLOCK
    rows = NSA_HPG * qb
    bt, bc = _nsa_bias_tables(rel_bias, t)
    cover_t = jnp.concatenate([_cover_matrix(ncp - 1, nsb), jnp.zeros((1, nsb), f32)], 0).T
    expand = jnp.asarray(np.repeat(np.eye(nsb, dtype=np.float32), SEL_BLOCK, axis=1), bf16)
    kcol = Z_KV // NSA_DH
    kv_spec = lambda off: pl.BlockSpec((t, NSA_DH), lambda b, g, i, off=off: (b, kcol + off + g))
    return pl.pallas_call(
        _nsa_prompt_body,
        grid=(bn, NSA_KV_GROUPS, nq),
        in_specs=[pl.BlockSpec((qb, NSA_HPG * NSA_DH), lambda b, g, i: (b * nq + i, Z_NQ // (NSA_HPG * NSA_DH) + g)),
                  pl.BlockSpec((None, None, None, ncp, NSA_DH), lambda b, g, i: (b, 0, g, 0, 0)),
                  pl.BlockSpec((None, None, None, ncp, NSA_DH), lambda b, g, i: (b, 1, g, 0, 0)),
                  kv_spec(2 * NSA_KV_GROUPS), kv_spec(3 * NSA_KV_GROUPS),
                  kv_spec(4 * NSA_KV_GROUPS), kv_spec(5 * NSA_KV_GROUPS),
                  pl.BlockSpec((None, qb, 3 * NSA_HPG), lambda b, g, i: (g, b * nq + i, 0)),
                  pl.BlockSpec((None, rows, ncp), lambda b, g, i: (g, 0, 0)),
                  pl.BlockSpec((None, N_BIAS_TILES, rows, qb), lambda b, g, i: (g, 0, 0, 0)),
                  pl.BlockSpec((nsb, ncp), lambda b, g, i: (0, 0)),
                  pl.BlockSpec((nsb, t), lambda b, g, i: (0, 0))],
        out_specs=pl.BlockSpec((qb, NSA_HPG * NSA_DH), lambda b, g, i: (b * nq + i, g)),
        out_shape=jax.ShapeDtypeStruct((bn * t, NSA_Q_W), bf16),
        scratch_shapes=[pltpu.VMEM((qb, t), f32), pltpu.VMEM((rows, t), f32), pltpu.VMEM((rows, qb), f32),
                        pltpu.VMEM((rows, qb), f32), pltpu.VMEM((rows, qb), f32), pltpu.VMEM((rows, NSA_DH), f32)],
        compiler_params=_cparams(("parallel", "parallel", "arbitrary")),
        name="nsa_prompt_attention",
    )(z, kvc, kvc, z, z, z, z, gates_g, bc, bt, cover_t, expand)


GDN_RT = 1024
GDN_GROUP = 4
_HI = lax.Precision.HIGHEST


def _mm1(a, b):
    return jnp.dot(a.astype(bf16), b.astype(bf16), preferred_element_type=f32)


def _mm3(a, b):
    ah, bh = a.astype(bf16), b.astype(bf16)
    al = (a - ah.astype(f32)).astype(bf16)
    bl = (b - bh.astype(f32)).astype(bf16)
    return (jnp.dot(ah, bh, preferred_element_type=f32) + jnp.dot(ah, bl, preferred_element_type=f32)
            + jnp.dot(al, bh, preferred_element_type=f32))


def _gdn_prep_body(q_ref, k_ref, v_ref, qp_ref, kp_ref, vp_ref, gb_ref, cwq_ref, cwk_ref, cwv_ref,
                   w_ref, u_ref, aqk_ref, qg_ref, kg_ref, gl_ref, qs_ref, ks_ref, vs_ref):
    rt = q_ref.shape[0]
    cs = GDN_CHUNK
    first = pl.program_id(2) == 0
    pad = SUBLANES

    def conv_silu(x_ref, prev_ref, scr_ref, cw_ref):
        scr_ref[0:pad, :] = jnp.where(first, 0.0, prev_ref[...])
        scr_ref[pad:pad + rt, :] = x_ref[...]
        y = scr_ref[pl.ds(pad - (GDN_CONV - 1), rt), :] * cw_ref[0:1, :]
        for j in range(1, GDN_CONV):
            y = y + scr_ref[pl.ds(pad - (GDN_CONV - 1) + j, rt), :] * cw_ref[j:j + 1, :]
        return y * jax.nn.sigmoid(y)

    q = conv_silu(q_ref, qp_ref, qs_ref, cwq_ref)
    k = conv_silu(k_ref, kp_ref, ks_ref, cwk_ref)
    v = conv_silu(v_ref, vp_ref, vs_ref, cwv_ref)
    q = q * lax.rsqrt(jnp.sum(q * q, -1, keepdims=True) + EPS) * (GDN_DK ** -0.5)
    k = k * lax.rsqrt(jnp.sum(k * k, -1, keepdims=True) + EPS)

    gb = gb_ref[...]
    x = gb[:, 1:2] + gb[:, 2:3]
    softplus = jnp.maximum(x, 0.0) + jnp.log1p(jnp.exp(-jnp.abs(x)))
    g = jnp.broadcast_to(-jnp.exp(gb[:, 3:4]) * softplus, (rt, GDN_DK))
    beta = jnp.broadcast_to(jax.nn.sigmoid(gb[:, 0:1]), (rt, GDN_DK))

    row_in_chunk = lax.broadcasted_iota(jnp.int32, (rt, GDN_DK), 0) % cs
    gcum_all = g
    step = 1
    while step < cs:
        gcum_all = gcum_all + jnp.where(row_in_chunk >= step, pltpu.roll(gcum_all, step, axis=0), 0.0)
        step *= 2

    gr = GDN_GROUP * cs
    ri = lax.broadcasted_iota(jnp.int32, (gr, gr), 0)
    ci = lax.broadcasted_iota(jnp.int32, (gr, gr), 1)
    same_chunk = (ri // cs) == (ci // cs)
    incl = same_chunk & (ri >= ci)
    strict = same_chunk & (ri > ci)
    eye = (ri == ci).astype(f32)
    gls = []
    for grp in range(rt // gr):
        sl = slice(grp * gr, (grp + 1) * gr)
        qc, kc, vc, bc_, gcum = q[sl], k[sl], v[sl], beta[sl], gcum_all[sl]
        g_col = jnp.concatenate([gcum] * (gr // GDN_DK), axis=1)
        g_row = jnp.broadcast_to(gcum.T[0:1, :], (gr, gr))
        dec = jnp.exp(jnp.where(incl, g_col - g_row, 0.0))
        eg = jnp.exp(gcum)
        lasts = [gcum[(c + 1) * cs - 1:(c + 1) * cs, :] for c in range(GDN_GROUP)]
        g_last = jnp.concatenate([jnp.broadcast_to(r, (cs, GDN_DK)) for r in lasts], axis=0)
        kb = kc * bc_
        kbf = kc.astype(bf16)
        lmat = lax.dot_general(kb.astype(bf16), kbf, _NT, preferred_element_type=f32) * jnp.where(strict, dec, 0.0)
        aqk = lax.dot_general(qc.astype(bf16), kbf, _NT, preferred_element_type=f32) * jnp.where(incl, dec, 0.0)
        inv = eye - lmat
        pw = lmat
        for _ in range(int(math.log2(cs)) - 1):
            pw = _mm1(pw, pw)
            inv = inv + _mm1(inv, pw)
        for _ in range(2):
            inv = inv + _mm1(inv, eye - inv - _mm3(lmat, inv))
        rhs = jnp.concatenate([kb * eg, vc * bc_], axis=1)
        wu = _mm3(inv, rhs)
        w_ref[sl, :] = wu[:, :GDN_DK].astype(w_ref.dtype)
        u_ref[sl, :] = wu[:, GDN_DK:]
        for c in range(GDN_GROUP):
            blk = slice(c * cs, (c + 1) * cs)
            aqk_ref[grp * gr + c * cs:grp * gr + (c + 1) * cs, :] = aqk[blk, blk].astype(aqk_ref.dtype)
        qg_ref[sl, :] = (qc * eg).astype(qg_ref.dtype)
        kg_ref[sl, :] = (kc * jnp.exp(g_last - gcum)).astype(kg_ref.dtype)
        gls.extend(jnp.exp(r) for r in lasts)
    gl_ref[...] = jnp.concatenate(gls, axis=0)


def _gdn_scan_body(w_ref, u_ref, aqk_ref, qg_ref, kg_ref, gl_ref, za_ref, gn_ref, o_ref, sfin_ref, s_ref):
    c = pl.program_id(1)
    nc = pl.num_programs(1)
    n_gl = gl_ref.shape[1]

    @pl.when(c == 0)
    def _():
        s_ref[...] = jnp.zeros(s_ref.shape, f32)

    for h in range(GDN_HEADS):
        s = s_ref[h]
        sb = s.astype(bf16)
        v_new = u_ref[h] - jnp.dot(w_ref[h], sb, preferred_element_type=f32)
        vb = v_new.astype(bf16)
        o = (jnp.dot(qg_ref[h], sb, preferred_element_type=f32)
             + jnp.dot(aqk_ref[h], vb, preferred_element_type=f32))
        gl = gl_ref[h, pl.ds(c % n_gl, 1), :]
        s_ref[h] = gl * s + lax.dot_general(kg_ref[h], vb, (((0,), (0,)), ((), ())), preferred_element_type=f32)
        on = o * lax.rsqrt(jnp.mean(o * o, -1, keepdims=True) + EPS) * gn_ref[...]
        za = za_ref[:, h * GDN_DV:(h + 1) * GDN_DV]
        o_ref[:, h * GDN_DV:(h + 1) * GDN_DV] = (on * (za * jax.nn.sigmoid(za))).astype(o_ref.dtype)

    @pl.when(c == nc - 1)
    def _():
        sfin_ref[...] = s_ref[...]


def gdn_prompt(z, gb, conv_w, gdn_norm, bn, t):
    rt, cs = GDN_RT, GDN_CHUNK
    n_rt = t // rt
    m = bn * t
    hcol = GDN_DK // LANES
    qkv = lambda part: pl.BlockSpec((rt, GDN_DK), lambda h, b, r, part=part: (b * n_rt + r, part * GDN_HEADS + h))
    prev = lambda part: pl.BlockSpec(
        (SUBLANES, GDN_DK),
        lambda h, b, r, part=part: (jnp.maximum((b * n_rt + r) * (rt // SUBLANES) - 1, 0), part * GDN_HEADS + h))
    cw = lambda part: pl.BlockSpec((GDN_CONV, GDN_DK), lambda h, b, r, part=part: (0, part * GDN_HEADS + h))
    per_row = lambda width: pl.BlockSpec((None, rt, width), lambda h, b, r: (h, b * n_rt + r, 0))
    w_c, u_c, aqk, qg, kg, gl = pl.pallas_call(
        _gdn_prep_body,
        grid=(GDN_HEADS, bn, n_rt),
        in_specs=[qkv(0), qkv(1), qkv(2), prev(0), prev(1), prev(2), per_row(4), cw(0), cw(1), cw(2)],
        out_specs=[per_row(GDN_DK), per_row(GDN_DV), per_row(cs), per_row(GDN_DK), per_row(GDN_DK),
                   pl.BlockSpec((None, rt // cs, GDN_DK), lambda h, b, r: (h, b * n_rt + r, 0))],
        out_shape=[jax.ShapeDtypeStruct((GDN_HEADS, m, GDN_DK), bf16),
                   jax.ShapeDtypeStruct((GDN_HEADS, m, GDN_DV), f32),
                   jax.ShapeDtypeStruct((GDN_HEADS, m, cs), bf16),
                   jax.ShapeDtypeStruct((GDN_HEADS, m, GDN_DK), bf16),
                   jax.ShapeDtypeStruct((GDN_HEADS, m, GDN_DK), bf16),
                   jax.ShapeDtypeStruct((GDN_HEADS, m // cs, GDN_DK), f32)],
        scratch_shapes=[pltpu.VMEM((rt + SUBLANES, GDN_DK), f32)] * 3,
        compiler_params=_cparams(("parallel", "parallel", "parallel")),
        name="gdn_prep",
    )(z, z, z, z, z, z, gb, conv_w, conv_w, conv_w)

    nc = t // cs
    n_gl = rt // cs
    heads = lambda width: pl.BlockSpec((GDN_HEADS, cs, width), lambda b, c: (0, b * nc + c, 0))
    o, s_fin = pl.pallas_call(
        _gdn_scan_body,
        grid=(bn, nc),
        in_specs=[heads(GDN_DK), heads(GDN_DV), heads(cs), heads(GDN_DK), heads(GDN_DK),
                  pl.BlockSpec((GDN_HEADS, n_gl, GDN_DK), lambda b, c: (0, (b * nc + c) // n_gl, 0)),
                  pl.BlockSpec((cs, GDN_V_W), lambda b, c: (b * nc + c, Z_ZG // GDN_V_W)),
                  pl.BlockSpec((1, GDN_DV), lambda b, c: (0, 0))],
        out_specs=[pl.BlockSpec((cs, GDN_V_W), lambda b, c: (b * nc + c, 0)),
                   pl.BlockSpec((None, GDN_HEADS, GDN_DK, GDN_DV), lambda b, c: (b, 0, 0, 0))],
        out_shape=[jax.ShapeDtypeStruct((m, GDN_V_W), bf16),
                   jax.ShapeDtypeStruct((bn, GDN_HEADS, GDN_DK, GDN_DV), f32)],
        scratch_shapes=[pltpu.VMEM((GDN_HEADS, GDN_DK, GDN_DV), f32)],
        compiler_params=_cparams(("parallel", "arbitrary")),
        name="gdn_scan",
    )(w_c, u_c, aqk, qg, kg, gl, z, gdn_norm.reshape(1, GDN_DV))
    return o, s_fin


GDN_SB = 8


def _gdn_sample_body(x_ref, za_ref, gb_ref, sc_ref, s0_ref, cw_ref, gn_ref, o_ref, s_ref):
    sb = GDN_SB
    x3 = sc_ref[...]
    y = x_ref[...] * cw_ref[GDN_CONV - 1:GDN_CONV, :]
    for j in range(GDN_CONV - 1):
        y = y + x3[:, j, :] * cw_ref[j:j + 1, :]
    xc = y * jax.nn.sigmoid(y)
    for h in range(GDN_HEADS):
        q = xc[:, h * GDN_DK:(h + 1) * GDN_DK]
        k = xc[:, GDN_QK_W + h * GDN_DK:GDN_QK_W + (h + 1) * GDN_DK]
        v = xc[:, 2 * GDN_QK_W + h * GDN_DV:2 * GDN_QK_W + (h + 1) * GDN_DV]
        q = q * lax.rsqrt(jnp.sum(q * q, -1, keepdims=True) + EPS) * (GDN_DK ** -0.5)
        k = k * lax.rsqrt(jnp.sum(k * k, -1, keepdims=True) + EPS)
        gb = gb_ref[h]
        x = gb[:, 1:2] + gb[:, 2:3]
        softplus = jnp.maximum(x, 0.0) + jnp.log1p(jnp.exp(-jnp.abs(x)))
        eg = jnp.broadcast_to(jnp.exp(-jnp.exp(gb[:, 3:4]) * softplus), (sb, GDN_DV))
        beta = jnp.broadcast_to(jax.nn.sigmoid(gb[:, 0:1]), (sb, GDN_DV))
        qk = jnp.broadcast_to(jnp.sum(q * k, -1, keepdims=True), (sb, GDN_DV))
        q_t = q.T
        k_t = k.T
        o_rows = []
        for i in range(sb):
            s = s0_ref[i, h]
            kcol = k_t[:, i:i + 1]
            k_s = jnp.sum(kcol * s, axis=0, keepdims=True)
            q_s = jnp.sum(q_t[:, i:i + 1] * s, axis=0, keepdims=True)
            eg_i = eg[i:i + 1]
            v_new = beta[i:i + 1] * (v[i:i + 1] - eg_i * k_s)
            o_rows.append(eg_i * q_s + qk[i:i + 1] * v_new)
            s_ref[i, h] = eg_i * s + kcol * v_new
        o = jnp.concatenate(o_rows, axis=0)
        on = o * lax.rsqrt(jnp.mean(o * o, -1, keepdims=True) + EPS) * gn_ref[...]
        za = za_ref[:, h * GDN_DV:(h + 1) * GDN_DV]
        o_ref[:, h * GDN_DV:(h + 1) * GDN_DV] = on * (za * jax.nn.sigmoid(za))


def gdn_sample(z, gb, state_conv, state_gdn, conv_w, gdn_norm, row0, bs):
    sb = GDN_SB
    r0 = row0 // sb
    return pl.pallas_call(
        _gdn_sample_body,
        grid=(bs // sb,),
        in_specs=[pl.BlockSpec((sb, CONV_CH), lambda i: (r0 + i, 0)),
                  pl.BlockSpec((sb, GDN_V_W), lambda i: (r0 + i, Z_ZG // GDN_V_W)),
                  pl.BlockSpec((GDN_HEADS, sb, 4), lambda i: (0, r0 + i, 0)),
                  pl.BlockSpec((sb, GDN_CONV - 1, CONV_CH), lambda i: (i, 0, 0)),
                  pl.BlockSpec((sb, GDN_HEADS, GDN_DK, GDN_DV), lambda i: (i, 0, 0, 0)),
                  pl.BlockSpec((GDN_CONV, CONV_CH), lambda i: (0, 0)),
                  pl.BlockSpec((1, GDN_DV), lambda i: (0, 0))],
        out_specs=[pl.BlockSpec((sb, GDN_V_W), lambda i: (i, 0)),
                   pl.BlockSpec((sb, GDN_HEADS, GDN_DK, GDN_DV), lambda i: (i, 0, 0, 0))],
        out_shape=[jax.ShapeDtypeStruct((bs, GDN_V_W), f32),
                   jax.ShapeDtypeStruct((bs, GDN_HEADS, GDN_DK, GDN_DV), f32)],
        compiler_params=_cparams(("parallel",)),
        name="gdn_sample",
    )(z, z, gb, state_conv, state_gdn, conv_w, gdn_norm.reshape(1, GDN_DV))


SMP_PAGES = 8
_TN = (((0,), (0,)), ((), ()))
_KV_ROW = 2 * NSA_KV_GROUPS * NSA_DH


def _page_specs(shape):
    return [pl.BlockSpec((None,) + shape, lambda b, p, pt, kk=kk: (pt[b, p * SMP_PAGES + kk], 0, 0))
            for kk in range(SMP_PAGES)]


def _compress_sample_body(pt_ref, *refs):
    pages = refs[:SMP_PAGES]
    w_ref, o_ref = refs[SMP_PAGES:]
    per = PAGE_SIZE // CMP_STRIDE
    slabs = 2 * NSA_KV_GROUPS
    for x in range(2):
        rows = []
        for kk in range(SMP_PAGES):
            for g in range(NSA_KV_GROUPS):
                slab = x * NSA_KV_GROUPS + g
                parts = [pages[kk][pl.ds(s * slabs + slab, per, stride=CMP_STRIDE * slabs), :]
                         for s in range(CMP_STRIDE)]
                rows.append(jnp.concatenate(parts, axis=1))
        lhs = jnp.concatenate(rows, axis=0).astype(bf16)
        prod = jnp.dot(lhs, w_ref[x], preferred_element_type=f32)
        for kk in range(SMP_PAGES):
            for g in range(NSA_KV_GROUPS):
                r = (kk * NSA_KV_GROUPS + g) * per
                o_ref[x, g, kk * per:(kk + 1) * per, :] = prod[r:r + per]


def compress_sample(page_table, cache, w2):
    bs, n_pages = page_table.shape
    per = PAGE_SIZE // CMP_STRIDE
    n_sub = n_pages * per
    grid_spec = pltpu.PrefetchScalarGridSpec(
        num_scalar_prefetch=1,
        grid=(bs, n_pages // SMP_PAGES),
        in_specs=(_page_specs((PAGE_SIZE * 2 * NSA_KV_GROUPS, NSA_DH))
                  + [pl.BlockSpec((2, CMP_STRIDE * NSA_DH, 2 * NSA_DH), lambda b, p, pt: (0, 0, 0))]),
        out_specs=pl.BlockSpec((None, 2, NSA_KV_GROUPS, SMP_PAGES * per, 2 * NSA_DH), lambda b, p, pt: (b, 0, 0, p, 0)),
    )
    return pl.pallas_call(
        _compress_sample_body,
        grid_spec=grid_spec,
        out_shape=jax.ShapeDtypeStruct((bs, 2, NSA_KV_GROUPS, n_sub, 2 * NSA_DH), f32),
        compiler_params=_cparams(("parallel", "arbitrary")),
        name="compress_sample",
    )(page_table, *([cache] * SMP_PAGES), w2)


def _nsa_sample_select_body(qt_ref, pm_ref, w_ref, pe_ref, bc_ref, cov_ref, win_ref, bw_ref, b0_ref, kn_ref, vn_ref,
                            oc_ref, ow_ref, mask_ref, sn_ref, *, past):
    scale = NSA_DH ** -0.5
    qt = qt_ref[...]
    n_sub = pm_ref.shape[2]
    lane1 = lax.broadcasted_iota(jnp.int32, (1, LANES), 1)
    lane_n = lax.broadcasted_iota(jnp.int32, (n_sub, LANES), 1)
    pe_term = []
    for x in range(2):
        pr = jnp.dot(pe_ref[x], w_ref[x], preferred_element_type=f32)
        pe_term.append(pr[0:1, :NSA_DH] + pr[1:2, NSA_DH:])
    oc_t = jnp.zeros((NSA_DH, LANES), f32)
    ow_t = jnp.zeros((NSA_DH, LANES), f32)
    psum_all = jnp.zeros((n_sub, LANES), f32)
    snew = jnp.zeros((SUBLANES, LANES), f32)
    for g in range(NSA_KV_GROUPS):
        in_g = (lane1 // NSA_HPG) == g
        kvc = []
        for x in range(2):
            pm = pm_ref[x, g]
            kvc.append((pm[:, :NSA_DH] + pltpu.roll(pm[:, NSA_DH:], n_sub - 1, axis=0) + pe_term[x]).astype(bf16))
        s = jnp.dot(kvc[0], qt, preferred_element_type=f32) * scale + bc_ref[...]
        e = jnp.exp(s - jnp.max(s, axis=0, keepdims=True))
        p = jnp.where(in_g, e / jnp.sum(e, axis=0, keepdims=True), 0.0)
        oc_t = oc_t + lax.dot_general(kvc[1], p.astype(bf16), _TN, preferred_element_type=f32)
        psum = jnp.sum(p, axis=1, keepdims=True)
        psum_all = jnp.where((lane_n // NSA_HPG) == g, psum, psum_all)
        slabs = 2 * NSA_KV_GROUPS
        w_buf = win_ref.shape[0] // slabs
        kw = win_ref[pl.ds(g, w_buf, stride=slabs), :].astype(bf16)
        vw = win_ref[pl.ds(NSA_KV_GROUPS + g, w_buf, stride=slabs), :].astype(bf16)
        sw = jnp.dot(kw, qt, preferred_element_type=f32) * scale + bw_ref[...]
        sn = jnp.dot(kn_ref[g], qt, preferred_element_type=f32) * scale + b0_ref[...]
        mw = jnp.maximum(jnp.max(sw, axis=0, keepdims=True), sn[1:2])
        ew = jnp.exp(sw - mw)
        en = jnp.exp(sn[1:2] - mw)
        lw = jnp.sum(ew, axis=0, keepdims=True) + en
        pw = jnp.where(in_g, ew / lw, 0.0)
        row = lax.broadcasted_iota(jnp.int32, (SUBLANES, LANES), 0)
        pn = jnp.where((row == 1) & in_g, en / lw, 0.0)
        ow_t = (ow_t + lax.dot_general(vw, pw.astype(bf16), _TN, preferred_element_type=f32)
                + lax.dot_general(vn_ref[g], pn.astype(bf16), _TN, preferred_element_type=f32))
        snew = jnp.where(in_g, sn, snew)
    oc_ref[...] = oc_t
    ow_ref[...] = ow_t
    sn_ref[...] = snew

    imp_t = jnp.dot(cov_ref[...], psum_all, preferred_element_type=f32, precision=_HI)
    nsb = imp_t.shape[0]
    n_blocks = past // SEL_BLOCK + 1
    blk = lax.broadcasted_iota(jnp.int32, (nsb, LANES), 0)
    cur = past // SEL_BLOCK
    forced = (blk == 0) | (blk == cur) | (blk == cur - 1)
    valid = (blk * SEL_BLOCK <= past) & (blk < n_blocks)
    score = jnp.where(valid, jnp.where(forced, FORCE_SCORE, imp_t), -1.0)
    sel_t = jnp.zeros((nsb, LANES), f32)
    for _ in range(min(SEL_TOPK, n_blocks)):
        mx = jnp.max(score, axis=0, keepdims=True)
        first = jnp.min(jnp.where(score == mx, blk, nsb), axis=0, keepdims=True)
        pick = blk == first
        sel_t = jnp.where(pick, 1.0, sel_t)
        score = jnp.where(pick, -2.0, score)
    mask_ref[...] = (sel_t - 1.0) * (-NEG_BIG)


def _nsa_sample_attend_body(pt_ref, *refs):
    pages = refs[:SMP_PAGES]
    (qt_ref, bias_ref, mask_ref, sn_ref, vn_ref, oc_ref, ow_ref, gate_ref, o_ref, m_ref, l_ref, acc_ref) = refs[SMP_PAGES:]
    p = pl.program_id(1)
    grp = lax.broadcasted_iota(jnp.int32, (1, LANES), 1) // NSA_HPG

    @pl.when(p == 0)
    def _():
        m_ref[...] = sn_ref[0:1, :]
        l_ref[...] = jnp.ones(l_ref.shape, f32)
        acc_ref[...] = vn_ref[...]

    qt = qt_ref[...]
    slabs = 2 * NSA_KV_GROUPS
    s = None
    for g in range(NSA_KV_GROUPS):
        kg = jnp.concatenate([pages[kk][pl.ds(g, PAGE_SIZE, stride=slabs), :] for kk in range(SMP_PAGES)], axis=0)
        sg = jnp.dot(kg.astype(bf16), qt, preferred_element_type=f32)
        s = sg if g == 0 else jnp.where(grp == g, sg, s)
    bpp = PAGE_SIZE // SEL_BLOCK
    mrows = []
    for kk in range(SMP_PAGES):
        for half in range(bpp):
            mrow = mask_ref[pl.ds((p * SMP_PAGES + kk) * bpp + half, 1), :]
            mrows.append(jnp.broadcast_to(mrow, (SEL_BLOCK, LANES)))
    s = s * (NSA_DH ** -0.5) + bias_ref[...] + jnp.concatenate(mrows, axis=0)
    m_old = m_ref[...]
    m_new = jnp.maximum(m_old, jnp.max(s, axis=0, keepdims=True))
    alpha = jnp.exp(m_old - m_new)
    pe = jnp.exp(s - m_new)
    l_ref[...] = alpha * l_ref[...] + jnp.sum(pe, axis=0, keepdims=True)
    pb = pe.astype(bf16)
    upd = jnp.zeros(acc_ref.shape, f32)
    for g in range(NSA_KV_GROUPS):
        vg = jnp.concatenate([pages[kk][pl.ds(NSA_KV_GROUPS + g, PAGE_SIZE, stride=slabs), :]
                              for kk in range(SMP_PAGES)], axis=0)
        upd = upd + lax.dot_general(vg.astype(bf16), jnp.where(grp == g, pb, jnp.zeros_like(pb)), _TN,
                                    preferred_element_type=f32)
    acc_ref[...] = alpha * acc_ref[...] + upd
    m_ref[...] = m_new

    @pl.when(p == pl.num_programs(1) - 1)
    def _():
        gt = jax.nn.sigmoid(gate_ref[...])
        o_s = acc_ref[...] / l_ref[...]
        o_ref[...] = gt[0:1] * oc_ref[...] + gt[1:2] * o_s + gt[2:3] * ow_ref[...]


def nsa_sample(z, zs, row0, cache_c, cache_s, page_table, win_buf, cmp_pe, cmp_w, rel_bias):
    bs, n_pages = page_table.shape
    past = n_pages * PAGE_SIZE
    n_sub = past // CMP_STRIDE
    n_blocks = past // SEL_BLOCK + 1
    nsb = -(-n_blocks // SUBLANES) * SUBLANES
    w_buf = win_buf.shape[1]
    tbl = rel_bias.astype(f32)
    lane_pad = lambda a: jnp.pad(a, [(0, 0)] * (a.ndim - 1) + [(0, LANES - a.shape[-1])])

    tbl_lanes = lane_pad(tbl)

    def bias_rows(dist, ok):
        return jnp.where(ok[:, None], _rel_bias_of(tbl_lanes, dist), NEG_BIG)

    dist_c = past - (jnp.arange(n_sub, dtype=jnp.int32) * CMP_STRIDE + (CMP_BLOCK - 1))
    bias_c = bias_rows(dist_c, dist_c >= 0)
    w_pos = past - w_buf + jnp.arange(w_buf, dtype=jnp.int32)
    dist_w = past - w_pos
    bias_w = bias_rows(dist_w, (dist_w < WINDOW) & (w_pos >= 0))
    bias_0 = jnp.broadcast_to(_rel_bias_of(tbl_lanes, jnp.zeros((1,), jnp.int32)), (SUBLANES, LANES))
    dist_s = past - jnp.arange(past, dtype=jnp.int32)
    bias_s = bias_rows(dist_s, dist_s >= 0)
    cover_t = jnp.pad(_cover_matrix(n_sub - 1, n_blocks), [(0, 1), (0, nsb - n_blocks)]).T

    w2 = cmp_w.reshape(2, CMP_BLOCK // CMP_STRIDE, CMP_STRIDE * NSA_DH, NSA_DH)
    w2 = jnp.transpose(w2, (0, 2, 1, 3)).reshape(2, CMP_STRIDE * NSA_DH, 2 * NSA_DH).astype(bf16)
    pe2 = jnp.pad(cmp_pe.reshape(2, CMP_BLOCK // CMP_STRIDE, CMP_STRIDE * NSA_DH), [(0, 0), (0, SUBLANES - 2), (0, 0)])
    pe2 = pe2.astype(bf16)

    zrow = z[row0:row0 + bs]
    q = zrow[:, Z_NQ:Z_KV].reshape(bs, NSA_HEADS, NSA_DH)
    qt = lane_pad(jnp.transpose(q, (0, 2, 1))).astype(bf16)
    kv = zrow[:, Z_KV:Z_GA].reshape(bs, 3, 2, NSA_KV_GROUPS, NSA_DH)
    zero = jnp.zeros((bs, NSA_KV_GROUPS, NSA_DH), f32)
    pad_rows = lambda r0, r1: jnp.pad(jnp.stack([r0, r1], 2), [(0, 0), (0, 0), (0, SUBLANES - 2), (0, 0)]).astype(bf16)
    k_new = pad_rows(kv[:, 1, 0], kv[:, 2, 0])
    v_new_w = pad_rows(zero, kv[:, 2, 1])
    v_new_s = lane_pad(jnp.transpose(jnp.repeat(kv[:, 1, 1], NSA_HPG, axis=1), (0, 2, 1)))
    gates = zs[row0:row0 + bs, 2 * GDN_HEADS:2 * GDN_HEADS + 3 * NSA_HEADS].reshape(bs, NSA_HEADS, 3)
    gates = jnp.pad(lane_pad(jnp.transpose(gates, (0, 2, 1))), [(0, 0), (0, SUBLANES - 3), (0, 0)])

    pm = compress_sample(page_table, cache_c.reshape(-1, PAGE_SIZE * 2 * NSA_KV_GROUPS, NSA_DH), w2)
    per_seq = lambda *shape: pl.BlockSpec((None,) + shape, lambda b: (b,) + (0,) * len(shape))
    const = lambda *shape: pl.BlockSpec(shape, lambda b: (0,) * len(shape))
    oc_t, ow_t, mask, s_new = pl.pallas_call(
        functools.partial(_nsa_sample_select_body, past=past),
        grid=(bs,),
        in_specs=[per_seq(NSA_DH, LANES), per_seq(2, NSA_KV_GROUPS, n_sub, 2 * NSA_DH),
                  const(2, CMP_STRIDE * NSA_DH, 2 * NSA_DH), const(2, SUBLANES, CMP_STRIDE * NSA_DH),
                  const(n_sub, LANES), const(nsb, n_sub), per_seq(w_buf * 2 * NSA_KV_GROUPS, NSA_DH),
                  const(w_buf, LANES),
                  const(SUBLANES, LANES), per_seq(NSA_KV_GROUPS, SUBLANES, NSA_DH),
                  per_seq(NSA_KV_GROUPS, SUBLANES, NSA_DH)],
        out_specs=[per_seq(NSA_DH, LANES), per_seq(NSA_DH, LANES), per_seq(nsb, LANES), per_seq(SUBLANES, LANES)],
        out_shape=[jax.ShapeDtypeStruct((bs, NSA_DH, LANES), f32), jax.ShapeDtypeStruct((bs, NSA_DH, LANES), f32),
                   jax.ShapeDtypeStruct((bs, nsb, LANES), f32), jax.ShapeDtypeStruct((bs, SUBLANES, LANES), f32)],
        compiler_params=_cparams(("parallel",)),
        name="nsa_sample_select",
    )(qt, pm, w2, pe2, bias_c, cover_t, win_buf.reshape(bs, w_buf * 2 * NSA_KV_GROUPS, NSA_DH), bias_w, bias_0,
      k_new, v_new_w)

    seq = lambda *shape: pl.BlockSpec((None,) + shape, lambda b, p, pt: (b,) + (0,) * len(shape))
    grid_spec = pltpu.PrefetchScalarGridSpec(
        num_scalar_prefetch=1,
        grid=(bs, n_pages // SMP_PAGES),
        in_specs=_page_specs((PAGE_SIZE * 2 * NSA_KV_GROUPS, NSA_DH)) + [seq(NSA_DH, LANES),
                                  pl.BlockSpec((SMP_PAGES * PAGE_SIZE, LANES), lambda b, p, pt: (p, 0)),
                                  seq(nsb, LANES), seq(SUBLANES, LANES), seq(NSA_DH, LANES), seq(NSA_DH, LANES),
                                  seq(NSA_DH, LANES), seq(SUBLANES, LANES)],
        out_specs=seq(NSA_DH, LANES),
        scratch_shapes=[pltpu.VMEM((1, LANES), f32), pltpu.VMEM((1, LANES), f32), pltpu.VMEM((NSA_DH, LANES), f32)],
    )
    o_t = pl.pallas_call(
        _nsa_sample_attend_body,
        grid_spec=grid_spec,
        out_shape=jax.ShapeDtypeStruct((bs, NSA_DH, LANES), f32),
        compiler_params=_cparams(("parallel", "arbitrary")),
        name="nsa_sample_attend",
    )(page_table, *([cache_s.reshape(-1, PAGE_SIZE * 2 * NSA_KV_GROUPS, NSA_DH)] * SMP_PAGES), qt, bias_s, mask, s_new, v_new_s,
      oc_t, ow_t, gates)
    return jnp.transpose(o_t[:, :, :NSA_HEADS], (0, 2, 1)).reshape(bs, NSA_Q_W)


def _rel_bucket(dist):
    n = jnp.maximum(dist, 0)
    max_exact = REL_BUCKETS // 2
    nf = jnp.maximum(n, 1).astype(f32)
    large = max_exact + (jnp.log(nf / max_exact) / math.log(REL_MAX_DIST / max_exact)
                         * (REL_BUCKETS - max_exact)).astype(jnp.int32)
    large = jnp.minimum(large, REL_BUCKETS - 1)
    return jnp.where(n < max_exact, n, large)


def _rel_bias_of(tbl, dist):
    one_hot = jax.nn.one_hot(_rel_bucket(dist), REL_BUCKETS, dtype=f32)
    return jnp.dot(one_hot, tbl, precision=lax.Precision.HIGHEST)


def _rel_bias_heads(tbl, dist):
    buckets = jnp.arange(REL_BUCKETS, dtype=jnp.int32).reshape((REL_BUCKETS,) + (1,) * dist.ndim)
    one_hot = (_rel_bucket(dist)[None] == buckets).astype(f32)
    return jnp.tensordot(tbl.T, one_hot, axes=1, precision=lax.Precision.HIGHEST)


def _cover_matrix(nc, ns):
    cs = np.arange(nc) * CMP_STRIDE
    ss = np.arange(ns) * SEL_BLOCK
    inter = np.minimum(cs[:, None] + CMP_BLOCK, ss[None, :] + SEL_BLOCK) - np.maximum(cs[:, None], ss[None, :])
    return jnp.asarray(np.clip(inter, 0, None) / CMP_BLOCK, dtype=f32)


def kernel(x_prompt, x_sample, p_prompt, p_sample, cache_cmp_kv, cache_slc_kv, page_table, state_win_kv, state_gdn, state_conv, g_mix, w_in, gdn_conv_w, gdn_dt_bias, gdn_a_log, gdn_norm, cmp_pe, cmp_w, rel_bias, w_proj_a, w_proj_b, w_out, g_ffn, w_router_group, b_router_group, w_router_expert, b_router_expert, w_gate, w_up, w_down, g_ple, w_ple_gate, w_ple_proj, g_final):
    bp, tp, d = x_prompt.shape
    bs, ts, _ = x_sample.shape
    n_p, n_s = bp * tp, bs * ts
    n_real = n_p + n_s
    mp = -(-n_real // ROW_ALIGN) * ROW_ALIGN
    pad = mp - n_real

    h = jnp.concatenate([x_prompt.reshape(n_p, d), x_sample.reshape(n_s, d), jnp.zeros((pad, d), f32)], 0)
    ple = jnp.concatenate([p_prompt[0].reshape(n_p, -1), p_sample[0].reshape(n_s, -1),
                           jnp.zeros((pad, p_prompt.shape[-1]), f32)], 0).astype(bf16)

    wt = jnp.swapaxes(w_in[0], 0, 1)
    o_beta = 4 * GDN_QK_W
    o_nq = o_beta + 2 * GDN_HEADS
    o_gate = o_nq + NSA_Q_W + 6 * NSA_KV_W
    o_ga = o_gate + 3 * NSA_HEADS
    n_small = 2 * GDN_HEADS + 3 * NSA_HEADS

    a = rmsnorm_rows(h, g_mix[0], bf16)
    z = proj_matmul(a, wt, ((0, o_beta), (o_nq, o_gate - o_nq), (o_ga, Z_COLS - Z_GA)))
    zs = proj_small(a, wt, o_beta, 2 * GDN_HEADS, o_gate, 3 * NSA_HEADS)

    def rows(x, lo, hi, which):
        if which == "p":
            return x[:n_p, lo:hi].reshape(bp, tp, hi - lo)
        return x[n_p:n_real, lo:hi].reshape(bs, ts, hi - lo)

    assert ts == 1 and tp % GDN_RT == 0 and n_p % GDN_SB == 0 and bs % GDN_SB == 0
    assert n_p % TM_ROWS == 0 and n_s <= TM_ROWS <= mp - n_p and tp % (NSA_KT * Q_BLOCK) == 0

    gb = jnp.stack([zs[:, 0:GDN_HEADS], zs[:, GDN_HEADS:2 * GDN_HEADS],
                    jnp.broadcast_to(gdn_dt_bias[0], (mp, GDN_HEADS)),
                    jnp.broadcast_to(gdn_a_log[0], (mp, GDN_HEADS))], -1)
    gb = jnp.transpose(gb, (1, 0, 2))
    gates_g = jnp.transpose(zs[:, 2 * GDN_HEADS:n_small].reshape(mp, NSA_KV_GROUPS, 3 * NSA_HPG), (1, 0, 2))

    outs = {}
    kv_shape = (2, NSA_KV_GROUPS, NSA_DH)
    kv_c_p = z[:n_p, Z_KV:Z_KV + _KV_ROW].reshape((bp, tp) + kv_shape)
    kv_s_p = z[:n_p, Z_KV + _KV_ROW:Z_KV + 2 * _KV_ROW].reshape((bp, tp) + kv_shape)
    w_keep = min(WINDOW, tp)
    win_p = jnp.stack([z[(b + 1) * tp - w_keep:(b + 1) * tp, Z_KV + 2 * _KV_ROW:Z_GA] for b in range(bp)], 0)
    win_p = win_p.reshape((bp, w_keep) + kv_shape)
    kvc = compress_prompt(z, cmp_w[0], cmp_pe[0], bp, tp)
    o_b_p = nsa_prompt_attention(z, gates_g, kvc, rel_bias, bp, tp)
    o_a_p, s_p = gdn_prompt(z, gb, gdn_conv_w[0], gdn_norm[0], bp, tp)
    conv_p = jnp.stack([z[(b + 1) * tp - (GDN_CONV - 1):(b + 1) * tp, :CONV_CH] for b in range(bp)], 0)
    outs["p"] = (kv_c_p, kv_s_p, win_p, s_p, conv_p)
    kv = rows(z, Z_KV, Z_GA, "s").reshape(bs, ts, 3, 2, NSA_KV_GROUPS, NSA_DH)
    o_b_s = nsa_sample(z, zs, n_p, cache_cmp_kv[0], cache_slc_kv[0], page_table, state_win_kv[0],
                       cmp_pe[0], cmp_w[0], rel_bias)
    win_s = jnp.concatenate([state_win_kv[0], kv[:, :, 2]], axis=1)[:, ts:]
    o_a_s, s_s = gdn_sample(z, gb, state_conv[0], state_gdn[0], gdn_conv_w[0], gdn_norm[0], n_p, bs)
    conv_s = jnp.concatenate([state_conv[0][:, 1:], rows(z, 0, CONV_CH, "s")], axis=1)
    outs["s"] = (kv[:, :, 0], kv[:, :, 1], win_s, s_s, conv_s)

    o_a = jnp.concatenate([o_a_p, o_a_s.astype(bf16), jnp.zeros((pad, GDN_V_W), bf16)], 0)
    o_b = jnp.concatenate([o_b_p, o_b_s.reshape(n_s, NSA_Q_W).astype(bf16), jnp.zeros((pad, NSA_Q_W), bf16)], 0)
    merged = merge_matmul(o_a, o_b, z, w_proj_a[0], w_proj_b[0])
    h = resid_matmul(merged, w_out[0], h)
    h = hier_moe(h, n_real, g_ffn[0], w_router_group[0], b_router_group[0], w_router_expert[0],
                 b_router_expert[0], w_gate[0], w_up[0], w_down[0])
    n3 = rmsnorm_rows(h, g_ple[0], bf16)
    h = ple_matmul(n3, w_ple_gate[0], ple, w_ple_proj[0], h)
    y_prompt = rmsnorm_rows(h, g_final, f32, 0, n_p).reshape(bp, tp, d)
    y_sample = rmsnorm_rows(h, g_final, f32, n_p, TM_ROWS)[:n_s].reshape(bs, ts, d)
    st_p, st_s = outs["p"], outs["s"]
    return (y_prompt, y_sample) + tuple(t[None] for t in st_p) + tuple(t[None] for t in st_s)
```

```python
import functools
import math

import jax
import jax.numpy as jnp
import numpy as np
from jax import lax
from jax.experimental import pallas as pl
from jax.experimental.pallas import tpu as pltpu

D_MODEL = 4096
GDN_HEADS = 16
GDN_DK = 128
GDN_DV = 128
GDN_CONV = 4
GDN_CHUNK = 64
NSA_HEADS = 16
NSA_KV_GROUPS = 4
NSA_HPG = NSA_HEADS // NSA_KV_GROUPS
NSA_DH = 128
CMP_BLOCK = 32
CMP_STRIDE = 16
SEL_BLOCK = 64
SEL_TOPK = 16
WINDOW = 512
Q_BLOCK = 128
FORCE_SCORE = 1.0e4
REL_BUCKETS = 32
REL_MAX_DIST = 1024
PAGE_SIZE = 128
MOE_GROUPS = 4
MOE_PER_GROUP = 8
MOE_EXPERTS = MOE_GROUPS * MOE_PER_GROUP
MOE_TOPK = 2
EXPERT_HIDDEN = 512
EPS = 1e-6

GDN_QK_W = GDN_HEADS * GDN_DK
GDN_V_W = GDN_HEADS * GDN_DV
CONV_CH = 2 * GDN_QK_W + GDN_V_W
NSA_Q_W = NSA_HEADS * NSA_DH
NSA_KV_W = NSA_KV_GROUPS * NSA_DH

LANES = 128
SUBLANES = 8
VMEM_LIMIT = 56 * 1024 * 1024

ROW_ALIGN = 8448
TM_DENSE = 1056
TN_DENSE = 512
TM_ROWS = 256
MOE_CHUNKS = 1

Z_Q, Z_K, Z_V, Z_ZG = 0, 2048, 4096, 6144
Z_NQ = 8192
Z_KV = 10240
Z_GA = 13312
Z_GB = 17408
Z_COLS = 21504

bf16 = jnp.bfloat16
f32 = jnp.float32


def _cparams(sem):
    return pltpu.CompilerParams(dimension_semantics=sem, vmem_limit_bytes=VMEM_LIMIT)


def _rmsnorm_body(x_ref, g_ref, o_ref):
    x = x_ref[...]
    y = x * lax.rsqrt(jnp.mean(x * x, -1, keepdims=True) + EPS)
    o_ref[...] = (y * g_ref[...]).astype(o_ref.dtype)


def rmsnorm_rows(x, g, out_dtype, row0=0, n_rows=None):
    m, d = x.shape
    n_rows = m if n_rows is None else n_rows
    t0 = row0 // TM_ROWS
    return pl.pallas_call(
        _rmsnorm_body,
        grid=(n_rows // TM_ROWS,),
        in_specs=[pl.BlockSpec((TM_ROWS, d), lambda i: (t0 + i, 0)),
                  pl.BlockSpec((1, d), lambda i: (0, 0))],
        out_specs=pl.BlockSpec((TM_ROWS, d), lambda i: (i, 0)),
        out_shape=jax.ShapeDtypeStruct((n_rows, d), out_dtype),
        compiler_params=_cparams(("parallel",)),
        name="rmsnorm_rows",
    )(x, g.reshape(1, d))


_NT = (((1,), (1,)), ((), ()))
PROJ_HEAD = 128


def _proj_body(a_ref, wa_ref, wb_ref, o_ref, ws_ref, *, parts):
    j = pl.program_id(0)

    @pl.when(pl.program_id(1) == 0)
    def _():
        for j_lo, j_hi, shift in parts:
            @pl.when((j >= j_lo) & (j < j_hi))
            def _(shift=shift):
                if shift:
                    w = jnp.concatenate([wa_ref[shift:, :], wb_ref[:shift, :]], axis=0)
                else:
                    w = wa_ref[...]
                ws_ref[...] = w.astype(bf16)
    o_ref[...] = lax.dot_general(a_ref[...], ws_ref[...], _NT, preferred_element_type=f32)


def proj_matmul(a, wt, groups):
    m, k = a.shape
    tn = TN_DENSE
    parts, j0 = [], 0
    for src, n in groups:
        shift = src - j0 * tn
        assert n % tn == 0 and 0 <= shift <= PROJ_HEAD and shift % SUBLANES == 0
        parts.append((j0, j0 + n // tn, shift))
        j0 += n // tn
    return pl.pallas_call(
        functools.partial(_proj_body, parts=tuple(parts)),
        grid=(j0, m // TM_DENSE),
        in_specs=[pl.BlockSpec((TM_DENSE, k), lambda j, i: (i, 0)),
                  pl.BlockSpec((tn, k), lambda j, i: (j, 0)),
                  pl.BlockSpec((PROJ_HEAD, k), lambda j, i: ((j + 1) * (tn // PROJ_HEAD), 0))],
        out_specs=pl.BlockSpec((TM_DENSE, tn), lambda j, i: (i, j)),
        out_shape=jax.ShapeDtypeStruct((m, j0 * tn), f32),
        scratch_shapes=[pltpu.VMEM((tn, k), bf16)],
        compiler_params=_cparams(("arbitrary", "arbitrary")),
        name="proj_matmul",
    )(a, wt, wt)


def _proj_small_body(a_ref, wa_ref, wb_ref, o_ref, ws_ref, *, lo_a, n_a, lo_b, n_b):
    @pl.when(pl.program_id(0) == 0)
    def _():
        pad = jnp.zeros((LANES - n_a - n_b, wa_ref.shape[1]), f32)
        w = jnp.concatenate([wa_ref[lo_a:lo_a + n_a, :], wb_ref[lo_b:lo_b + n_b, :], pad], axis=0)
        ws_ref[...] = w.astype(bf16)
    o_ref[...] = lax.dot_general(a_ref[...], ws_ref[...], _NT, preferred_element_type=f32)


def proj_small(a, wt, row_a, n_a, row_b, n_b):
    m, k = a.shape
    blk = LANES
    ja, lo_a = divmod(row_a, blk)
    jb, lo_b = divmod(row_b, blk)
    assert lo_a + n_a <= blk and lo_b + n_b <= blk and n_a + n_b <= LANES
    assert lo_a % SUBLANES == 0 and lo_b % SUBLANES == 0 and n_a % SUBLANES == 0 and n_b % SUBLANES == 0
    return pl.pallas_call(
        functools.partial(_proj_small_body, lo_a=lo_a, n_a=n_a, lo_b=lo_b, n_b=n_b),
        grid=(m // TM_DENSE,),
        in_specs=[pl.BlockSpec((TM_DENSE, k), lambda i: (i, 0)),
                  pl.BlockSpec((blk, k), lambda i: (ja, 0)),
                  pl.BlockSpec((blk, k), lambda i: (jb, 0))],
        out_specs=pl.BlockSpec((TM_DENSE, LANES), lambda i: (i, 0)),
        out_shape=jax.ShapeDtypeStruct((m, LANES), f32),
        scratch_shapes=[pltpu.VMEM((LANES, k), bf16)],
        compiler_params=_cparams(("arbitrary",)),
        name="proj_small",
    )(a, wt, wt)


def _merge_body(oa_ref, ob_ref, ga_ref, gb_ref, wa_ref, wb_ref, o_ref, wa_s, wb_s):
    @pl.when(pl.program_id(1) == 0)
    def _():
        wa_s[...] = wa_ref[...].astype(bf16)
        wb_s[...] = wb_ref[...].astype(bf16)
    pa = jnp.dot(oa_ref[...], wa_s[...], preferred_element_type=f32)
    pb = jnp.dot(ob_ref[...], wb_s[...], preferred_element_type=f32)
    o_ref[...] = (jax.nn.sigmoid(ga_ref[...]) * pa + jax.nn.sigmoid(gb_ref[...]) * pb).astype(o_ref.dtype)


def merge_matmul(o_a, o_b, z, w_a, w_b):
    m, ka = o_a.shape
    kb = o_b.shape[1]
    n = w_a.shape[1]
    tn = TN_DENSE
    ja, jb = Z_GA // tn, Z_GB // tn
    return pl.pallas_call(
        _merge_body,
        grid=(n // tn, m // TM_DENSE),
        in_specs=[pl.BlockSpec((TM_DENSE, ka), lambda j, i: (i, 0)),
                  pl.BlockSpec((TM_DENSE, kb), lambda j, i: (i, 0)),
                  pl.BlockSpec((TM_DENSE, tn), lambda j, i: (i, ja + j)),
                  pl.BlockSpec((TM_DENSE, tn), lambda j, i: (i, jb + j)),
                  pl.BlockSpec((ka, tn), lambda j, i: (0, j)),
                  pl.BlockSpec((kb, tn), lambda j, i: (0, j))],
        out_specs=pl.BlockSpec((TM_DENSE, tn), lambda j, i: (i, j)),
        out_shape=jax.ShapeDtypeStruct((m, n), bf16),
        scratch_shapes=[pltpu.VMEM((ka, tn), bf16), pltpu.VMEM((kb, tn), bf16)],
        compiler_params=_cparams(("arbitrary", "arbitrary")),
        name="merge_matmul",
    )(o_a, o_b, z, z, w_a, w_b)


def _resid_body(a_ref, w_ref, h_ref, o_ref, wb_ref):
    @pl.when(pl.program_id(1) == 0)
    def _():
        wb_ref[...] = w_ref[...].astype(bf16)
    o_ref[...] = h_ref[...] + jnp.dot(a_ref[...], wb_ref[...], preferred_element_type=f32)


def resid_matmul(a, w, h):
    m, k = a.shape
    n = w.shape[1]
    tn = TN_DENSE
    return pl.pallas_call(
        _resid_body,
        grid=(n // tn, m // TM_DENSE),
        in_specs=[pl.BlockSpec((TM_DENSE, k), lambda j, i: (i, 0)),
                  pl.BlockSpec((k, tn), lambda j, i: (0, j)),
                  pl.BlockSpec((TM_DENSE, tn), lambda j, i: (i, j))],
        out_specs=pl.BlockSpec((TM_DENSE, tn), lambda j, i: (i, j)),
        out_shape=jax.ShapeDtypeStruct((m, n), f32),
        scratch_shapes=[pltpu.VMEM((k, tn), bf16)],
        compiler_params=_cparams(("arbitrary", "arbitrary")),
        name="resid_matmul",
    )(a, w, h)


def _ple_body(a_ref, w_ref, p_ref, wp_ref, h_ref, o_ref, wb_ref, wpb_ref):
    @pl.when(pl.program_id(1) == 0)
    def _():
        wb_ref[...] = w_ref[...].astype(bf16)
        wpb_ref[...] = wp_ref[...].astype(bf16)
    gate = jax.nn.sigmoid(jnp.dot(a_ref[...], wb_ref[...], preferred_element_type=f32))
    emb = jnp.dot(p_ref[...], wpb_ref[...], preferred_element_type=f32)
    o_ref[...] = h_ref[...] + gate * emb


def ple_matmul(a, w_gate, p, w_proj, h):
    m, k = a.shape
    kp = p.shape[1]
    n = w_gate.shape[1]
    tn = TN_DENSE
    return pl.pallas_call(
        _ple_body,
        grid=(n // tn, m // TM_DENSE),
        in_specs=[pl.BlockSpec((TM_DENSE, k), lambda j, i: (i, 0)),
                  pl.BlockSpec((k, tn), lambda j, i: (0, j)),
                  pl.BlockSpec((TM_DENSE, kp), lambda j, i: (i, 0)),
                  pl.BlockSpec((kp, tn), lambda j, i: (0, j)),
                  pl.BlockSpec((TM_DENSE, tn), lambda j, i: (i, j))],
        out_specs=pl.BlockSpec((TM_DENSE, tn), lambda j, i: (i, j)),
        out_shape=jax.ShapeDtypeStruct((m, n), f32),
        scratch_shapes=[pltpu.VMEM((k, tn), bf16), pltpu.VMEM((kp, tn), bf16)],
        compiler_params=_cparams(("arbitrary", "arbitrary")),
        name="ple_matmul",
    )(a, w_gate, p, w_proj, h)


def _router_body(h_ref, g_ref, wr_ref, br_ref, m_ref, r_ref):
    x = h_ref[...]
    y = x * lax.rsqrt(jnp.mean(x * x, -1, keepdims=True) + EPS) * g_ref[...]
    m_ref[...] = y.astype(bf16)
    logits = jnp.dot(y.astype(bf16), wr_ref[...].astype(bf16), preferred_element_type=f32) + br_ref[...]
    lane = lax.broadcasted_iota(jnp.int32, logits.shape, 1)
    neg = -jnp.inf
    lg = jnp.where(lane < MOE_GROUPS, logits, neg)
    eg = jnp.exp(lg - jnp.max(lg, -1, keepdims=True))
    pg = eg / jnp.sum(eg, -1, keepdims=True)
    pg_top = jnp.max(pg, -1, keepdims=True)
    g_idx = jnp.min(jnp.where(pg == pg_top, lane, LANES), -1, keepdims=True)
    lo = MOE_GROUPS + MOE_PER_GROUP * g_idx
    emask = (lane >= lo) & (lane < lo + MOE_PER_GROUP)
    le = jnp.where(emask, logits, neg)
    ee = jnp.exp(le - jnp.max(le, -1, keepdims=True))
    pe = jnp.where(emask, ee / jnp.sum(ee, -1, keepdims=True), -1.0)
    v1 = jnp.max(pe, -1, keepdims=True)
    i1 = jnp.min(jnp.where(pe == v1, lane, LANES), -1, keepdims=True)
    pe2 = jnp.where(lane == i1, -1.0, pe)
    v2 = jnp.max(pe2, -1, keepdims=True)
    i2 = jnp.min(jnp.where(pe2 == v2, lane, LANES), -1, keepdims=True)
    den = v1 + v2
    w1 = pg_top * v1 / den
    w2 = pg_top * v2 / den
    e1 = (i1 - MOE_GROUPS).astype(f32)
    e2 = (i2 - MOE_GROUPS).astype(f32)
    r_ref[...] = jnp.where(lane == 0, e1, jnp.where(lane == 1, e2,
                           jnp.where(lane == 2, w1, jnp.where(lane == 3, w2, 0.0))))


def moe_router(h, g_ffn, w_router, b_router):
    m, d = h.shape
    return pl.pallas_call(
        _router_body,
        grid=(m // TM_ROWS,),
        in_specs=[pl.BlockSpec((TM_ROWS, d), lambda i: (i, 0)),
                  pl.BlockSpec((1, d), lambda i: (0, 0)),
                  pl.BlockSpec((d, LANES), lambda i: (0, 0)),
                  pl.BlockSpec((1, LANES), lambda i: (0, 0))],
        out_specs=[pl.BlockSpec((TM_ROWS, d), lambda i: (i, 0)),
                   pl.BlockSpec((TM_ROWS, LANES), lambda i: (i, 0))],
        out_shape=[jax.ShapeDtypeStruct((m, d), bf16),
                   jax.ShapeDtypeStruct((m, LANES), f32)],
        compiler_params=_cparams(("parallel",)),
        name="moe_router",
    )(h, g_ffn.reshape(1, d), w_router, b_router)


def _expert_body(te_ref, tv_ref, x_ref, rw_ref, wg_ref, wu_ref, wd_ref, *rest, tile0):
    o_ref, wg_s, wu_s, wd_s = rest[-4:]
    first = pl.program_id(0) == 0
    t = tile0 + pl.program_id(0)
    new_expert = first | (te_ref[t] != te_ref[jnp.maximum(t - 1, 0)])

    @pl.when(new_expert)
    def _():
        wg_s[...] = wg_ref[...].astype(bf16)
        wu_s[...] = wu_ref[...].astype(bf16)
        wd_s[...] = wd_ref[...].astype(bf16)

    @pl.when(tv_ref[t] > 0)
    def _():
        x = x_ref[...]
        gate = jnp.dot(x, wg_s[...], preferred_element_type=f32)
        up = jnp.dot(x, wu_s[...], preferred_element_type=f32)
        hid = (jax.nn.silu(gate) * up * rw_ref[...]).astype(bf16)
        o_ref[...] = jnp.dot(hid, wd_s[...], preferred_element_type=f32)

    @pl.when(tv_ref[t] == 0)
    def _():
        o_ref[...] = jnp.zeros_like(o_ref)


def expert_matmul(tile_expert, tile_valid, xs, row_w, wg, wu, wd, ys, tile0, n_slots):
    r, d = xs.shape
    f = wg.shape[2]
    weights = lambda shape: pl.BlockSpec((None,) + shape, lambda t, te, tv: (te[tile0 + t], 0, 0),
                                         pipeline_mode=pl.Buffered(1))
    in_specs = [pl.BlockSpec((TM_ROWS, d), lambda t, te, tv: (t, 0)),
                pl.BlockSpec((TM_ROWS, 1), lambda t, te, tv: (t, 0)),
                weights((d, f)), weights((d, f)), weights((f, d))]
    args = [tile_expert, tile_valid, xs, row_w, wg, wu, wd]
    aliases = {}
    if ys is not None:
        in_specs.append(pl.BlockSpec(memory_space=pl.ANY))
        args.append(ys)
        aliases = {len(args) - 1: 0}
    grid_spec = pltpu.PrefetchScalarGridSpec(
        num_scalar_prefetch=2,
        grid=(r // TM_ROWS,),
        in_specs=in_specs,
        out_specs=pl.BlockSpec((TM_ROWS, d), lambda t, te, tv: (tile0 + t, 0)),
        scratch_shapes=[pltpu.VMEM((d, f), bf16), pltpu.VMEM((d, f), bf16), pltpu.VMEM((f, d), bf16)],
    )
    return pl.pallas_call(
        functools.partial(_expert_body, tile0=tile0),
        grid_spec=grid_spec,
        out_shape=jax.ShapeDtypeStruct((n_slots, d), f32),
        input_output_aliases=aliases,
        compiler_params=_cparams(("arbitrary",)),
        name="expert_matmul",
    )(*args)


def hier_moe(h, n_real, g_ffn, w_rg, b_rg, w_re, b_re, w_gate, w_up, w_down):
    mp, d = h.shape
    n_route = MOE_GROUPS + MOE_EXPERTS
    w_router = jnp.zeros((d, LANES), f32).at[:, :MOE_GROUPS].set(w_rg).at[:, MOE_GROUPS:n_route].set(w_re)
    b_router = jnp.zeros((1, LANES), f32).at[0, :MOE_GROUPS].set(b_rg).at[0, MOE_GROUPS:n_route].set(b_re)
    m_bf, slab = moe_router(h, g_ffn, w_router, b_router)
    ids = slab[:n_real, 0:2].astype(jnp.int32)
    wts = slab[:n_real, 2:4]

    tm = TM_ROWS
    n_assign = n_real * MOE_TOPK
    n_slots = -(-(n_assign + MOE_EXPERTS * (tm - 1)) // tm) * tm
    e_flat = ids.reshape(-1)
    order = jnp.argsort(e_flat, stable=True)
    e_sorted = e_flat[order]
    counts = jnp.sum(jax.nn.one_hot(e_flat, MOE_EXPERTS, dtype=jnp.int32), axis=0)
    padded = -(-counts // tm) * tm
    start_p = jnp.cumsum(padded) - padded
    start = jnp.cumsum(counts) - counts
    slot_sorted = (start_p[e_sorted] + (jnp.arange(n_assign, dtype=jnp.int32) - start[e_sorted])).astype(jnp.int32)
    slot_of = slot_sorted[jnp.argsort(order)]
    n_tiles = n_slots // tm
    tile_start = jnp.arange(n_tiles, dtype=jnp.int32) * tm
    ends = jnp.cumsum(padded)
    tile_expert = jnp.minimum(jnp.searchsorted(ends, tile_start, side="right"), MOE_EXPERTS - 1).astype(jnp.int32)
    tile_valid = (tile_start < ends[-1]).astype(jnp.int32)
    slot = jnp.arange(n_slots, dtype=jnp.int32)
    slot_e = jnp.repeat(tile_expert, tm)
    off = slot - start_p[slot_e]
    used = (off < counts[slot_e]) & (jnp.repeat(tile_valid, tm) > 0)
    assign = order[jnp.clip(start[slot_e] + off, 0, n_assign - 1)]
    src_tok = jnp.where(used, assign // MOE_TOPK, 0).astype(jnp.int32)
    row_w = jnp.where(used, wts.reshape(-1)[assign], 0.0)

    ys = None
    for c in range(MOE_CHUNKS):
        t0, t1 = (c * n_tiles) // MOE_CHUNKS, ((c + 1) * n_tiles) // MOE_CHUNKS
        xs = m_bf.at[src_tok[t0 * tm:t1 * tm]].get(mode="promise_in_bounds")
        ys = expert_matmul(tile_expert, tile_valid, xs, row_w[t0 * tm:t1 * tm].reshape(-1, 1),
                           w_gate, w_up, w_down, ys, t0, n_slots)
    slot2 = slot_of.reshape(n_real, MOE_TOPK)
    y = ys.at[slot2[:, 0]].get(mode="promise_in_bounds") + ys.at[slot2[:, 1]].get(mode="promise_in_bounds")
    return h.at[:n_real].add(y)


NEG_BIG = -1e30
TILE_FAR = REL_MAX_DIST // Q_BLOCK + 1
TILE_WIN_EDGE = TILE_FAR + 1
TILE_NONE = TILE_FAR + 2
N_BIAS_TILES = TILE_FAR + 3
NSA_KT = 4


def _compress_body(x_ref, w_ref, pe_ref, o_ref):
    ns = o_ref.shape[0]
    acc0 = jnp.zeros((ns, NSA_DH), f32)
    acc1 = jnp.zeros((ns, NSA_DH), f32)
    for s in range(CMP_STRIDE):
        xs = x_ref[pl.ds(s, ns, stride=CMP_STRIDE), :]
        a0 = (xs + pe_ref[s:s + 1, :]).astype(bf16)
        a1 = (xs + pe_ref[CMP_STRIDE + s:CMP_STRIDE + s + 1, :]).astype(bf16)
        acc0 = acc0 + jnp.dot(a0, w_ref[s].astype(bf16), preferred_element_type=f32)
        acc1 = acc1 + jnp.dot(a1, w_ref[CMP_STRIDE + s].astype(bf16), preferred_element_type=f32)
    o_ref[...] = (acc0 + pltpu.roll(acc1, ns - 1, axis=0)).astype(o_ref.dtype)


def compress_prompt(z, cmp_w, cmp_pe, bn, t):
    ns = t // CMP_STRIDE
    col0 = Z_KV // NSA_DH
    return pl.pallas_call(
        _compress_body,
        grid=(bn, 2, NSA_KV_GROUPS),
        in_specs=[pl.BlockSpec((t, NSA_DH), lambda b, x, g: (b, col0 + x * NSA_KV_GROUPS + g)),
                  pl.BlockSpec((None, CMP_BLOCK, NSA_DH, NSA_DH), lambda b, x, g: (x, 0, 0, 0)),
                  pl.BlockSpec((None, CMP_BLOCK, NSA_DH), lambda b, x, g: (x, 0, 0))],
        out_specs=pl.BlockSpec((None, None, None, ns, NSA_DH), lambda b, x, g: (b, x, g, 0, 0)),
        out_shape=jax.ShapeDtypeStruct((bn, 2, NSA_KV_GROUPS, ns, NSA_DH), bf16),
        compiler_params=_cparams(("parallel", "parallel", "parallel")),
        name="compress_prompt",
    )(z, cmp_w, cmp_pe)


def _nsa_prompt_body(q_ref, kc_ref, vc_ref, ks_ref, vs_ref, kw_ref, vw_ref, gate_ref, bc_ref, bt_ref,
                     cov_ref, exp_ref, o_ref, mb_ref, s_ref, mx_ref, m_ref, l_ref, acc_ref):
    g = pl.program_id(1)
    i = pl.program_id(2)
    qb = Q_BLOCK
    rows = NSA_HPG * qb
    qf = q_ref[...]
    q = jnp.concatenate([qf[:, h * NSA_DH:(h + 1) * NSA_DH] for h in range(NSA_HPG)], axis=0).astype(bf16)

    ncp = kc_ref.shape[0]
    n_done = (qb // CMP_STRIDE) * (i + 1)
    bias_c = pltpu.roll(bc_ref[...], n_done % ncp, axis=1)
    lane_c = lax.broadcasted_iota(jnp.int32, bias_c.shape, 1)
    bias_c = jnp.where(lane_c < n_done, bias_c, NEG_BIG)
    s = lax.dot_general(q, kc_ref[...], _NT, preferred_element_type=f32) * (NSA_DH ** -0.5) + bias_c
    m = jnp.max(s, -1, keepdims=True)
    m = jnp.where(m < 0.5 * NEG_BIG, 0.0, m)
    e = jnp.exp(s - m)
    p = e / jnp.maximum(jnp.sum(e, -1, keepdims=True), 1e-30)
    o_c = jnp.dot(p.astype(bf16), vc_ref[...], preferred_element_type=f32)

    pr = p.astype(bf16).astype(f32)
    psum = pr[0:qb]
    for h in range(1, NSA_HPG):
        psum = psum + pr[h * qb:(h + 1) * qb]
    imp_t = lax.dot_general(cov_ref[...], psum, _NT, preferred_element_type=f32,
                            precision=lax.Precision.HIGHEST)
    nsb = imp_t.shape[0]
    blk = lax.broadcasted_iota(jnp.int32, (nsb, qb), 0)
    qpos = i * qb + lax.broadcasted_iota(jnp.int32, (nsb, qb), 1)
    cur = qpos // SEL_BLOCK
    forced = (blk == 0) | (blk == cur) | (blk == cur - 1)
    valid = blk * SEL_BLOCK <= qpos
    score = jnp.where(valid, jnp.where(forced, FORCE_SCORE, imp_t), -1.0)
    sel_t = jnp.zeros((nsb, qb), f32)
    for _ in range(min(SEL_TOPK, nsb)):
        mx = jnp.max(score, axis=0, keepdims=True)
        first = jnp.min(jnp.where(score == mx, blk, nsb), axis=0, keepdims=True)
        pick = blk == first
        sel_t = jnp.where(pick, 1.0, sel_t)
        score = jnp.where(pick, -2.0, score)
    unsel = ((sel_t - 1.0) * (-NEG_BIG)).T.astype(bf16)
    mb_ref[...] = jnp.dot(unsel, exp_ref[...], preferred_element_type=f32)

    scale = NSA_DH ** -0.5

    mx_ref[...] = jnp.full(mx_ref.shape, NEG_BIG, f32)

    kw_ = NSA_KT * qb
    n_steps = (i + NSA_KT) // NSA_KT

    def score_step(jj, carry):
        r0 = pl.multiple_of(jj * kw_, kw_)
        k = ks_ref[pl.ds(r0, kw_), :].astype(bf16)
        biases = []
        for tt in range(NSA_KT):
            d = i - (jj * NSA_KT + tt)
            biases.append(bt_ref[jnp.where(d < 0, TILE_NONE, jnp.minimum(d, TILE_FAR))])
        mb = mb_ref[:, pl.ds(r0, kw_)]
        s = (lax.dot_general(q, k, _NT, preferred_element_type=f32) * scale
             + jnp.concatenate(biases, axis=1) + jnp.concatenate([mb] * NSA_HPG, axis=0))
        s_ref[:, pl.ds(r0, kw_)] = s
        mx = mx_ref[...]
        for tt in range(NSA_KT):
            mx = jnp.maximum(mx, s[:, tt * qb:(tt + 1) * qb])
        mx_ref[...] = mx
        return carry

    lax.fori_loop(0, n_steps, score_step, 0)
    m_ref[...] = jnp.broadcast_to(jnp.max(mx_ref[...], -1, keepdims=True), m_ref.shape)
    l_ref[...] = jnp.zeros(l_ref.shape, f32)
    acc_ref[...] = jnp.zeros(acc_ref.shape, f32)

    def pv_step(jj, carry):
        r0 = pl.multiple_of(jj * kw_, kw_)
        p = jnp.exp(s_ref[:, pl.ds(r0, kw_)] - jnp.concatenate([m_ref[...]] * NSA_KT, axis=1))
        v = vs_ref[pl.ds(r0, kw_), :].astype(bf16)
        lsum = l_ref[...]
        for tt in range(NSA_KT):
            lsum = lsum + p[:, tt * qb:(tt + 1) * qb]
        l_ref[...] = lsum
        acc_ref[...] = acc_ref[...] + jnp.dot(p.astype(bf16), v, preferred_element_type=f32)
        return carry

    lax.fori_loop(0, n_steps, pv_step, 0)
    o_s = acc_ref[...] / jnp.sum(l_ref[...], -1, keepdims=True)

    n_win = WINDOW // qb
    j0 = jnp.maximum(i - n_win, 0)
    s_w = []
    for tt in range(n_win + 1):
        d = i - (j0 + tt)
        r0 = pl.multiple_of((j0 + tt) * qb, qb)
        k = kw_ref[pl.ds(r0, qb), :].astype(bf16)
        tile = jnp.where(d < 0, TILE_NONE, jnp.where(d == n_win, TILE_WIN_EDGE, d))
        s_w.append(lax.dot_general(q, k, _NT, preferred_element_type=f32) * scale + bt_ref[tile])
    mw = s_w[0]
    for s in s_w[1:]:
        mw = jnp.maximum(mw, s)
    mw = jnp.max(mw, -1, keepdims=True)
    lw = jnp.zeros((rows, qb), f32)
    o_w = jnp.zeros((rows, NSA_DH), f32)
    for tt in range(n_win + 1):
        r0 = pl.multiple_of((j0 + tt) * qb, qb)
        p = jnp.exp(s_w[tt] - mw)
        lw = lw + p
        o_w = o_w + jnp.dot(p.astype(bf16), vw_ref[pl.ds(r0, qb), :].astype(bf16), preferred_element_type=f32)
    o_w = o_w / jnp.sum(lw, -1, keepdims=True)

    gt = jax.nn.sigmoid(gate_ref[...])
    outs = []
    for h in range(NSA_HPG):
        c = 3 * h
        sl = slice(h * qb, (h + 1) * qb)
        outs.append(gt[:, c:c + 1] * o_c[sl] + gt[:, c + 1:c + 2] * o_s[sl] + gt[:, c + 2:c + 3] * o_w[sl])
    o_ref[...] = jnp.concatenate(outs, axis=1).astype(o_ref.dtype)


def _nsa_bias_tables(rel_bias, t):
    qb = Q_BLOCK
    nq = t // qb
    tbl = rel_bias.astype(f32)
    r = jnp.arange(qb, dtype=jnp.int32)[:, None]
    c = jnp.arange(qb, dtype=jnp.int32)[None, :]
    dists = [delta * qb + r - c for delta in range(TILE_FAR + 1)]
    oks = [d_ >= 0 for d_ in dists]
    d_win = (WINDOW // qb) * qb + r - c
    dists += [d_win, d_win]
    oks += [(d_win >= 0) & (d_win < WINDOW), jnp.zeros((qb, qb), bool)]
    bt = jnp.where(jnp.stack(oks, 0)[None], _rel_bias_heads(tbl, jnp.stack(dists, 0)), NEG_BIG)
    bt = bt.reshape(NSA_KV_GROUPS, NSA_HPG, N_BIAS_TILES, qb, qb)
    bt = jnp.transpose(bt, (0, 2, 1, 3, 4)).reshape(NSA_KV_GROUPS, N_BIAS_TILES, NSA_HPG * qb, qb)
    ncp = t // CMP_STRIDE
    m_back = (ncp - 1) - jnp.arange(ncp, dtype=jnp.int32)[None, :]
    dist_c = r + CMP_STRIDE * m_back + CMP_STRIDE - (qb + CMP_BLOCK - 1)
    bc = jnp.where((dist_c >= 0)[None], _rel_bias_heads(tbl, dist_c), NEG_BIG)
    return bt, bc.reshape(NSA_KV_GROUPS, NSA_HPG * qb, ncp)


def nsa_prompt_attention(z, gates_g, kvc, rel_bias, bn, t):
    qb = Q_BLOCK
    nq = t // qb
    ncp = t // CMP_STRIDE
    nsb = t // SEL_BLOCK
    rows = NSA_HPG * qb
    bt, bc = _nsa_bias_tables(rel_bias, t)
    cover_t = jnp.concatenate([_cover_matrix(ncp - 1, nsb), jnp.zeros((1, nsb), f32)], 0).T
    expand = jnp.asarray(np.repeat(np.eye(nsb, dtype=np.float32), SEL_BLOCK, axis=1), bf16)
    kcol = Z_KV // NSA_DH
    kv_spec = lambda off: pl.BlockSpec((t, NSA_DH), lambda b, g, i, off=off: (b, kcol + off + g))
    return pl.pallas_call(
        _nsa_prompt_body,
        grid=(bn, NSA_KV_GROUPS, nq),
        in_specs=[pl.BlockSpec((qb, NSA_HPG * NSA_DH), lambda b, g, i: (b * nq + i, Z_NQ // (NSA_HPG * NSA_DH) + g)),
                  pl.BlockSpec((None, None, None, ncp, NSA_DH), lambda b, g, i: (b, 0, g, 0, 0)),
                  pl.BlockSpec((None, None, None, ncp, NSA_DH), lambda b, g, i: (b, 1, g, 0, 0)),
                  kv_spec(2 * NSA_KV_GROUPS), kv_spec(3 * NSA_KV_GROUPS),
                  kv_spec(4 * NSA_KV_GROUPS), kv_spec(5 * NSA_KV_GROUPS),
                  pl.BlockSpec((None, qb, 3 * NSA_HPG), lambda b, g, i: (g, b * nq + i, 0)),
                  pl.BlockSpec((None, rows, ncp), lambda b, g, i: (g, 0, 0)),
                  pl.BlockSpec((None, N_BIAS_TILES, rows, qb), lambda b, g, i: (g, 0, 0, 0)),
                  pl.BlockSpec((nsb, ncp), lambda b, g, i: (0, 0)),
                  pl.BlockSpec((nsb, t), lambda b, g, i: (0, 0))],
        out_specs=pl.BlockSpec((qb, NSA_HPG * NSA_DH), lambda b, g, i: (b * nq + i, g)),
        out_shape=jax.ShapeDtypeStruct((bn * t, NSA_Q_W), bf16),
        scratch_shapes=[pltpu.VMEM((qb, t), f32), pltpu.VMEM((rows, t), f32), pltpu.VMEM((rows, qb), f32),
                        pltpu.VMEM((rows, qb), f32), pltpu.VMEM((rows, qb), f32), pltpu.VMEM((rows, NSA_DH), f32)],
        compiler_params=_cparams(("parallel", "parallel", "arbitrary")),
        name="nsa_prompt_attention",
    )(z, kvc, kvc, z, z, z, z, gates_g, bc, bt, cover_t, expand)


GDN_RT = 1024
GDN_GROUP = 4
_HI = lax.Precision.HIGHEST


def _mm1(a, b):
    return jnp.dot(a.astype(bf16), b.astype(bf16), preferred_element_type=f32)


def _mm3(a, b):
    ah, bh = a.astype(bf16), b.astype(bf16)
    al = (a - ah.astype(f32)).astype(bf16)
    bl = (b - bh.astype(f32)).astype(bf16)
    return (jnp.dot(ah, bh, preferred_element_type=f32) + jnp.dot(ah, bl, preferred_element_type=f32)
            + jnp.dot(al, bh, preferred_element_type=f32))


def _gdn_prep_body(q_ref, k_ref, v_ref, qp_ref, kp_ref, vp_ref, gb_ref, cwq_ref, cwk_ref, cwv_ref,
                   w_ref, u_ref, aqk_ref, qg_ref, kg_ref, gl_ref, qs_ref, ks_ref, vs_ref):
    rt = q_ref.shape[0]
    cs = GDN_CHUNK
    first = pl.program_id(2) == 0
    pad = SUBLANES

    def conv_silu(x_ref, prev_ref, scr_ref, cw_ref):
        scr_ref[0:pad, :] = jnp.where(first, 0.0, prev_ref[...])
        scr_ref[pad:pad + rt, :] = x_ref[...]
        y = scr_ref[pl.ds(pad - (GDN_CONV - 1), rt), :] * cw_ref[0:1, :]
        for j in range(1, GDN_CONV):
            y = y + scr_ref[pl.ds(pad - (GDN_CONV - 1) + j, rt), :] * cw_ref[j:j + 1, :]
        return y * jax.nn.sigmoid(y)

    q = conv_silu(q_ref, qp_ref, qs_ref, cwq_ref)
    k = conv_silu(k_ref, kp_ref, ks_ref, cwk_ref)
    v = conv_silu(v_ref, vp_ref, vs_ref, cwv_ref)
    q = q * lax.rsqrt(jnp.sum(q * q, -1, keepdims=True) + EPS) * (GDN_DK ** -0.5)
    k = k * lax.rsqrt(jnp.sum(k * k, -1, keepdims=True) + EPS)

    gb = gb_ref[...]
    x = gb[:, 1:2] + gb[:, 2:3]
    softplus = jnp.maximum(x, 0.0) + jnp.log1p(jnp.exp(-jnp.abs(x)))
    g = jnp.broadcast_to(-jnp.exp(gb[:, 3:4]) * softplus, (rt, GDN_DK))
    beta = jnp.broadcast_to(jax.nn.sigmoid(gb[:, 0:1]), (rt, GDN_DK))

    row_in_chunk = lax.broadcasted_iota(jnp.int32, (rt, GDN_DK), 0) % cs
    gcum_all = g
    step = 1
    while step < cs:
        gcum_all = gcum_all + jnp.where(row_in_chunk >= step, pltpu.roll(gcum_all, step, axis=0), 0.0)
        step *= 2

    gr = GDN_GROUP * cs
    ri = lax.broadcasted_iota(jnp.int32, (gr, gr), 0)
    ci = lax.broadcasted_iota(jnp.int32, (gr, gr), 1)
    same_chunk = (ri // cs) == (ci // cs)
    incl = same_chunk & (ri >= ci)
    strict = same_chunk & (ri > ci)
    eye = (ri == ci).astype(f32)
    gls = []
    for grp in range(rt // gr):
        sl = slice(grp * gr, (grp + 1) * gr)
        qc, kc, vc, bc_, gcum = q[sl], k[sl], v[sl], beta[sl], gcum_all[sl]
        g_col = jnp.concatenate([gcum] * (gr // GDN_DK), axis=1)
        g_row = jnp.broadcast_to(gcum.T[0:1, :], (gr, gr))
        dec = jnp.exp(jnp.where(incl, g_col - g_row, 0.0))
        eg = jnp.exp(gcum)
        lasts = [gcum[(c + 1) * cs - 1:(c + 1) * cs, :] for c in range(GDN_GROUP)]
        g_last = jnp.concatenate([jnp.broadcast_to(r, (cs, GDN_DK)) for r in lasts], axis=0)
        kb = kc * bc_
        kbf = kc.astype(bf16)
        lmat = lax.dot_general(kb.astype(bf16), kbf, _NT, preferred_element_type=f32) * jnp.where(strict, dec, 0.0)
        aqk = lax.dot_general(qc.astype(bf16), kbf, _NT, preferred_element_type=f32) * jnp.where(incl, dec, 0.0)
        inv = eye - lmat
        pw = lmat
        for _ in range(int(math.log2(cs)) - 1):
            pw = _mm1(pw, pw)
            inv = inv + _mm1(inv, pw)
        for _ in range(2):
            inv = inv + _mm1(inv, eye - inv - _mm3(lmat, inv))
        rhs = jnp.concatenate([kb * eg, vc * bc_], axis=1)
        wu = _mm3(inv, rhs)
        w_ref[sl, :] = wu[:, :GDN_DK].astype(w_ref.dtype)
        u_ref[sl, :] = wu[:, GDN_DK:]
        for c in range(GDN_GROUP):
            blk = slice(c * cs, (c + 1) * cs)
            aqk_ref[grp * gr + c * cs:grp * gr + (c + 1) * cs, :] = aqk[blk, blk].astype(aqk_ref.dtype)
        qg_ref[sl, :] = (qc * eg).astype(qg_ref.dtype)
        kg_ref[sl, :] = (kc * jnp.exp(g_last - gcum)).astype(kg_ref.dtype)
        gls.extend(jnp.exp(r) for r in lasts)
    gl_ref[...] = jnp.concatenate(gls, axis=0)


def _gdn_scan_body(w_ref, u_ref, aqk_ref, qg_ref, kg_ref, gl_ref, za_ref, gn_ref, o_ref, sfin_ref, s_ref):
    c = pl.program_id(1)
    nc = pl.num_programs(1)
    n_gl = gl_ref.shape[1]

    @pl.when(c == 0)
    def _():
        s_ref[...] = jnp.zeros(s_ref.shape, f32)

    for h in range(GDN_HEADS):
        s = s_ref[h]
        sb = s.astype(bf16)
        v_new = u_ref[h] - jnp.dot(w_ref[h], sb, preferred_element_type=f32)
        vb = v_new.astype(bf16)
        o = (jnp.dot(qg_ref[h], sb, preferred_element_type=f32)
             + jnp.dot(aqk_ref[h], vb, preferred_element_type=f32))
        gl = gl_ref[h, pl.ds(c % n_gl, 1), :]
        s_ref[h] = gl * s + lax.dot_general(kg_ref[h], vb, (((0,), (0,)), ((), ())), preferred_element_type=f32)
        on = o * lax.rsqrt(jnp.mean(o * o, -1, keepdims=True) + EPS) * gn_ref[...]
        za = za_ref[:, h * GDN_DV:(h + 1) * GDN_DV]
        o_ref[:, h * GDN_DV:(h + 1) * GDN_DV] = (on * (za * jax.nn.sigmoid(za))).astype(o_ref.dtype)

    @pl.when(c == nc - 1)
    def _():
        sfin_ref[...] = s_ref[...]


def gdn_prompt(z, gb, conv_w, gdn_norm, bn, t):
    rt, cs = GDN_RT, GDN_CHUNK
    n_rt = t // rt
    m = bn * t
    hcol = GDN_DK // LANES
    qkv = lambda part: pl.BlockSpec((rt, GDN_DK), lambda h, b, r, part=part: (b * n_rt + r, part * GDN_HEADS + h))
    prev = lambda part: pl.BlockSpec(
        (SUBLANES, GDN_DK),
        lambda h, b, r, part=part: (jnp.maximum((b * n_rt + r) * (rt // SUBLANES) - 1, 0), part * GDN_HEADS + h))
    cw = lambda part: pl.BlockSpec((GDN_CONV, GDN_DK), lambda h, b, r, part=part: (0, part * GDN_HEADS + h))
    per_row = lambda width: pl.BlockSpec((None, rt, width), lambda h, b, r: (h, b * n_rt + r, 0))
    w_c, u_c, aqk, qg, kg, gl = pl.pallas_call(
        _gdn_prep_body,
        grid=(GDN_HEADS, bn, n_rt),
        in_specs=[qkv(0), qkv(1), qkv(2), prev(0), prev(1), prev(2), per_row(4), cw(0), cw(1), cw(2)],
        out_specs=[per_row(GDN_DK), per_row(GDN_DV), per_row(cs), per_row(GDN_DK), per_row(GDN_DK),
                   pl.BlockSpec((None, rt // cs, GDN_DK), lambda h, b, r: (h, b * n_rt + r, 0))],
        out_shape=[jax.ShapeDtypeStruct((GDN_HEADS, m, GDN_DK), bf16),
                   jax.ShapeDtypeStruct((GDN_HEADS, m, GDN_DV), f32),
                   jax.ShapeDtypeStruct((GDN_HEADS, m, cs), bf16),
                   jax.ShapeDtypeStruct((GDN_HEADS, m, GDN_DK), bf16),
                   jax.ShapeDtypeStruct((GDN_HEADS, m, GDN_DK), bf16),
                   jax.ShapeDtypeStruct((GDN_HEADS, m // cs, GDN_DK), f32)],
        scratch_shapes=[pltpu.VMEM((rt + SUBLANES, GDN_DK), f32)] * 3,
        compiler_params=_cparams(("parallel", "parallel", "parallel")),
        name="gdn_prep",
    )(z, z, z, z, z, z, gb, conv_w, conv_w, conv_w)

    nc = t // cs
    n_gl = rt // cs
    heads = lambda width: pl.BlockSpec((GDN_HEADS, cs, width), lambda b, c: (0, b * nc + c, 0))
    o, s_fin = pl.pallas_call(
        _gdn_scan_body,
        grid=(bn, nc),
        in_specs=[heads(GDN_DK), heads(GDN_DV), heads(cs), heads(GDN_DK), heads(GDN_DK),
                  pl.BlockSpec((GDN_HEADS, n_gl, GDN_DK), lambda b, c: (0, (b * nc + c) // n_gl, 0)),
                  pl.BlockSpec((cs, GDN_V_W), lambda b, c: (b * nc + c, Z_ZG // GDN_V_W)),
                  pl.BlockSpec((1, GDN_DV), lambda b, c: (0, 0))],
        out_specs=[pl.BlockSpec((cs, GDN_V_W), lambda b, c: (b * nc + c, 0)),
                   pl.BlockSpec((None, GDN_HEADS, GDN_DK, GDN_DV), lambda b, c: (b, 0, 0, 0))],
        out_shape=[jax.ShapeDtypeStruct((m, GDN_V_W), bf16),
                   jax.ShapeDtypeStruct((bn, GDN_HEADS, GDN_DK, GDN_DV), f32)],
        scratch_shapes=[pltpu.VMEM((GDN_HEADS, GDN_DK, GDN_DV), f32)],
        compiler_params=_cparams(("parallel", "arbitrary")),
        name="gdn_scan",
    )(w_c, u_c, aqk, qg, kg, gl, z, gdn_norm.reshape(1, GDN_DV))
    return o, s_fin


GDN_SB = 8


def _gdn_sample_body(x_ref, za_ref, gb_ref, sc_ref, s0_ref, cw_ref, gn_ref, o_ref, s_ref):
    sb = GDN_SB
    x3 = sc_ref[...]
    y = x_ref[...] * cw_ref[GDN_CONV - 1:GDN_CONV, :]
    for j in range(GDN_CONV - 1):
        y = y + x3[:, j, :] * cw_ref[j:j + 1, :]
    xc = y * jax.nn.sigmoid(y)
    for h in range(GDN_HEADS):
        q = xc[:, h * GDN_DK:(h + 1) * GDN_DK]
        k = xc[:, GDN_QK_W + h * GDN_DK:GDN_QK_W + (h + 1) * GDN_DK]
        v = xc[:, 2 * GDN_QK_W + h * GDN_DV:2 * GDN_QK_W + (h + 1) * GDN_DV]
        q = q * lax.rsqrt(jnp.sum(q * q, -1, keepdims=True) + EPS) * (GDN_DK ** -0.5)
        k = k * lax.rsqrt(jnp.sum(k * k, -1, keepdims=True) + EPS)
        gb = gb_ref[h]
        x = gb[:, 1:2] + gb[:, 2:3]
        softplus = jnp.maximum(x, 0.0) + jnp.log1p(jnp.exp(-jnp.abs(x)))
        eg = jnp.broadcast_to(jnp.exp(-jnp.exp(gb[:, 3:4]) * softplus), (sb, GDN_DV))
        beta = jnp.broadcast_to(jax.nn.sigmoid(gb[:, 0:1]), (sb, GDN_DV))
        qk = jnp.broadcast_to(jnp.sum(q * k, -1, keepdims=True), (sb, GDN_DV))
        q_t = q.T
        k_t = k.T
        o_rows = []
        for i in range(sb):
            s = s0_ref[i, h]
            kcol = k_t[:, i:i + 1]
            k_s = jnp.sum(kcol * s, axis=0, keepdims=True)
            q_s = jnp.sum(q_t[:, i:i + 1] * s, axis=0, keepdims=True)
            eg_i = eg[i:i + 1]
            v_new = beta[i:i + 1] * (v[i:i + 1] - eg_i * k_s)
            o_rows.append(eg_i * q_s + qk[i:i + 1] * v_new)
            s_ref[i, h] = eg_i * s + kcol * v_new
        o = jnp.concatenate(o_rows, axis=0)
        on = o * lax.rsqrt(jnp.mean(o * o, -1, keepdims=True) + EPS) * gn_ref[...]
        za = za_ref[:, h * GDN_DV:(h + 1) * GDN_DV]
        o_ref[:, h * GDN_DV:(h + 1) * GDN_DV] = on * (za * jax.nn.sigmoid(za))


def gdn_sample(z, gb, state_conv, state_gdn, conv_w, gdn_norm, row0, bs):
    sb = GDN_SB
    r0 = row0 // sb
    return pl.pallas_call(
        _gdn_sample_body,
        grid=(bs // sb,),
        in_specs=[pl.BlockSpec((sb, CONV_CH), lambda i: (r0 + i, 0)),
                  pl.BlockSpec((sb, GDN_V_W), lambda i: (r0 + i, Z_ZG // GDN_V_W)),
                  pl.BlockSpec((GDN_HEADS, sb, 4), lambda i: (0, r0 + i, 0)),
                  pl.BlockSpec((sb, GDN_CONV - 1, CONV_CH), lambda i: (i, 0, 0)),
                  pl.BlockSpec((sb, GDN_HEADS, GDN_DK, GDN_DV), lambda i: (i, 0, 0, 0)),
                  pl.BlockSpec((GDN_CONV, CONV_CH), lambda i: (0, 0)),
                  pl.BlockSpec((1, GDN_DV), lambda i: (0, 0))],
        out_specs=[pl.BlockSpec((sb, GDN_V_W), lambda i: (i, 0)),
                   pl.BlockSpec((sb, GDN_HEADS, GDN_DK, GDN_DV), lambda i: (i, 0, 0, 0))],
        out_shape=[jax.ShapeDtypeStruct((bs, GDN_V_W), f32),
                   jax.ShapeDtypeStruct((bs, GDN_HEADS, GDN_DK, GDN_DV), f32)],
        compiler_params=_cparams(("parallel",)),
        name="gdn_sample",
    )(z, z, gb, state_conv, state_gdn, conv_w, gdn_norm.reshape(1, GDN_DV))


SMP_PAGES = 16
_TN = (((0,), (0,)), ((), ()))
_KV_ROW = 2 * NSA_KV_GROUPS * NSA_DH


def _page_specs(shape):
    return [pl.BlockSpec((None,) + shape, lambda b, p, pt, kk=kk: (pt[b, p * SMP_PAGES + kk], 0, 0))
            for kk in range(SMP_PAGES)]


def _compress_sample_body(pt_ref, *refs):
    pages = refs[:SMP_PAGES]
    w_ref, o_ref = refs[SMP_PAGES:]
    per = PAGE_SIZE // CMP_STRIDE
    slabs = 2 * NSA_KV_GROUPS
    for x in range(2):
        rows = []
        for kk in range(SMP_PAGES):
            for g in range(NSA_KV_GROUPS):
                slab = x * NSA_KV_GROUPS + g
                parts = [pages[kk][pl.ds(s * slabs + slab, per, stride=CMP_STRIDE * slabs), :]
                         for s in range(CMP_STRIDE)]
                rows.append(jnp.concatenate(parts, axis=1))
        lhs = jnp.concatenate(rows, axis=0).astype(bf16)
        prod = jnp.dot(lhs, w_ref[x], preferred_element_type=f32)
        for kk in range(SMP_PAGES):
            for g in range(NSA_KV_GROUPS):
                r = (kk * NSA_KV_GROUPS + g) * per
                o_ref[x, g, kk * per:(kk + 1) * per, :] = prod[r:r + per]


def compress_sample(page_table, cache, w2):
    bs, n_pages = page_table.shape
    per = PAGE_SIZE // CMP_STRIDE
    n_sub = n_pages * per
    grid_spec = pltpu.PrefetchScalarGridSpec(
        num_scalar_prefetch=1,
        grid=(bs, n_pages // SMP_PAGES),
        in_specs=(_page_specs((PAGE_SIZE * 2 * NSA_KV_GROUPS, NSA_DH))
                  + [pl.BlockSpec((2, CMP_STRIDE * NSA_DH, 2 * NSA_DH), lambda b, p, pt: (0, 0, 0))]),
        out_specs=pl.BlockSpec((None, 2, NSA_KV_GROUPS, SMP_PAGES * per, 2 * NSA_DH), lambda b, p, pt: (b, 0, 0, p, 0)),
    )
    return pl.pallas_call(
        _compress_sample_body,
        grid_spec=grid_spec,
        out_shape=jax.ShapeDtypeStruct((bs, 2, NSA_KV_GROUPS, n_sub, 2 * NSA_DH), f32),
        compiler_params=_cparams(("parallel", "arbitrary")),
        name="compress_sample",
    )(page_table, *([cache] * SMP_PAGES), w2)


def _nsa_sample_select_body(qt_ref, pm_ref, w_ref, pe_ref, bc_ref, cov_ref, win_ref, bw_ref, b0_ref, kn_ref, vn_ref,
                            oc_ref, ow_ref, mask_ref, sn_ref, *, past):
    scale = NSA_DH ** -0.5
    qt = qt_ref[...]
    n_sub = pm_ref.shape[2]
    lane1 = lax.broadcasted_iota(jnp.int32, (1, LANES), 1)
    lane_n = lax.broadcasted_iota(jnp.int32, (n_sub, LANES), 1)
    pe_term = []
    for x in range(2):
        pr = jnp.dot(pe_ref[x], w_ref[x], preferred_element_type=f32)
        pe_term.append(pr[0:1, :NSA_DH] + pr[1:2, NSA_DH:])
    oc_t = jnp.zeros((NSA_DH, LANES), f32)
    ow_t = jnp.zeros((NSA_DH, LANES), f32)
    psum_all = jnp.zeros((n_sub, LANES), f32)
    snew = jnp.zeros((SUBLANES, LANES), f32)
    for g in range(NSA_KV_GROUPS):
        in_g = (lane1 // NSA_HPG) == g
        kvc = []
        for x in range(2):
            pm = pm_ref[x, g]
            kvc.append((pm[:, :NSA_DH] + pltpu.roll(pm[:, NSA_DH:], n_sub - 1, axis=0) + pe_term[x]).astype(bf16))
        s = jnp.dot(kvc[0], qt, preferred_element_type=f32) * scale + bc_ref[...]
        e = jnp.exp(s - jnp.max(s, axis=0, keepdims=True))
        p = jnp.where(in_g, e / jnp.sum(e, axis=0, keepdims=True), 0.0)
        oc_t = oc_t + lax.dot_general(kvc[1], p.astype(bf16), _TN, preferred_element_type=f32)
        psum = jnp.sum(p.astype(bf16).astype(f32), axis=1, keepdims=True)
        psum_all = jnp.where((lane_n // NSA_HPG) == g, psum, psum_all)
        slabs = 2 * NSA_KV_GROUPS
        w_buf = win_ref.shape[0] // slabs
        kw = win_ref[pl.ds(g, w_buf, stride=slabs), :].astype(bf16)
        vw = win_ref[pl.ds(NSA_KV_GROUPS + g, w_buf, stride=slabs), :].astype(bf16)
        sw = jnp.dot(kw, qt, preferred_element_type=f32) * scale + bw_ref[...]
        sn = jnp.dot(kn_ref[g], qt, preferred_element_type=f32) * scale + b0_ref[...]
        mw = jnp.maximum(jnp.max(sw, axis=0, keepdims=True), sn[1:2])
        ew = jnp.exp(sw - mw)
        en = jnp.exp(sn[1:2] - mw)
        lw = jnp.sum(ew, axis=0, keepdims=True) + en
        pw = jnp.where(in_g, ew / lw, 0.0)
        row = lax.broadcasted_iota(jnp.int32, (SUBLANES, LANES), 0)
        pn = jnp.where((row == 1) & in_g, en / lw, 0.0)
        ow_t = (ow_t + lax.dot_general(vw, pw.astype(bf16), _TN, preferred_element_type=f32)
                + lax.dot_general(vn_ref[g], pn.astype(bf16), _TN, preferred_element_type=f32))
        snew = jnp.where(in_g, sn, snew)
    oc_ref[...] = oc_t
    ow_ref[...] = ow_t
    sn_ref[...] = snew

    imp_t = jnp.dot(cov_ref[...], psum_all, preferred_element_type=f32, precision=_HI)
    nsb = imp_t.shape[0]
    n_blocks = past // SEL_BLOCK + 1
    blk = lax.broadcasted_iota(jnp.int32, (nsb, LANES), 0)
    cur = past // SEL_BLOCK
    forced = (blk == 0) | (blk == cur) | (blk == cur - 1)
    valid = (blk * SEL_BLOCK <= past) & (blk < n_blocks)
    score = jnp.where(valid, jnp.where(forced, FORCE_SCORE, imp_t), -1.0)
    sel_t = jnp.zeros((nsb, LANES), f32)
    for _ in range(min(SEL_TOPK, n_blocks)):
        mx = jnp.max(score, axis=0, keepdims=True)
        first = jnp.min(jnp.where(score == mx, blk, nsb), axis=0, keepdims=True)
        pick = blk == first
        sel_t = jnp.where(pick, 1.0, sel_t)
        score = jnp.where(pick, -2.0, score)
    mask_ref[...] = (sel_t - 1.0) * (-NEG_BIG)


def _nsa_sample_attend_body(pt_ref, *refs):
    pages = refs[:SMP_PAGES]
    (qt_ref, bias_ref, mask_ref, sn_ref, vn_ref, oc_ref, ow_ref, gate_ref, o_ref, m_ref, l_ref, acc_ref) = refs[SMP_PAGES:]
    p = pl.program_id(1)
    grp = lax.broadcasted_iota(jnp.int32, (1, LANES), 1) // NSA_HPG

    @pl.when(p == 0)
    def _():
        m_ref[...] = sn_ref[0:1, :]
        l_ref[...] = jnp.ones(l_ref.shape, f32)
        acc_ref[...] = vn_ref[...]

    qt = qt_ref[...]
    slabs = 2 * NSA_KV_GROUPS
    s = None
    for g in range(NSA_KV_GROUPS):
        kg = jnp.concatenate([pages[kk][pl.ds(g, PAGE_SIZE, stride=slabs), :] for kk in range(SMP_PAGES)], axis=0)
        sg = jnp.dot(kg.astype(bf16), qt, preferred_element_type=f32)
        s = sg if g == 0 else jnp.where(grp == g, sg, s)
    bpp = PAGE_SIZE // SEL_BLOCK
    mrows = []
    for kk in range(SMP_PAGES):
        for half in range(bpp):
            mrow = mask_ref[pl.ds((p * SMP_PAGES + kk) * bpp + half, 1), :]
            mrows.append(jnp.broadcast_to(mrow, (SEL_BLOCK, LANES)))
    s = s * (NSA_DH ** -0.5) + bias_ref[...] + jnp.concatenate(mrows, axis=0)
    m_old = m_ref[...]
    m_new = jnp.maximum(m_old, jnp.max(s, axis=0, keepdims=True))
    alpha = jnp.exp(m_old - m_new)
    pe = jnp.exp(s - m_new)
    l_ref[...] = alpha * l_ref[...] + jnp.sum(pe, axis=0, keepdims=True)
    pb = pe.astype(bf16)
    upd = jnp.zeros(acc_ref.shape, f32)
    for g in range(NSA_KV_GROUPS):
        vg = jnp.concatenate([pages[kk][pl.ds(NSA_KV_GROUPS + g, PAGE_SIZE, stride=slabs), :]
                              for kk in range(SMP_PAGES)], axis=0)
        upd = upd + lax.dot_general(vg.astype(bf16), jnp.where(grp == g, pb, jnp.zeros_like(pb)), _TN,
                                    preferred_element_type=f32)
    acc_ref[...] = alpha * acc_ref[...] + upd
    m_ref[...] = m_new

    @pl.when(p == pl.num_programs(1) - 1)
    def _():
        gt = jax.nn.sigmoid(gate_ref[...])
        o_s = acc_ref[...] / l_ref[...]
        o_ref[...] = gt[0:1] * oc_ref[...] + gt[1:2] * o_s + gt[2:3] * ow_ref[...]


def nsa_sample(z, zs, row0, cache_c, cache_s, page_table, win_buf, cmp_pe, cmp_w, rel_bias):
    bs, n_pages = page_table.shape
    past = n_pages * PAGE_SIZE
    n_sub = past // CMP_STRIDE
    n_blocks = past // SEL_BLOCK + 1
    nsb = -(-n_blocks // SUBLANES) * SUBLANES
    w_buf = win_buf.shape[1]
    tbl = rel_bias.astype(f32)
    lane_pad = lambda a: jnp.pad(a, [(0, 0)] * (a.ndim - 1) + [(0, LANES - a.shape[-1])])

    tbl_lanes = lane_pad(tbl)

    def bias_rows(dist, ok):
        return jnp.where(ok[:, None], _rel_bias_of(tbl_lanes, dist), NEG_BIG)

    dist_c = past - (jnp.arange(n_sub, dtype=jnp.int32) * CMP_STRIDE + (CMP_BLOCK - 1))
    bias_c = bias_rows(dist_c, dist_c >= 0)
    w_pos = past - w_buf + jnp.arange(w_buf, dtype=jnp.int32)
    dist_w = past - w_pos
    bias_w = bias_rows(dist_w, (dist_w < WINDOW) & (w_pos >= 0))
    bias_0 = jnp.broadcast_to(_rel_bias_of(tbl_lanes, jnp.zeros((1,), jnp.int32)), (SUBLANES, LANES))
    dist_s = past - jnp.arange(past, dtype=jnp.int32)
    bias_s = bias_rows(dist_s, dist_s >= 0)
    cover_t = jnp.pad(_cover_matrix(n_sub - 1, n_blocks), [(0, 1), (0, nsb - n_blocks)]).T

    w2 = cmp_w.reshape(2, CMP_BLOCK // CMP_STRIDE, CMP_STRIDE * NSA_DH, NSA_DH)
    w2 = jnp.transpose(w2, (0, 2, 1, 3)).reshape(2, CMP_STRIDE * NSA_DH, 2 * NSA_DH).astype(bf16)
    pe2 = jnp.pad(cmp_pe.reshape(2, CMP_BLOCK // CMP_STRIDE, CMP_STRIDE * NSA_DH), [(0, 0), (0, SUBLANES - 2), (0, 0)])
    pe2 = pe2.astype(bf16)

    zrow = z[row0:row0 + bs]
    q = zrow[:, Z_NQ:Z_KV].reshape(bs, NSA_HEADS, NSA_DH)
    qt = lane_pad(jnp.transpose(q, (0, 2, 1))).astype(bf16)
    kv = zrow[:, Z_KV:Z_GA].reshape(bs, 3, 2, NSA_KV_GROUPS, NSA_DH)
    zero = jnp.zeros((bs, NSA_KV_GROUPS, NSA_DH), f32)
    pad_rows = lambda r0, r1: jnp.pad(jnp.stack([r0, r1], 2), [(0, 0), (0, 0), (0, SUBLANES - 2), (0, 0)]).astype(bf16)
    k_new = pad_rows(kv[:, 1, 0], kv[:, 2, 0])
    v_new_w = pad_rows(zero, kv[:, 2, 1])
    v_new_s = lane_pad(jnp.transpose(jnp.repeat(kv[:, 1, 1], NSA_HPG, axis=1), (0, 2, 1)))
    gates = zs[row0:row0 + bs, 2 * GDN_HEADS:2 * GDN_HEADS + 3 * NSA_HEADS].reshape(bs, NSA_HEADS, 3)
    gates = jnp.pad(lane_pad(jnp.transpose(gates, (0, 2, 1))), [(0, 0), (0, SUBLANES - 3), (0, 0)])

    pm = compress_sample(page_table, cache_c.reshape(-1, PAGE_SIZE * 2 * NSA_KV_GROUPS, NSA_DH), w2)
    per_seq = lambda *shape: pl.BlockSpec((None,) + shape, lambda b: (b,) + (0,) * len(shape))
    const = lambda *shape: pl.BlockSpec(shape, lambda b: (0,) * len(shape))
    oc_t, ow_t, mask, s_new = pl.pallas_call(
        functools.partial(_nsa_sample_select_body, past=past),
        grid=(bs,),
        in_specs=[per_seq(NSA_DH, LANES), per_seq(2, NSA_KV_GROUPS, n_sub, 2 * NSA_DH),
                  const(2, CMP_STRIDE * NSA_DH, 2 * NSA_DH), const(2, SUBLANES, CMP_STRIDE * NSA_DH),
                  const(n_sub, LANES), const(nsb, n_sub), per_seq(w_buf * 2 * NSA_KV_GROUPS, NSA_DH),
                  const(w_buf, LANES),
                  const(SUBLANES, LANES), per_seq(NSA_KV_GROUPS, SUBLANES, NSA_DH),
                  per_seq(NSA_KV_GROUPS, SUBLANES, NSA_DH)],
        out_specs=[per_seq(NSA_DH, LANES), per_seq(NSA_DH, LANES), per_seq(nsb, LANES), per_seq(SUBLANES, LANES)],
        out_shape=[jax.ShapeDtypeStruct((bs, NSA_DH, LANES), f32), jax.ShapeDtypeStruct((bs, NSA_DH, LANES), f32),
                   jax.ShapeDtypeStruct((bs, nsb, LANES), f32), jax.ShapeDtypeStruct((bs, SUBLANES, LANES), f32)],
        compiler_params=_cparams(("parallel",)),
        name="nsa_sample_select",
    )(qt, pm, w2, pe2, bias_c, cover_t, win_buf.reshape(bs, w_buf * 2 * NSA_KV_GROUPS, NSA_DH), bias_w, bias_0,
      k_new, v_new_w)

    seq = lambda *shape: pl.BlockSpec((None,) + shape, lambda b, p, pt: (b,) + (0,) * len(shape))
    grid_spec = pltpu.PrefetchScalarGridSpec(
        num_scalar_prefetch=1,
        grid=(bs, n_pages // SMP_PAGES),
        in_specs=_page_specs((PAGE_SIZE * 2 * NSA_KV_GROUPS, NSA_DH)) + [seq(NSA_DH, LANES),
                                  pl.BlockSpec((SMP_PAGES * PAGE_SIZE, LANES), lambda b, p, pt: (p, 0)),
                                  seq(nsb, LANES), seq(SUBLANES, LANES), seq(NSA_DH, LANES), seq(NSA_DH, LANES),
                                  seq(NSA_DH, LANES), seq(SUBLANES, LANES)],
        out_specs=seq(NSA_DH, LANES),
        scratch_shapes=[pltpu.VMEM((1, LANES), f32), pltpu.VMEM((1, LANES), f32), pltpu.VMEM((NSA_DH, LANES), f32)],
    )
    o_t = pl.pallas_call(
        _nsa_sample_attend_body,
        grid_spec=grid_spec,
        out_shape=jax.ShapeDtypeStruct((bs, NSA_DH, LANES), f32),
        compiler_params=_cparams(("parallel", "arbitrary")),
        name="nsa_sample_attend",
    )(page_table, *([cache_s.reshape(-1, PAGE_SIZE * 2 * NSA_KV_GROUPS, NSA_DH)] * SMP_PAGES), qt, bias_s, mask, s_new, v_new_s,
      oc_t, ow_t, gates)
    return jnp.transpose(o_t[:, :, :NSA_HEADS], (0, 2, 1)).reshape(bs, NSA_Q_W)


def _rel_bucket(dist):
    n = jnp.maximum(dist, 0)
    max_exact = REL_BUCKETS // 2
    nf = jnp.maximum(n, 1).astype(f32)
    large = max_exact + (jnp.log(nf / max_exact) / math.log(REL_MAX_DIST / max_exact)
                         * (REL_BUCKETS - max_exact)).astype(jnp.int32)
    large = jnp.minimum(large, REL_BUCKETS - 1)
    return jnp.where(n < max_exact, n, large)


def _rel_bias_of(tbl, dist):
    one_hot = jax.nn.one_hot(_rel_bucket(dist), REL_BUCKETS, dtype=f32)
    return jnp.dot(one_hot, tbl, precision=lax.Precision.HIGHEST)


def _rel_bias_heads(tbl, dist):
    buckets = jnp.arange(REL_BUCKETS, dtype=jnp.int32).reshape((REL_BUCKETS,) + (1,) * dist.ndim)
    one_hot = (_rel_bucket(dist)[None] == buckets).astype(f32)
    return jnp.tensordot(tbl.T, one_hot, axes=1, precision=lax.Precision.HIGHEST)


def _cover_matrix(nc, ns):
    cs = np.arange(nc) * CMP_STRIDE
    ss = np.arange(ns) * SEL_BLOCK
    inter = np.minimum(cs[:, None] + CMP_BLOCK, ss[None, :] + SEL_BLOCK) - np.maximum(cs[:, None], ss[None, :])
    return jnp.asarray(np.clip(inter, 0, None) / CMP_BLOCK, dtype=f32)


def kernel(x_prompt, x_sample, p_prompt, p_sample, cache_cmp_kv, cache_slc_kv, page_table, state_win_kv, state_gdn, state_conv, g_mix, w_in, gdn_conv_w, gdn_dt_bias, gdn_a_log, gdn_norm, cmp_pe, cmp_w, rel_bias, w_proj_a, w_proj_b, w_out, g_ffn, w_router_group, b_router_group, w_router_expert, b_router_expert, w_gate, w_up, w_down, g_ple, w_ple_gate, w_ple_proj, g_final):
    bp, tp, d = x_prompt.shape
    bs, ts, _ = x_sample.shape
    n_p, n_s = bp * tp, bs * ts
    n_real = n_p + n_s
    mp = -(-n_real // ROW_ALIGN) * ROW_ALIGN
    pad = mp - n_real

    h = jnp.concatenate([x_prompt.reshape(n_p, d), x_sample.reshape(n_s, d), jnp.zeros((pad, d), f32)], 0)
    ple = jnp.concatenate([p_prompt[0].reshape(n_p, -1), p_sample[0].reshape(n_s, -1),
                           jnp.zeros((pad, p_prompt.shape[-1]), f32)], 0).astype(bf16)

    wt = jnp.swapaxes(w_in[0], 0, 1)
    o_beta = 4 * GDN_QK_W
    o_nq = o_beta + 2 * GDN_HEADS
    o_gate = o_nq + NSA_Q_W + 6 * NSA_KV_W
    o_ga = o_gate + 3 * NSA_HEADS
    n_small = 2 * GDN_HEADS + 3 * NSA_HEADS

    a = rmsnorm_rows(h, g_mix[0], bf16)
    z = proj_matmul(a, wt, ((0, o_beta), (o_nq, o_gate - o_nq), (o_ga, Z_COLS - Z_GA)))
    zs = proj_small(a, wt, o_beta, 2 * GDN_HEADS, o_gate, 3 * NSA_HEADS)

    def rows(x, lo, hi, which):
        if which == "p":
            return x[:n_p, lo:hi].reshape(bp, tp, hi - lo)
        return x[n_p:n_real, lo:hi].reshape(bs, ts, hi - lo)

    assert ts == 1 and tp % GDN_RT == 0 and n_p % GDN_SB == 0 and bs % GDN_SB == 0
    assert n_p % TM_ROWS == 0 and n_s <= TM_ROWS <= mp - n_p and tp % (NSA_KT * Q_BLOCK) == 0

    gb = jnp.stack([zs[:, 0:GDN_HEADS], zs[:, GDN_HEADS:2 * GDN_HEADS],
                    jnp.broadcast_to(gdn_dt_bias[0], (mp, GDN_HEADS)),
                    jnp.broadcast_to(gdn_a_log[0], (mp, GDN_HEADS))], -1)
    gb = jnp.transpose(gb, (1, 0, 2))
    gates_g = jnp.transpose(zs[:, 2 * GDN_HEADS:n_small].reshape(mp, NSA_KV_GROUPS, 3 * NSA_HPG), (1, 0, 2))

    outs = {}
    kv_shape = (2, NSA_KV_GROUPS, NSA_DH)
    kv_c_p = z[:n_p, Z_KV:Z_KV + _KV_ROW].reshape((bp, tp) + kv_shape)
    kv_s_p = z[:n_p, Z_KV + _KV_ROW:Z_KV + 2 * _KV_ROW].reshape((bp, tp) + kv_shape)
    w_keep = min(WINDOW, tp)
    win_p = jnp.stack([z[(b + 1) * tp - w_keep:(b + 1) * tp, Z_KV + 2 * _KV_ROW:Z_GA] for b in range(bp)], 0)
    win_p = win_p.reshape((bp, w_keep) + kv_shape)
    kvc = compress_prompt(z, cmp_w[0], cmp_pe[0], bp, tp)
    o_b_p = nsa_prompt_attention(z, gates_g, kvc, rel_bias, bp, tp)
    o_a_p, s_p = gdn_prompt(z, gb, gdn_conv_w[0], gdn_norm[0], bp, tp)
    conv_p = jnp.stack([z[(b + 1) * tp - (GDN_CONV - 1):(b + 1) * tp, :CONV_CH] for b in range(bp)], 0)
    outs["p"] = (kv_c_p, kv_s_p, win_p, s_p, conv_p)
    kv = rows(z, Z_KV, Z_GA, "s").reshape(bs, ts, 3, 2, NSA_KV_GROUPS, NSA_DH)
    o_b_s = nsa_sample(z, zs, n_p, cache_cmp_kv[0], cache_slc_kv[0], page_table, state_win_kv[0],
                       cmp_pe[0], cmp_w[0], rel_bias)
    win_s = jnp.concatenate([state_win_kv[0], kv[:, :, 2]], axis=1)[:, ts:]
    o_a_s, s_s = gdn_sample(z, gb, state_conv[0], state_gdn[0], gdn_conv_w[0], gdn_norm[0], n_p, bs)
    conv_s = jnp.concatenate([state_conv[0][:, 1:], rows(z, 0, CONV_CH, "s")], axis=1)
    outs["s"] = (kv[:, :, 0], kv[:, :, 1], win_s, s_s, conv_s)

    o_a = jnp.concatenate([o_a_p, o_a_s.astype(bf16), jnp.zeros((pad, GDN_V_W), bf16)], 0)
    o_b = jnp.concatenate([o_b_p, o_b_s.reshape(n_s, NSA_Q_W).astype(bf16), jnp.zeros((pad, NSA_Q_W), bf16)], 0)
    merged = merge_matmul(o_a, o_b, z, w_proj_a[0], w_proj_b[0])
    h = resid_matmul(merged, w_out[0], h)
    h = hier_moe(h, n_real, g_ffn[0], w_router_group[0], b_router_group[0], w_router_expert[0],
                 b_router_expert[0], w_gate[0], w_up[0], w_down[0])
    n3 = rmsnorm_rows(h, g_ple[0], bf16)
    h = ple_matmul(n3, w_ple_gate[0], ple, w_ple_proj[0], h)
    y_prompt = rmsnorm_rows(h, g_final, f32, 0, n_p).reshape(bp, tp, d)
    y_sample = rmsnorm_rows(h, g_final, f32, n_p, TM_ROWS)[:n_s].reshape(bs, ts, d)
    st_p, st_s = outs["p"], outs["s"]
    return (y_prompt, y_sample) + tuple(t[None] for t in st_p) + tuple(t[None] for t in st_s)
```

```python
import functools
import math

import jax
import jax.numpy as jnp
import numpy as np
from jax import lax
from jax.experimental import pallas as pl
from jax.experimental.pallas import tpu as pltpu

D_MODEL = 4096
GDN_HEADS = 16
GDN_DK = 128
GDN_DV = 128
GDN_CONV = 4
GDN_CHUNK = 64
NSA_HEADS = 16
NSA_KV_GROUPS = 4
NSA_HPG = NSA_HEADS // NSA_KV_GROUPS
NSA_DH = 128
CMP_BLOCK = 32
CMP_STRIDE = 16
SEL_BLOCK = 64
SEL_TOPK = 16
WINDOW = 512
Q_BLOCK = 128
FORCE_SCORE = 1.0e4
REL_BUCKETS = 32
REL_MAX_DIST = 1024
PAGE_SIZE = 128
MOE_GROUPS = 4
MOE_PER_GROUP = 8
MOE_EXPERTS = MOE_GROUPS * MOE_PER_GROUP
MOE_TOPK = 2
EXPERT_HIDDEN = 512
EPS = 1e-6

GDN_QK_W = GDN_HEADS * GDN_DK
GDN_V_W = GDN_HEADS * GDN_DV
CONV_CH = 2 * GDN_QK_W + GDN_V_W
NSA_Q_W = NSA_HEADS * NSA_DH
NSA_KV_W = NSA_KV_GROUPS * NSA_DH

LANES = 128
SUBLANES = 8
VMEM_LIMIT = 56 * 1024 * 1024

ROW_ALIGN = 8448
TM_DENSE = 1056
TN_DENSE = 512
TM_ROWS = 256
MOE_CHUNKS = 1

Z_Q, Z_K, Z_V, Z_ZG = 0, 2048, 4096, 6144
Z_NQ = 8192
Z_KV = 10240
Z_GA = 13312
Z_GB = 17408
Z_COLS = 21504

bf16 = jnp.bfloat16
f32 = jnp.float32


def _cparams(sem):
    return pltpu.CompilerParams(dimension_semantics=sem, vmem_limit_bytes=VMEM_LIMIT)


def _rmsnorm_body(x_ref, g_ref, o_ref):
    x = x_ref[...]
    y = x * lax.rsqrt(jnp.mean(x * x, -1, keepdims=True) + EPS)
    o_ref[...] = (y * g_ref[...]).astype(o_ref.dtype)


def rmsnorm_rows(x, g, out_dtype, row0=0, n_rows=None):
    m, d = x.shape
    n_rows = m if n_rows is None else n_rows
    t0 = row0 // TM_ROWS
    return pl.pallas_call(
        _rmsnorm_body,
        grid=(n_rows // TM_ROWS,),
        in_specs=[pl.BlockSpec((TM_ROWS, d), lambda i: (t0 + i, 0)),
                  pl.BlockSpec((1, d), lambda i: (0, 0))],
        out_specs=pl.BlockSpec((TM_ROWS, d), lambda i: (i, 0)),
        out_shape=jax.ShapeDtypeStruct((n_rows, d), out_dtype),
        compiler_params=_cparams(("parallel",)),
        name="rmsnorm_rows",
    )(x, g.reshape(1, d))


_NT = (((1,), (1,)), ((), ()))
PROJ_HEAD = 128


def _proj_body(a_ref, wa_ref, wb_ref, o_ref, ws_ref, *, parts):
    j = pl.program_id(0)

    @pl.when(pl.program_id(1) == 0)
    def _():
        for j_lo, j_hi, shift in parts:
            @pl.when((j >= j_lo) & (j < j_hi))
            def _(shift=shift):
                if shift:
                    w = jnp.concatenate([wa_ref[shift:, :], wb_ref[:shift, :]], axis=0)
                else:
                    w = wa_ref[...]
                ws_ref[...] = w.astype(bf16)
    o_ref[...] = lax.dot_general(a_ref[...], ws_ref[...], _NT, preferred_element_type=f32)


def proj_matmul(a, wt, groups):
    m, k = a.shape
    tn = TN_DENSE
    parts, j0 = [], 0
    for src, n in groups:
        shift = src - j0 * tn
        assert n % tn == 0 and 0 <= shift <= PROJ_HEAD and shift % SUBLANES == 0
        parts.append((j0, j0 + n // tn, shift))
        j0 += n // tn
    return pl.pallas_call(
        functools.partial(_proj_body, parts=tuple(parts)),
        grid=(j0, m // TM_DENSE),
        in_specs=[pl.BlockSpec((TM_DENSE, k), lambda j, i: (i, 0)),
                  pl.BlockSpec((tn, k), lambda j, i: (j, 0)),
                  pl.BlockSpec((PROJ_HEAD, k), lambda j, i: ((j + 1) * (tn // PROJ_HEAD), 0))],
        out_specs=pl.BlockSpec((TM_DENSE, tn), lambda j, i: (i, j)),
        out_shape=jax.ShapeDtypeStruct((m, j0 * tn), f32),
        scratch_shapes=[pltpu.VMEM((tn, k), bf16)],
        compiler_params=_cparams(("arbitrary", "arbitrary")),
        name="proj_matmul",
    )(a, wt, wt)


def _proj_small_body(a_ref, wa_ref, wb_ref, o_ref, ws_ref, *, lo_a, n_a, lo_b, n_b):
    @pl.when(pl.program_id(0) == 0)
    def _():
        pad = jnp.zeros((LANES - n_a - n_b, wa_ref.shape[1]), f32)
        w = jnp.concatenate([wa_ref[lo_a:lo_a + n_a, :], wb_ref[lo_b:lo_b + n_b, :], pad], axis=0)
        ws_ref[...] = w.astype(bf16)
    o_ref[...] = lax.dot_general(a_ref[...], ws_ref[...], _NT, preferred_element_type=f32)


def proj_small(a, wt, row_a, n_a, row_b, n_b):
    m, k = a.shape
    blk = LANES
    ja, lo_a = divmod(row_a, blk)
    jb, lo_b = divmod(row_b, blk)
    assert lo_a + n_a <= blk and lo_b + n_b <= blk and n_a + n_b <= LANES
    assert lo_a % SUBLANES == 0 and lo_b % SUBLANES == 0 and n_a % SUBLANES == 0 and n_b % SUBLANES == 0
    return pl.pallas_call(
        functools.partial(_proj_small_body, lo_a=lo_a, n_a=n_a, lo_b=lo_b, n_b=n_b),
        grid=(m // TM_DENSE,),
        in_specs=[pl.BlockSpec((TM_DENSE, k), lambda i: (i, 0)),
                  pl.BlockSpec((blk, k), lambda i: (ja, 0)),
                  pl.BlockSpec((blk, k), lambda i: (jb, 0))],
        out_specs=pl.BlockSpec((TM_DENSE, LANES), lambda i: (i, 0)),
        out_shape=jax.ShapeDtypeStruct((m, LANES), f32),
        scratch_shapes=[pltpu.VMEM((LANES, k), bf16)],
        compiler_params=_cparams(("arbitrary",)),
        name="proj_small",
    )(a, wt, wt)


def _merge_body(oa_ref, ob_ref, ga_ref, gb_ref, wa_ref, wb_ref, o_ref, wa_s, wb_s):
    @pl.when(pl.program_id(1) == 0)
    def _():
        wa_s[...] = wa_ref[...].astype(bf16)
        wb_s[...] = wb_ref[...].astype(bf16)
    pa = jnp.dot(oa_ref[...], wa_s[...], preferred_element_type=f32)
    pb = jnp.dot(ob_ref[...], wb_s[...], preferred_element_type=f32)
    o_ref[...] = (jax.nn.sigmoid(ga_ref[...]) * pa + jax.nn.sigmoid(gb_ref[...]) * pb).astype(o_ref.dtype)


def merge_matmul(o_a, o_b, z, w_a, w_b):
    m, ka = o_a.shape
    kb = o_b.shape[1]
    n = w_a.shape[1]
    tn = TN_DENSE
    ja, jb = Z_GA // tn, Z_GB // tn
    return pl.pallas_call(
        _merge_body,
        grid=(n // tn, m // TM_DENSE),
        in_specs=[pl.BlockSpec((TM_DENSE, ka), lambda j, i: (i, 0)),
                  pl.BlockSpec((TM_DENSE, kb), lambda j, i: (i, 0)),
                  pl.BlockSpec((TM_DENSE, tn), lambda j, i: (i, ja + j)),
                  pl.BlockSpec((TM_DENSE, tn), lambda j, i: (i, jb + j)),
                  pl.BlockSpec((ka, tn), lambda j, i: (0, j)),
                  pl.BlockSpec((kb, tn), lambda j, i: (0, j))],
        out_specs=pl.BlockSpec((TM_DENSE, tn), lambda j, i: (i, j)),
        out_shape=jax.ShapeDtypeStruct((m, n), bf16),
        scratch_shapes=[pltpu.VMEM((ka, tn), bf16), pltpu.VMEM((kb, tn), bf16)],
        compiler_params=_cparams(("arbitrary", "arbitrary")),
        name="merge_matmul",
    )(o_a, o_b, z, z, w_a, w_b)


def _resid_body(a_ref, w_ref, h_ref, o_ref, wb_ref):
    @pl.when(pl.program_id(1) == 0)
    def _():
        wb_ref[...] = w_ref[...].astype(bf16)
    o_ref[...] = h_ref[...] + jnp.dot(a_ref[...], wb_ref[...], preferred_element_type=f32)


def resid_matmul(a, w, h):
    m, k = a.shape
    n = w.shape[1]
    tn = TN_DENSE
    return pl.pallas_call(
        _resid_body,
        grid=(n // tn, m // TM_DENSE),
        in_specs=[pl.BlockSpec((TM_DENSE, k), lambda j, i: (i, 0)),
                  pl.BlockSpec((k, tn), lambda j, i: (0, j)),
                  pl.BlockSpec((TM_DENSE, tn), lambda j, i: (i, j))],
        out_specs=pl.BlockSpec((TM_DENSE, tn), lambda j, i: (i, j)),
        out_shape=jax.ShapeDtypeStruct((m, n), f32),
        scratch_shapes=[pltpu.VMEM((k, tn), bf16)],
        compiler_params=_cparams(("arbitrary", "arbitrary")),
        name="resid_matmul",
    )(a, w, h)


def _ple_body(a_ref, w_ref, p_ref, wp_ref, h_ref, o_ref, wb_ref, wpb_ref):
    @pl.when(pl.program_id(1) == 0)
    def _():
        wb_ref[...] = w_ref[...].astype(bf16)
        wpb_ref[...] = wp_ref[...].astype(bf16)
    gate = jax.nn.sigmoid(jnp.dot(a_ref[...], wb_ref[...], preferred_element_type=f32))
    emb = jnp.dot(p_ref[...], wpb_ref[...], preferred_element_type=f32)
    o_ref[...] = h_ref[...] + gate * emb


def ple_matmul(a, w_gate, p, w_proj, h):
    m, k = a.shape
    kp = p.shape[1]
    n = w_gate.shape[1]
    tn = TN_DENSE
    return pl.pallas_call(
        _ple_body,
        grid=(n // tn, m // TM_DENSE),
        in_specs=[pl.BlockSpec((TM_DENSE, k), lambda j, i: (i, 0)),
                  pl.BlockSpec((k, tn), lambda j, i: (0, j)),
                  pl.BlockSpec((TM_DENSE, kp), lambda j, i: (i, 0)),
                  pl.BlockSpec((kp, tn), lambda j, i: (0, j)),
                  pl.BlockSpec((TM_DENSE, tn), lambda j, i: (i, j))],
        out_specs=pl.BlockSpec((TM_DENSE, tn), lambda j, i: (i, j)),
        out_shape=jax.ShapeDtypeStruct((m, n), f32),
        scratch_shapes=[pltpu.VMEM((k, tn), bf16), pltpu.VMEM((kp, tn), bf16)],
        compiler_params=_cparams(("arbitrary", "arbitrary")),
        name="ple_matmul",
    )(a, w_gate, p, w_proj, h)


def _router_body(h_ref, g_ref, wr_ref, br_ref, m_ref, r_ref):
    x = h_ref[...]
    y = x * lax.rsqrt(jnp.mean(x * x, -1, keepdims=True) + EPS) * g_ref[...]
    m_ref[...] = y.astype(bf16)
    logits = jnp.dot(y.astype(bf16), wr_ref[...].astype(bf16), preferred_element_type=f32) + br_ref[...]
    lane = lax.broadcasted_iota(jnp.int32, logits.shape, 1)
    neg = -jnp.inf
    lg = jnp.where(lane < MOE_GROUPS, logits, neg)
    eg = jnp.exp(lg - jnp.max(lg, -1, keepdims=True))
    pg = eg / jnp.sum(eg, -1, keepdims=True)
    pg_top = jnp.max(pg, -1, keepdims=True)
    g_idx = jnp.min(jnp.where(pg == pg_top, lane, LANES), -1, keepdims=True)
    lo = MOE_GROUPS + MOE_PER_GROUP * g_idx
    emask = (lane >= lo) & (lane < lo + MOE_PER_GROUP)
    le = jnp.where(emask, logits, neg)
    ee = jnp.exp(le - jnp.max(le, -1, keepdims=True))
    pe = jnp.where(emask, ee / jnp.sum(ee, -1, keepdims=True), -1.0)
    v1 = jnp.max(pe, -1, keepdims=True)
    i1 = jnp.min(jnp.where(pe == v1, lane, LANES), -1, keepdims=True)
    pe2 = jnp.where(lane == i1, -1.0, pe)
    v2 = jnp.max(pe2, -1, keepdims=True)
    i2 = jnp.min(jnp.where(pe2 == v2, lane, LANES), -1, keepdims=True)
    den = v1 + v2
    w1 = pg_top * v1 / den
    w2 = pg_top * v2 / den
    e1 = (i1 - MOE_GROUPS).astype(f32)
    e2 = (i2 - MOE_GROUPS).astype(f32)
    r_ref[...] = jnp.where(lane == 0, e1, jnp.where(lane == 1, e2,
                           jnp.where(lane == 2, w1, jnp.where(lane == 3, w2, 0.0))))


def moe_router(h, g_ffn, w_router, b_router):
    m, d = h.shape
    return pl.pallas_call(
        _router_body,
        grid=(m // TM_ROWS,),
        in_specs=[pl.BlockSpec((TM_ROWS, d), lambda i: (i, 0)),
                  pl.BlockSpec((1, d), lambda i: (0, 0)),
                  pl.BlockSpec((d, LANES), lambda i: (0, 0)),
                  pl.BlockSpec((1, LANES), lambda i: (0, 0))],
        out_specs=[pl.BlockSpec((TM_ROWS, d), lambda i: (i, 0)),
                   pl.BlockSpec((TM_ROWS, LANES), lambda i: (i, 0))],
        out_shape=[jax.ShapeDtypeStruct((m, d), bf16),
                   jax.ShapeDtypeStruct((m, LANES), f32)],
        compiler_params=_cparams(("parallel",)),
        name="moe_router",
    )(h, g_ffn.reshape(1, d), w_router, b_router)


def _expert_body(te_ref, tv_ref, x_ref, rw_ref, wg_ref, wu_ref, wd_ref, *rest, tile0):
    o_ref, wg_s, wu_s, wd_s = rest[-4:]
    first = pl.program_id(0) == 0
    t = tile0 + pl.program_id(0)
    new_expert = first | (te_ref[t] != te_ref[jnp.maximum(t - 1, 0)])

    @pl.when(new_expert)
    def _():
        wg_s[...] = wg_ref[...].astype(bf16)
        wu_s[...] = wu_ref[...].astype(bf16)
        wd_s[...] = wd_ref[...].astype(bf16)

    @pl.when(tv_ref[t] > 0)
    def _():
        x = x_ref[...]
        gate = jnp.dot(x, wg_s[...], preferred_element_type=f32)
        up = jnp.dot(x, wu_s[...], preferred_element_type=f32)
        hid = (jax.nn.silu(gate) * up * rw_ref[...]).astype(bf16)
        o_ref[...] = jnp.dot(hid, wd_s[...], preferred_element_type=f32)

    @pl.when(tv_ref[t] == 0)
    def _():
        o_ref[...] = jnp.zeros_like(o_ref)


def expert_matmul(tile_expert, tile_valid, xs, row_w, wg, wu, wd, ys, tile0, n_slots):
    r, d = xs.shape
    f = wg.shape[2]
    weights = lambda shape: pl.BlockSpec((None,) + shape, lambda t, te, tv: (te[tile0 + t], 0, 0),
                                         pipeline_mode=pl.Buffered(1))
    in_specs = [pl.BlockSpec((TM_ROWS, d), lambda t, te, tv: (t, 0)),
                pl.BlockSpec((TM_ROWS, 1), lambda t, te, tv: (t, 0)),
                weights((d, f)), weights((d, f)), weights((f, d))]
    args = [tile_expert, tile_valid, xs, row_w, wg, wu, wd]
    aliases = {}
    if ys is not None:
        in_specs.append(pl.BlockSpec(memory_space=pl.ANY))
        args.append(ys)
        aliases = {len(args) - 1: 0}
    grid_spec = pltpu.PrefetchScalarGridSpec(
        num_scalar_prefetch=2,
        grid=(r // TM_ROWS,),
        in_specs=in_specs,
        out_specs=pl.BlockSpec((TM_ROWS, d), lambda t, te, tv: (tile0 + t, 0)),
        scratch_shapes=[pltpu.VMEM((d, f), bf16), pltpu.VMEM((d, f), bf16), pltpu.VMEM((f, d), bf16)],
    )
    return pl.pallas_call(
        functools.partial(_expert_body, tile0=tile0),
        grid_spec=grid_spec,
        out_shape=jax.ShapeDtypeStruct((n_slots, d), f32),
        input_output_aliases=aliases,
        compiler_params=_cparams(("arbitrary",)),
        name="expert_matmul",
    )(*args)


def hier_moe(h, n_real, g_ffn, w_rg, b_rg, w_re, b_re, w_gate, w_up, w_down):
    mp, d = h.shape
    n_route = MOE_GROUPS + MOE_EXPERTS
    w_router = jnp.zeros((d, LANES), f32).at[:, :MOE_GROUPS].set(w_rg).at[:, MOE_GROUPS:n_route].set(w_re)
    b_router = jnp.zeros((1, LANES), f32).at[0, :MOE_GROUPS].set(b_rg).at[0, MOE_GROUPS:n_route].set(b_re)
    m_bf, slab = moe_router(h, g_ffn, w_router, b_router)
    ids = slab[:n_real, 0:2].astype(jnp.int32)
    wts = slab[:n_real, 2:4]

    tm = TM_ROWS
    n_assign = n_real * MOE_TOPK
    n_slots = -(-(n_assign + MOE_EXPERTS * (tm - 1)) // tm) * tm
    e_flat = ids.reshape(-1)
    order = jnp.argsort(e_flat, stable=True)
    e_sorted = e_flat[order]
    counts = jnp.sum(jax.nn.one_hot(e_flat, MOE_EXPERTS, dtype=jnp.int32), axis=0)
    padded = -(-counts // tm) * tm
    start_p = jnp.cumsum(padded) - padded
    start = jnp.cumsum(counts) - counts
    slot_sorted = (start_p[e_sorted] + (jnp.arange(n_assign, dtype=jnp.int32) - start[e_sorted])).astype(jnp.int32)
    slot_of = slot_sorted[jnp.argsort(order)]
    n_tiles = n_slots // tm
    tile_start = jnp.arange(n_tiles, dtype=jnp.int32) * tm
    ends = jnp.cumsum(padded)
    tile_expert = jnp.minimum(jnp.searchsorted(ends, tile_start, side="right"), MOE_EXPERTS - 1).astype(jnp.int32)
    tile_valid = (tile_start < ends[-1]).astype(jnp.int32)
    slot = jnp.arange(n_slots, dtype=jnp.int32)
    slot_e = jnp.repeat(tile_expert, tm)
    off = slot - start_p[slot_e]
    used = (off < counts[slot_e]) & (jnp.repeat(tile_valid, tm) > 0)
    assign = order[jnp.clip(start[slot_e] + off, 0, n_assign - 1)]
    src_tok = jnp.where(used, assign // MOE_TOPK, 0).astype(jnp.int32)
    row_w = jnp.where(used, wts.reshape(-1)[assign], 0.0)

    ys = None
    for c in range(MOE_CHUNKS):
        t0, t1 = (c * n_tiles) // MOE_CHUNKS, ((c + 1) * n_tiles) // MOE_CHUNKS
        xs = m_bf.at[src_tok[t0 * tm:t1 * tm]].get(mode="promise_in_bounds")
        ys = expert_matmul(tile_expert, tile_valid, xs, row_w[t0 * tm:t1 * tm].reshape(-1, 1),
                           w_gate, w_up, w_down, ys, t0, n_slots)
    slot2 = slot_of.reshape(n_real, MOE_TOPK)
    y = ys.at[slot2[:, 0]].get(mode="promise_in_bounds") + ys.at[slot2[:, 1]].get(mode="promise_in_bounds")
    return h.at[:n_real].add(y)


NEG_BIG = -1e30
TILE_FAR = REL_MAX_DIST // Q_BLOCK + 1
TILE_WIN_EDGE = TILE_FAR + 1
TILE_NONE = TILE_FAR + 2
N_BIAS_TILES = TILE_FAR + 3
NSA_KT = 4


def _compress_body(x_ref, w_ref, pe_ref, o_ref):
    ns = o_ref.shape[0]
    acc0 = jnp.zeros((ns, NSA_DH), f32)
    acc1 = jnp.zeros((ns, NSA_DH), f32)
    for s in range(CMP_STRIDE):
        xs = x_ref[pl.ds(s, ns, stride=CMP_STRIDE), :]
        a0 = (xs + pe_ref[s:s + 1, :]).astype(bf16)
        a1 = (xs + pe_ref[CMP_STRIDE + s:CMP_STRIDE + s + 1, :]).astype(bf16)
        acc0 = acc0 + jnp.dot(a0, w_ref[s].astype(bf16), preferred_element_type=f32)
        acc1 = acc1 + jnp.dot(a1, w_ref[CMP_STRIDE + s].astype(bf16), preferred_element_type=f32)
    o_ref[...] = (acc0 + pltpu.roll(acc1, ns - 1, axis=0)).astype(o_ref.dtype)


def compress_prompt(z, cmp_w, cmp_pe, bn, t):
    ns = t // CMP_STRIDE
    col0 = Z_KV // NSA_DH
    return pl.pallas_call(
        _compress_body,
        grid=(bn, 2, NSA_KV_GROUPS),
        in_specs=[pl.BlockSpec((t, NSA_DH), lambda b, x, g: (b, col0 + x * NSA_KV_GROUPS + g)),
                  pl.BlockSpec((None, CMP_BLOCK, NSA_DH, NSA_DH), lambda b, x, g: (x, 0, 0, 0)),
                  pl.BlockSpec((None, CMP_BLOCK, NSA_DH), lambda b, x, g: (x, 0, 0))],
        out_specs=pl.BlockSpec((None, None, None, ns, NSA_DH), lambda b, x, g: (b, x, g, 0, 0)),
        out_shape=jax.ShapeDtypeStruct((bn, 2, NSA_KV_GROUPS, ns, NSA_DH), bf16),
        compiler_params=_cparams(("parallel", "parallel", "parallel")),
        name="compress_prompt",
    )(z, cmp_w, cmp_pe)


def _nsa_prompt_body(q_ref, kc_ref, vc_ref, ks_ref, vs_ref, kw_ref, vw_ref, gate_ref, bc_ref, bt_ref,
                     cov_ref, exp_ref, o_ref, mb_ref, s_ref, mx_ref, m_ref, l_ref, acc_ref):
    g = pl.program_id(1)
    i = pl.program_id(2)
    qb = Q_BLOCK
    rows = NSA_HPG * qb
    qf = q_ref[...]
    q = jnp.concatenate([qf[:, h * NSA_DH:(h + 1) * NSA_DH] for h in range(NSA_HPG)], axis=0).astype(bf16)

    ncp = kc_ref.shape[0]
    n_done = (qb // CMP_STRIDE) * (i + 1)
    bias_c = pltpu.roll(bc_ref[...], n_done % ncp, axis=1)
    lane_c = lax.broadcasted_iota(jnp.int32, bias_c.shape, 1)
    bias_c = jnp.where(lane_c < n_done, bias_c, NEG_BIG)
    s = lax.dot_general(q, kc_ref[...], _NT, preferred_element_type=f32) * (NSA_DH ** -0.5) + bias_c
    m = jnp.max(s, -1, keepdims=True)
    m = jnp.where(m < 0.5 * NEG_BIG, 0.0, m)
    e = jnp.exp(s - m)
    p = e / jnp.maximum(jnp.sum(e, -1, keepdims=True), 1e-30)
    o_c = jnp.dot(p.astype(bf16), vc_ref[...], preferred_element_type=f32)

    pr = p.astype(bf16).astype(f32)
    psum = pr[0:qb]
    for h in range(1, NSA_HPG):
        psum = psum + pr[h * qb:(h + 1) * qb]
    imp_t = lax.dot_general(cov_ref[...], psum, _NT, preferred_element_type=f32,
                            precision=lax.Precision.HIGHEST)
    nsb = imp_t.shape[0]
    blk = lax.broadcasted_iota(jnp.int32, (nsb, qb), 0)
    qpos = i * qb + lax.broadcasted_iota(jnp.int32, (nsb, qb), 1)
    cur = qpos // SEL_BLOCK
    forced = (blk == 0) | (blk == cur) | (blk == cur - 1)
    valid = blk * SEL_BLOCK <= qpos
    score = jnp.where(valid, jnp.where(forced, FORCE_SCORE, imp_t), -1.0)
    sel_t = jnp.zeros((nsb, qb), f32)
    for _ in range(min(SEL_TOPK, nsb)):
        mx = jnp.max(score, axis=0, keepdims=True)
        first = jnp.min(jnp.where(score == mx, blk, nsb), axis=0, keepdims=True)
        pick = blk == first
        sel_t = jnp.where(pick, 1.0, sel_t)
        score = jnp.where(pick, -2.0, score)
    unsel = ((sel_t - 1.0) * (-NEG_BIG)).T.astype(bf16)

    scale = NSA_DH ** -0.5

    mx_ref[...] = jnp.full(mx_ref.shape, NEG_BIG, f32)

    kw_ = NSA_KT * qb
    n_steps = (i + NSA_KT) // NSA_KT

    def score_step(jj, carry):
        r0 = pl.multiple_of(jj * kw_, kw_)
        k = ks_ref[pl.ds(r0, kw_), :].astype(bf16)
        biases = []
        for tt in range(NSA_KT):
            d = i - (jj * NSA_KT + tt)
            biases.append(bt_ref[jnp.where(d < 0, TILE_NONE, jnp.minimum(d, TILE_FAR))])
        mb = jnp.dot(unsel, exp_ref[:, pl.ds(r0, kw_)], preferred_element_type=f32)
        s = (lax.dot_general(q, k, _NT, preferred_element_type=f32) * scale
             + jnp.concatenate(biases, axis=1) + jnp.concatenate([mb] * NSA_HPG, axis=0))
        s_ref[:, pl.ds(r0, kw_)] = s
        mx = mx_ref[...]
        for tt in range(NSA_KT):
            mx = jnp.maximum(mx, s[:, tt * qb:(tt + 1) * qb])
        mx_ref[...] = mx
        return carry

    lax.fori_loop(0, n_steps, score_step, 0)
    m_ref[...] = jnp.broadcast_to(jnp.max(mx_ref[...], -1, keepdims=True), m_ref.shape)
    l_ref[...] = jnp.zeros(l_ref.shape, f32)
    acc_ref[...] = jnp.zeros(acc_ref.shape, f32)

    def pv_step(jj, carry):
        r0 = pl.multiple_of(jj * kw_, kw_)
        p = jnp.exp(s_ref[:, pl.ds(r0, kw_)] - jnp.concatenate([m_ref[...]] * NSA_KT, axis=1))
        v = vs_ref[pl.ds(r0, kw_), :].astype(bf16)
        lsum = l_ref[...]
        for tt in range(NSA_KT):
            lsum = lsum + p[:, tt * qb:(tt + 1) * qb]
        l_ref[...] = lsum
        acc_ref[...] = acc_ref[...] + jnp.dot(p.astype(bf16), v, preferred_element_type=f32)
        return carry

    lax.fori_loop(0, n_steps, pv_step, 0)
    o_s = acc_ref[...] / jnp.sum(l_ref[...], -1, keepdims=True)

    n_win = WINDOW // qb
    j0 = jnp.maximum(i - n_win, 0)
    s_w = []
    for tt in range(n_win + 1):
        d = i - (j0 + tt)
        r0 = pl.multiple_of((j0 + tt) * qb, qb)
        k = kw_ref[pl.ds(r0, qb), :].astype(bf16)
        tile = jnp.where(d < 0, TILE_NONE, jnp.where(d == n_win, TILE_WIN_EDGE, d))
        s_w.append(lax.dot_general(q, k, _NT, preferred_element_type=f32) * scale + bt_ref[tile])
    mw = s_w[0]
    for s in s_w[1:]:
        mw = jnp.maximum(mw, s)
    mw = jnp.max(mw, -1, keepdims=True)
    lw = jnp.zeros((rows, qb), f32)
    o_w = jnp.zeros((rows, NSA_DH), f32)
    for tt in range(n_win + 1):
        r0 = pl.multiple_of((j0 + tt) * qb, qb)
        p = jnp.exp(s_w[tt] - mw)
        lw = lw + p
        o_w = o_w + jnp.dot(p.astype(bf16), vw_ref[pl.ds(r0, qb), :].astype(bf16), preferred_element_type=f32)
    o_w = o_w / jnp.sum(lw, -1, keepdims=True)

    gt = jax.nn.sigmoid(gate_ref[...])
    outs = []
    for h in range(NSA_HPG):
        c = 3 * h
        sl = slice(h * qb, (h + 1) * qb)
        outs.append(gt[:, c:c + 1] * o_c[sl] + gt[:, c + 1:c + 2] * o_s[sl] + gt[:, c + 2:c + 3] * o_w[sl])
    o_ref[...] = jnp.concatenate(outs, axis=1).astype(o_ref.dtype)


def _nsa_bias_tables(rel_bias, t):
    qb = Q_BLOCK
    nq = t // qb
    tbl = rel_bias.astype(f32)
    r = jnp.arange(qb, dtype=jnp.int32)[:, None]
    c = jnp.arange(qb, dtype=jnp.int32)[None, :]
    dists = [delta * qb + r - c for delta in range(TILE_FAR + 1)]
    oks = [d_ >= 0 for d_ in dists]
    d_win = (WINDOW // qb) * qb + r - c
    dists += [d_win, d_win]
    oks += [(d_win >= 0) & (d_win < WINDOW), jnp.zeros((qb, qb), bool)]
    bt = jnp.where(jnp.stack(oks, 0)[None], _rel_bias_heads(tbl, jnp.stack(dists, 0)), NEG_BIG)
    bt = bt.reshape(NSA_KV_GROUPS, NSA_HPG, N_BIAS_TILES, qb, qb)
    bt = jnp.transpose(bt, (0, 2, 1, 3, 4)).reshape(NSA_KV_GROUPS, N_BIAS_TILES, NSA_HPG * qb, qb)
    ncp = t // CMP_STRIDE
    m_back = (ncp - 1) - jnp.arange(ncp, dtype=jnp.int32)[None, :]
    dist_c = r + CMP_STRIDE * m_back + CMP_STRIDE - (qb + CMP_BLOCK - 1)
    bc = jnp.where((dist_c >= 0)[None], _rel_bias_heads(tbl, dist_c), NEG_BIG)
    return bt, bc.reshape(NSA_KV_GROUPS, NSA_HPG * qb, ncp)


def nsa_prompt_attention(z, gates_g, kvc, rel_bias, bn, t):
    qb = Q_BLOCK
    nq = t // qb
    ncp = t // CMP_STRIDE
    nsb = t // SEL_BLOCK
    rows = NSA_HPG * qb
    bt, bc = _nsa_bias_tables(rel_bias, t)
    cover_t = jnp.concatenate([_cover_matrix(ncp - 1, nsb), jnp.zeros((1, nsb), f32)], 0).T
    expand = jnp.asarray(np.repeat(np.eye(nsb, dtype=np.float32), SEL_BLOCK, axis=1), bf16)
    kcol = Z_KV // NSA_DH
    kv_spec = lambda off: pl.BlockSpec((t, NSA_DH), lambda b, g, i, off=off: (b, kcol + off + g))
    return pl.pallas_call(
        _nsa_prompt_body,
        grid=(bn, NSA_KV_GROUPS, nq),
        in_specs=[pl.BlockSpec((qb, NSA_HPG * NSA_DH), lambda b, g, i: (b * nq + i, Z_NQ // (NSA_HPG * NSA_DH) + g)),
                  pl.BlockSpec((None, None, None, ncp, NSA_DH), lambda b, g, i: (b, 0, g, 0, 0)),
                  pl.BlockSpec((None, None, None, ncp, NSA_DH), lambda b, g, i: (b, 1, g, 0, 0)),
                  kv_spec(2 * NSA_KV_GROUPS), kv_spec(3 * NSA_KV_GROUPS),
                  kv_spec(4 * NSA_KV_GROUPS), kv_spec(5 * NSA_KV_GROUPS),
                  pl.BlockSpec((None, qb, 3 * NSA_HPG), lambda b, g, i: (g, b * nq + i, 0)),
                  pl.BlockSpec((None, rows, ncp), lambda b, g, i: (g, 0, 0)),
                  pl.BlockSpec((None, N_BIAS_TILES, rows, qb), lambda b, g, i: (g, 0, 0, 0)),
                  pl.BlockSpec((nsb, ncp), lambda b, g, i: (0, 0)),
                  pl.BlockSpec((nsb, t), lambda b, g, i: (0, 0))],
        out_specs=pl.BlockSpec((qb, NSA_HPG * NSA_DH), lambda b, g, i: (b * nq + i, g)),
        out_shape=jax.ShapeDtypeStruct((bn * t, NSA_Q_W), bf16),
        scratch_shapes=[pltpu.VMEM((qb, t), f32), pltpu.VMEM((rows, t), f32), pltpu.VMEM((rows, qb), f32),
                        pltpu.VMEM((rows, qb), f32), pltpu.VMEM((rows, qb), f32), pltpu.VMEM((rows, NSA_DH), f32)],
        compiler_params=_cparams(("parallel", "parallel", "arbitrary")),
        name="nsa_prompt_attention",
    )(z, kvc, kvc, z, z, z, z, gates_g, bc, bt, cover_t, expand)


GDN_RT = 1024
GDN_GROUP = 4
_HI = lax.Precision.HIGHEST


def _mm1(a, b):
    return jnp.dot(a.astype(bf16), b.astype(bf16), preferred_element_type=f32)


def _mm3(a, b):
    ah, bh = a.astype(bf16), b.astype(bf16)
    al = (a - ah.astype(f32)).astype(bf16)
    bl = (b - bh.astype(f32)).astype(bf16)
    return (jnp.dot(ah, bh, preferred_element_type=f32) + jnp.dot(ah, bl, preferred_element_type=f32)
            + jnp.dot(al, bh, preferred_element_type=f32))


def _gdn_prep_body(q_ref, k_ref, v_ref, qp_ref, kp_ref, vp_ref, gb_ref, cwq_ref, cwk_ref, cwv_ref,
                   w_ref, u_ref, aqk_ref, qg_ref, kg_ref, gl_ref, qs_ref, ks_ref, vs_ref):
    rt = q_ref.shape[0]
    cs = GDN_CHUNK
    first = pl.program_id(2) == 0
    pad = SUBLANES

    def conv_silu(x_ref, prev_ref, scr_ref, cw_ref):
        scr_ref[0:pad, :] = jnp.where(first, 0.0, prev_ref[...])
        scr_ref[pad:pad + rt, :] = x_ref[...]
        y = scr_ref[pl.ds(pad - (GDN_CONV - 1), rt), :] * cw_ref[0:1, :]
        for j in range(1, GDN_CONV):
            y = y + scr_ref[pl.ds(pad - (GDN_CONV - 1) + j, rt), :] * cw_ref[j:j + 1, :]
        return y * jax.nn.sigmoid(y)

    q = conv_silu(q_ref, qp_ref, qs_ref, cwq_ref)
    k = conv_silu(k_ref, kp_ref, ks_ref, cwk_ref)
    v = conv_silu(v_ref, vp_ref, vs_ref, cwv_ref)
    q = q * lax.rsqrt(jnp.sum(q * q, -1, keepdims=True) + EPS) * (GDN_DK ** -0.5)
    k = k * lax.rsqrt(jnp.sum(k * k, -1, keepdims=True) + EPS)

    gb = gb_ref[...]
    x = gb[:, 1:2] + gb[:, 2:3]
    softplus = jnp.maximum(x, 0.0) + jnp.log1p(jnp.exp(-jnp.abs(x)))
    g = jnp.broadcast_to(-jnp.exp(gb[:, 3:4]) * softplus, (rt, GDN_DK))
    beta = jnp.broadcast_to(jax.nn.sigmoid(gb[:, 0:1]), (rt, GDN_DK))

    row_in_chunk = lax.broadcasted_iota(jnp.int32, (rt, GDN_DK), 0) % cs
    gcum_all = g
    step = 1
    while step < cs:
        gcum_all = gcum_all + jnp.where(row_in_chunk >= step, pltpu.roll(gcum_all, step, axis=0), 0.0)
        step *= 2

    gr = GDN_GROUP * cs
    ri = lax.broadcasted_iota(jnp.int32, (gr, gr), 0)
    ci = lax.broadcasted_iota(jnp.int32, (gr, gr), 1)
    same_chunk = (ri // cs) == (ci // cs)
    incl = same_chunk & (ri >= ci)
    strict = same_chunk & (ri > ci)
    eye = (ri == ci).astype(f32)
    gls = []
    for grp in range(rt // gr):
        sl = slice(grp * gr, (grp + 1) * gr)
        qc, kc, vc, bc_, gcum = q[sl], k[sl], v[sl], beta[sl], gcum_all[sl]
        g_col = jnp.concatenate([gcum] * (gr // GDN_DK), axis=1)
        g_row = jnp.broadcast_to(gcum.T[0:1, :], (gr, gr))
        dec = jnp.exp(jnp.where(incl, g_col - g_row, 0.0))
        eg = jnp.exp(gcum)
        lasts = [gcum[(c + 1) * cs - 1:(c + 1) * cs, :] for c in range(GDN_GROUP)]
        g_last = jnp.concatenate([jnp.broadcast_to(r, (cs, GDN_DK)) for r in lasts], axis=0)
        kb = kc * bc_
        kbf = kc.astype(bf16)
        lmat = lax.dot_general(kb.astype(bf16), kbf, _NT, preferred_element_type=f32) * jnp.where(strict, dec, 0.0)
        aqk = lax.dot_general(qc.astype(bf16), kbf, _NT, preferred_element_type=f32) * jnp.where(incl, dec, 0.0)
        inv = eye - lmat
        pw = lmat
        for _ in range(int(math.log2(cs)) - 1):
            pw = _mm1(pw, pw)
            inv = inv + _mm1(inv, pw)
        for _ in range(2):
            inv = inv + _mm1(inv, eye - inv - _mm3(lmat, inv))
        rhs = jnp.concatenate([kb * eg, vc * bc_], axis=1)
        wu = _mm3(inv, rhs)
        w_ref[sl, :] = wu[:, :GDN_DK].astype(w_ref.dtype)
        u_ref[sl, :] = wu[:, GDN_DK:]
        for c in range(GDN_GROUP):
            blk = slice(c * cs, (c + 1) * cs)
            aqk_ref[grp * gr + c * cs:grp * gr + (c + 1) * cs, :] = aqk[blk, blk].astype(aqk_ref.dtype)
        qg_ref[sl, :] = (qc * eg).astype(qg_ref.dtype)
        kg_ref[sl, :] = (kc * jnp.exp(g_last - gcum)).astype(kg_ref.dtype)
        gls.extend(jnp.exp(r) for r in lasts)
    gl_ref[...] = jnp.concatenate(gls, axis=0)


def _gdn_scan_body(w_ref, u_ref, aqk_ref, qg_ref, kg_ref, gl_ref, za_ref, gn_ref, o_ref, sfin_ref, s_ref):
    c = pl.program_id(1)
    nc = pl.num_programs(1)
    n_gl = gl_ref.shape[1]

    @pl.when(c == 0)
    def _():
        s_ref[...] = jnp.zeros(s_ref.shape, f32)

    for h in range(GDN_HEADS):
        s = s_ref[h]
        sb = s.astype(bf16)
        v_new = u_ref[h] - jnp.dot(w_ref[h], sb, preferred_element_type=f32)
        vb = v_new.astype(bf16)
        o = (jnp.dot(qg_ref[h], sb, preferred_element_type=f32)
             + jnp.dot(aqk_ref[h], vb, preferred_element_type=f32))
        gl = gl_ref[h, pl.ds(c % n_gl, 1), :]
        s_ref[h] = gl * s + lax.dot_general(kg_ref[h], vb, (((0,), (0,)), ((), ())), preferred_element_type=f32)
        on = o * lax.rsqrt(jnp.mean(o * o, -1, keepdims=True) + EPS) * gn_ref[...]
        za = za_ref[:, h * GDN_DV:(h + 1) * GDN_DV]
        o_ref[:, h * GDN_DV:(h + 1) * GDN_DV] = (on * (za * jax.nn.sigmoid(za))).astype(o_ref.dtype)

    @pl.when(c == nc - 1)
    def _():
        sfin_ref[...] = s_ref[...]


def gdn_prompt(z, gb, conv_w, gdn_norm, bn, t):
    rt, cs = GDN_RT, GDN_CHUNK
    n_rt = t // rt
    m = bn * t
    hcol = GDN_DK // LANES
    qkv = lambda part: pl.BlockSpec((rt, GDN_DK), lambda h, b, r, part=part: (b * n_rt + r, part * GDN_HEADS + h))
    prev = lambda part: pl.BlockSpec(
        (SUBLANES, GDN_DK),
        lambda h, b, r, part=part: (jnp.maximum((b * n_rt + r) * (rt // SUBLANES) - 1, 0), part * GDN_HEADS + h))
    cw = lambda part: pl.BlockSpec((GDN_CONV, GDN_DK), lambda h, b, r, part=part: (0, part * GDN_HEADS + h))
    per_row = lambda width: pl.BlockSpec((None, rt, width), lambda h, b, r: (h, b * n_rt + r, 0))
    w_c, u_c, aqk, qg, kg, gl = pl.pallas_call(
        _gdn_prep_body,
        grid=(GDN_HEADS, bn, n_rt),
        in_specs=[qkv(0), qkv(1), qkv(2), prev(0), prev(1), prev(2), per_row(4), cw(0), cw(1), cw(2)],
        out_specs=[per_row(GDN_DK), per_row(GDN_DV), per_row(cs), per_row(GDN_DK), per_row(GDN_DK),
                   pl.BlockSpec((None, rt // cs, GDN_DK), lambda h, b, r: (h, b * n_rt + r, 0))],
        out_shape=[jax.ShapeDtypeStruct((GDN_HEADS, m, GDN_DK), bf16),
                   jax.ShapeDtypeStruct((GDN_HEADS, m, GDN_DV), f32),
                   jax.ShapeDtypeStruct((GDN_HEADS, m, cs), bf16),
                   jax.ShapeDtypeStruct((GDN_HEADS, m, GDN_DK), bf16),
                   jax.ShapeDtypeStruct((GDN_HEADS, m, GDN_DK), bf16),
                   jax.ShapeDtypeStruct((GDN_HEADS, m // cs, GDN_DK), f32)],
        scratch_shapes=[pltpu.VMEM((rt + SUBLANES, GDN_DK), f32)] * 3,
        compiler_params=_cparams(("parallel", "parallel", "parallel")),
        name="gdn_prep",
    )(z, z, z, z, z, z, gb, conv_w, conv_w, conv_w)

    nc = t // cs
    n_gl = rt // cs
    heads = lambda width: pl.BlockSpec((GDN_HEADS, cs, width), lambda b, c: (0, b * nc + c, 0))
    o, s_fin = pl.pallas_call(
        _gdn_scan_body,
        grid=(bn, nc),
        in_specs=[heads(GDN_DK), heads(GDN_DV), heads(cs), heads(GDN_DK), heads(GDN_DK),
                  pl.BlockSpec((GDN_HEADS, n_gl, GDN_DK), lambda b, c: (0, (b * nc + c) // n_gl, 0)),
                  pl.BlockSpec((cs, GDN_V_W), lambda b, c: (b * nc + c, Z_ZG // GDN_V_W)),
                  pl.BlockSpec((1, GDN_DV), lambda b, c: (0, 0))],
        out_specs=[pl.BlockSpec((cs, GDN_V_W), lambda b, c: (b * nc + c, 0)),
                   pl.BlockSpec((None, GDN_HEADS, GDN_DK, GDN_DV), lambda b, c: (b, 0, 0, 0))],
        out_shape=[jax.ShapeDtypeStruct((m, GDN_V_W), bf16),
                   jax.ShapeDtypeStruct((bn, GDN_HEADS, GDN_DK, GDN_DV), f32)],
        scratch_shapes=[pltpu.VMEM((GDN_HEADS, GDN_DK, GDN_DV), f32)],
        compiler_params=_cparams(("parallel", "arbitrary")),
        name="gdn_scan",
    )(w_c, u_c, aqk, qg, kg, gl, z, gdn_norm.reshape(1, GDN_DV))
    return o, s_fin


GDN_SB = 8


def _gdn_sample_body(x_ref, za_ref, gb_ref, sc_ref, s0_ref, cw_ref, gn_ref, o_ref, s_ref):
    sb = GDN_SB
    x3 = sc_ref[...]
    y = x_ref[...] * cw_ref[GDN_CONV - 1:GDN_CONV, :]
    for j in range(GDN_CONV - 1):
        y = y + x3[:, j, :] * cw_ref[j:j + 1, :]
    xc = y * jax.nn.sigmoid(y)
    for h in range(GDN_HEADS):
        q = xc[:, h * GDN_DK:(h + 1) * GDN_DK]
        k = xc[:, GDN_QK_W + h * GDN_DK:GDN_QK_W + (h + 1) * GDN_DK]
        v = xc[:, 2 * GDN_QK_W + h * GDN_DV:2 * GDN_QK_W + (h + 1) * GDN_DV]
        q = q * lax.rsqrt(jnp.sum(q * q, -1, keepdims=True) + EPS) * (GDN_DK ** -0.5)
        k = k * lax.rsqrt(jnp.sum(k * k, -1, keepdims=True) + EPS)
        gb = gb_ref[h]
        x = gb[:, 1:2] + gb[:, 2:3]
        softplus = jnp.maximum(x, 0.0) + jnp.log1p(jnp.exp(-jnp.abs(x)))
        eg = jnp.broadcast_to(jnp.exp(-jnp.exp(gb[:, 3:4]) * softplus), (sb, GDN_DV))
        beta = jnp.broadcast_to(jax.nn.sigmoid(gb[:, 0:1]), (sb, GDN_DV))
        qk = jnp.broadcast_to(jnp.sum(q * k, -1, keepdims=True), (sb, GDN_DV))
        q_t = q.T
        k_t = k.T
        o_rows = []
        for i in range(sb):
            s = s0_ref[i, h]
            kcol = k_t[:, i:i + 1]
            k_s = jnp.sum(kcol * s, axis=0, keepdims=True)
            q_s = jnp.sum(q_t[:, i:i + 1] * s, axis=0, keepdims=True)
            eg_i = eg[i:i + 1]
            v_new = beta[i:i + 1] * (v[i:i + 1] - eg_i * k_s)
            o_rows.append(eg_i * q_s + qk[i:i + 1] * v_new)
            s_ref[i, h] = eg_i * s + kcol * v_new
        o = jnp.concatenate(o_rows, axis=0)
        on = o * lax.rsqrt(jnp.mean(o * o, -1, keepdims=True) + EPS) * gn_ref[...]
        za = za_ref[:, h * GDN_DV:(h + 1) * GDN_DV]
        o_ref[:, h * GDN_DV:(h + 1) * GDN_DV] = on * (za * jax.nn.sigmoid(za))


def gdn_sample(z, gb, state_conv, state_gdn, conv_w, gdn_norm, row0, bs):
    sb = GDN_SB
    r0 = row0 // sb
    return pl.pallas_call(
        _gdn_sample_body,
        grid=(bs // sb,),
        in_specs=[pl.BlockSpec((sb, CONV_CH), lambda i: (r0 + i, 0)),
                  pl.BlockSpec((sb, GDN_V_W), lambda i: (r0 + i, Z_ZG // GDN_V_W)),
                  pl.BlockSpec((GDN_HEADS, sb, 4), lambda i: (0, r0 + i, 0)),
                  pl.BlockSpec((sb, GDN_CONV - 1, CONV_CH), lambda i: (i, 0, 0)),
                  pl.BlockSpec((sb, GDN_HEADS, GDN_DK, GDN_DV), lambda i: (i, 0, 0, 0)),
                  pl.BlockSpec((GDN_CONV, CONV_CH), lambda i: (0, 0)),
                  pl.BlockSpec((1, GDN_DV), lambda i: (0, 0))],
        out_specs=[pl.BlockSpec((sb, GDN_V_W), lambda i: (i, 0)),
                   pl.BlockSpec((sb, GDN_HEADS, GDN_DK, GDN_DV), lambda i: (i, 0, 0, 0))],
        out_shape=[jax.ShapeDtypeStruct((bs, GDN_V_W), f32),
                   jax.ShapeDtypeStruct((bs, GDN_HEADS, GDN_DK, GDN_DV), f32)],
        compiler_params=_cparams(("parallel",)),
        name="gdn_sample",
    )(z, z, gb, state_conv, state_gdn, conv_w, gdn_norm.reshape(1, GDN_DV))


SMP_PAGES = 16
_TN = (((0,), (0,)), ((), ()))
_KV_ROW = 2 * NSA_KV_GROUPS * NSA_DH


def _page_specs(shape):
    return [pl.BlockSpec((None,) + shape, lambda b, p, pt, kk=kk: (pt[b, p * SMP_PAGES + kk], 0, 0))
            for kk in range(SMP_PAGES)]


def _compress_sample_body(pt_ref, *refs):
    pages = refs[:SMP_PAGES]
    w_ref, o_ref = refs[SMP_PAGES:]
    per = PAGE_SIZE // CMP_STRIDE
    slabs = 2 * NSA_KV_GROUPS
    for x in range(2):
        rows = []
        for kk in range(SMP_PAGES):
            for g in range(NSA_KV_GROUPS):
                slab = x * NSA_KV_GROUPS + g
                parts = [pages[kk][pl.ds(s * slabs + slab, per, stride=CMP_STRIDE * slabs), :]
                         for s in range(CMP_STRIDE)]
                rows.append(jnp.concatenate(parts, axis=1))
        lhs = jnp.concatenate(rows, axis=0).astype(bf16)
        prod = jnp.dot(lhs, w_ref[x], preferred_element_type=f32)
        for kk in range(SMP_PAGES):
            for g in range(NSA_KV_GROUPS):
                r = (kk * NSA_KV_GROUPS + g) * per
                o_ref[x, g, kk * per:(kk + 1) * per, :] = prod[r:r + per]


def compress_sample(page_table, cache, w2):
    bs, n_pages = page_table.shape
    per = PAGE_SIZE // CMP_STRIDE
    n_sub = n_pages * per
    grid_spec = pltpu.PrefetchScalarGridSpec(
        num_scalar_prefetch=1,
        grid=(bs, n_pages // SMP_PAGES),
        in_specs=(_page_specs((PAGE_SIZE * 2 * NSA_KV_GROUPS, NSA_DH))
                  + [pl.BlockSpec((2, CMP_STRIDE * NSA_DH, 2 * NSA_DH), lambda b, p, pt: (0, 0, 0))]),
        out_specs=pl.BlockSpec((None, 2, NSA_KV_GROUPS, SMP_PAGES * per, 2 * NSA_DH), lambda b, p, pt: (b, 0, 0, p, 0)),
    )
    return pl.pallas_call(
        _compress_sample_body,
        grid_spec=grid_spec,
        out_shape=jax.ShapeDtypeStruct((bs, 2, NSA_KV_GROUPS, n_sub, 2 * NSA_DH), f32),
        compiler_params=_cparams(("parallel", "arbitrary")),
        name="compress_sample",
    )(page_table, *([cache] * SMP_PAGES), w2)


def _nsa_sample_select_body(qt_ref, pm_ref, w_ref, pe_ref, bc_ref, cov_ref, win_ref, bw_ref, b0_ref, kn_ref, vn_ref,
                            oc_ref, ow_ref, mask_ref, sn_ref, *, past):
    scale = NSA_DH ** -0.5
    qt = qt_ref[...]
    n_sub = pm_ref.shape[2]
    lane1 = lax.broadcasted_iota(jnp.int32, (1, LANES), 1)
    lane_n = lax.broadcasted_iota(jnp.int32, (n_sub, LANES), 1)
    pe_term = []
    for x in range(2):
        pr = jnp.dot(pe_ref[x], w_ref[x], preferred_element_type=f32)
        pe_term.append(pr[0:1, :NSA_DH] + pr[1:2, NSA_DH:])
    oc_t = jnp.zeros((NSA_DH, LANES), f32)
    ow_t = jnp.zeros((NSA_DH, LANES), f32)
    psum_all = jnp.zeros((n_sub, LANES), f32)
    snew = jnp.zeros((SUBLANES, LANES), f32)
    for g in range(NSA_KV_GROUPS):
        in_g = (lane1 // NSA_HPG) == g
        kvc = []
        for x in range(2):
            pm = pm_ref[x, g]
            kvc.append((pm[:, :NSA_DH] + pltpu.roll(pm[:, NSA_DH:], n_sub - 1, axis=0) + pe_term[x]).astype(bf16))
        s = jnp.dot(kvc[0], qt, preferred_element_type=f32) * scale + bc_ref[...]
        e = jnp.exp(s - jnp.max(s, axis=0, keepdims=True))
        p = jnp.where(in_g, e / jnp.sum(e, axis=0, keepdims=True), 0.0)
        oc_t = oc_t + lax.dot_general(kvc[1], p.astype(bf16), _TN, preferred_element_type=f32)
        psum = jnp.sum(p.astype(bf16).astype(f32), axis=1, keepdims=True)
        psum_all = jnp.where((lane_n // NSA_HPG) == g, psum, psum_all)
        slabs = 2 * NSA_KV_GROUPS
        w_buf = win_ref.shape[0] // slabs
        kw = win_ref[pl.ds(g, w_buf, stride=slabs), :].astype(bf16)
        vw = win_ref[pl.ds(NSA_KV_GROUPS + g, w_buf, stride=slabs), :].astype(bf16)
        sw = jnp.dot(kw, qt, preferred_element_type=f32) * scale + bw_ref[...]
        sn = jnp.dot(kn_ref[g], qt, preferred_element_type=f32) * scale + b0_ref[...]
        mw = jnp.maximum(jnp.max(sw, axis=0, keepdims=True), sn[1:2])
        ew = jnp.exp(sw - mw)
        en = jnp.exp(sn[1:2] - mw)
        lw = jnp.sum(ew, axis=0, keepdims=True) + en
        pw = jnp.where(in_g, ew / lw, 0.0)
        row = lax.broadcasted_iota(jnp.int32, (SUBLANES, LANES), 0)
        pn = jnp.where((row == 1) & in_g, en / lw, 0.0)
        ow_t = (ow_t + lax.dot_general(vw, pw.astype(bf16), _TN, preferred_element_type=f32)
                + lax.dot_general(vn_ref[g], pn.astype(bf16), _TN, preferred_element_type=f32))
        snew = jnp.where(in_g, sn, snew)
    oc_ref[...] = oc_t
    ow_ref[...] = ow_t
    sn_ref[...] = snew

    imp_t = jnp.dot(cov_ref[...], psum_all, preferred_element_type=f32, precision=_HI)
    nsb = imp_t.shape[0]
    n_blocks = past // SEL_BLOCK + 1
    blk = lax.broadcasted_iota(jnp.int32, (nsb, LANES), 0)
    cur = past // SEL_BLOCK
    forced = (blk == 0) | (blk == cur) | (blk == cur - 1)
    valid = (blk * SEL_BLOCK <= past) & (blk < n_blocks)
    score = jnp.where(valid, jnp.where(forced, FORCE_SCORE, imp_t), -1.0)
    sel_t = jnp.zeros((nsb, LANES), f32)
    for _ in range(min(SEL_TOPK, n_blocks)):
        mx = jnp.max(score, axis=0, keepdims=True)
        first = jnp.min(jnp.where(score == mx, blk, nsb), axis=0, keepdims=True)
        pick = blk == first
        sel_t = jnp.where(pick, 1.0, sel_t)
        score = jnp.where(pick, -2.0, score)
    mask_ref[...] = (sel_t - 1.0) * (-NEG_BIG)


def _nsa_sample_attend_body(pt_ref, *refs):
    pages = refs[:SMP_PAGES]
    (qt_ref, bias_ref, mask_ref, sn_ref, vn_ref, oc_ref, ow_ref, gate_ref, o_ref, m_ref, l_ref, acc_ref) = refs[SMP_PAGES:]
    p = pl.program_id(1)
    grp = lax.broadcasted_iota(jnp.int32, (1, LANES), 1) // NSA_HPG

    @pl.when(p == 0)
    def _():
        m_ref[...] = sn_ref[0:1, :]
        l_ref[...] = jnp.ones(l_ref.shape, f32)
        acc_ref[...] = vn_ref[...]

    qt = qt_ref[...]
    slabs = 2 * NSA_KV_GROUPS
    s = None
    for g in range(NSA_KV_GROUPS):
        kg = jnp.concatenate([pages[kk][pl.ds(g, PAGE_SIZE, stride=slabs), :] for kk in range(SMP_PAGES)], axis=0)
        sg = jnp.dot(kg.astype(bf16), qt, preferred_element_type=f32)
        s = sg if g == 0 else jnp.where(grp == g, sg, s)
    bpp = PAGE_SIZE // SEL_BLOCK
    mrows = []
    for kk in range(SMP_PAGES):
        for half in range(bpp):
            mrow = mask_ref[pl.ds((p * SMP_PAGES + kk) * bpp + half, 1), :]
            mrows.append(jnp.broadcast_to(mrow, (SEL_BLOCK, LANES)))
    s = s * (NSA_DH ** -0.5) + bias_ref[...] + jnp.concatenate(mrows, axis=0)
    m_old = m_ref[...]
    m_new = jnp.maximum(m_old, jnp.max(s, axis=0, keepdims=True))
    alpha = jnp.exp(m_old - m_new)
    pe = jnp.exp(s - m_new)
    l_ref[...] = alpha * l_ref[...] + jnp.sum(pe, axis=0, keepdims=True)
    pb = pe.astype(bf16)
    upd = jnp.zeros(acc_ref.shape, f32)
    for g in range(NSA_KV_GROUPS):
        vg = jnp.concatenate([pages[kk][pl.ds(NSA_KV_GROUPS + g, PAGE_SIZE, stride=slabs), :]
                              for kk in range(SMP_PAGES)], axis=0)
        upd = upd + lax.dot_general(vg.astype(bf16), jnp.where(grp == g, pb, jnp.zeros_like(pb)), _TN,
                                    preferred_element_type=f32)
    acc_ref[...] = alpha * acc_ref[...] + upd
    m_ref[...] = m_new

    @pl.when(p == pl.num_programs(1) - 1)
    def _():
        gt = jax.nn.sigmoid(gate_ref[...])
        o_s = acc_ref[...] / l_ref[...]
        o_ref[...] = gt[0:1] * oc_ref[...] + gt[1:2] * o_s + gt[2:3] * ow_ref[...]


def nsa_sample(z, zs, row0, cache_c, cache_s, page_table, win_buf, cmp_pe, cmp_w, rel_bias):
    bs, n_pages = page_table.shape
    past = n_pages * PAGE_SIZE
    n_sub = past // CMP_STRIDE
    n_blocks = past // SEL_BLOCK + 1
    nsb = -(-n_blocks // SUBLANES) * SUBLANES
    w_buf = win_buf.shape[1]
    tbl = rel_bias.astype(f32)
    lane_pad = lambda a: jnp.pad(a, [(0, 0)] * (a.ndim - 1) + [(0, LANES - a.shape[-1])])

    tbl_lanes = lane_pad(tbl)

    def bias_rows(dist, ok):
        return jnp.where(ok[:, None], _rel_bias_of(tbl_lanes, dist), NEG_BIG)

    dist_c = past - (jnp.arange(n_sub, dtype=jnp.int32) * CMP_STRIDE + (CMP_BLOCK - 1))
    bias_c = bias_rows(dist_c, dist_c >= 0)
    w_pos = past - w_buf + jnp.arange(w_buf, dtype=jnp.int32)
    dist_w = past - w_pos
    bias_w = bias_rows(dist_w, (dist_w < WINDOW) & (w_pos >= 0))
    bias_0 = jnp.broadcast_to(_rel_bias_of(tbl_lanes, jnp.zeros((1,), jnp.int32)), (SUBLANES, LANES))
    dist_s = past - jnp.arange(past, dtype=jnp.int32)
    bias_s = bias_rows(dist_s, dist_s >= 0)
    cover_t = jnp.pad(_cover_matrix(n_sub - 1, n_blocks), [(0, 1), (0, nsb - n_blocks)]).T

    w2 = cmp_w.reshape(2, CMP_BLOCK // CMP_STRIDE, CMP_STRIDE * NSA_DH, NSA_DH)
    w2 = jnp.transpose(w2, (0, 2, 1, 3)).reshape(2, CMP_STRIDE * NSA_DH, 2 * NSA_DH).astype(bf16)
    pe2 = jnp.pad(cmp_pe.reshape(2, CMP_BLOCK // CMP_STRIDE, CMP_STRIDE * NSA_DH), [(0, 0), (0, SUBLANES - 2), (0, 0)])
    pe2 = pe2.astype(bf16)

    zrow = z[row0:row0 + bs]
    q = zrow[:, Z_NQ:Z_KV].reshape(bs, NSA_HEADS, NSA_DH)
    qt = lane_pad(jnp.transpose(q, (0, 2, 1))).astype(bf16)
    kv = zrow[:, Z_KV:Z_GA].reshape(bs, 3, 2, NSA_KV_GROUPS, NSA_DH)
    zero = jnp.zeros((bs, NSA_KV_GROUPS, NSA_DH), f32)
    pad_rows = lambda r0, r1: jnp.pad(jnp.stack([r0, r1], 2), [(0, 0), (0, 0), (0, SUBLANES - 2), (0, 0)]).astype(bf16)
    k_new = pad_rows(kv[:, 1, 0], kv[:, 2, 0])
    v_new_w = pad_rows(zero, kv[:, 2, 1])
    v_new_s = lane_pad(jnp.transpose(jnp.repeat(kv[:, 1, 1], NSA_HPG, axis=1), (0, 2, 1)))
    gates = zs[row0:row0 + bs, 2 * GDN_HEADS:2 * GDN_HEADS + 3 * NSA_HEADS].reshape(bs, NSA_HEADS, 3)
    gates = jnp.pad(lane_pad(jnp.transpose(gates, (0, 2, 1))), [(0, 0), (0, SUBLANES - 3), (0, 0)])

    pm = compress_sample(page_table, cache_c.reshape(-1, PAGE_SIZE * 2 * NSA_KV_GROUPS, NSA_DH), w2)
    per_seq = lambda *shape: pl.BlockSpec((None,) + shape, lambda b: (b,) + (0,) * len(shape))
    const = lambda *shape: pl.BlockSpec(shape, lambda b: (0,) * len(shape))
    oc_t, ow_t, mask, s_new = pl.pallas_call(
        functools.partial(_nsa_sample_select_body, past=past),
        grid=(bs,),
        in_specs=[per_seq(NSA_DH, LANES), per_seq(2, NSA_KV_GROUPS, n_sub, 2 * NSA_DH),
                  const(2, CMP_STRIDE * NSA_DH, 2 * NSA_DH), const(2, SUBLANES, CMP_STRIDE * NSA_DH),
                  const(n_sub, LANES), const(nsb, n_sub), per_seq(w_buf * 2 * NSA_KV_GROUPS, NSA_DH),
                  const(w_buf, LANES),
                  const(SUBLANES, LANES), per_seq(NSA_KV_GROUPS, SUBLANES, NSA_DH),
                  per_seq(NSA_KV_GROUPS, SUBLANES, NSA_DH)],
        out_specs=[per_seq(NSA_DH, LANES), per_seq(NSA_DH, LANES), per_seq(nsb, LANES), per_seq(SUBLANES, LANES)],
        out_shape=[jax.ShapeDtypeStruct((bs, NSA_DH, LANES), f32), jax.ShapeDtypeStruct((bs, NSA_DH, LANES), f32),
                   jax.ShapeDtypeStruct((bs, nsb, LANES), f32), jax.ShapeDtypeStruct((bs, SUBLANES, LANES), f32)],
        compiler_params=_cparams(("parallel",)),
        name="nsa_sample_select",
    )(qt, pm, w2, pe2, bias_c, cover_t, win_buf.reshape(bs, w_buf * 2 * NSA_KV_GROUPS, NSA_DH), bias_w, bias_0,
      k_new, v_new_w)

    seq = lambda *shape: pl.BlockSpec((None,) + shape, lambda b, p, pt: (b,) + (0,) * len(shape))
    grid_spec = pltpu.PrefetchScalarGridSpec(
        num_scalar_prefetch=1,
        grid=(bs, n_pages // SMP_PAGES),
        in_specs=_page_specs((PAGE_SIZE * 2 * NSA_KV_GROUPS, NSA_DH)) + [seq(NSA_DH, LANES),
                                  pl.BlockSpec((SMP_PAGES * PAGE_SIZE, LANES), lambda b, p, pt: (p, 0)),
                                  seq(nsb, LANES), seq(SUBLANES, LANES), seq(NSA_DH, LANES), seq(NSA_DH, LANES),
                                  seq(NSA_DH, LANES), seq(SUBLANES, LANES)],
        out_specs=seq(NSA_DH, LANES),
        scratch_shapes=[pltpu.VMEM((1, LANES), f32), pltpu.VMEM((1, LANES), f32), pltpu.VMEM((NSA_DH, LANES), f32)],
    )
    o_t = pl.pallas_call(
        _nsa_sample_attend_body,
        grid_spec=grid_spec,
        out_shape=jax.ShapeDtypeStruct((bs, NSA_DH, LANES), f32),
        compiler_params=_cparams(("parallel", "arbitrary")),
        name="nsa_sample_attend",
    )(page_table, *([cache_s.reshape(-1, PAGE_SIZE * 2 * NSA_KV_GROUPS, NSA_DH)] * SMP_PAGES), qt, bias_s, mask, s_new, v_new_s,
      oc_t, ow_t, gates)
    return jnp.transpose(o_t[:, :, :NSA_HEADS], (0, 2, 1)).reshape(bs, NSA_Q_W)


def _rel_bucket(dist):
    n = jnp.maximum(dist, 0)
    max_exact = REL_BUCKETS // 2
    nf = jnp.maximum(n, 1).astype(f32)
    large = max_exact + (jnp.log(nf / max_exact) / math.log(REL_MAX_DIST / max_exact)
                         * (REL_BUCKETS - max_exact)).astype(jnp.int32)
    large = jnp.minimum(large, REL_BUCKETS - 1)
    return jnp.where(n < max_exact, n, large)


def _rel_bias_of(tbl, dist):
    one_hot = jax.nn.one_hot(_rel_bucket(dist), REL_BUCKETS, dtype=f32)
    return jnp.dot(one_hot, tbl, precision=lax.Precision.HIGHEST)


def _rel_bias_heads(tbl, dist):
    buckets = jnp.arange(REL_BUCKETS, dtype=jnp.int32).reshape((REL_BUCKETS,) + (1,) * dist.ndim)
    one_hot = (_rel_bucket(dist)[None] == buckets).astype(f32)
    return jnp.tensordot(tbl.T, one_hot, axes=1, precision=lax.Precision.HIGHEST)


def _cover_matrix(nc, ns):
    cs = np.arange(nc) * CMP_STRIDE
    ss = np.arange(ns) * SEL_BLOCK
    inter = np.minimum(cs[:, None] + CMP_BLOCK, ss[None, :] + SEL_BLOCK) - np.maximum(cs[:, None], ss[None, :])
    return jnp.asarray(np.clip(inter, 0, None) / CMP_BLOCK, dtype=f32)


def kernel(x_prompt, x_sample, p_prompt, p_sample, cache_cmp_kv, cache_slc_kv, page_table, state_win_kv, state_gdn, state_conv, g_mix, w_in, gdn_conv_w, gdn_dt_bias, gdn_a_log, gdn_norm, cmp_pe, cmp_w, rel_bias, w_proj_a, w_proj_b, w_out, g_ffn, w_router_group, b_router_group, w_router_expert, b_router_expert, w_gate, w_up, w_down, g_ple, w_ple_gate, w_ple_proj, g_final):
    bp, tp, d = x_prompt.shape
    bs, ts, _ = x_sample.shape
    n_p, n_s = bp * tp, bs * ts
    n_real = n_p + n_s
    mp = -(-n_real // ROW_ALIGN) * ROW_ALIGN
    pad = mp - n_real

    h = jnp.concatenate([x_prompt.reshape(n_p, d), x_sample.reshape(n_s, d), jnp.zeros((pad, d), f32)], 0)
    ple = jnp.concatenate([p_prompt[0].reshape(n_p, -1), p_sample[0].reshape(n_s, -1),
                           jnp.zeros((pad, p_prompt.shape[-1]), f32)], 0).astype(bf16)

    wt = jnp.swapaxes(w_in[0], 0, 1)
    o_beta = 4 * GDN_QK_W
    o_nq = o_beta + 2 * GDN_HEADS
    o_gate = o_nq + NSA_Q_W + 6 * NSA_KV_W
    o_ga = o_gate + 3 * NSA_HEADS
    n_small = 2 * GDN_HEADS + 3 * NSA_HEADS

    a = rmsnorm_rows(h, g_mix[0], bf16)
    z = proj_matmul(a, wt, ((0, o_beta), (o_nq, o_gate - o_nq), (o_ga, Z_COLS - Z_GA)))
    zs = proj_small(a, wt, o_beta, 2 * GDN_HEADS, o_gate, 3 * NSA_HEADS)

    def rows(x, lo, hi, which):
        if which == "p":
            return x[:n_p, lo:hi].reshape(bp, tp, hi - lo)
        return x[n_p:n_real, lo:hi].reshape(bs, ts, hi - lo)

    assert ts == 1 and tp % GDN_RT == 0 and n_p % GDN_SB == 0 and bs % GDN_SB == 0
    assert n_p % TM_ROWS == 0 and n_s <= TM_ROWS <= mp - n_p and tp % (NSA_KT * Q_BLOCK) == 0

    gb = jnp.stack([zs[:, 0:GDN_HEADS], zs[:, GDN_HEADS:2 * GDN_HEADS],
                    jnp.broadcast_to(gdn_dt_bias[0], (mp, GDN_HEADS)),
                    jnp.broadcast_to(gdn_a_log[0], (mp, GDN_HEADS))], -1)
    gb = jnp.transpose(gb, (1, 0, 2))
    gates_g = jnp.transpose(zs[:, 2 * GDN_HEADS:n_small].reshape(mp, NSA_KV_GROUPS, 3 * NSA_HPG), (1, 0, 2))

    outs = {}
    kv_shape = (2, NSA_KV_GROUPS, NSA_DH)
    kv_c_p = z[:n_p, Z_KV:Z_KV + _KV_ROW].reshape((bp, tp) + kv_shape)
    kv_s_p = z[:n_p, Z_KV + _KV_ROW:Z_KV + 2 * _KV_ROW].reshape((bp, tp) + kv_shape)
    w_keep = min(WINDOW, tp)
    win_p = jnp.stack([z[(b + 1) * tp - w_keep:(b + 1) * tp, Z_KV + 2 * _KV_ROW:Z_GA] for b in range(bp)], 0)
    win_p = win_p.reshape((bp, w_keep) + kv_shape)
    kvc = compress_prompt(z, cmp_w[0], cmp_pe[0], bp, tp)
    o_b_p = nsa_prompt_attention(z, gates_g, kvc, rel_bias, bp, tp)
    o_a_p, s_p = gdn_prompt(z, gb, gdn_conv_w[0], gdn_norm[0], bp, tp)
    conv_p = jnp.stack([z[(b + 1) * tp - (GDN_CONV - 1):(b + 1) * tp, :CONV_CH] for b in range(bp)], 0)
    outs["p"] = (kv_c_p, kv_s_p, win_p, s_p, conv_p)
    kv = rows(z, Z_KV, Z_GA, "s").reshape(bs, ts, 3, 2, NSA_KV_GROUPS, NSA_DH)
    o_b_s = nsa_sample(z, zs, n_p, cache_cmp_kv[0], cache_slc_kv[0], page_table, state_win_kv[0],
                       cmp_pe[0], cmp_w[0], rel_bias)
    win_s = jnp.concatenate([state_win_kv[0], kv[:, :, 2]], axis=1)[:, ts:]
    o_a_s, s_s = gdn_sample(z, gb, state_conv[0], state_gdn[0], gdn_conv_w[0], gdn_norm[0], n_p, bs)
    conv_s = jnp.concatenate([state_conv[0][:, 1:], rows(z, 0, CONV_CH, "s")], axis=1)
    outs["s"] = (kv[:, :, 0], kv[:, :, 1], win_s, s_s, conv_s)

    o_a = jnp.concatenate([o_a_p, o_a_s.astype(bf16), jnp.zeros((pad, GDN_V_W), bf16)], 0)
    o_b = jnp.concatenate([o_b_p, o_b_s.reshape(n_s, NSA_Q_W).astype(bf16), jnp.zeros((pad, NSA_Q_W), bf16)], 0)
    merged = merge_matmul(o_a, o_b, z, w_proj_a[0], w_proj_b[0])
    h = resid_matmul(merged, w_out[0], h)
    h = hier_moe(h, n_real, g_ffn[0], w_router_group[0], b_router_group[0], w_router_expert[0],
                 b_router_expert[0], w_gate[0], w_up[0], w_down[0])
    n3 = rmsnorm_rows(h, g_ple[0], bf16)
    h = ple_matmul(n3, w_ple_gate[0], ple, w_ple_proj[0], h)
    y_prompt = rmsnorm_rows(h, g_final, f32, 0, n_p).reshape(bp, tp, d)
    y_sample = rmsnorm_rows(h, g_final, f32, n_p, TM_ROWS)[:n_s].reshape(bs, ts, d)
    st_p, st_s = outs["p"], outs["s"]
    return (y_prompt, y_sample) + tuple(t[None] for t in st_p) + tuple(t[None] for t in st_s)
```
